```python
import jax, jax.numpy as jnp
from jax import lax
import numpy as np

D_MODEL = 1024
BATCH = 16
SEQ = 4096
DEPTH = 1
DEC_BATCH = 8
DEC_SEQ = 64
PAST_LEN = 4096

CHUNK = 64
N_META = 16
SB_HEADS = 8
SB_HEAD_DIM = 64
SB_WIDTH = SB_HEADS * SB_HEAD_DIM
SB_BLOCK = 128
RET_HEADS = 4
RET_DK = 128
RET_DV = 128
RET_WIDTH = RET_HEADS * RET_DV
MIX_WIDTH = SB_WIDTH + RET_WIDTH
IN_WIDTH = 3 * SB_WIDTH + 2 * RET_HEADS * RET_DK + 2 * RET_WIDTH
D_FF = 2816
CONV_W = 3
ROPE_BASE = 10000.0
EPS = 1e-5

kernel_name = "stickbreak_retention_hymba_convffn_stream"


def _rmsnorm(x, g):
    xf = x.astype(jnp.float32)
    y = xf * lax.rsqrt(jnp.mean(xf * xf, axis=-1, keepdims=True) + EPS)
    return (y * g.astype(jnp.float32)).astype(x.dtype)


def _heads(t, n):
    b, l, _ = t.shape
    return t.reshape(b, l, n, -1).transpose(0, 2, 1, 3)


def _project(x, g_norm_mix, w_in):
    h = _rmsnorm(x, g_norm_mix)
    p = h @ w_in
    s1 = SB_WIDTH
    s2 = 2 * SB_WIDTH
    s3 = 3 * SB_WIDTH
    s4 = s3 + RET_HEADS * RET_DK
    s5 = s4 + RET_HEADS * RET_DK
    s6 = s5 + RET_WIDTH
    sb_q, sb_k, sb_v, r_q, r_k, r_v, r_g = jnp.split(p, [s1, s2, s3, s4, s5, s6], axis=-1)
    return (_heads(sb_q, SB_HEADS), _heads(sb_k, SB_HEADS), _heads(sb_v, SB_HEADS),
            _heads(r_q, RET_HEADS), _heads(r_k, RET_HEADS), _heads(r_v, RET_HEADS), r_g)


def _stick_breaking(q, k, v, q_pos, k_pos):
    z = jnp.einsum('bhqd,bhkd->bhqk', q, k).astype(jnp.float32) * (SB_HEAD_DIM ** -0.5)
    mask = k_pos[None, :] < q_pos[:, None]
    log_1m = jnp.where(mask, jax.nn.log_sigmoid(-z), 0.0)
    later = lax.cumsum(log_1m, axis=z.ndim - 1, reverse=True) - log_1m
    a = jnp.where(mask, jnp.exp(jax.nn.log_sigmoid(z) + later), 0.0)
    return jnp.einsum('bhqk,bhkd->bhqd', a.astype(v.dtype), v)


def _stick_breaking_prompt(q, k, v):
    b, h, l, d = q.shape
    nb = -(-l // SB_BLOCK)
    lp = nb * SB_BLOCK
    pad = ((0, 0), (0, 0), (0, lp - l), (0, 0))
    qp, kp, vp = jnp.pad(q, pad), jnp.pad(k, pad), jnp.pad(v, pad)
    k_pos = jnp.arange(lp)
    q_blocks = qp.reshape(b, h, nb, SB_BLOCK, d).transpose(2, 0, 1, 3, 4)
    pos_blocks = k_pos.reshape(nb, SB_BLOCK)
    out = lax.map(lambda a: _stick_breaking(a[0], kp, vp, a[1], k_pos), (q_blocks, pos_blocks))
    return out.transpose(1, 2, 0, 3, 4).reshape(b, h, lp, d)[:, :, :l]


def _ret_log_gamma():
    return jnp.log(1.0 - 2.0 ** (-5.0 - jnp.arange(RET_HEADS, dtype=jnp.float32)))


def _rope(t, pos):
    half = t.shape[-1] // 2
    inv = ROPE_BASE ** (-jnp.arange(half, dtype=jnp.float32) / half)
    ang = pos.astype(jnp.float32)[:, None] * inv[None, :]
    cos, sin = jnp.cos(ang), jnp.sin(ang)
    t1, t2 = t[..., :half], t[..., half:]
    return jnp.concatenate([t1 * cos - t2 * sin, t1 * sin + t2 * cos], axis=-1)


def _retention_prep(r_q, r_k, r_v, pos):
    q = _rope(r_q.astype(jnp.float32), pos)
    k = _rope(r_k.astype(jnp.float32), pos) * (RET_DK ** -0.5)
    return q, k, r_v.astype(jnp.float32)


def _retention_chunk(s, q, k, v):
    l = q.shape[2]
    log_g = _ret_log_gamma()[:, None, None]
    n = jnp.arange(l, dtype=jnp.float32)
    diff = n[:, None] - n[None, :]
    decay = jnp.where(diff >= 0, jnp.exp(jnp.maximum(diff, 0.0) * log_g), 0.0)
    inner = jnp.einsum('bhqk,bhke->bhqe', jnp.einsum('bhqd,bhkd->bhqk', q, k) * decay, v)
    cross = jnp.einsum('bhqd,bhde->bhqe', q, s) * jnp.exp((n + 1.0)[:, None] * log_g)
    k_dec = k * jnp.exp((l - 1.0 - n)[:, None] * log_g)
    s_new = jnp.exp(l * log_g) * s + jnp.einsum('bhkd,bhke->bhde', k_dec, v)
    return s_new, inner + cross


def _head_norm(o, g, n_heads):
    b, _, l, _ = o.shape
    of = o.astype(jnp.float32)
    y = of * lax.rsqrt(jnp.mean(of * of, axis=-1, keepdims=True) + EPS)
    y = y * g.astype(jnp.float32).reshape(n_heads, 1, -1)
    return y.transpose(0, 2, 1, 3).reshape(b, l, -1)


def _merge(sb_o, ret_o, r_g, g_sb_out, g_ret_out, w_out, dtype):
    sb = _head_norm(sb_o, g_sb_out, SB_HEADS)
    ret = _head_norm(ret_o, g_ret_out, RET_HEADS) * jax.nn.silu(r_g.astype(jnp.float32))
    return jnp.concatenate([sb, ret], axis=-1).astype(dtype) @ w_out


def _conv_ffn(x, conv_prev, g_norm_ffn, w_up, conv_w, conv_b, w_down):
    h = _rmsnorm(x, g_norm_ffn)
    u = h @ w_up
    l = u.shape[1]
    ext = jnp.concatenate([conv_prev.astype(u.dtype), u], axis=1)
    c = conv_b
    for i in range(CONV_W):
        c = c + conv_w[i] * ext[:, i:i + l]
    gate, val = jnp.split(c, 2, axis=-1)
    return (jax.nn.silu(gate) * val) @ w_down, ext[:, -(CONV_W - 1):]


def _prompt_path(x, meta_tokens, g_norm_mix, w_in, g_sb_out, g_ret_out, w_out,
                 g_norm_ffn, w_up, conv_w, conv_b, w_down, g_norm_final):
    b, seq, d = x.shape
    meta = jnp.broadcast_to(meta_tokens.astype(x.dtype)[None], (b, N_META, d))
    xs = jnp.concatenate([meta, x], axis=1)
    l = N_META + seq
    sb_q, sb_k, sb_v, r_q, r_k, r_v, r_g = _project(xs, g_norm_mix, w_in)
    sb_o = _stick_breaking_prompt(sb_q, sb_k, sb_v)
    pos = jnp.arange(l) - N_META
    q, k, v = _retention_prep(r_q, r_k, r_v, pos)
    s0 = jnp.zeros((b, RET_HEADS, RET_DK, RET_DV), jnp.float32)
    s_meta, o_meta = _retention_chunk(s0, q[:, :, :N_META], k[:, :, :N_META], v[:, :, :N_META])
    nc = seq // CHUNK

    def to_chunks(t):
        return t[:, :, N_META:].reshape(b, RET_HEADS, nc, CHUNK, -1).transpose(2, 0, 1, 3, 4)

    def step(s, qkv):
        return _retention_chunk(s, qkv[0], qkv[1], qkv[2])

    s_final, o_chunks = lax.scan(step, s_meta, (to_chunks(q), to_chunks(k), to_chunks(v)))
    o_frames = o_chunks.transpose(1, 2, 0, 3, 4).reshape(b, RET_HEADS, seq, RET_DV)
    ret_o = jnp.concatenate([o_meta, o_frames], axis=2)
    h = xs + _merge(sb_o, ret_o, r_g, g_sb_out, g_ret_out, w_out, xs.dtype)
    conv0 = jnp.zeros((b, CONV_W - 1, 2 * D_FF), h.dtype)
    f, conv_state = _conv_ffn(h, conv0, g_norm_ffn, w_up, conv_w, conv_b, w_down)
    h = h + f
    y = _rmsnorm(h, g_norm_final)[:, N_META:]
    return y, sb_k, sb_v, s_final, conv_state


def _sample_path(x, cache_sb_k, cache_sb_v, state_ret, state_conv, g_norm_mix, w_in, g_sb_out,
                 g_ret_out, w_out, g_norm_ffn, w_up, conv_w, conv_b, w_down, g_norm_final):
    ls = x.shape[1]
    past = cache_sb_k.shape[2]
    sb_q, sb_k, sb_v, r_q, r_k, r_v, r_g = _project(x, g_norm_mix, w_in)
    k_all = jnp.concatenate([cache_sb_k.astype(sb_k.dtype), sb_k], axis=2)
    v_all = jnp.concatenate([cache_sb_v.astype(sb_v.dtype), sb_v], axis=2)
    sb_o = _stick_breaking(sb_q, k_all, v_all, past + jnp.arange(ls), jnp.arange(past + ls))
    pos = past + jnp.arange(ls)
    q, k, v = _retention_prep(r_q, r_k, r_v, pos)
    s_new, ret_o = _retention_chunk(state_ret.astype(jnp.float32), q, k, v)
    h = x + _merge(sb_o, ret_o, r_g, g_sb_out, g_ret_out, w_out, x.dtype)
    f, conv_state = _conv_ffn(h, state_conv, g_norm_ffn, w_up, conv_w, conv_b, w_down)
    h = h + f
    y = _rmsnorm(h, g_norm_final)
    return y, sb_k, sb_v, s_new, conv_state


def setup_inputs(seed: int = 0) -> dict:
    key = jax.random.key(seed)
    ks = jax.random.split(key, 18)
    f32 = jnp.float32
    nrm = lambda k, shape, s: jax.random.normal(k, shape, f32) * s
    return {
        "x_prompt": nrm(ks[0], (BATCH, SEQ, D_MODEL), 1.0),
        "x_sample": nrm(ks[1], (DEC_BATCH, DEC_SEQ, D_MODEL), 1.0),
        "cache_sb_k": nrm(ks[2], (DEC_BATCH, SB_HEADS, PAST_LEN, SB_HEAD_DIM), 1.0),
        "cache_sb_v": nrm(ks[3], (DEC_BATCH, SB_HEADS, PAST_LEN, SB_HEAD_DIM), 1.0),
        "state_ret": nrm(ks[4], (DEC_BATCH, RET_HEADS, RET_DK, RET_DV), 0.05),
        "state_conv": nrm(ks[5], (DEC_BATCH, CONV_W - 1, 2 * D_FF), 1.0),
        "meta_tokens": nrm(ks[6], (N_META, D_MODEL), 1.0),
        "g_norm_mix": 1.0 + nrm(ks[7], (D_MODEL,), 0.01),
        "w_in": nrm(ks[8], (D_MODEL, IN_WIDTH), D_MODEL ** -0.5),
        "g_sb_out": 1.0 + nrm(ks[9], (SB_WIDTH,), 0.01),
        "g_ret_out": 1.0 + nrm(ks[10], (RET_WIDTH,), 0.01),
        "w_out": nrm(ks[11], (MIX_WIDTH, D_MODEL), MIX_WIDTH ** -0.5),
        "g_norm_ffn": 1.0 + nrm(ks[12], (D_MODEL,), 0.01),
        "w_up": nrm(ks[13], (D_MODEL, 2 * D_FF), D_MODEL ** -0.5),
        "conv_w": nrm(ks[14], (CONV_W, 2 * D_FF), CONV_W ** -0.5),
        "conv_b": nrm(ks[15], (2 * D_FF,), 0.01),
        "w_down": nrm(ks[16], (D_FF, D_MODEL), D_FF ** -0.5),
        "g_norm_final": 1.0 + nrm(ks[17], (D_MODEL,), 0.01),
    }


def reference(x_prompt, x_sample, cache_sb_k, cache_sb_v, state_ret, state_conv, meta_tokens,
              g_norm_mix, w_in, g_sb_out, g_ret_out, w_out, g_norm_ffn, w_up, conv_w, conv_b,
              w_down, g_norm_final):
    y_prompt, sbk_p, sbv_p, ret_p, conv_p = _prompt_path(
        x_prompt, meta_tokens, g_norm_mix, w_in, g_sb_out, g_ret_out, w_out,
        g_norm_ffn, w_up, conv_w, conv_b, w_down, g_norm_final)
    y_sample, sbk_s, sbv_s, ret_s, conv_s = _sample_path(
        x_sample, cache_sb_k, cache_sb_v, state_ret, state_conv, g_norm_mix, w_in, g_sb_out,
        g_ret_out, w_out, g_norm_ffn, w_up, conv_w, conv_b, w_down, g_norm_final)
    return (y_prompt, y_sample, sbk_p, sbv_p, ret_p, conv_p, sbk_s, sbv_s, ret_s, conv_s)
```

```python
import functools
import math

import jax
import jax.numpy as jnp
from jax import lax
from jax.experimental import pallas as pl
from jax.experimental.pallas import tpu as pltpu

D_MODEL = 1024
N_META = 16
SB_HEADS = 8
SB_HEAD_DIM = 64
SB_WIDTH = SB_HEADS * SB_HEAD_DIM
RET_HEADS = 4
RET_DK = 128
RET_DV = 128
RET_WIDTH = RET_HEADS * RET_DV
MIX_WIDTH = SB_WIDTH + RET_WIDTH
GROUP = 512
N_GROUPS = 7
IN_WIDTH = N_GROUPS * GROUP
D_FF = 2816
CONV_W = 3
ROPE_BASE = 10000.0
EPS = 1e-5

LANES = 128
SUBLANES = 8
FF_CHUNK = 256
N_FF_CHUNKS = D_FF // FF_CHUNK
SB_TILE = 128
VMEM_LIMIT = 56 * 1024 * 1024

BF16 = jnp.bfloat16
F32 = jnp.float32


def _const_spec(shape):
    zeros = (0,) * len(shape)
    return pl.BlockSpec(shape, lambda *_: zeros)


def _project_kernel(x_ref, g_ref, w_ref, cos_ref, sin_ref,
                    q_ref, k_ref, v_ref, ko_ref, vo_ref, rq_ref, rk_ref, rv_ref, rg_ref):
    x = x_ref[0]
    ms = jnp.mean(x * x, axis=-1, keepdims=True)
    h = (x * lax.rsqrt(ms + EPS) * g_ref[...]).astype(BF16)

    def group(i):
        return jnp.dot(h, w_ref[:, i * GROUP:(i + 1) * GROUP], preferred_element_type=F32)

    def split_heads(p, out_ref):
        for hh in range(SB_HEADS):
            out_ref[0, hh] = p[:, hh * SB_HEAD_DIM:(hh + 1) * SB_HEAD_DIM]

    def rope(p, out_ref, scale):
        cos = cos_ref[...]
        sin = sin_ref[...]
        for hh in range(RET_HEADS):
            t = p[:, hh * RET_DK:(hh + 1) * RET_DK]
            r = t * cos + pltpu.roll(t, RET_DK // 2, 1) * sin
            if scale is not None:
                r = r * scale
            out_ref[0, :, hh * RET_DK:(hh + 1) * RET_DK] = r.astype(BF16)

    q_ref[0] = (group(0) * (SB_HEAD_DIM ** -0.5)).astype(BF16)
    pk = group(1)
    k_ref[0] = pk.astype(BF16)
    split_heads(pk, ko_ref)
    pv = group(2)
    v_ref[0] = pv.astype(BF16)
    split_heads(pv, vo_ref)
    rope(group(3), rq_ref, None)
    rope(group(4), rk_ref, RET_DK ** -0.5)
    rv_ref[0] = group(5).astype(BF16)
    rg_ref[0] = group(6).astype(BF16)


def _project(x, g_norm, w_in_bf, cos2, sin2, tm):
    b, l, d = x.shape
    grid = (b, l // tm)
    row_spec = lambda w: pl.BlockSpec((1, tm, w), lambda bi, i: (bi, i, 0))
    head_spec = pl.BlockSpec((1, SB_HEADS, tm, SB_HEAD_DIM), lambda bi, i: (bi, 0, i, 0))
    tab_spec = pl.BlockSpec((tm, RET_DK), lambda bi, i: (i, 0))
    act = jax.ShapeDtypeStruct((b, l, GROUP), BF16)
    heads = jax.ShapeDtypeStruct((b, SB_HEADS, l, SB_HEAD_DIM), F32)
    return pl.pallas_call(
        _project_kernel,
        grid=grid,
        in_specs=[row_spec(d), _const_spec((1, d)), _const_spec((d, IN_WIDTH)), tab_spec, tab_spec],
        out_specs=[row_spec(GROUP)] * 3 + [head_spec] * 2 + [row_spec(GROUP)] * 4,
        out_shape=[act] * 3 + [heads] * 2 + [act] * 4,
        compiler_params=pltpu.CompilerParams(
            dimension_semantics=("arbitrary", "arbitrary"), vmem_limit_bytes=VMEM_LIMIT),
        name="project",
    )(x, g_norm.reshape(1, d), w_in_bf, cos2, sin2)


def _split_bf16(x):
    hi = x.astype(BF16)
    lo = (x - hi.astype(F32)).astype(BF16)
    return hi, lo


def _attention_kernel(q_ref, kc_ref, vc_ref, kp_ref, vp_ref, w_ref, j_ref, g_ref, o_ref,
                      carry_ref, acc_ref, *, n_q, n_past, past_valid):
    t = SB_TILE
    lane = lax.broadcasted_iota(jnp.int32, (t, LANES), 1)
    row = lax.broadcasted_iota(jnp.int32, (t, t), 0)
    col = lax.broadcasted_iota(jnp.int32, (t, t), 1)
    head_lanes = (lane < SB_HEAD_DIM, lane >= SB_HEAD_DIM)

    def block_step(hd, qm, kblk, vblk, mask):
        z = lax.dot_general(qm, kblk, (((1,), (1,)), ((), ())), preferred_element_type=F32)
        sp = jnp.maximum(z, 0.0) + jnp.log(1.0 + jnp.exp(-jnp.abs(z)))
        spm = sp if mask is None else jnp.where(mask, sp, 0.0)
        hi, lo = _split_bf16(spm)
        r = jnp.dot(jnp.concatenate([hi, lo], axis=1), w_ref[...], preferred_element_type=F32)
        carry = carry_ref[hd]
        arg = (z - sp) - (r[:, :t] + carry)
        a = jnp.exp(arg)
        if mask is not None:
            a = jnp.where(mask, a, 0.0)
        acc_ref[hd] += jnp.dot(a.astype(BF16), vblk, preferred_element_type=F32)
        carry_ref[hd] = carry + r[:, t:]

    def q_tile(i, _):
        q2 = q_ref[0, pl.ds(pl.multiple_of(i * t, t), t), :]
        qms = [jnp.where(m, q2, jnp.zeros_like(q2)) for m in head_lanes]
        carry_ref[...] = jnp.zeros_like(carry_ref)
        acc_ref[...] = jnp.zeros_like(acc_ref)

        def both_heads(k_src, v_src, j, mask):
            start = pl.multiple_of(j * t, t)
            kblk = k_src[0, pl.ds(start, t), :]
            vblk = v_src[0, pl.ds(start, t), :]
            for hd in range(2):
                block_step(hd, qms[hd], kblk, vblk, mask)

        both_heads(kc_ref, vc_ref, i, col < row)

        def cur_body(jj, _):
            both_heads(kc_ref, vc_ref, i - 1 - jj, None)
            return 0

        lax.fori_loop(0, i, cur_body, 0)

        if n_past > 0:
            full_past = n_past
            if past_valid < t:
                both_heads(kp_ref, vp_ref, n_past - 1, col < past_valid)
                full_past = n_past - 1

            def past_body(jj, _):
                both_heads(kp_ref, vp_ref, full_past - 1 - jj, None)
                return 0

            if full_past > 0:
                lax.fori_loop(0, full_past, past_body, 0)

        o = jnp.where(head_lanes[0], acc_ref[0], acc_ref[1])
        hi, lo = _split_bf16(o * o)
        jm = j_ref[...]
        ms = (jnp.dot(hi, jm, preferred_element_type=F32)
              + jnp.dot(lo, jm, preferred_element_type=F32)) * (1.0 / SB_HEAD_DIM)
        y = o * lax.rsqrt(ms + EPS) * g_ref[...]
        o_ref[0, pl.ds(pl.multiple_of(i * t, t), t), :] = y.astype(BF16)
        return 0

    lax.fori_loop(0, n_q, q_tile, 0)


def _attention(q, k_cur, v_cur, k_past, v_past, past_len, g_sb_out):
    b, lq, _ = q.shape
    t = SB_TILE
    assert lq % t == 0
    if k_past is None:
        k_past = jnp.zeros((1, t, SB_WIDTH), BF16)
        v_past = jnp.zeros((1, t, SB_WIDTH), BF16)
        n_past, past_valid = 0, t
    else:
        assert k_past.shape[1] % t == 0
        n_past = -(-past_len // t)
        past_valid = past_len - (n_past - 1) * t
        k_past = k_past[:, :n_past * t]
        v_past = v_past[:, :n_past * t]
    bp, p, _ = k_past.shape
    jj = jnp.arange(t)
    upper = (jj[:, None] > jj[None, :]).astype(BF16)
    ones = jnp.ones((t, LANES), BF16)
    half = jnp.concatenate([upper, ones], axis=1)
    w = jnp.concatenate([half, half], axis=0)
    ll = jnp.arange(LANES) // SB_HEAD_DIM
    jm = (ll[:, None] == ll[None, :]).astype(BF16)
    pair = 2 * SB_HEAD_DIM
    cur_spec = pl.BlockSpec((1, lq, pair), lambda bi, hp: (bi, 0, hp))
    past_spec = pl.BlockSpec((1, p, pair), (lambda bi, hp: (bi, 0, hp)) if bp > 1
                             else (lambda bi, hp: (0, 0, hp)))
    kernel = functools.partial(_attention_kernel, n_q=lq // t, n_past=n_past, past_valid=past_valid)
    return pl.pallas_call(
        kernel,
        grid=(b, SB_HEADS // 2),
        in_specs=[cur_spec, cur_spec, cur_spec, past_spec, past_spec,
                  _const_spec(w.shape), _const_spec(jm.shape),
                  pl.BlockSpec((1, pair), lambda bi, hp: (0, hp))],
        out_specs=cur_spec,
        out_shape=jax.ShapeDtypeStruct((b, lq, SB_WIDTH), BF16),
        scratch_shapes=[pltpu.VMEM((2, t, LANES), F32), pltpu.VMEM((2, t, LANES), F32)],
        compiler_params=pltpu.CompilerParams(
            dimension_semantics=("arbitrary", "arbitrary"), vmem_limit_bytes=VMEM_LIMIT),
        name="attention",
    )(q, k_cur, v_cur, k_past, v_past, w, jm, g_sb_out.reshape(1, SB_WIDTH))


def _retention_kernel(q_ref, k_ref, v_ref, gate_ref, s0_ref, g_ref, o_ref, s_ref, *, chunk, n_chunks):
    c = chunk
    head = pl.program_id(1).astype(F32)
    log_g = jnp.log(1.0 - jnp.exp2(jnp.full((1, LANES), -5.0, F32) - head))
    n_row = lax.broadcasted_iota(jnp.int32, (c, LANES), 0).astype(F32)
    q_decay = jnp.exp((n_row + 1.0) * log_g)
    k_decay = jnp.exp((c - 1.0 - n_row) * log_g)
    s_decay = jnp.exp(float(c) * log_g)
    diff = (lax.broadcasted_iota(jnp.int32, (c, c), 0)
            - lax.broadcasted_iota(jnp.int32, (c, c), 1))
    log_g_cc = log_g if c == LANES else jnp.log(1.0 - jnp.exp2(jnp.full((1, c), -5.0, F32) - head))
    decay = jnp.where(diff >= 0, jnp.exp(jnp.maximum(diff, 0).astype(F32) * log_g_cc), 0.0)
    gain = g_ref[...]

    def body(ci, s):
        rows = pl.ds(pl.multiple_of(ci * c, c), c)
        q = q_ref[0, rows, :]
        k = k_ref[0, rows, :]
        v = v_ref[0, rows, :]
        qk = lax.dot_general(q, k, (((1,), (1,)), ((), ())), preferred_element_type=F32)
        inner = jnp.dot((qk * decay).astype(BF16), v, preferred_element_type=F32)
        cross = jnp.dot(q, s.astype(BF16), preferred_element_type=F32) * q_decay
        o = inner + cross
        k_dec = (k.astype(F32) * k_decay).astype(BF16)
        s_new = s_decay * s + lax.dot_general(k_dec, v, (((0,), (0,)), ((), ())),
                                              preferred_element_type=F32)
        y = o * lax.rsqrt(jnp.mean(o * o, axis=-1, keepdims=True) + EPS) * gain
        gate = gate_ref[0, rows, :].astype(F32)
        o_ref[0, rows, :] = (y * (gate * jax.nn.sigmoid(gate))).astype(BF16)
        return s_new

    s_ref[0, 0] = lax.fori_loop(0, n_chunks, body, s0_ref[0, 0])


def _retention(rq, rk, rv, rgate, s0, g_ret_out, chunk):
    b, l, _ = rq.shape
    assert l % chunk == 0
    bs = s0.shape[0]
    seq_spec = pl.BlockSpec((1, l, RET_DK), lambda bi, hh: (bi, 0, hh))
    state_spec = pl.BlockSpec((1, 1, RET_DK, RET_DV), lambda bi, hh: (bi, hh, 0, 0))
    s0_spec = state_spec if bs > 1 else pl.BlockSpec((1, 1, RET_DK, RET_DV), lambda bi, hh: (0, hh, 0, 0))
    kernel = functools.partial(_retention_kernel, chunk=chunk, n_chunks=l // chunk)
    return pl.pallas_call(
        kernel,
        grid=(b, RET_HEADS),
        in_specs=[seq_spec, seq_spec, seq_spec, seq_spec, s0_spec,
                  pl.BlockSpec((1, RET_DV), lambda bi, hh: (0, hh))],
        out_specs=[seq_spec, state_spec],
        out_shape=[jax.ShapeDtypeStruct((b, l, RET_WIDTH), BF16),
                   jax.ShapeDtypeStruct((b, RET_HEADS, RET_DK, RET_DV), F32)],
        compiler_params=pltpu.CompilerParams(
            dimension_semantics=("arbitrary", "arbitrary"), vmem_limit_bytes=VMEM_LIMIT),
        name="retention",
    )(rq, rk, rv, rgate, s0, g_ret_out.reshape(1, RET_WIDTH))


def _merge_ffn_kernel(x_ref, sb_ref, ret_ref, conv0_ref, wo_ref, gf_ref, wug_ref, wuv_ref,
                      cw_ref, cb_ref, wd_ref, gl_ref, y_ref, conv_ref,
                      h_ref, hn_ref, acc_ref, ubuf_ref, carry_ref, *, tm):
    i = pl.program_id(1)

    @pl.when(i == 0)
    def _():
        carry_ref[...] = conv0_ref[0]

    mixed = jnp.concatenate([sb_ref[0], ret_ref[0]], axis=1)
    h = x_ref[0] + jnp.dot(mixed, wo_ref[...], preferred_element_type=F32)
    h_ref[...] = h
    ms = jnp.mean(h * h, axis=-1, keepdims=True)
    hn_ref[...] = (h * lax.rsqrt(ms + EPS) * gf_ref[...]).astype(BF16)
    acc_ref[...] = jnp.zeros_like(acc_ref)
    lo = SUBLANES - (CONV_W - 1)

    def ff_chunk(j, _):
        hn = hn_ref[...]
        ubuf_ref[0:SUBLANES, :] = carry_ref[j]
        ubuf_ref[SUBLANES:, 0:FF_CHUNK] = jnp.dot(hn, wug_ref[j], preferred_element_type=F32)
        ubuf_ref[SUBLANES:, FF_CHUNK:] = jnp.dot(hn, wuv_ref[j], preferred_element_type=F32)
        carry_ref[j] = ubuf_ref[tm:tm + SUBLANES, :]
        cw = cw_ref[j]
        c = cb_ref[j]
        for tap in range(CONV_W):
            c = c + cw[tap:tap + 1, :] * ubuf_ref[lo + tap:lo + tap + tm, :]
        gate = c[:, :FF_CHUNK]
        act = (gate * jax.nn.sigmoid(gate)) * c[:, FF_CHUNK:]
        acc_ref[...] += jnp.dot(act.astype(BF16), wd_ref[j], preferred_element_type=F32)
        return 0

    lax.fori_loop(0, N_FF_CHUNKS, ff_chunk, 0)

    hh = h_ref[...] + acc_ref[...]
    ms2 = jnp.mean(hh * hh, axis=-1, keepdims=True)
    y_ref[0] = hh * lax.rsqrt(ms2 + EPS) * gl_ref[...]

    @pl.when(i == pl.num_programs(1) - 1)
    def _():
        conv_ref[0] = carry_ref[...]


def _merge_ffn(x, sb_n, ret_n, conv0, w_out_bf, g_norm_ffn, wug, wuv, cw, cb, wd, g_norm_final, tm):
    b, l, d = x.shape
    bs = conv0.shape[0]
    row_spec = lambda w: pl.BlockSpec((1, tm, w), lambda bi, i: (bi, i, 0))
    conv_shape = (1, N_FF_CHUNKS, SUBLANES, 2 * FF_CHUNK)
    conv_spec = pl.BlockSpec(conv_shape, lambda bi, i: (bi, 0, 0, 0))
    conv0_spec = conv_spec if bs > 1 else pl.BlockSpec(conv_shape, lambda bi, i: (0, 0, 0, 0))
    kernel = functools.partial(_merge_ffn_kernel, tm=tm)
    return pl.pallas_call(
        kernel,
        grid=(b, l // tm),
        in_specs=[row_spec(d), row_spec(SB_WIDTH), row_spec(RET_WIDTH), conv0_spec,
                  _const_spec(w_out_bf.shape), _const_spec((1, d)),
                  _const_spec(wug.shape), _const_spec(wuv.shape),
                  _const_spec(cw.shape), _const_spec(cb.shape), _const_spec(wd.shape),
                  _const_spec((1, d))],
        out_specs=[row_spec(d), conv_spec],
        out_shape=[jax.ShapeDtypeStruct((b, l, d), F32),
                   jax.ShapeDtypeStruct((b,) + conv_shape[1:], F32)],
        scratch_shapes=[pltpu.VMEM((tm, d), F32), pltpu.VMEM((tm, d), BF16), pltpu.VMEM((tm, d), F32),
                        pltpu.VMEM((tm + SUBLANES, 2 * FF_CHUNK), F32),
                        pltpu.VMEM(conv_shape[1:], F32)],
        compiler_params=pltpu.CompilerParams(
            dimension_semantics=("arbitrary", "arbitrary"), vmem_limit_bytes=VMEM_LIMIT),
        name="merge_ffn",
    )(x, sb_n, ret_n, conv0, w_out_bf, g_norm_ffn.reshape(1, d), wug, wuv, cw, cb, wd,
      g_norm_final.reshape(1, d))


def _rope_tables(pos):
    half = RET_DK // 2
    inv = ROPE_BASE ** (-jnp.arange(half, dtype=F32) / half)
    ang = pos.astype(F32)[:, None] * inv[None, :]
    cos, sin = jnp.cos(ang), jnp.sin(ang)
    return jnp.concatenate([cos, cos], axis=1), jnp.concatenate([-sin, sin], axis=1)


def _conv_state_to_chunks(state):
    b = state.shape[0]
    s = state.reshape(b, CONV_W - 1, 2, N_FF_CHUNKS, FF_CHUNK).transpose(0, 3, 1, 2, 4)
    s = s.reshape(b, N_FF_CHUNKS, CONV_W - 1, 2 * FF_CHUNK)
    return jnp.pad(s, ((0, 0), (0, 0), (SUBLANES - (CONV_W - 1), 0), (0, 0)))


def _conv_state_from_chunks(chunks):
    b = chunks.shape[0]
    s = chunks[:, :, SUBLANES - (CONV_W - 1):, :].reshape(b, N_FF_CHUNKS, CONV_W - 1, 2, FF_CHUNK)
    return s.transpose(0, 2, 3, 1, 4).reshape(b, CONV_W - 1, 2 * D_FF)


def _pad_rows(a, rows):
    return a if a.shape[1] == rows else jnp.pad(a, ((0, 0), (0, rows - a.shape[1]), (0, 0)))


def _stream_step(x, pos0, k_past, v_past, past_len, s0, conv0, wts, tm, chunk):
    l = x.shape[1]
    cos2, sin2 = _rope_tables(pos0 + jnp.arange(l))
    q, k, v, k_out, v_out, rq, rk, rv, rgate = _project(x, wts["g_norm_mix"], wts["w_in"], cos2, sin2, tm)
    lq = -(-l // SB_TILE) * SB_TILE
    sb_n = _attention(_pad_rows(q, lq), _pad_rows(k, lq), _pad_rows(v, lq),
                      k_past, v_past, past_len, wts["g_sb_out"])[:, :l]
    ret_n, s_new = _retention(rq, rk, rv, rgate, s0, wts["g_ret_out"], chunk)
    y, conv_new = _merge_ffn(x, sb_n, ret_n, conv0, wts["w_out"], wts["g_norm_ffn"], wts["wug"],
                             wts["wuv"], wts["cw"], wts["cb"], wts["wd"], wts["g_norm_final"], tm)
    return y, k, v, k_out, v_out, s_new, conv_new


def kernel(x_prompt, x_sample, cache_sb_k, cache_sb_v, state_ret, state_conv, meta_tokens, g_norm_mix, w_in, g_sb_out, g_ret_out, w_out, g_norm_ffn, w_up, conv_w, conv_b, w_down, g_norm_final):
    b, seq, d = x_prompt.shape
    bd, ls, _ = x_sample.shape
    past = cache_sb_k.shape[2]

    def ff_cols(a):
        r = a.shape[0]
        return a.reshape(r, 2, N_FF_CHUNKS, FF_CHUNK).transpose(2, 0, 1, 3).reshape(N_FF_CHUNKS, r, 2 * FF_CHUNK)

    w_up_bf = w_up.astype(BF16)
    wts = dict(
        g_norm_mix=g_norm_mix, g_sb_out=g_sb_out, g_ret_out=g_ret_out, g_norm_ffn=g_norm_ffn,
        g_norm_final=g_norm_final,
        w_in=w_in.astype(BF16), w_out=w_out.astype(BF16),
        wug=w_up_bf[:, :D_FF].reshape(d, N_FF_CHUNKS, FF_CHUNK).transpose(1, 0, 2),
        wuv=w_up_bf[:, D_FF:].reshape(d, N_FF_CHUNKS, FF_CHUNK).transpose(1, 0, 2),
        cw=ff_cols(conv_w), cb=ff_cols(conv_b.reshape(1, 2 * D_FF)),
        wd=w_down.astype(BF16).reshape(N_FF_CHUNKS, FF_CHUNK, d),
    )

    zero_state = jnp.zeros((1, RET_HEADS, RET_DK, RET_DV), F32)
    zero_conv = jnp.zeros((1, N_FF_CHUNKS, SUBLANES, 2 * FF_CHUNK), F32)
    _, k_m, v_m, k_m_out, v_m_out, s_meta, conv_meta = _stream_step(
        meta_tokens[None], -N_META, None, None, 0, zero_state, zero_conv, wts, N_META, N_META)

    y_prompt, _, _, k_p_out, v_p_out, s_prompt, conv_prompt = _stream_step(
        x_prompt, 0, _pad_rows(k_m, SB_TILE), _pad_rows(v_m, SB_TILE), N_META, s_meta, conv_meta,
        wts, 512, 256)
    rep = lambda a: jnp.broadcast_to(a, (b,) + a.shape[1:])
    new_k_prompt = jnp.concatenate([rep(k_m_out), k_p_out], axis=2)
    new_v_prompt = jnp.concatenate([rep(v_m_out), v_p_out], axis=2)

    to_rows = lambda c: c.transpose(0, 2, 1, 3).reshape(bd, past, SB_WIDTH).astype(BF16)
    y_sample, _, _, k_s_out, v_s_out, s_sample, conv_sample = _stream_step(
        x_sample, past, to_rows(cache_sb_k), to_rows(cache_sb_v), past, state_ret,
        _conv_state_to_chunks(state_conv), wts, ls, ls)

    return (y_prompt, y_sample, new_k_prompt, new_v_prompt, s_prompt,
            _conv_state_from_chunks(conv_prompt), k_s_out, v_s_out, s_sample,
            _conv_state_from_chunks(conv_sample))
```

```python
import functools

import jax
import jax.numpy as jnp
from jax import lax
from jax.experimental import pallas as pl
from jax.experimental.pallas import tpu as pltpu

D_MODEL = 1024
N_META = 16
SB_HEADS = 8
SB_HEAD_DIM = 64
SB_WIDTH = SB_HEADS * SB_HEAD_DIM
RET_HEADS = 4
RET_DK = 128
RET_DV = 128
RET_WIDTH = RET_HEADS * RET_DV
MIX_WIDTH = SB_WIDTH + RET_WIDTH
GROUP = 512
N_GROUPS = 7
IN_WIDTH = N_GROUPS * GROUP
D_FF = 2816
CONV_W = 3
ROPE_BASE = 10000.0
EPS = 1e-5

LANES = 128
SUBLANES = 8
FF_CHUNK = 256
N_FF_CHUNKS = D_FF // FF_CHUNK
SB_KEYS = 128
SB_QUERY_TILE = 512
SB_UNROLL = 2
PAIR = 2 * SB_HEAD_DIM
VMEM_LIMIT = 56 * 1024 * 1024

BF16 = jnp.bfloat16
F32 = jnp.float32


def _const_spec(shape):
    zeros = (0,) * len(shape)
    return pl.BlockSpec(shape, lambda *_: zeros)


def _project_kernel(x_ref, g_ref, w_ref, cos_ref, sin_ref,
                    q_ref, k_ref, v_ref, ko_ref, vo_ref, rq_ref, rk_ref, rv_ref, rg_ref):
    x = x_ref[0]
    ms = jnp.mean(x * x, axis=-1, keepdims=True)
    h = (x * lax.rsqrt(ms + EPS) * g_ref[...]).astype(BF16)

    def group(i):
        return jnp.dot(h, w_ref[:, i * GROUP:(i + 1) * GROUP], preferred_element_type=F32)

    def split_heads(p, out_ref):
        for hh in range(SB_HEADS):
            out_ref[0, hh] = p[:, hh * SB_HEAD_DIM:(hh + 1) * SB_HEAD_DIM]

    def rope(p, out_ref, scale):
        cos = cos_ref[...]
        sin = sin_ref[...]
        for hh in range(RET_HEADS):
            t = p[:, hh * RET_DK:(hh + 1) * RET_DK]
            r = t * cos + pltpu.roll(t, RET_DK // 2, 1) * sin
            if scale is not None:
                r = r * scale
            out_ref[0, :, hh * RET_DK:(hh + 1) * RET_DK] = r.astype(BF16)

    q_ref[0] = (group(0) * (SB_HEAD_DIM ** -0.5)).astype(BF16)
    pk = group(1)
    k_ref[0] = pk.astype(BF16)
    split_heads(pk, ko_ref)
    pv = group(2)
    v_ref[0] = pv.astype(BF16)
    split_heads(pv, vo_ref)
    rope(group(3), rq_ref, None)
    rope(group(4), rk_ref, RET_DK ** -0.5)
    rv_ref[0] = group(5).astype(BF16)
    rg_ref[0] = group(6).astype(BF16)


def _project(x, g_norm, w_in_bf, cos2, sin2, tm):
    b, l, d = x.shape
    grid = (b, l // tm)
    row_spec = lambda w: pl.BlockSpec((1, tm, w), lambda bi, i: (bi, i, 0))
    head_spec = pl.BlockSpec((1, SB_HEADS, tm, SB_HEAD_DIM), lambda bi, i: (bi, 0, i, 0))
    tab_spec = pl.BlockSpec((tm, RET_DK), lambda bi, i: (i, 0))
    act = jax.ShapeDtypeStruct((b, l, GROUP), BF16)
    heads = jax.ShapeDtypeStruct((b, SB_HEADS, l, SB_HEAD_DIM), F32)
    return pl.pallas_call(
        _project_kernel,
        grid=grid,
        in_specs=[row_spec(d), _const_spec((1, d)), _const_spec((d, IN_WIDTH)), tab_spec, tab_spec],
        out_specs=[row_spec(GROUP)] * 3 + [head_spec] * 2 + [row_spec(GROUP)] * 4,
        out_shape=[act] * 3 + [heads] * 2 + [act] * 4,
        compiler_params=pltpu.CompilerParams(
            dimension_semantics=("arbitrary", "arbitrary"), vmem_limit_bytes=VMEM_LIMIT),
        name="project",
    )(x, g_norm.reshape(1, d), w_in_bf, cos2, sin2)


def _split_bf16(x):
    hi = x.astype(BF16)
    lo = (x - hi.astype(F32)).astype(BF16)
    return hi, lo


def _attention_kernel(q_ref, kc_ref, vc_ref, kp_ref, vp_ref, bdu_ref, bdo_ref, j_ref, g_ref, o_ref,
                      kxc_ref, vxc_ref, kxp_ref, vxp_ref, carry_ref, acc_ref,
                      *, tq, n_q, n_past, past_valid):
    tk = SB_KEYS
    per_tile = tq // tk
    head0 = lax.broadcasted_iota(jnp.int32, (tk, LANES), 1) < SB_HEAD_DIM
    key_in_block = lax.broadcasted_iota(jnp.int32, (tq, 2 * tk), 1) & (tk - 1)
    delta = key_in_block - lax.broadcasted_iota(jnp.int32, (tq, 2 * tk), 0)

    def expand(src_ref, dst_ref, n_blocks):
        def body(j, _):
            blk = src_ref[0, pl.ds(pl.multiple_of(j * tk, tk), tk), :]
            zero = jnp.zeros_like(blk)
            dst_ref[j, 0:tk, :] = jnp.where(head0, blk, zero)
            dst_ref[j, tk:2 * tk, :] = jnp.where(head0, zero, blk)
            return 0
        lax.fori_loop(0, n_blocks, body, 0)

    expand(kc_ref, kxc_ref, n_q * per_tile)
    expand(vc_ref, vxc_ref, n_q * per_tile)
    if n_past > 0:
        expand(kp_ref, kxp_ref, n_past)
        expand(vp_ref, vxp_ref, n_past)

    def add_blocks(kx_ref, vx_ref, rows, j_last, masks):
        q2 = q_ref[0, rows, :]
        carry = carry_ref[...]
        for u, mask in enumerate(masks):
            j = j_last - u
            z = lax.dot_general(q2, kx_ref[j], (((1,), (1,)), ((), ())), preferred_element_type=F32)
            sp = jnp.maximum(z, 0.0) + jnp.log(1.0 + jnp.exp(-jnp.abs(z)))
            spm = sp if mask is None else jnp.where(mask, sp, 0.0)
            hi = spm.astype(BF16)
            later = jnp.dot(hi, bdu_ref[...], preferred_element_type=F32)
            total = jnp.dot(hi, bdo_ref[...], preferred_element_type=F32)
            a = jnp.exp((z - sp) - (later + carry))
            if mask is not None:
                a = jnp.where(mask, a, 0.0)
            acc_ref[...] += jnp.dot(a.astype(BF16), vx_ref[j], preferred_element_type=F32)
            carry = carry + total
        carry_ref[...] = carry

    def groups(n, size):
        return [min(size, n - s) for s in range(0, n, size)]

    def q_tile(i, _):
        rows = pl.ds(pl.multiple_of(i * tq, tq), tq)
        carry_ref[...] = jnp.zeros_like(carry_ref)
        acc_ref[...] = jnp.zeros_like(acc_ref)

        jj = per_tile - 1
        for g in groups(per_tile, SB_UNROLL):
            masks = [delta < -(jj - u) * tk for u in range(g)]
            add_blocks(kxc_ref, vxc_ref, rows, i * per_tile + jj, masks)
            jj -= g

        unroll = SB_UNROLL if per_tile % SB_UNROLL == 0 else 1

        def cur_body(m, _):
            add_blocks(kxc_ref, vxc_ref, rows, i * per_tile - 1 - m * unroll, [None] * unroll)
            return 0

        lax.fori_loop(0, (i * per_tile) // unroll, cur_body, 0)

        if n_past > 0:
            n_full = n_past
            if past_valid < tk:
                add_blocks(kxp_ref, vxp_ref, rows, n_past - 1, [key_in_block < past_valid])
                n_full = n_past - 1
            n_loop = n_full // SB_UNROLL

            def past_body(m, _):
                add_blocks(kxp_ref, vxp_ref, rows, n_full - 1 - m * SB_UNROLL, [None] * SB_UNROLL)
                return 0

            if n_loop > 0:
                lax.fori_loop(0, n_loop, past_body, 0)
            if n_full % SB_UNROLL:
                add_blocks(kxp_ref, vxp_ref, rows, n_full % SB_UNROLL - 1, [None] * (n_full % SB_UNROLL))

        o = acc_ref[...]
        hi, lo = _split_bf16(o * o)
        jm = j_ref[...]
        ms = (jnp.dot(hi, jm, preferred_element_type=F32)
              + jnp.dot(lo, jm, preferred_element_type=F32)) * (1.0 / SB_HEAD_DIM)
        y = o * lax.rsqrt(ms + EPS) * g_ref[...]
        o_ref[0, rows, :] = y.astype(BF16)
        return 0

    lax.fori_loop(0, n_q, q_tile, 0)


def _attention(q, k_cur, v_cur, k_past, v_past, past_len, g_sb_out):
    b, lq, _ = q.shape
    tk = SB_KEYS
    tq = min(SB_QUERY_TILE, lq)
    assert lq % tq == 0 and tq % tk == 0
    if k_past is None:
        k_past = jnp.zeros((1, tk, SB_WIDTH), BF16)
        v_past = jnp.zeros((1, tk, SB_WIDTH), BF16)
        n_past, past_valid = 0, tk
    else:
        assert k_past.shape[1] % tk == 0
        n_past = -(-past_len // tk)
        past_valid = past_len - (n_past - 1) * tk
        k_past = k_past[:, :n_past * tk]
        v_past = v_past[:, :n_past * tk]
    bp, p, _ = k_past.shape
    jj = jnp.arange(2 * tk)
    same_head = (jj[:, None] // tk) == (jj[None, :] // tk)
    bdu = (same_head & (jj[:, None] > jj[None, :])).astype(BF16)
    bdo = same_head.astype(BF16)
    ll = jnp.arange(LANES) // SB_HEAD_DIM
    jm = (ll[:, None] == ll[None, :]).astype(BF16)
    cur_spec = pl.BlockSpec((1, lq, PAIR), lambda bi, hp: (bi, 0, hp))
    past_spec = pl.BlockSpec((1, p, PAIR), (lambda bi, hp: (bi, 0, hp)) if bp > 1
                             else (lambda bi, hp: (0, 0, hp)))
    kernel = functools.partial(_attention_kernel, tq=tq, n_q=lq // tq, n_past=n_past,
                               past_valid=past_valid)
    stacked = lambda n: pltpu.VMEM((n, 2 * tk, PAIR), BF16)
    return pl.pallas_call(
        kernel,
        grid=(b, SB_HEADS // 2),
        in_specs=[cur_spec, cur_spec, cur_spec, past_spec, past_spec,
                  _const_spec(bdu.shape), _const_spec(bdo.shape), _const_spec(jm.shape),
                  pl.BlockSpec((1, PAIR), lambda bi, hp: (0, hp))],
        out_specs=cur_spec,
        out_shape=jax.ShapeDtypeStruct((b, lq, SB_WIDTH), BF16),
        scratch_shapes=[stacked(lq // tk), stacked(lq // tk), stacked(p // tk), stacked(p // tk),
                        pltpu.VMEM((tq, 2 * tk), F32), pltpu.VMEM((tq, PAIR), F32)],
        compiler_params=pltpu.CompilerParams(
            dimension_semantics=("arbitrary", "arbitrary"), vmem_limit_bytes=VMEM_LIMIT),
        name="attention",
    )(q, k_cur, v_cur, k_past, v_past, bdu, bdo, jm, g_sb_out.reshape(1, SB_WIDTH))


def _retention_kernel(q_ref, k_ref, v_ref, gate_ref, s0_ref, g_ref, o_ref, s_ref, *, chunk, n_chunks):
    c = chunk
    head = pl.program_id(1).astype(F32)
    log_g = jnp.log(1.0 - jnp.exp2(jnp.full((1, LANES), -5.0, F32) - head))
    n_row = lax.broadcasted_iota(jnp.int32, (c, LANES), 0).astype(F32)
    q_decay = jnp.exp((n_row + 1.0) * log_g)
    k_decay = jnp.exp((c - 1.0 - n_row) * log_g)
    s_decay = jnp.exp(float(c) * log_g)
    diff = (lax.broadcasted_iota(jnp.int32, (c, c), 0)
            - lax.broadcasted_iota(jnp.int32, (c, c), 1))
    log_g_cc = log_g if c == LANES else jnp.log(1.0 - jnp.exp2(jnp.full((1, c), -5.0, F32) - head))
    decay = jnp.where(diff >= 0, jnp.exp(jnp.maximum(diff, 0).astype(F32) * log_g_cc), 0.0)
    gain = g_ref[...]

    def body(ci, s):
        rows = pl.ds(pl.multiple_of(ci * c, c), c)
        q = q_ref[0, rows, :]
        k = k_ref[0, rows, :]
        v = v_ref[0, rows, :]
        qk = lax.dot_general(q, k, (((1,), (1,)), ((), ())), preferred_element_type=F32)
        inner = jnp.dot((qk * decay).astype(BF16), v, preferred_element_type=F32)
        cross = jnp.dot(q, s.astype(BF16), preferred_element_type=F32) * q_decay
        o = inner + cross
        k_dec = (k.astype(F32) * k_decay).astype(BF16)
        s_new = s_decay * s + lax.dot_general(k_dec, v, (((0,), (0,)), ((), ())),
                                              preferred_element_type=F32)
        y = o * lax.rsqrt(jnp.mean(o * o, axis=-1, keepdims=True) + EPS) * gain
        gate = gate_ref[0, rows, :].astype(F32)
        o_ref[0, rows, :] = (y * (gate * jax.nn.sigmoid(gate))).astype(BF16)
        return s_new

    s_ref[0, 0] = lax.fori_loop(0, n_chunks, body, s0_ref[0, 0])


def _retention(rq, rk, rv, rgate, s0, g_ret_out, chunk):
    b, l, _ = rq.shape
    assert l % chunk == 0
    bs = s0.shape[0]
    seq_spec = pl.BlockSpec((1, l, RET_DK), lambda bi, hh: (bi, 0, hh))
    state_spec = pl.BlockSpec((1, 1, RET_DK, RET_DV), lambda bi, hh: (bi, hh, 0, 0))
    s0_spec = state_spec if bs > 1 else pl.BlockSpec((1, 1, RET_DK, RET_DV), lambda bi, hh: (0, hh, 0, 0))
    kernel = functools.partial(_retention_kernel, chunk=chunk, n_chunks=l // chunk)
    return pl.pallas_call(
        kernel,
        grid=(b, RET_HEADS),
        in_specs=[seq_spec, seq_spec, seq_spec, seq_spec, s0_spec,
                  pl.BlockSpec((1, RET_DV), lambda bi, hh: (0, hh))],
        out_specs=[seq_spec, state_spec],
        out_shape=[jax.ShapeDtypeStruct((b, l, RET_WIDTH), BF16),
                   jax.ShapeDtypeStruct((b, RET_HEADS, RET_DK, RET_DV), F32)],
        compiler_params=pltpu.CompilerParams(
            dimension_semantics=("arbitrary", "arbitrary"), vmem_limit_bytes=VMEM_LIMIT),
        name="retention",
    )(rq, rk, rv, rgate, s0, g_ret_out.reshape(1, RET_WIDTH))


def _merge_ffn_kernel(x_ref, sb_ref, ret_ref, conv0_ref, wo_ref, gf_ref, wug_ref, wuv_ref,
                      cw_ref, cb_ref, wd_ref, gl_ref, y_ref, conv_ref,
                      h_ref, hn_ref, acc_ref, ubuf_ref, carry_ref, *, tm):
    i = pl.program_id(1)

    @pl.when(i == 0)
    def _():
        carry_ref[...] = conv0_ref[0]

    mixed = jnp.concatenate([sb_ref[0], ret_ref[0]], axis=1)
    h = x_ref[0] + jnp.dot(mixed, wo_ref[...], preferred_element_type=F32)
    h_ref[...] = h
    ms = jnp.mean(h * h, axis=-1, keepdims=True)
    hn_ref[...] = (h * lax.rsqrt(ms + EPS) * gf_ref[...]).astype(BF16)
    acc_ref[...] = jnp.zeros_like(acc_ref)
    lo = SUBLANES - (CONV_W - 1)

    def ff_chunk(j, _):
        hn = hn_ref[...]
        ubuf_ref[0:SUBLANES, :] = carry_ref[j]
        ubuf_ref[SUBLANES:, 0:FF_CHUNK] = jnp.dot(hn, wug_ref[j], preferred_element_type=F32)
        ubuf_ref[SUBLANES:, FF_CHUNK:] = jnp.dot(hn, wuv_ref[j], preferred_element_type=F32)
        carry_ref[j] = ubuf_ref[tm:tm + SUBLANES, :]
        cw = cw_ref[j]
        c = cb_ref[j]
        for tap in range(CONV_W):
            c = c + cw[tap:tap + 1, :] * ubuf_ref[lo + tap:lo + tap + tm, :]
        gate = c[:, :FF_CHUNK]
        act = (gate * jax.nn.sigmoid(gate)) * c[:, FF_CHUNK:]
        acc_ref[...] += jnp.dot(act.astype(BF16), wd_ref[j], preferred_element_type=F32)
        return 0

    lax.fori_loop(0, N_FF_CHUNKS, ff_chunk, 0)

    hh = h_ref[...] + acc_ref[...]
    ms2 = jnp.mean(hh * hh, axis=-1, keepdims=True)
    y_ref[0] = hh * lax.rsqrt(ms2 + EPS) * gl_ref[...]

    @pl.when(i == pl.num_programs(1) - 1)
    def _():
        conv_ref[0] = carry_ref[...]


def _merge_ffn(x, sb_n, ret_n, conv0, w_out_bf, g_norm_ffn, wug, wuv, cw, cb, wd, g_norm_final, tm):
    b, l, d = x.shape
    bs = conv0.shape[0]
    row_spec = lambda w: pl.BlockSpec((1, tm, w), lambda bi, i: (bi, i, 0))
    conv_shape = (1, N_FF_CHUNKS, SUBLANES, 2 * FF_CHUNK)
    conv_spec = pl.BlockSpec(conv_shape, lambda bi, i: (bi, 0, 0, 0))
    conv0_spec = conv_spec if bs > 1 else pl.BlockSpec(conv_shape, lambda bi, i: (0, 0, 0, 0))
    kernel = functools.partial(_merge_ffn_kernel, tm=tm)
    return pl.pallas_call(
        kernel,
        grid=(b, l // tm),
        in_specs=[row_spec(d), row_spec(SB_WIDTH), row_spec(RET_WIDTH), conv0_spec,
                  _const_spec(w_out_bf.shape), _const_spec((1, d)),
                  _const_spec(wug.shape), _const_spec(wuv.shape),
                  _const_spec(cw.shape), _const_spec(cb.shape), _const_spec(wd.shape),
                  _const_spec((1, d))],
        out_specs=[row_spec(d), conv_spec],
        out_shape=[jax.ShapeDtypeStruct((b, l, d), F32),
                   jax.ShapeDtypeStruct((b,) + conv_shape[1:], F32)],
        scratch_shapes=[pltpu.VMEM((tm, d), F32), pltpu.VMEM((tm, d), BF16), pltpu.VMEM((tm, d), F32),
                        pltpu.VMEM((tm + SUBLANES, 2 * FF_CHUNK), F32),
                        pltpu.VMEM(conv_shape[1:], F32)],
        compiler_params=pltpu.CompilerParams(
            dimension_semantics=("arbitrary", "arbitrary"), vmem_limit_bytes=VMEM_LIMIT),
        name="merge_ffn",
    )(x, sb_n, ret_n, conv0, w_out_bf, g_norm_ffn.reshape(1, d), wug, wuv, cw, cb, wd,
      g_norm_final.reshape(1, d))


def _rope_tables(pos):
    half = RET_DK // 2
    inv = ROPE_BASE ** (-jnp.arange(half, dtype=F32) / half)
    ang = pos.astype(F32)[:, None] * inv[None, :]
    cos, sin = jnp.cos(ang), jnp.sin(ang)
    return jnp.concatenate([cos, cos], axis=1), jnp.concatenate([-sin, sin], axis=1)


def _conv_state_to_chunks(state):
    b = state.shape[0]
    s = state.reshape(b, CONV_W - 1, 2, N_FF_CHUNKS, FF_CHUNK).transpose(0, 3, 1, 2, 4)
    s = s.reshape(b, N_FF_CHUNKS, CONV_W - 1, 2 * FF_CHUNK)
    return jnp.pad(s, ((0, 0), (0, 0), (SUBLANES - (CONV_W - 1), 0), (0, 0)))


def _conv_state_from_chunks(chunks):
    b = chunks.shape[0]
    s = chunks[:, :, SUBLANES - (CONV_W - 1):, :].reshape(b, N_FF_CHUNKS, CONV_W - 1, 2, FF_CHUNK)
    return s.transpose(0, 2, 3, 1, 4).reshape(b, CONV_W - 1, 2 * D_FF)


def _pad_rows(a, rows):
    return a if a.shape[1] == rows else jnp.pad(a, ((0, 0), (0, rows - a.shape[1]), (0, 0)))


def _stream_step(x, pos0, k_past, v_past, past_len, s0, conv0, wts, tm, chunk):
    l = x.shape[1]
    cos2, sin2 = _rope_tables(pos0 + jnp.arange(l))
    q, k, v, k_out, v_out, rq, rk, rv, rgate = _project(x, wts["g_norm_mix"], wts["w_in"], cos2, sin2, tm)
    lq = -(-l // SB_KEYS) * SB_KEYS
    sb_n = _attention(_pad_rows(q, lq), _pad_rows(k, lq), _pad_rows(v, lq),
                      k_past, v_past, past_len, wts["g_sb_out"])[:, :l]
    ret_n, s_new = _retention(rq, rk, rv, rgate, s0, wts["g_ret_out"], chunk)
    y, conv_new = _merge_ffn(x, sb_n, ret_n, conv0, wts["w_out"], wts["g_norm_ffn"], wts["wug"],
                             wts["wuv"], wts["cw"], wts["cb"], wts["wd"], wts["g_norm_final"], tm)
    return y, k, v, k_out, v_out, s_new, conv_new


def kernel(x_prompt, x_sample, cache_sb_k, cache_sb_v, state_ret, state_conv, meta_tokens, g_norm_mix, w_in, g_sb_out, g_ret_out, w_out, g_norm_ffn, w_up, conv_w, conv_b, w_down, g_norm_final):
    b, seq, d = x_prompt.shape
    bd, ls, _ = x_sample.shape
    past = cache_sb_k.shape[2]

    def ff_cols(a):
        r = a.shape[0]
        return a.reshape(r, 2, N_FF_CHUNKS, FF_CHUNK).transpose(2, 0, 1, 3).reshape(N_FF_CHUNKS, r, 2 * FF_CHUNK)

    w_up_bf = w_up.astype(BF16)
    wts = dict(
        g_norm_mix=g_norm_mix, g_sb_out=g_sb_out, g_ret_out=g_ret_out, g_norm_ffn=g_norm_ffn,
        g_norm_final=g_norm_final,
        w_in=w_in.astype(BF16), w_out=w_out.astype(BF16),
        wug=w_up_bf[:, :D_FF].reshape(d, N_FF_CHUNKS, FF_CHUNK).transpose(1, 0, 2),
        wuv=w_up_bf[:, D_FF:].reshape(d, N_FF_CHUNKS, FF_CHUNK).transpose(1, 0, 2),
        cw=ff_cols(conv_w), cb=ff_cols(conv_b.reshape(1, 2 * D_FF)),
        wd=w_down.astype(BF16).reshape(N_FF_CHUNKS, FF_CHUNK, d),
    )

    zero_state = jnp.zeros((1, RET_HEADS, RET_DK, RET_DV), F32)
    zero_conv = jnp.zeros((1, N_FF_CHUNKS, SUBLANES, 2 * FF_CHUNK), F32)
    _, k_m, v_m, k_m_out, v_m_out, s_meta, conv_meta = _stream_step(
        meta_tokens[None], -N_META, None, None, 0, zero_state, zero_conv, wts, N_META, N_META)

    y_prompt, _, _, k_p_out, v_p_out, s_prompt, conv_prompt = _stream_step(
        x_prompt, 0, _pad_rows(k_m, SB_KEYS), _pad_rows(v_m, SB_KEYS), N_META, s_meta, conv_meta,
        wts, 512, 256)
    rep = lambda a: jnp.broadcast_to(a, (b,) + a.shape[1:])
    new_k_prompt = jnp.concatenate([rep(k_m_out), k_p_out], axis=2)
    new_v_prompt = jnp.concatenate([rep(v_m_out), v_p_out], axis=2)

    to_rows = lambda c: c.transpose(0, 2, 1, 3).reshape(bd, past, SB_WIDTH).astype(BF16)
    y_sample, _, _, k_s_out, v_s_out, s_sample, conv_sample = _stream_step(
        x_sample, past, to_rows(cache_sb_k), to_rows(cache_sb_v), past, state_ret,
        _conv_state_to_chunks(state_conv), wts, ls, ls)

    return (y_prompt, y_sample, new_k_prompt, new_v_prompt, s_prompt,
            _conv_state_from_chunks(conv_prompt), k_s_out, v_s_out, s_sample,
            _conv_state_from_chunks(conv_sample))
```

```python
import functools

import jax
import jax.numpy as jnp
from jax import lax
from jax.experimental import pallas as pl
from jax.experimental.pallas import tpu as pltpu

D_MODEL = 1024
N_META = 16
SB_HEADS = 8
SB_HEAD_DIM = 64
SB_WIDTH = SB_HEADS * SB_HEAD_DIM
RET_HEADS = 4
RET_DK = 128
RET_DV = 128
RET_WIDTH = RET_HEADS * RET_DV
MIX_WIDTH = SB_WIDTH + RET_WIDTH
GROUP = 512
N_GROUPS = 7
IN_WIDTH = N_GROUPS * GROUP
D_FF = 2816
CONV_W = 3
ROPE_BASE = 10000.0
EPS = 1e-5

LANES = 128
SUBLANES = 8
FF_CHUNK = 256
N_FF_CHUNKS = D_FF // FF_CHUNK
SB_KEYS = 128
SB_QUERY_TILE = 512
SB_UNROLL = 2
PAIR = 2 * SB_HEAD_DIM
SB_EXHAUSTED = 105.0
VMEM_LIMIT = 56 * 1024 * 1024

BF16 = jnp.bfloat16
F32 = jnp.float32


def _const_spec(shape):
    zeros = (0,) * len(shape)
    return pl.BlockSpec(shape, lambda *_: zeros)


def _project_kernel(x_ref, g_ref, w_ref, cos_ref, sin_ref,
                    q_ref, k_ref, v_ref, ko_ref, vo_ref, rq_ref, rk_ref, rv_ref, rg_ref):
    x = x_ref[0]
    ms = jnp.mean(x * x, axis=-1, keepdims=True)
    h = (x * lax.rsqrt(ms + EPS) * g_ref[...]).astype(BF16)

    def group(i):
        return jnp.dot(h, w_ref[:, i * GROUP:(i + 1) * GROUP], preferred_element_type=F32)

    def split_heads(p, out_ref):
        for hh in range(SB_HEADS):
            out_ref[0, hh] = p[:, hh * SB_HEAD_DIM:(hh + 1) * SB_HEAD_DIM]

    def rope(p, out_ref, scale):
        cos = cos_ref[...]
        sin = sin_ref[...]
        for hh in range(RET_HEADS):
            t = p[:, hh * RET_DK:(hh + 1) * RET_DK]
            r = t * cos + pltpu.roll(t, RET_DK // 2, 1) * sin
            if scale is not None:
                r = r * scale
            out_ref[0, :, hh * RET_DK:(hh + 1) * RET_DK] = r.astype(BF16)

    q_ref[0] = (group(0) * (SB_HEAD_DIM ** -0.5)).astype(BF16)
    pk = group(1)
    k_ref[0] = pk.astype(BF16)
    split_heads(pk, ko_ref)
    pv = group(2)
    v_ref[0] = pv.astype(BF16)
    split_heads(pv, vo_ref)
    rope(group(3), rq_ref, None)
    rope(group(4), rk_ref, RET_DK ** -0.5)
    rv_ref[0] = group(5).astype(BF16)
    rg_ref[0] = group(6).astype(BF16)


def _project(x, g_norm, w_in_bf, cos2, sin2, tm):
    b, l, d = x.shape
    grid = (b, l // tm)
    row_spec = lambda w: pl.BlockSpec((1, tm, w), lambda bi, i: (bi, i, 0))
    head_spec = pl.BlockSpec((1, SB_HEADS, tm, SB_HEAD_DIM), lambda bi, i: (bi, 0, i, 0))
    tab_spec = pl.BlockSpec((tm, RET_DK), lambda bi, i: (i, 0))
    act = jax.ShapeDtypeStruct((b, l, GROUP), BF16)
    heads = jax.ShapeDtypeStruct((b, SB_HEADS, l, SB_HEAD_DIM), F32)
    return pl.pallas_call(
        _project_kernel,
        grid=grid,
        in_specs=[row_spec(d), _const_spec((1, d)), _const_spec((d, IN_WIDTH)), tab_spec, tab_spec],
        out_specs=[row_spec(GROUP)] * 3 + [head_spec] * 2 + [row_spec(GROUP)] * 4,
        out_shape=[act] * 3 + [heads] * 2 + [act] * 4,
        compiler_params=pltpu.CompilerParams(
            dimension_semantics=("arbitrary", "arbitrary"), vmem_limit_bytes=VMEM_LIMIT),
        name="project",
    )(x, g_norm.reshape(1, d), w_in_bf, cos2, sin2)


def _split_bf16(x):
    hi = x.astype(BF16)
    lo = (x - hi.astype(F32)).astype(BF16)
    return hi, lo


def _attention_kernel(q_ref, kc_ref, vc_ref, kp_ref, vp_ref, bdu_ref, bdo_ref, j_ref, g_ref, o_ref,
                      kxc_ref, vxc_ref, kxp_ref, vxp_ref, carry_ref, acc_ref,
                      *, tq, n_q, n_past, past_valid):
    tk = SB_KEYS
    per_tile = tq // tk
    head0 = lax.broadcasted_iota(jnp.int32, (tk, LANES), 1) < SB_HEAD_DIM
    key_in_block = lax.broadcasted_iota(jnp.int32, (tq, 2 * tk), 1) & (tk - 1)
    delta = key_in_block - lax.broadcasted_iota(jnp.int32, (tq, 2 * tk), 0)

    def expand(src_ref, dst_ref, n_blocks):
        def body(j, _):
            blk = src_ref[0, pl.ds(pl.multiple_of(j * tk, tk), tk), :]
            zero = jnp.zeros_like(blk)
            dst_ref[j, 0:tk, :] = jnp.where(head0, blk, zero)
            dst_ref[j, tk:2 * tk, :] = jnp.where(head0, zero, blk)
            return 0
        lax.fori_loop(0, n_blocks, body, 0)

    expand(kc_ref, kxc_ref, n_q * per_tile)
    expand(vc_ref, vxc_ref, n_q * per_tile)
    if n_past > 0:
        expand(kp_ref, kxp_ref, n_past)
        expand(vp_ref, vxp_ref, n_past)

    def add_blocks(kx_ref, vx_ref, rows, j_last, masks):
        q2 = q_ref[0, rows, :]
        carry = carry_ref[...]
        for u, mask in enumerate(masks):
            j = j_last - u
            z = lax.dot_general(q2, kx_ref[j], (((1,), (1,)), ((), ())), preferred_element_type=F32)
            sp = jnp.maximum(z, 0.0) + jnp.log(1.0 + jnp.exp(-jnp.abs(z)))
            spm = sp if mask is None else jnp.where(mask, sp, 0.0)
            hi = spm.astype(BF16)
            later = jnp.dot(hi, bdu_ref[...], preferred_element_type=F32)
            total = jnp.dot(hi, bdo_ref[...], preferred_element_type=F32)
            a = jnp.exp((z - sp) - (later + carry))
            if mask is not None:
                a = jnp.where(mask, a, 0.0)
            acc_ref[...] += jnp.dot(a.astype(BF16), vx_ref[j], preferred_element_type=F32)
            carry = carry + total
        carry_ref[...] = carry
        return carry

    def stick_left(carry):
        return (jnp.min(carry) < SB_EXHAUSTED).astype(jnp.int32)

    def sweep(kx_ref, vx_ref, rows, j_top, n_steps, unroll, alive):
        def cond(state):
            m, live = state
            return jnp.logical_and(m < n_steps, live > 0)

        def body(state):
            m, _ = state
            carry = add_blocks(kx_ref, vx_ref, rows, j_top - m * unroll, [None] * unroll)
            return m + 1, stick_left(carry)

        return lax.while_loop(cond, body, (jnp.int32(0), alive))[1]

    def groups(n, size):
        return [min(size, n - s) for s in range(0, n, size)]

    def q_tile(i, _):
        rows = pl.ds(pl.multiple_of(i * tq, tq), tq)
        carry_ref[...] = jnp.zeros_like(carry_ref)
        acc_ref[...] = jnp.zeros_like(acc_ref)

        jj = per_tile - 1
        for g in groups(per_tile, SB_UNROLL):
            masks = [delta < -(jj - u) * tk for u in range(g)]
            carry = add_blocks(kxc_ref, vxc_ref, rows, i * per_tile + jj, masks)
            jj -= g

        unroll = SB_UNROLL if per_tile % SB_UNROLL == 0 else 1
        alive = sweep(kxc_ref, vxc_ref, rows, i * per_tile - 1, (i * per_tile) // unroll, unroll,
                      stick_left(carry))

        if n_past > 0:
            n_full = n_past
            if past_valid < tk:
                @pl.when(alive > 0)
                def _():
                    add_blocks(kxp_ref, vxp_ref, rows, n_past - 1, [key_in_block < past_valid])

                n_full = n_past - 1
                if n_full > 0:
                    alive = stick_left(carry_ref[...])
            if n_full // SB_UNROLL > 0:
                alive = sweep(kxp_ref, vxp_ref, rows, n_full - 1, n_full // SB_UNROLL, SB_UNROLL, alive)
            if n_full % SB_UNROLL:
                alive = sweep(kxp_ref, vxp_ref, rows, n_full % SB_UNROLL - 1, n_full % SB_UNROLL, 1, alive)

        o = acc_ref[...]
        hi, lo = _split_bf16(o * o)
        jm = j_ref[...]
        ms = (jnp.dot(hi, jm, preferred_element_type=F32)
              + jnp.dot(lo, jm, preferred_element_type=F32)) * (1.0 / SB_HEAD_DIM)
        y = o * lax.rsqrt(ms + EPS) * g_ref[...]
        o_ref[0, rows, :] = y.astype(BF16)
        return 0

    lax.fori_loop(0, n_q, q_tile, 0)


def _attention(q, k_cur, v_cur, k_past, v_past, past_len, g_sb_out):
    b, lq, _ = q.shape
    tk = SB_KEYS
    tq = min(SB_QUERY_TILE, lq)
    assert lq % tq == 0 and tq % tk == 0
    if k_past is None:
        k_past = jnp.zeros((1, tk, SB_WIDTH), BF16)
        v_past = jnp.zeros((1, tk, SB_WIDTH), BF16)
        n_past, past_valid = 0, tk
    else:
        assert k_past.shape[1] % tk == 0
        n_past = -(-past_len // tk)
        past_valid = past_len - (n_past - 1) * tk
        k_past = k_past[:, :n_past * tk]
        v_past = v_past[:, :n_past * tk]
    bp, p, _ = k_past.shape
    jj = jnp.arange(2 * tk)
    same_head = (jj[:, None] // tk) == (jj[None, :] // tk)
    bdu = (same_head & (jj[:, None] > jj[None, :])).astype(BF16)
    bdo = same_head.astype(BF16)
    ll = jnp.arange(LANES) // SB_HEAD_DIM
    jm = (ll[:, None] == ll[None, :]).astype(BF16)
    cur_spec = pl.BlockSpec((1, lq, PAIR), lambda bi, hp: (bi, 0, hp))
    past_spec = pl.BlockSpec((1, p, PAIR), (lambda bi, hp: (bi, 0, hp)) if bp > 1
                             else (lambda bi, hp: (0, 0, hp)))
    kernel = functools.partial(_attention_kernel, tq=tq, n_q=lq // tq, n_past=n_past,
                               past_valid=past_valid)
    stacked = lambda n: pltpu.VMEM((n, 2 * tk, PAIR), BF16)
    return pl.pallas_call(
        kernel,
        grid=(b, SB_HEADS // 2),
        in_specs=[cur_spec, cur_spec, cur_spec, past_spec, past_spec,
                  _const_spec(bdu.shape), _const_spec(bdo.shape), _const_spec(jm.shape),
                  pl.BlockSpec((1, PAIR), lambda bi, hp: (0, hp))],
        out_specs=cur_spec,
        out_shape=jax.ShapeDtypeStruct((b, lq, SB_WIDTH), BF16),
        scratch_shapes=[stacked(lq // tk), stacked(lq // tk), stacked(p // tk), stacked(p // tk),
                        pltpu.VMEM((tq, 2 * tk), F32), pltpu.VMEM((tq, PAIR), F32)],
        compiler_params=pltpu.CompilerParams(
            dimension_semantics=("arbitrary", "arbitrary"), vmem_limit_bytes=VMEM_LIMIT),
        name="attention",
    )(q, k_cur, v_cur, k_past, v_past, bdu, bdo, jm, g_sb_out.reshape(1, SB_WIDTH))


def _retention_kernel(q_ref, k_ref, v_ref, gate_ref, s0_ref, g_ref, o_ref, s_ref, *, chunk, n_chunks):
    c = chunk
    head = pl.program_id(1).astype(F32)
    log_g = jnp.log(1.0 - jnp.exp2(jnp.full((1, LANES), -5.0, F32) - head))
    n_row = lax.broadcasted_iota(jnp.int32, (c, LANES), 0).astype(F32)
    q_decay = jnp.exp((n_row + 1.0) * log_g)
    k_decay = jnp.exp((c - 1.0 - n_row) * log_g)
    s_decay = jnp.exp(float(c) * log_g)
    diff = (lax.broadcasted_iota(jnp.int32, (c, c), 0)
            - lax.broadcasted_iota(jnp.int32, (c, c), 1))
    log_g_cc = log_g if c == LANES else jnp.log(1.0 - jnp.exp2(jnp.full((1, c), -5.0, F32) - head))
    decay = jnp.where(diff >= 0, jnp.exp(jnp.maximum(diff, 0).astype(F32) * log_g_cc), 0.0)
    gain = g_ref[...]

    def body(ci, s):
        rows = pl.ds(pl.multiple_of(ci * c, c), c)
        q = q_ref[0, rows, :]
        k = k_ref[0, rows, :]
        v = v_ref[0, rows, :]
        qk = lax.dot_general(q, k, (((1,), (1,)), ((), ())), preferred_element_type=F32)
        inner = jnp.dot((qk * decay).astype(BF16), v, preferred_element_type=F32)
        cross = jnp.dot(q, s.astype(BF16), preferred_element_type=F32) * q_decay
        o = inner + cross
        k_dec = (k.astype(F32) * k_decay).astype(BF16)
        s_new = s_decay * s + lax.dot_general(k_dec, v, (((0,), (0,)), ((), ())),
                                              preferred_element_type=F32)
        y = o * lax.rsqrt(jnp.mean(o * o, axis=-1, keepdims=True) + EPS) * gain
        gate = gate_ref[0, rows, :].astype(F32)
        o_ref[0, rows, :] = (y * (gate * jax.nn.sigmoid(gate))).astype(BF16)
        return s_new

    s_ref[0, 0] = lax.fori_loop(0, n_chunks, body, s0_ref[0, 0])


def _retention(rq, rk, rv, rgate, s0, g_ret_out, chunk):
    b, l, _ = rq.shape
    assert l % chunk == 0
    bs = s0.shape[0]
    seq_spec = pl.BlockSpec((1, l, RET_DK), lambda bi, hh: (bi, 0, hh))
    state_spec = pl.BlockSpec((1, 1, RET_DK, RET_DV), lambda bi, hh: (bi, hh, 0, 0))
    s0_spec = state_spec if bs > 1 else pl.BlockSpec((1, 1, RET_DK, RET_DV), lambda bi, hh: (0, hh, 0, 0))
    kernel = functools.partial(_retention_kernel, chunk=chunk, n_chunks=l // chunk)
    return pl.pallas_call(
        kernel,
        grid=(b, RET_HEADS),
        in_specs=[seq_spec, seq_spec, seq_spec, seq_spec, s0_spec,
                  pl.BlockSpec((1, RET_DV), lambda bi, hh: (0, hh))],
        out_specs=[seq_spec, state_spec],
        out_shape=[jax.ShapeDtypeStruct((b, l, RET_WIDTH), BF16),
                   jax.ShapeDtypeStruct((b, RET_HEADS, RET_DK, RET_DV), F32)],
        compiler_params=pltpu.CompilerParams(
            dimension_semantics=("arbitrary", "arbitrary"), vmem_limit_bytes=VMEM_LIMIT),
        name="retention",
    )(rq, rk, rv, rgate, s0, g_ret_out.reshape(1, RET_WIDTH))


def _merge_ffn_kernel(x_ref, sb_ref, ret_ref, conv0_ref, wo_ref, gf_ref, wug_ref, wuv_ref,
                      cw_ref, cb_ref, wd_ref, gl_ref, y_ref, conv_ref,
                      h_ref, hn_ref, acc_ref, ubuf_ref, carry_ref, *, tm):
    i = pl.program_id(1)

    @pl.when(i == 0)
    def _():
        carry_ref[...] = conv0_ref[0]

    mixed = jnp.concatenate([sb_ref[0], ret_ref[0]], axis=1)
    h = x_ref[0] + jnp.dot(mixed, wo_ref[...], preferred_element_type=F32)
    h_ref[...] = h
    ms = jnp.mean(h * h, axis=-1, keepdims=True)
    hn_ref[...] = (h * lax.rsqrt(ms + EPS) * gf_ref[...]).astype(BF16)
    acc_ref[...] = jnp.zeros_like(acc_ref)
    lo = SUBLANES - (CONV_W - 1)

    def ff_chunk(j, _):
        hn = hn_ref[...]
        ubuf_ref[0:SUBLANES, :] = carry_ref[j]
        ubuf_ref[SUBLANES:, 0:FF_CHUNK] = jnp.dot(hn, wug_ref[j], preferred_element_type=F32)
        ubuf_ref[SUBLANES:, FF_CHUNK:] = jnp.dot(hn, wuv_ref[j], preferred_element_type=F32)
        carry_ref[j] = ubuf_ref[tm:tm + SUBLANES, :]
        cw = cw_ref[j]
        c = cb_ref[j]
        for tap in range(CONV_W):
            c = c + cw[tap:tap + 1, :] * ubuf_ref[lo + tap:lo + tap + tm, :]
        gate = c[:, :FF_CHUNK]
        act = (gate * jax.nn.sigmoid(gate)) * c[:, FF_CHUNK:]
        acc_ref[...] += jnp.dot(act.astype(BF16), wd_ref[j], preferred_element_type=F32)
        return 0

    lax.fori_loop(0, N_FF_CHUNKS, ff_chunk, 0)

    hh = h_ref[...] + acc_ref[...]
    ms2 = jnp.mean(hh * hh, axis=-1, keepdims=True)
    y_ref[0] = hh * lax.rsqrt(ms2 + EPS) * gl_ref[...]

    @pl.when(i == pl.num_programs(1) - 1)
    def _():
        conv_ref[0] = carry_ref[...]


def _merge_ffn(x, sb_n, ret_n, conv0, w_out_bf, g_norm_ffn, wug, wuv, cw, cb, wd, g_norm_final, tm):
    b, l, d = x.shape
    bs = conv0.shape[0]
    row_spec = lambda w: pl.BlockSpec((1, tm, w), lambda bi, i: (bi, i, 0))
    conv_shape = (1, N_FF_CHUNKS, SUBLANES, 2 * FF_CHUNK)
    conv_spec = pl.BlockSpec(conv_shape, lambda bi, i: (bi, 0, 0, 0))
    conv0_spec = conv_spec if bs > 1 else pl.BlockSpec(conv_shape, lambda bi, i: (0, 0, 0, 0))
    kernel = functools.partial(_merge_ffn_kernel, tm=tm)
    return pl.pallas_call(
        kernel,
        grid=(b, l // tm),
        in_specs=[row_spec(d), row_spec(SB_WIDTH), row_spec(RET_WIDTH), conv0_spec,
                  _const_spec(w_out_bf.shape), _const_spec((1, d)),
                  _const_spec(wug.shape), _const_spec(wuv.shape),
                  _const_spec(cw.shape), _const_spec(cb.shape), _const_spec(wd.shape),
                  _const_spec((1, d))],
        out_specs=[row_spec(d), conv_spec],
        out_shape=[jax.ShapeDtypeStruct((b, l, d), F32),
                   jax.ShapeDtypeStruct((b,) + conv_shape[1:], F32)],
        scratch_shapes=[pltpu.VMEM((tm, d), F32), pltpu.VMEM((tm, d), BF16), pltpu.VMEM((tm, d), F32),
                        pltpu.VMEM((tm + SUBLANES, 2 * FF_CHUNK), F32),
                        pltpu.VMEM(conv_shape[1:], F32)],
        compiler_params=pltpu.CompilerParams(
            dimension_semantics=("arbitrary", "arbitrary"), vmem_limit_bytes=VMEM_LIMIT),
        name="merge_ffn",
    )(x, sb_n, ret_n, conv0, w_out_bf, g_norm_ffn.reshape(1, d), wug, wuv, cw, cb, wd,
      g_norm_final.reshape(1, d))


def _rope_tables(pos):
    half = RET_DK // 2
    inv = ROPE_BASE ** (-jnp.arange(half, dtype=F32) / half)
    ang = pos.astype(F32)[:, None] * inv[None, :]
    cos, sin = jnp.cos(ang), jnp.sin(ang)
    return jnp.concatenate([cos, cos], axis=1), jnp.concatenate([-sin, sin], axis=1)


def _conv_state_to_chunks(state):
    b = state.shape[0]
    s = state.reshape(b, CONV_W - 1, 2, N_FF_CHUNKS, FF_CHUNK).transpose(0, 3, 1, 2, 4)
    s = s.reshape(b, N_FF_CHUNKS, CONV_W - 1, 2 * FF_CHUNK)
    return jnp.pad(s, ((0, 0), (0, 0), (SUBLANES - (CONV_W - 1), 0), (0, 0)))


def _conv_state_from_chunks(chunks):
    b = chunks.shape[0]
    s = chunks[:, :, SUBLANES - (CONV_W - 1):, :].reshape(b, N_FF_CHUNKS, CONV_W - 1, 2, FF_CHUNK)
    return s.transpose(0, 2, 3, 1, 4).reshape(b, CONV_W - 1, 2 * D_FF)


def _pad_rows(a, rows):
    return a if a.shape[1] == rows else jnp.pad(a, ((0, 0), (0, rows - a.shape[1]), (0, 0)))


def _stream_step(x, pos0, k_past, v_past, past_len, s0, conv0, wts, tm, chunk):
    l = x.shape[1]
    cos2, sin2 = _rope_tables(pos0 + jnp.arange(l))
    q, k, v, k_out, v_out, rq, rk, rv, rgate = _project(x, wts["g_norm_mix"], wts["w_in"], cos2, sin2, tm)
    lq = -(-l // SB_KEYS) * SB_KEYS
    sb_n = _attention(_pad_rows(q, lq), _pad_rows(k, lq), _pad_rows(v, lq),
                      k_past, v_past, past_len, wts["g_sb_out"])[:, :l]
    ret_n, s_new = _retention(rq, rk, rv, rgate, s0, wts["g_ret_out"], chunk)
    y, conv_new = _merge_ffn(x, sb_n, ret_n, conv0, wts["w_out"], wts["g_norm_ffn"], wts["wug"],
                             wts["wuv"], wts["cw"], wts["cb"], wts["wd"], wts["g_norm_final"], tm)
    return y, k, v, k_out, v_out, s_new, conv_new


def kernel(x_prompt, x_sample, cache_sb_k, cache_sb_v, state_ret, state_conv, meta_tokens, g_norm_mix, w_in, g_sb_out, g_ret_out, w_out, g_norm_ffn, w_up, conv_w, conv_b, w_down, g_norm_final):
    b, seq, d = x_prompt.shape
    bd, ls, _ = x_sample.shape
    past = cache_sb_k.shape[2]

    def ff_cols(a):
        r = a.shape[0]
        return a.reshape(r, 2, N_FF_CHUNKS, FF_CHUNK).transpose(2, 0, 1, 3).reshape(N_FF_CHUNKS, r, 2 * FF_CHUNK)

    w_up_bf = w_up.astype(BF16)
    wts = dict(
        g_norm_mix=g_norm_mix, g_sb_out=g_sb_out, g_ret_out=g_ret_out, g_norm_ffn=g_norm_ffn,
        g_norm_final=g_norm_final,
        w_in=w_in.astype(BF16), w_out=w_out.astype(BF16),
        wug=w_up_bf[:, :D_FF].reshape(d, N_FF_CHUNKS, FF_CHUNK).transpose(1, 0, 2),
        wuv=w_up_bf[:, D_FF:].reshape(d, N_FF_CHUNKS, FF_CHUNK).transpose(1, 0, 2),
        cw=ff_cols(conv_w), cb=ff_cols(conv_b.reshape(1, 2 * D_FF)),
        wd=w_down.astype(BF16).reshape(N_FF_CHUNKS, FF_CHUNK, d),
    )

    zero_state = jnp.zeros((1, RET_HEADS, RET_DK, RET_DV), F32)
    zero_conv = jnp.zeros((1, N_FF_CHUNKS, SUBLANES, 2 * FF_CHUNK), F32)
    _, k_m, v_m, k_m_out, v_m_out, s_meta, conv_meta = _stream_step(
        meta_tokens[None], -N_META, None, None, 0, zero_state, zero_conv, wts, N_META, N_META)

    y_prompt, _, _, k_p_out, v_p_out, s_prompt, conv_prompt = _stream_step(
        x_prompt, 0, _pad_rows(k_m, SB_KEYS), _pad_rows(v_m, SB_KEYS), N_META, s_meta, conv_meta,
        wts, 512, 256)
    rep = lambda a: jnp.broadcast_to(a, (b,) + a.shape[1:])
    new_k_prompt = jnp.concatenate([rep(k_m_out), k_p_out], axis=2)
    new_v_prompt = jnp.concatenate([rep(v_m_out), v_p_out], axis=2)

    to_rows = lambda c: c.transpose(0, 2, 1, 3).reshape(bd, past, SB_WIDTH).astype(BF16)
    y_sample, _, _, k_s_out, v_s_out, s_sample, conv_sample = _stream_step(
        x_sample, past, to_rows(cache_sb_k), to_rows(cache_sb_v), past, state_ret,
        _conv_state_to_chunks(state_conv), wts, ls, ls)

    return (y_prompt, y_sample, new_k_prompt, new_v_prompt, s_prompt,
            _conv_state_from_chunks(conv_prompt), k_s_out, v_s_out, s_sample,
            _conv_state_from_chunks(conv_sample))
```

```python
import functools

import jax
import jax.numpy as jnp
from jax import lax
from jax.experimental import pallas as pl
from jax.experimental.pallas import tpu as pltpu

D_MODEL = 1024
N_META = 16
SB_HEADS = 8
SB_HEAD_DIM = 64
SB_WIDTH = SB_HEADS * SB_HEAD_DIM
RET_HEADS = 4
RET_DK = 128
RET_DV = 128
RET_WIDTH = RET_HEADS * RET_DV
MIX_WIDTH = SB_WIDTH + RET_WIDTH
GROUP = 512
N_GROUPS = 7
IN_WIDTH = N_GROUPS * GROUP
D_FF = 2816
CONV_W = 3
ROPE_BASE = 10000.0
EPS = 1e-5

LANES = 128
SUBLANES = 8
FF_CHUNK = 256
N_FF_CHUNKS = D_FF // FF_CHUNK
FF_STRIP = 64
SB_KEYS = 128
SB_QUERY_TILE = 512
SB_UNROLL = 2
PAIR = 2 * SB_HEAD_DIM
SB_EXHAUSTED = 105.0
VMEM_LIMIT = 56 * 1024 * 1024

BF16 = jnp.bfloat16
F32 = jnp.float32


def _const_spec(shape):
    zeros = (0,) * len(shape)
    return pl.BlockSpec(shape, lambda *_: zeros, pipeline_mode=pl.Buffered(1))


def _project_kernel(x_ref, g_ref, w_ref, cos_ref, sin_ref,
                    q_ref, k_ref, v_ref, ko_ref, vo_ref, rq_ref, rk_ref, rv_ref, rg_ref):
    x = x_ref[0]
    ms = jnp.mean(x * x, axis=-1, keepdims=True)
    h = (x * lax.rsqrt(ms + EPS) * g_ref[...]).astype(BF16)

    def group(i):
        return jnp.dot(h, w_ref[:, i * GROUP:(i + 1) * GROUP], preferred_element_type=F32)

    def split_heads(p, out_ref):
        for hh in range(SB_HEADS):
            out_ref[0, hh] = p[:, hh * SB_HEAD_DIM:(hh + 1) * SB_HEAD_DIM]

    def rope(p, out_ref, scale):
        cos = cos_ref[...]
        sin = sin_ref[...]
        for hh in range(RET_HEADS):
            t = p[:, hh * RET_DK:(hh + 1) * RET_DK]
            r = t * cos + pltpu.roll(t, RET_DK // 2, 1) * sin
            if scale is not None:
                r = r * scale
            out_ref[0, :, hh * RET_DK:(hh + 1) * RET_DK] = r.astype(BF16)

    q_ref[0] = (group(0) * (SB_HEAD_DIM ** -0.5)).astype(BF16)
    pk = group(1)
    k_ref[0] = pk.astype(BF16)
    split_heads(pk, ko_ref)
    pv = group(2)
    v_ref[0] = pv.astype(BF16)
    split_heads(pv, vo_ref)
    rope(group(3), rq_ref, None)
    rope(group(4), rk_ref, RET_DK ** -0.5)
    rv_ref[0] = group(5).astype(BF16)
    rg_ref[0] = group(6).astype(BF16)


def _project(x, g_norm, w_in_bf, cos2, sin2, tm):
    b, l, d = x.shape
    grid = (b, l // tm)
    row_spec = lambda w: pl.BlockSpec((1, tm, w), lambda bi, i: (bi, i, 0))
    head_spec = pl.BlockSpec((1, SB_HEADS, tm, SB_HEAD_DIM), lambda bi, i: (bi, 0, i, 0))
    tab_spec = pl.BlockSpec((tm, RET_DK), lambda bi, i: (i, 0))
    act = jax.ShapeDtypeStruct((b, l, GROUP), BF16)
    heads = jax.ShapeDtypeStruct((b, SB_HEADS, l, SB_HEAD_DIM), F32)
    return pl.pallas_call(
        _project_kernel,
        grid=grid,
        in_specs=[row_spec(d), _const_spec((1, d)), _const_spec((d, IN_WIDTH)), tab_spec, tab_spec],
        out_specs=[row_spec(GROUP)] * 3 + [head_spec] * 2 + [row_spec(GROUP)] * 4,
        out_shape=[act] * 3 + [heads] * 2 + [act] * 4,
        compiler_params=pltpu.CompilerParams(
            dimension_semantics=("arbitrary", "arbitrary"), vmem_limit_bytes=VMEM_LIMIT),
        name="project",
    )(x, g_norm.reshape(1, d), w_in_bf, cos2, sin2)


def _split_bf16(x):
    hi = x.astype(BF16)
    lo = (x - hi.astype(F32)).astype(BF16)
    return hi, lo


def _attention_kernel(q_ref, kc_ref, vc_ref, kp_ref, vp_ref, bdu_ref, bdo_ref, j_ref, g_ref, o_ref,
                      kxc_ref, vxc_ref, kxp_ref, vxp_ref, carry_ref, acc_ref,
                      *, tq, n_q, n_past, past_valid):
    tk = SB_KEYS
    per_tile = tq // tk
    head0 = lax.broadcasted_iota(jnp.int32, (tk, LANES), 1) < SB_HEAD_DIM
    key_in_block = lax.broadcasted_iota(jnp.int32, (tq, 2 * tk), 1) & (tk - 1)
    delta = key_in_block - lax.broadcasted_iota(jnp.int32, (tq, 2 * tk), 0)

    def expand(src_ref, dst_ref, n_blocks):
        def body(j, _):
            blk = src_ref[0, pl.ds(pl.multiple_of(j * tk, tk), tk), :]
            zero = jnp.zeros_like(blk)
            dst_ref[j, 0:tk, :] = jnp.where(head0, blk, zero)
            dst_ref[j, tk:2 * tk, :] = jnp.where(head0, zero, blk)
            return 0
        lax.fori_loop(0, n_blocks, body, 0)

    expand(kc_ref, kxc_ref, n_q * per_tile)
    expand(vc_ref, vxc_ref, n_q * per_tile)
    if n_past > 0:
        expand(kp_ref, kxp_ref, n_past)
        expand(vp_ref, vxp_ref, n_past)

    def add_blocks(kx_ref, vx_ref, rows, j_last, masks):
        q2 = q_ref[0, rows, :]
        carry = carry_ref[...]
        for u, mask in enumerate(masks):
            j = j_last - u
            z = lax.dot_general(q2, kx_ref[j], (((1,), (1,)), ((), ())), preferred_element_type=F32)
            sp = jnp.maximum(z, 0.0) + jnp.log(1.0 + jnp.exp(-jnp.abs(z)))
            spm = sp if mask is None else jnp.where(mask, sp, 0.0)
            hi = spm.astype(BF16)
            later = jnp.dot(hi, bdu_ref[...], preferred_element_type=F32)
            total = jnp.dot(hi, bdo_ref[...], preferred_element_type=F32)
            a = jnp.exp((z - sp) - (later + carry))
            if mask is not None:
                a = jnp.where(mask, a, 0.0)
            acc_ref[...] += jnp.dot(a.astype(BF16), vx_ref[j], preferred_element_type=F32)
            carry = carry + total
        carry_ref[...] = carry
        return carry

    def stick_left(carry):
        return (jnp.min(carry) < SB_EXHAUSTED).astype(jnp.int32)

    def sweep(kx_ref, vx_ref, rows, j_top, n_steps, unroll, alive):
        def cond(state):
            m, live = state
            return jnp.logical_and(m < n_steps, live > 0)

        def body(state):
            m, _ = state
            carry = add_blocks(kx_ref, vx_ref, rows, j_top - m * unroll, [None] * unroll)
            return m + 1, stick_left(carry)

        return lax.while_loop(cond, body, (jnp.int32(0), alive))[1]

    def groups(n, size):
        return [min(size, n - s) for s in range(0, n, size)]

    def q_tile(i, _):
        rows = pl.ds(pl.multiple_of(i * tq, tq), tq)
        carry_ref[...] = jnp.zeros_like(carry_ref)
        acc_ref[...] = jnp.zeros_like(acc_ref)

        jj = per_tile - 1
        for g in groups(per_tile, SB_UNROLL):
            masks = [delta < -(jj - u) * tk for u in range(g)]
            carry = add_blocks(kxc_ref, vxc_ref, rows, i * per_tile + jj, masks)
            jj -= g

        unroll = SB_UNROLL if per_tile % SB_UNROLL == 0 else 1
        alive = sweep(kxc_ref, vxc_ref, rows, i * per_tile - 1, (i * per_tile) // unroll, unroll,
                      stick_left(carry))

        if n_past > 0:
            n_full = n_past
            if past_valid < tk:
                @pl.when(alive > 0)
                def _():
                    add_blocks(kxp_ref, vxp_ref, rows, n_past - 1, [key_in_block < past_valid])

                n_full = n_past - 1
                if n_full > 0:
                    alive = stick_left(carry_ref[...])
            if n_full // SB_UNROLL > 0:
                alive = sweep(kxp_ref, vxp_ref, rows, n_full - 1, n_full // SB_UNROLL, SB_UNROLL, alive)
            if n_full % SB_UNROLL:
                alive = sweep(kxp_ref, vxp_ref, rows, n_full % SB_UNROLL - 1, n_full % SB_UNROLL, 1, alive)

        o = acc_ref[...]
        hi, lo = _split_bf16(o * o)
        jm = j_ref[...]
        ms = (jnp.dot(hi, jm, preferred_element_type=F32)
              + jnp.dot(lo, jm, preferred_element_type=F32)) * (1.0 / SB_HEAD_DIM)
        y = o * lax.rsqrt(ms + EPS) * g_ref[...]
        o_ref[0, rows, :] = y.astype(BF16)
        return 0

    lax.fori_loop(0, n_q, q_tile, 0)


def _attention(q, k_cur, v_cur, k_past, v_past, past_len, g_sb_out):
    b, lq, _ = q.shape
    tk = SB_KEYS
    tq = min(SB_QUERY_TILE, lq)
    assert lq % tq == 0 and tq % tk == 0
    if k_past is None:
        k_past = jnp.zeros((1, tk, SB_WIDTH), BF16)
        v_past = jnp.zeros((1, tk, SB_WIDTH), BF16)
        n_past, past_valid = 0, tk
    else:
        assert k_past.shape[1] % tk == 0
        n_past = -(-past_len // tk)
        past_valid = past_len - (n_past - 1) * tk
        k_past = k_past[:, :n_past * tk]
        v_past = v_past[:, :n_past * tk]
    bp, p, _ = k_past.shape
    jj = jnp.arange(2 * tk)
    same_head = (jj[:, None] // tk) == (jj[None, :] // tk)
    bdu = (same_head & (jj[:, None] > jj[None, :])).astype(BF16)
    bdo = same_head.astype(BF16)
    ll = jnp.arange(LANES) // SB_HEAD_DIM
    jm = (ll[:, None] == ll[None, :]).astype(BF16)
    cur_spec = pl.BlockSpec((1, lq, PAIR), lambda bi, hp: (bi, 0, hp))
    past_spec = pl.BlockSpec((1, p, PAIR), (lambda bi, hp: (bi, 0, hp)) if bp > 1
                             else (lambda bi, hp: (0, 0, hp)))
    kernel = functools.partial(_attention_kernel, tq=tq, n_q=lq // tq, n_past=n_past,
                               past_valid=past_valid)
    stacked = lambda n: pltpu.VMEM((n, 2 * tk, PAIR), BF16)
    return pl.pallas_call(
        kernel,
        grid=(b, SB_HEADS // 2),
        in_specs=[cur_spec, cur_spec, cur_spec, past_spec, past_spec,
                  _const_spec(bdu.shape), _const_spec(bdo.shape), _const_spec(jm.shape),
                  pl.BlockSpec((1, PAIR), lambda bi, hp: (0, hp))],
        out_specs=cur_spec,
        out_shape=jax.ShapeDtypeStruct((b, lq, SB_WIDTH), BF16),
        scratch_shapes=[stacked(lq // tk), stacked(lq // tk), stacked(p // tk), stacked(p // tk),
                        pltpu.VMEM((tq, 2 * tk), F32), pltpu.VMEM((tq, PAIR), F32)],
        compiler_params=pltpu.CompilerParams(
            dimension_semantics=("arbitrary", "arbitrary"), vmem_limit_bytes=VMEM_LIMIT),
        name="attention",
    )(q, k_cur, v_cur, k_past, v_past, bdu, bdo, jm, g_sb_out.reshape(1, SB_WIDTH))


def _retention_kernel(q_ref, k_ref, v_ref, gate_ref, s0_ref, g_ref, o_ref, s_ref, *, chunk, n_chunks):
    c = chunk
    head = pl.program_id(1).astype(F32)
    log_g = jnp.log(1.0 - jnp.exp2(jnp.full((1, LANES), -5.0, F32) - head))
    n_row = lax.broadcasted_iota(jnp.int32, (c, LANES), 0).astype(F32)
    q_decay = jnp.exp((n_row + 1.0) * log_g)
    k_decay = jnp.exp((c - 1.0 - n_row) * log_g)
    s_decay = jnp.exp(float(c) * log_g)
    diff = (lax.broadcasted_iota(jnp.int32, (c, c), 0)
            - lax.broadcasted_iota(jnp.int32, (c, c), 1))
    log_g_cc = log_g if c == LANES else jnp.log(1.0 - jnp.exp2(jnp.full((1, c), -5.0, F32) - head))
    decay = jnp.where(diff >= 0, jnp.exp(jnp.maximum(diff, 0).astype(F32) * log_g_cc), 0.0)
    gain = g_ref[...]

    def body(ci, s):
        rows = pl.ds(pl.multiple_of(ci * c, c), c)
        q = q_ref[0, rows, :]
        k = k_ref[0, rows, :]
        v = v_ref[0, rows, :]
        qk = lax.dot_general(q, k, (((1,), (1,)), ((), ())), preferred_element_type=F32)
        inner = jnp.dot((qk * decay).astype(BF16), v, preferred_element_type=F32)
        cross = jnp.dot(q, s.astype(BF16), preferred_element_type=F32) * q_decay
        o = inner + cross
        k_dec = (k.astype(F32) * k_decay).astype(BF16)
        s_new = s_decay * s + lax.dot_general(k_dec, v, (((0,), (0,)), ((), ())),
                                              preferred_element_type=F32)
        y = o * lax.rsqrt(jnp.mean(o * o, axis=-1, keepdims=True) + EPS) * gain
        gate = gate_ref[0, rows, :].astype(F32)
        o_ref[0, rows, :] = (y * (gate * jax.nn.sigmoid(gate))).astype(BF16)
        return s_new

    s_ref[0, 0] = lax.fori_loop(0, n_chunks, body, s0_ref[0, 0])


def _retention(rq, rk, rv, rgate, s0, g_ret_out, chunk):
    b, l, _ = rq.shape
    assert l % chunk == 0
    bs = s0.shape[0]
    seq_spec = pl.BlockSpec((1, l, RET_DK), lambda bi, hh: (bi, 0, hh))
    state_spec = pl.BlockSpec((1, 1, RET_DK, RET_DV), lambda bi, hh: (bi, hh, 0, 0))
    s0_spec = state_spec if bs > 1 else pl.BlockSpec((1, 1, RET_DK, RET_DV), lambda bi, hh: (0, hh, 0, 0))
    kernel = functools.partial(_retention_kernel, chunk=chunk, n_chunks=l // chunk)
    return pl.pallas_call(
        kernel,
        grid=(b, RET_HEADS),
        in_specs=[seq_spec, seq_spec, seq_spec, seq_spec, s0_spec,
                  pl.BlockSpec((1, RET_DV), lambda bi, hh: (0, hh))],
        out_specs=[seq_spec, state_spec],
        out_shape=[jax.ShapeDtypeStruct((b, l, RET_WIDTH), BF16),
                   jax.ShapeDtypeStruct((b, RET_HEADS, RET_DK, RET_DV), F32)],
        compiler_params=pltpu.CompilerParams(
            dimension_semantics=("arbitrary", "arbitrary"), vmem_limit_bytes=VMEM_LIMIT),
        name="retention",
    )(rq, rk, rv, rgate, s0, g_ret_out.reshape(1, RET_WIDTH))


def _merge_ffn_kernel(x_ref, sb_ref, ret_ref, conv0_ref, wo_ref, gf_ref, wug_ref, wuv_ref,
                      cw_ref, cb_ref, wd_ref, gl_ref, y_ref, conv_ref,
                      h_ref, hn_ref, acc_ref, ubuf_ref, act_ref, carry_ref, *, tm):
    i = pl.program_id(1)

    @pl.when(i == 0)
    def _():
        carry_ref[...] = conv0_ref[0]

    mixed = jnp.concatenate([sb_ref[0], ret_ref[0]], axis=1)
    h = x_ref[0] + jnp.dot(mixed, wo_ref[...], preferred_element_type=F32)
    h_ref[...] = h
    ms = jnp.mean(h * h, axis=-1, keepdims=True)
    hn_ref[...] = (h * lax.rsqrt(ms + EPS) * gf_ref[...]).astype(BF16)
    acc_ref[...] = jnp.zeros_like(acc_ref)
    lo = SUBLANES - (CONV_W - 1)

    def up_project(j):
        ubuf = ubuf_ref.at[j % 2]
        hn = hn_ref[...]
        ubuf[0:SUBLANES, :] = carry_ref[j]
        ubuf[SUBLANES:, 0:FF_CHUNK] = jnp.dot(hn, wug_ref[j], preferred_element_type=F32)
        ubuf[SUBLANES:, FF_CHUNK:] = jnp.dot(hn, wuv_ref[j], preferred_element_type=F32)
        carry_ref[j] = ubuf[tm:tm + SUBLANES, :]

    def gated_conv(j):
        ubuf = ubuf_ref.at[j % 2]
        act = act_ref.at[j % 2]
        cw = cw_ref[j]
        cb = cb_ref[j]
        strip = min(tm, FF_STRIP)
        for r in range(0, tm, strip):
            c = cb
            for tap in range(CONV_W):
                c = c + cw[tap:tap + 1, :] * ubuf[lo + tap + r:lo + tap + r + strip, :]
            gate = c[:, :FF_CHUNK]
            act[r:r + strip, :] = ((gate * jax.nn.sigmoid(gate)) * c[:, FF_CHUNK:]).astype(BF16)

    up_project(0)
    for j in range(N_FF_CHUNKS):
        if j + 1 < N_FF_CHUNKS:
            up_project(j + 1)
        gated_conv(j)
        acc_ref[...] += jnp.dot(act_ref[j % 2], wd_ref[j], preferred_element_type=F32)

    hh = h_ref[...] + acc_ref[...]
    ms2 = jnp.mean(hh * hh, axis=-1, keepdims=True)
    y_ref[0] = hh * lax.rsqrt(ms2 + EPS) * gl_ref[...]

    @pl.when(i == pl.num_programs(1) - 1)
    def _():
        conv_ref[0] = carry_ref[...]


def _merge_ffn(x, sb_n, ret_n, conv0, w_out_bf, g_norm_ffn, wug, wuv, cw, cb, wd, g_norm_final, tm):
    b, l, d = x.shape
    bs = conv0.shape[0]
    row_spec = lambda w: pl.BlockSpec((1, tm, w), lambda bi, i: (bi, i, 0))
    conv_shape = (1, N_FF_CHUNKS, SUBLANES, 2 * FF_CHUNK)
    conv_spec = pl.BlockSpec(conv_shape, lambda bi, i: (bi, 0, 0, 0))
    conv0_spec = conv_spec if bs > 1 else pl.BlockSpec(conv_shape, lambda bi, i: (0, 0, 0, 0))
    kernel = functools.partial(_merge_ffn_kernel, tm=tm)
    return pl.pallas_call(
        kernel,
        grid=(b, l // tm),
        in_specs=[row_spec(d), row_spec(SB_WIDTH), row_spec(RET_WIDTH), conv0_spec,
                  _const_spec(w_out_bf.shape), _const_spec((1, d)),
                  _const_spec(wug.shape), _const_spec(wuv.shape),
                  _const_spec(cw.shape), _const_spec(cb.shape), _const_spec(wd.shape),
                  _const_spec((1, d))],
        out_specs=[row_spec(d), conv_spec],
        out_shape=[jax.ShapeDtypeStruct((b, l, d), F32),
                   jax.ShapeDtypeStruct((b,) + conv_shape[1:], F32)],
        scratch_shapes=[pltpu.VMEM((tm, d), F32), pltpu.VMEM((tm, d), BF16), pltpu.VMEM((tm, d), F32),
                        pltpu.VMEM((2, tm + SUBLANES, 2 * FF_CHUNK), F32),
                        pltpu.VMEM((2, tm, FF_CHUNK), BF16),
                        pltpu.VMEM(conv_shape[1:], F32)],
        compiler_params=pltpu.CompilerParams(
            dimension_semantics=("arbitrary", "arbitrary"), vmem_limit_bytes=VMEM_LIMIT),
        name="merge_ffn",
    )(x, sb_n, ret_n, conv0, w_out_bf, g_norm_ffn.reshape(1, d), wug, wuv, cw, cb, wd,
      g_norm_final.reshape(1, d))


def _rope_tables(pos):
    half = RET_DK // 2
    inv = ROPE_BASE ** (-jnp.arange(half, dtype=F32) / half)
    ang = pos.astype(F32)[:, None] * inv[None, :]
    cos, sin = jnp.cos(ang), jnp.sin(ang)
    return jnp.concatenate([cos, cos], axis=1), jnp.concatenate([-sin, sin], axis=1)


def _conv_state_to_chunks(state):
    b = state.shape[0]
    s = state.reshape(b, CONV_W - 1, 2, N_FF_CHUNKS, FF_CHUNK).transpose(0, 3, 1, 2, 4)
    s = s.reshape(b, N_FF_CHUNKS, CONV_W - 1, 2 * FF_CHUNK)
    return jnp.pad(s, ((0, 0), (0, 0), (SUBLANES - (CONV_W - 1), 0), (0, 0)))


def _conv_state_from_chunks(chunks):
    b = chunks.shape[0]
    s = chunks[:, :, SUBLANES - (CONV_W - 1):, :].reshape(b, N_FF_CHUNKS, CONV_W - 1, 2, FF_CHUNK)
    return s.transpose(0, 2, 3, 1, 4).reshape(b, CONV_W - 1, 2 * D_FF)


def _pad_rows(a, rows):
    return a if a.shape[1] == rows else jnp.pad(a, ((0, 0), (0, rows - a.shape[1]), (0, 0)))


def _stream_step(x, pos0, k_past, v_past, past_len, s0, conv0, wts, tm, chunk):
    l = x.shape[1]
    cos2, sin2 = _rope_tables(pos0 + jnp.arange(l))
    q, k, v, k_out, v_out, rq, rk, rv, rgate = _project(x, wts["g_norm_mix"], wts["w_in"], cos2, sin2, tm)
    lq = -(-l // SB_KEYS) * SB_KEYS
    sb_n = _attention(_pad_rows(q, lq), _pad_rows(k, lq), _pad_rows(v, lq),
                      k_past, v_past, past_len, wts["g_sb_out"])[:, :l]
    ret_n, s_new = _retention(rq, rk, rv, rgate, s0, wts["g_ret_out"], chunk)
    y, conv_new = _merge_ffn(x, sb_n, ret_n, conv0, wts["w_out"], wts["g_norm_ffn"], wts["wug"],
                             wts["wuv"], wts["cw"], wts["cb"], wts["wd"], wts["g_norm_final"], tm)
    return y, k, v, k_out, v_out, s_new, conv_new


def kernel(x_prompt, x_sample, cache_sb_k, cache_sb_v, state_ret, state_conv, meta_tokens, g_norm_mix, w_in, g_sb_out, g_ret_out, w_out, g_norm_ffn, w_up, conv_w, conv_b, w_down, g_norm_final):
    b, seq, d = x_prompt.shape
    bd, ls, _ = x_sample.shape
    past = cache_sb_k.shape[2]

    def ff_cols(a):
        r = a.shape[0]
        return a.reshape(r, 2, N_FF_CHUNKS, FF_CHUNK).transpose(2, 0, 1, 3).reshape(N_FF_CHUNKS, r, 2 * FF_CHUNK)

    w_up_bf = w_up.astype(BF16)
    wts = dict(
        g_norm_mix=g_norm_mix, g_sb_out=g_sb_out, g_ret_out=g_ret_out, g_norm_ffn=g_norm_ffn,
        g_norm_final=g_norm_final,
        w_in=w_in.astype(BF16), w_out=w_out.astype(BF16),
        wug=w_up_bf[:, :D_FF].reshape(d, N_FF_CHUNKS, FF_CHUNK).transpose(1, 0, 2),
        wuv=w_up_bf[:, D_FF:].reshape(d, N_FF_CHUNKS, FF_CHUNK).transpose(1, 0, 2),
        cw=ff_cols(conv_w), cb=ff_cols(conv_b.reshape(1, 2 * D_FF)),
        wd=w_down.astype(BF16).reshape(N_FF_CHUNKS, FF_CHUNK, d),
    )

    zero_state = jnp.zeros((1, RET_HEADS, RET_DK, RET_DV), F32)
    zero_conv = jnp.zeros((1, N_FF_CHUNKS, SUBLANES, 2 * FF_CHUNK), F32)
    _, k_m, v_m, k_m_out, v_m_out, s_meta, conv_meta = _stream_step(
        meta_tokens[None], -N_META, None, None, 0, zero_state, zero_conv, wts, N_META, N_META)

    y_prompt, _, _, k_p_out, v_p_out, s_prompt, conv_prompt = _stream_step(
        x_prompt, 0, _pad_rows(k_m, SB_KEYS), _pad_rows(v_m, SB_KEYS), N_META, s_meta, conv_meta,
        wts, 512, 256)
    rep = lambda a: jnp.broadcast_to(a, (b,) + a.shape[1:])
    new_k_prompt = jnp.concatenate([rep(k_m_out), k_p_out], axis=2)
    new_v_prompt = jnp.concatenate([rep(v_m_out), v_p_out], axis=2)

    to_rows = lambda c: c.transpose(0, 2, 1, 3).reshape(bd, past, SB_WIDTH).astype(BF16)
    y_sample, _, _, k_s_out, v_s_out, s_sample, conv_sample = _stream_step(
        x_sample, past, to_rows(cache_sb_k), to_rows(cache_sb_v), past, state_ret,
        _conv_state_to_chunks(state_conv), wts, ls, ls)

    return (y_prompt, y_sample, new_k_prompt, new_v_prompt, s_prompt,
            _conv_state_from_chunks(conv_prompt), k_s_out, v_s_out, s_sample,
            _conv_state_from_chunks(conv_sample))
```

```python
import functools

import jax
import jax.numpy as jnp
from jax import lax
from jax.experimental import pallas as pl
from jax.experimental.pallas import tpu as pltpu

D_MODEL = 1024
N_META = 16
SB_HEADS = 8
SB_HEAD_DIM = 64
SB_WIDTH = SB_HEADS * SB_HEAD_DIM
RET_HEADS = 4
RET_DK = 128
RET_DV = 128
RET_WIDTH = RET_HEADS * RET_DV
MIX_WIDTH = SB_WIDTH + RET_WIDTH
GROUP = 512
N_GROUPS = 7
IN_WIDTH = N_GROUPS * GROUP
D_FF = 2816
CONV_W = 3
ROPE_BASE = 10000.0
EPS = 1e-5

LANES = 128
SUBLANES = 8
FF_CHUNK = 256
N_FF_CHUNKS = D_FF // FF_CHUNK
FF_STRIP = 64
SB_KEYS = 128
SB_QUERY_TILE = 256
SB_UNROLL = 2
PAIR = 2 * SB_HEAD_DIM
SB_EXHAUSTED = 105.0
VMEM_LIMIT = 56 * 1024 * 1024

BF16 = jnp.bfloat16
F32 = jnp.float32


def _const_spec(shape):
    zeros = (0,) * len(shape)
    return pl.BlockSpec(shape, lambda *_: zeros, pipeline_mode=pl.Buffered(1))


def _project_kernel(x_ref, g_ref, w_ref, cos_ref, sin_ref,
                    q_ref, k_ref, v_ref, ko_ref, vo_ref, rq_ref, rk_ref, rv_ref, rg_ref):
    x = x_ref[0]
    ms = jnp.mean(x * x, axis=-1, keepdims=True)
    h = (x * lax.rsqrt(ms + EPS) * g_ref[...]).astype(BF16)

    def group(i):
        return jnp.dot(h, w_ref[:, i * GROUP:(i + 1) * GROUP], preferred_element_type=F32)

    def split_heads(p, out_ref):
        for hh in range(SB_HEADS):
            out_ref[0, hh] = p[:, hh * SB_HEAD_DIM:(hh + 1) * SB_HEAD_DIM]

    def rope(p, out_ref, scale):
        cos = cos_ref[...]
        sin = sin_ref[...]
        for hh in range(RET_HEADS):
            t = p[:, hh * RET_DK:(hh + 1) * RET_DK]
            r = t * cos + pltpu.roll(t, RET_DK // 2, 1) * sin
            if scale is not None:
                r = r * scale
            out_ref[0, :, hh * RET_DK:(hh + 1) * RET_DK] = r.astype(BF16)

    q_ref[0] = (group(0) * (SB_HEAD_DIM ** -0.5)).astype(BF16)
    pk = group(1)
    k_ref[0] = pk.astype(BF16)
    split_heads(pk, ko_ref)
    pv = group(2)
    v_ref[0] = pv.astype(BF16)
    split_heads(pv, vo_ref)
    rope(group(3), rq_ref, None)
    rope(group(4), rk_ref, RET_DK ** -0.5)
    rv_ref[0] = group(5).astype(BF16)
    rg_ref[0] = group(6).astype(BF16)


def _project(x, g_norm, w_in_bf, cos2, sin2, tm, head_row0=0):
    b, l, d = x.shape
    grid = (b, l // tm)
    row_spec = lambda w: pl.BlockSpec((1, tm, w), lambda bi, i: (bi, i, 0))
    head_blk = (1, SB_HEADS, tm, SB_HEAD_DIM)
    head_spec = pl.BlockSpec(tuple(pl.Element(n) for n in head_blk),
                             lambda bi, i: (bi, 0, pl.multiple_of(head_row0 + i * tm, SUBLANES), 0))
    tab_spec = pl.BlockSpec((tm, RET_DK), lambda bi, i: (i, 0))
    act = jax.ShapeDtypeStruct((b, l, GROUP), BF16)
    heads = jax.ShapeDtypeStruct((b, SB_HEADS, head_row0 + l, SB_HEAD_DIM), F32)
    return pl.pallas_call(
        _project_kernel,
        grid=grid,
        in_specs=[row_spec(d), _const_spec((1, d)), _const_spec((d, IN_WIDTH)), tab_spec, tab_spec],
        out_specs=[row_spec(GROUP)] * 3 + [head_spec] * 2 + [row_spec(GROUP)] * 4,
        out_shape=[act] * 3 + [heads] * 2 + [act] * 4,
        compiler_params=pltpu.CompilerParams(
            dimension_semantics=("arbitrary", "arbitrary"), vmem_limit_bytes=VMEM_LIMIT),
        name="project",
    )(x, g_norm.reshape(1, d), w_in_bf, cos2, sin2)


def _split_bf16(x):
    hi = x.astype(BF16)
    lo = (x - hi.astype(F32)).astype(BF16)
    return hi, lo


def _attention_kernel(q_ref, kc_ref, vc_ref, kp_ref, vp_ref, bdu_ref, bdo_ref, j_ref, g_ref, o_ref,
                      kxc_ref, vxc_ref, kxp_ref, vxp_ref, carry_ref, acc_ref,
                      *, tq, n_q, n_past, past_valid):
    tk = SB_KEYS
    per_tile = tq // tk
    head0 = lax.broadcasted_iota(jnp.int32, (tk, LANES), 1) < SB_HEAD_DIM
    key_in_block = lax.broadcasted_iota(jnp.int32, (tq, 2 * tk), 1) & (tk - 1)
    delta = key_in_block - lax.broadcasted_iota(jnp.int32, (tq, 2 * tk), 0)

    def expand(src_ref, dst_ref, n_blocks):
        def body(j, _):
            blk = src_ref[0, pl.ds(pl.multiple_of(j * tk, tk), tk), :]
            zero = jnp.zeros_like(blk)
            dst_ref[j, 0:tk, :] = jnp.where(head0, blk, zero)
            dst_ref[j, tk:2 * tk, :] = jnp.where(head0, zero, blk)
            return 0
        lax.fori_loop(0, n_blocks, body, 0)

    expand(kc_ref, kxc_ref, n_q * per_tile)
    expand(vc_ref, vxc_ref, n_q * per_tile)
    if n_past > 0:
        expand(kp_ref, kxp_ref, n_past)
        expand(vp_ref, vxp_ref, n_past)

    def add_blocks(kx_ref, vx_ref, rows, j_last, masks):
        q2 = q_ref[0, rows, :]
        carry = carry_ref[...]
        for u, mask in enumerate(masks):
            j = j_last - u
            z = lax.dot_general(q2, kx_ref[j], (((1,), (1,)), ((), ())), preferred_element_type=F32)
            sp = jnp.maximum(z, 0.0) + jnp.log(1.0 + jnp.exp(-jnp.abs(z)))
            spm = sp if mask is None else jnp.where(mask, sp, 0.0)
            hi = spm.astype(BF16)
            later = jnp.dot(hi, bdu_ref[...], preferred_element_type=F32)
            total = jnp.dot(hi, bdo_ref[...], preferred_element_type=F32)
            a = jnp.exp((z - sp) - (later + carry))
            if mask is not None:
                a = jnp.where(mask, a, 0.0)
            acc_ref[...] += jnp.dot(a.astype(BF16), vx_ref[j], preferred_element_type=F32)
            carry = carry + total
        carry_ref[...] = carry
        return carry

    def stick_left(carry):
        return (jnp.min(carry) < SB_EXHAUSTED).astype(jnp.int32)

    def sweep(kx_ref, vx_ref, rows, j_top, n_steps, unroll, alive):
        def cond(state):
            m, live = state
            return jnp.logical_and(m < n_steps, live > 0)

        def body(state):
            m, _ = state
            carry = add_blocks(kx_ref, vx_ref, rows, j_top - m * unroll, [None] * unroll)
            return m + 1, stick_left(carry)

        return lax.while_loop(cond, body, (jnp.int32(0), alive))[1]

    def groups(n, size):
        return [min(size, n - s) for s in range(0, n, size)]

    def q_tile(i, _):
        rows = pl.ds(pl.multiple_of(i * tq, tq), tq)
        carry_ref[...] = jnp.zeros_like(carry_ref)
        acc_ref[...] = jnp.zeros_like(acc_ref)

        jj = per_tile - 1
        for g in groups(per_tile, SB_UNROLL):
            masks = [delta < -(jj - u) * tk for u in range(g)]
            carry = add_blocks(kxc_ref, vxc_ref, rows, i * per_tile + jj, masks)
            jj -= g

        unroll = SB_UNROLL if per_tile % SB_UNROLL == 0 else 1
        alive = sweep(kxc_ref, vxc_ref, rows, i * per_tile - 1, (i * per_tile) // unroll, unroll,
                      stick_left(carry))

        if n_past > 0:
            n_full = n_past
            if past_valid < tk:
                @pl.when(alive > 0)
                def _():
                    add_blocks(kxp_ref, vxp_ref, rows, n_past - 1, [key_in_block < past_valid])

                n_full = n_past - 1
                if n_full > 0:
                    alive = stick_left(carry_ref[...])
            if n_full // SB_UNROLL > 0:
                alive = sweep(kxp_ref, vxp_ref, rows, n_full - 1, n_full // SB_UNROLL, SB_UNROLL, alive)
            if n_full % SB_UNROLL:
                alive = sweep(kxp_ref, vxp_ref, rows, n_full % SB_UNROLL - 1, n_full % SB_UNROLL, 1, alive)

        o = acc_ref[...]
        hi, lo = _split_bf16(o * o)
        jm = j_ref[...]
        ms = (jnp.dot(hi, jm, preferred_element_type=F32)
              + jnp.dot(lo, jm, preferred_element_type=F32)) * (1.0 / SB_HEAD_DIM)
        y = o * lax.rsqrt(ms + EPS) * g_ref[...]
        o_ref[0, rows, :] = y.astype(BF16)
        return 0

    lax.fori_loop(0, n_q, q_tile, 0)


def _attention(q, k_cur, v_cur, k_past, v_past, past_len, g_sb_out):
    b, lq, _ = q.shape
    tk = SB_KEYS
    tq = min(SB_QUERY_TILE, lq)
    assert lq % tq == 0 and tq % tk == 0
    if k_past is None:
        k_past = jnp.zeros((1, tk, SB_WIDTH), BF16)
        v_past = jnp.zeros((1, tk, SB_WIDTH), BF16)
        n_past, past_valid = 0, tk
    else:
        assert k_past.shape[1] % tk == 0
        n_past = -(-past_len // tk)
        past_valid = past_len - (n_past - 1) * tk
        k_past = k_past[:, :n_past * tk]
        v_past = v_past[:, :n_past * tk]
    bp, p, _ = k_past.shape
    jj = jnp.arange(2 * tk)
    same_head = (jj[:, None] // tk) == (jj[None, :] // tk)
    bdu = (same_head & (jj[:, None] > jj[None, :])).astype(BF16)
    bdo = same_head.astype(BF16)
    ll = jnp.arange(LANES) // SB_HEAD_DIM
    jm = (ll[:, None] == ll[None, :]).astype(BF16)
    cur_spec = pl.BlockSpec((1, lq, PAIR), lambda bi, hp: (bi, 0, hp))
    past_spec = pl.BlockSpec((1, p, PAIR), (lambda bi, hp: (bi, 0, hp)) if bp > 1
                             else (lambda bi, hp: (0, 0, hp)))
    kernel = functools.partial(_attention_kernel, tq=tq, n_q=lq // tq, n_past=n_past,
                               past_valid=past_valid)
    stacked = lambda n: pltpu.VMEM((n, 2 * tk, PAIR), BF16)
    return pl.pallas_call(
        kernel,
        grid=(b, SB_HEADS // 2),
        in_specs=[cur_spec, cur_spec, cur_spec, past_spec, past_spec,
                  _const_spec(bdu.shape), _const_spec(bdo.shape), _const_spec(jm.shape),
                  pl.BlockSpec((1, PAIR), lambda bi, hp: (0, hp))],
        out_specs=cur_spec,
        out_shape=jax.ShapeDtypeStruct((b, lq, SB_WIDTH), BF16),
        scratch_shapes=[stacked(lq // tk), stacked(lq // tk), stacked(p // tk), stacked(p // tk),
                        pltpu.VMEM((tq, 2 * tk), F32), pltpu.VMEM((tq, PAIR), F32)],
        compiler_params=pltpu.CompilerParams(
            dimension_semantics=("arbitrary", "arbitrary"), vmem_limit_bytes=VMEM_LIMIT),
        name="attention",
    )(q, k_cur, v_cur, k_past, v_past, bdu, bdo, jm, g_sb_out.reshape(1, SB_WIDTH))


def _retention_kernel(q_ref, k_ref, v_ref, gate_ref, s0_ref, g_ref, o_ref, s_ref, *, chunk, n_chunks):
    c = chunk
    head = pl.program_id(1).astype(F32)
    log_g = jnp.log(1.0 - jnp.exp2(jnp.full((1, LANES), -5.0, F32) - head))
    n_row = lax.broadcasted_iota(jnp.int32, (c, LANES), 0).astype(F32)
    q_decay = jnp.exp((n_row + 1.0) * log_g)
    k_decay = jnp.exp((c - 1.0 - n_row) * log_g)
    s_decay = jnp.exp(float(c) * log_g)
    diff = (lax.broadcasted_iota(jnp.int32, (c, c), 0)
            - lax.broadcasted_iota(jnp.int32, (c, c), 1))
    log_g_cc = log_g if c == LANES else jnp.log(1.0 - jnp.exp2(jnp.full((1, c), -5.0, F32) - head))
    decay = jnp.where(diff >= 0, jnp.exp(jnp.maximum(diff, 0).astype(F32) * log_g_cc), 0.0)
    gain = g_ref[...]

    def body(ci, s):
        rows = pl.ds(pl.multiple_of(ci * c, c), c)
        q = q_ref[0, rows, :]
        k = k_ref[0, rows, :]
        v = v_ref[0, rows, :]
        qk = lax.dot_general(q, k, (((1,), (1,)), ((), ())), preferred_element_type=F32)
        inner = jnp.dot((qk * decay).astype(BF16), v, preferred_element_type=F32)
        cross = jnp.dot(q, s.astype(BF16), preferred_element_type=F32) * q_decay
        o = inner + cross
        k_dec = (k.astype(F32) * k_decay).astype(BF16)
        s_new = s_decay * s + lax.dot_general(k_dec, v, (((0,), (0,)), ((), ())),
                                              preferred_element_type=F32)
        y = o * lax.rsqrt(jnp.mean(o * o, axis=-1, keepdims=True) + EPS) * gain
        gate = gate_ref[0, rows, :].astype(F32)
        o_ref[0, rows, :] = (y * (gate * jax.nn.sigmoid(gate))).astype(BF16)
        return s_new

    s_ref[0, 0] = lax.fori_loop(0, n_chunks, body, s0_ref[0, 0])


def _retention(rq, rk, rv, rgate, s0, g_ret_out, chunk):
    b, l, _ = rq.shape
    assert l % chunk == 0
    bs = s0.shape[0]
    seq_spec = pl.BlockSpec((1, l, RET_DK), lambda bi, hh: (bi, 0, hh))
    state_spec = pl.BlockSpec((1, 1, RET_DK, RET_DV), lambda bi, hh: (bi, hh, 0, 0))
    s0_spec = state_spec if bs > 1 else pl.BlockSpec((1, 1, RET_DK, RET_DV), lambda bi, hh: (0, hh, 0, 0))
    kernel = functools.partial(_retention_kernel, chunk=chunk, n_chunks=l // chunk)
    return pl.pallas_call(
        kernel,
        grid=(b, RET_HEADS),
        in_specs=[seq_spec, seq_spec, seq_spec, seq_spec, s0_spec,
                  pl.BlockSpec((1, RET_DV), lambda bi, hh: (0, hh))],
        out_specs=[seq_spec, state_spec],
        out_shape=[jax.ShapeDtypeStruct((b, l, RET_WIDTH), BF16),
                   jax.ShapeDtypeStruct((b, RET_HEADS, RET_DK, RET_DV), F32)],
        compiler_params=pltpu.CompilerParams(
            dimension_semantics=("arbitrary", "arbitrary"), vmem_limit_bytes=VMEM_LIMIT),
        name="retention",
    )(rq, rk, rv, rgate, s0, g_ret_out.reshape(1, RET_WIDTH))


def _merge_ffn_kernel(x_ref, sb_ref, ret_ref, conv0_ref, wo_ref, gf_ref, wug_ref, wuv_ref,
                      cw_ref, cb_ref, wd_ref, gl_ref, y_ref, conv_ref,
                      h_ref, hn_ref, acc_ref, ubuf_ref, act_ref, carry_ref, *, tm):
    i = pl.program_id(1)

    @pl.when(i == 0)
    def _():
        carry_ref[...] = conv0_ref[0]

    mixed = jnp.concatenate([sb_ref[0], ret_ref[0]], axis=1)
    h = x_ref[0] + jnp.dot(mixed, wo_ref[...], preferred_element_type=F32)
    h_ref[...] = h
    ms = jnp.mean(h * h, axis=-1, keepdims=True)
    hn_ref[...] = (h * lax.rsqrt(ms + EPS) * gf_ref[...]).astype(BF16)
    acc_ref[...] = jnp.zeros_like(acc_ref)
    lo = SUBLANES - (CONV_W - 1)

    def up_project(j):
        ubuf = ubuf_ref.at[j % 2]
        hn = hn_ref[...]
        ubuf[0:SUBLANES, :] = carry_ref[j]
        ubuf[SUBLANES:, 0:FF_CHUNK] = jnp.dot(hn, wug_ref[j], preferred_element_type=F32)
        ubuf[SUBLANES:, FF_CHUNK:] = jnp.dot(hn, wuv_ref[j], preferred_element_type=F32)
        carry_ref[j] = ubuf[tm:tm + SUBLANES, :]

    def gated_conv(j):
        ubuf = ubuf_ref.at[j % 2]
        act = act_ref.at[j % 2]
        cw = cw_ref[j]
        cb = cb_ref[j]
        strip = min(tm, FF_STRIP)
        for r in range(0, tm, strip):
            c = cb
            for tap in range(CONV_W):
                c = c + cw[tap:tap + 1, :] * ubuf[lo + tap + r:lo + tap + r + strip, :]
            gate = c[:, :FF_CHUNK]
            act[r:r + strip, :] = ((gate * jax.nn.sigmoid(gate)) * c[:, FF_CHUNK:]).astype(BF16)

    up_project(0)
    for j in range(N_FF_CHUNKS):
        if j + 1 < N_FF_CHUNKS:
            up_project(j + 1)
        gated_conv(j)
        acc_ref[...] += jnp.dot(act_ref[j % 2], wd_ref[j], preferred_element_type=F32)

    hh = h_ref[...] + acc_ref[...]
    ms2 = jnp.mean(hh * hh, axis=-1, keepdims=True)
    y_ref[0] = hh * lax.rsqrt(ms2 + EPS) * gl_ref[...]

    @pl.when(i == pl.num_programs(1) - 1)
    def _():
        conv_ref[0] = carry_ref[...]


def _merge_ffn(x, sb_n, ret_n, conv0, w_out_bf, g_norm_ffn, wug, wuv, cw, cb, wd, g_norm_final, tm):
    b, l, d = x.shape
    bs = conv0.shape[0]
    row_spec = lambda w: pl.BlockSpec((1, tm, w), lambda bi, i: (bi, i, 0))
    conv_shape = (1, N_FF_CHUNKS, SUBLANES, 2 * FF_CHUNK)
    conv_spec = pl.BlockSpec(conv_shape, lambda bi, i: (bi, 0, 0, 0))
    conv0_spec = conv_spec if bs > 1 else pl.BlockSpec(conv_shape, lambda bi, i: (0, 0, 0, 0))
    kernel = functools.partial(_merge_ffn_kernel, tm=tm)
    return pl.pallas_call(
        kernel,
        grid=(b, l // tm),
        in_specs=[row_spec(d), row_spec(SB_WIDTH), row_spec(RET_WIDTH), conv0_spec,
                  _const_spec(w_out_bf.shape), _const_spec((1, d)),
                  _const_spec(wug.shape), _const_spec(wuv.shape),
                  _const_spec(cw.shape), _const_spec(cb.shape), _const_spec(wd.shape),
                  _const_spec((1, d))],
        out_specs=[row_spec(d), conv_spec],
        out_shape=[jax.ShapeDtypeStruct((b, l, d), F32),
                   jax.ShapeDtypeStruct((b,) + conv_shape[1:], F32)],
        scratch_shapes=[pltpu.VMEM((tm, d), F32), pltpu.VMEM((tm, d), BF16), pltpu.VMEM((tm, d), F32),
                        pltpu.VMEM((2, tm + SUBLANES, 2 * FF_CHUNK), F32),
                        pltpu.VMEM((2, tm, FF_CHUNK), BF16),
                        pltpu.VMEM(conv_shape[1:], F32)],
        compiler_params=pltpu.CompilerParams(
            dimension_semantics=("arbitrary", "arbitrary"), vmem_limit_bytes=VMEM_LIMIT),
        name="merge_ffn",
    )(x, sb_n, ret_n, conv0, w_out_bf, g_norm_ffn.reshape(1, d), wug, wuv, cw, cb, wd,
      g_norm_final.reshape(1, d))


def _rope_tables(pos):
    half = RET_DK // 2
    inv = ROPE_BASE ** (-jnp.arange(half, dtype=F32) / half)
    ang = pos.astype(F32)[:, None] * inv[None, :]
    cos, sin = jnp.cos(ang), jnp.sin(ang)
    return jnp.concatenate([cos, cos], axis=1), jnp.concatenate([-sin, sin], axis=1)


def _conv_state_to_chunks(state):
    b = state.shape[0]
    s = state.reshape(b, CONV_W - 1, 2, N_FF_CHUNKS, FF_CHUNK).transpose(0, 3, 1, 2, 4)
    s = s.reshape(b, N_FF_CHUNKS, CONV_W - 1, 2 * FF_CHUNK)
    return jnp.pad(s, ((0, 0), (0, 0), (SUBLANES - (CONV_W - 1), 0), (0, 0)))


def _conv_state_from_chunks(chunks):
    b = chunks.shape[0]
    s = chunks[:, :, SUBLANES - (CONV_W - 1):, :].reshape(b, N_FF_CHUNKS, CONV_W - 1, 2, FF_CHUNK)
    return s.transpose(0, 2, 3, 1, 4).reshape(b, CONV_W - 1, 2 * D_FF)


def _pad_rows(a, rows):
    return a if a.shape[1] == rows else jnp.pad(a, ((0, 0), (0, rows - a.shape[1]), (0, 0)))


def _prefix_rows_kernel(k_any, v_any, k_rows_ref, v_rows_ref, k_ref, v_ref):
    del k_any, v_any
    k_ref[...] = k_rows_ref[...]
    v_ref[...] = v_rows_ref[...]


def _write_prefix_rows(k_big, v_big, k_rows, v_rows):
    b = k_big.shape[0]
    blk = (1,) + k_rows.shape[1:]
    any_spec = pl.BlockSpec(memory_space=pl.ANY)
    rows_spec = pl.BlockSpec(blk, lambda bi: (0, 0, 0, 0))
    out_spec = pl.BlockSpec(blk, lambda bi: (bi, 0, 0, 0))
    big = jax.ShapeDtypeStruct(k_big.shape, k_big.dtype)
    return pl.pallas_call(
        _prefix_rows_kernel,
        grid=(b,),
        in_specs=[any_spec, any_spec, rows_spec, rows_spec],
        out_specs=[out_spec, out_spec],
        out_shape=[big, big],
        input_output_aliases={0: 0, 1: 1},
        name="prefix_rows",
    )(k_big, v_big, k_rows, v_rows)


def _stream_step(x, pos0, k_past, v_past, past_len, s0, conv0, wts, tm, chunk, head_row0=0):
    l = x.shape[1]
    cos2, sin2 = _rope_tables(pos0 + jnp.arange(l))
    q, k, v, k_out, v_out, rq, rk, rv, rgate = _project(x, wts["g_norm_mix"], wts["w_in"], cos2, sin2,
                                                       tm, head_row0)
    lq = -(-l // SB_KEYS) * SB_KEYS
    sb_n = _attention(_pad_rows(q, lq), _pad_rows(k, lq), _pad_rows(v, lq),
                      k_past, v_past, past_len, wts["g_sb_out"])[:, :l]
    ret_n, s_new = _retention(rq, rk, rv, rgate, s0, wts["g_ret_out"], chunk)
    y, conv_new = _merge_ffn(x, sb_n, ret_n, conv0, wts["w_out"], wts["g_norm_ffn"], wts["wug"],
                             wts["wuv"], wts["cw"], wts["cb"], wts["wd"], wts["g_norm_final"], tm)
    return y, k, v, k_out, v_out, s_new, conv_new


def kernel(x_prompt, x_sample, cache_sb_k, cache_sb_v, state_ret, state_conv, meta_tokens, g_norm_mix, w_in, g_sb_out, g_ret_out, w_out, g_norm_ffn, w_up, conv_w, conv_b, w_down, g_norm_final):
    b, seq, d = x_prompt.shape
    bd, ls, _ = x_sample.shape
    past = cache_sb_k.shape[2]

    def ff_cols(a):
        r = a.shape[0]
        return a.reshape(r, 2, N_FF_CHUNKS, FF_CHUNK).transpose(2, 0, 1, 3).reshape(N_FF_CHUNKS, r, 2 * FF_CHUNK)

    w_up_bf = w_up.astype(BF16)
    wts = dict(
        g_norm_mix=g_norm_mix, g_sb_out=g_sb_out, g_ret_out=g_ret_out, g_norm_ffn=g_norm_ffn,
        g_norm_final=g_norm_final,
        w_in=w_in.astype(BF16), w_out=w_out.astype(BF16),
        wug=w_up_bf[:, :D_FF].reshape(d, N_FF_CHUNKS, FF_CHUNK).transpose(1, 0, 2),
        wuv=w_up_bf[:, D_FF:].reshape(d, N_FF_CHUNKS, FF_CHUNK).transpose(1, 0, 2),
        cw=ff_cols(conv_w), cb=ff_cols(conv_b.reshape(1, 2 * D_FF)),
        wd=w_down.astype(BF16).reshape(N_FF_CHUNKS, FF_CHUNK, d),
    )

    zero_state = jnp.zeros((1, RET_HEADS, RET_DK, RET_DV), F32)
    zero_conv = jnp.zeros((1, N_FF_CHUNKS, SUBLANES, 2 * FF_CHUNK), F32)
    _, k_m, v_m, k_m_out, v_m_out, s_meta, conv_meta = _stream_step(
        meta_tokens[None], -N_META, None, None, 0, zero_state, zero_conv, wts, N_META, N_META)

    y_prompt, _, _, k_p_out, v_p_out, s_prompt, conv_prompt = _stream_step(
        x_prompt, 0, _pad_rows(k_m, SB_KEYS), _pad_rows(v_m, SB_KEYS), N_META, s_meta, conv_meta,
        wts, 512, 256, head_row0=N_META)
    new_k_prompt, new_v_prompt = _write_prefix_rows(k_p_out, v_p_out, k_m_out, v_m_out)

    to_rows = lambda c: c.transpose(0, 2, 1, 3).reshape(bd, past, SB_WIDTH).astype(BF16)
    y_sample, _, _, k_s_out, v_s_out, s_sample, conv_sample = _stream_step(
        x_sample, past, to_rows(cache_sb_k), to_rows(cache_sb_v), past, state_ret,
        _conv_state_to_chunks(state_conv), wts, ls, ls)

    return (y_prompt, y_sample, new_k_prompt, new_v_prompt, s_prompt,
            _conv_state_from_chunks(conv_prompt), k_s_out, v_s_out, s_sample,
            _conv_state_from_chunks(conv_sample))
```

```python
import functools
import math

import jax
import jax.numpy as jnp
from jax import lax
from jax.experimental import pallas as pl
from jax.experimental.pallas import tpu as pltpu

D_MODEL = 1024
N_META = 16
SB_HEADS = 8
SB_HEAD_DIM = 64
SB_WIDTH = SB_HEADS * SB_HEAD_DIM
RET_HEADS = 4
RET_DK = 128
RET_DV = 128
RET_WIDTH = RET_HEADS * RET_DV
MIX_WIDTH = SB_WIDTH + RET_WIDTH
GROUP = 512
N_GROUPS = 7
IN_WIDTH = N_GROUPS * GROUP
D_FF = 2816
CONV_W = 3
ROPE_BASE = 10000.0
EPS = 1e-5

LANES = 128
SUBLANES = 8
FF_CHUNK = 256
N_FF_CHUNKS = D_FF // FF_CHUNK
FF_STRIP = 64
SB_KEYS = 128
SB_QUERY_TILE = 512
SB_UNROLL = 2
PAIR = 2 * SB_HEAD_DIM
SB_EXHAUSTED = 152.0
SB_Q_SCALE = SB_HEAD_DIM ** -0.5 * math.log2(math.e)
VMEM_LIMIT = 56 * 1024 * 1024

BF16 = jnp.bfloat16
F32 = jnp.float32


def _const_spec(shape):
    zeros = (0,) * len(shape)
    return pl.BlockSpec(shape, lambda *_: zeros, pipeline_mode=pl.Buffered(1))


def _project_kernel(x_ref, g_ref, w_ref, cos_ref, sin_ref,
                    q_ref, k_ref, v_ref, ko_ref, vo_ref, rq_ref, rk_ref, rv_ref, rg_ref):
    x = x_ref[0]
    ms = jnp.mean(x * x, axis=-1, keepdims=True)
    h = (x * lax.rsqrt(ms + EPS) * g_ref[...]).astype(BF16)

    def group(i):
        return jnp.dot(h, w_ref[:, i * GROUP:(i + 1) * GROUP], preferred_element_type=F32)

    def split_heads(p, out_ref):
        for hh in range(SB_HEADS):
            out_ref[0, hh] = p[:, hh * SB_HEAD_DIM:(hh + 1) * SB_HEAD_DIM]

    def rope(p, out_ref, scale):
        cos = cos_ref[...]
        sin = sin_ref[...]
        for hh in range(RET_HEADS):
            t = p[:, hh * RET_DK:(hh + 1) * RET_DK]
            r = t * cos + pltpu.roll(t, RET_DK // 2, 1) * sin
            if scale is not None:
                r = r * scale
            out_ref[0, :, hh * RET_DK:(hh + 1) * RET_DK] = r.astype(BF16)

    q_ref[0] = (group(0) * SB_Q_SCALE).astype(BF16)
    pk = group(1)
    k_ref[0] = pk.astype(BF16)
    split_heads(pk, ko_ref)
    pv = group(2)
    v_ref[0] = pv.astype(BF16)
    split_heads(pv, vo_ref)
    rope(group(3), rq_ref, None)
    rope(group(4), rk_ref, RET_DK ** -0.5)
    rv_ref[0] = group(5).astype(BF16)
    rg_ref[0] = group(6).astype(BF16)


def _project(x, g_norm, w_in_bf, cos2, sin2, tm, head_row0=0):
    b, l, d = x.shape
    grid = (b, l // tm)
    row_spec = lambda w: pl.BlockSpec((1, tm, w), lambda bi, i: (bi, i, 0))
    head_blk = (1, SB_HEADS, tm, SB_HEAD_DIM)
    head_spec = pl.BlockSpec(tuple(pl.Element(n) for n in head_blk),
                             lambda bi, i: (bi, 0, pl.multiple_of(head_row0 + i * tm, SUBLANES), 0))
    tab_spec = pl.BlockSpec((tm, RET_DK), lambda bi, i: (i, 0))
    act = jax.ShapeDtypeStruct((b, l, GROUP), BF16)
    heads = jax.ShapeDtypeStruct((b, SB_HEADS, head_row0 + l, SB_HEAD_DIM), F32)
    return pl.pallas_call(
        _project_kernel,
        grid=grid,
        in_specs=[row_spec(d), _const_spec((1, d)), _const_spec((d, IN_WIDTH)), tab_spec, tab_spec],
        out_specs=[row_spec(GROUP)] * 3 + [head_spec] * 2 + [row_spec(GROUP)] * 4,
        out_shape=[act] * 3 + [heads] * 2 + [act] * 4,
        compiler_params=pltpu.CompilerParams(
            dimension_semantics=("arbitrary", "arbitrary"), vmem_limit_bytes=VMEM_LIMIT),
        name="project",
    )(x, g_norm.reshape(1, d), w_in_bf, cos2, sin2)


def _split_bf16(x):
    hi = x.astype(BF16)
    lo = (x - hi.astype(F32)).astype(BF16)
    return hi, lo


def _attention_kernel(q_ref, kc_ref, vc_ref, kp_ref, vp_ref, bdu_ref, bdo_ref, j_ref, g_ref, o_ref,
                      kxc_ref, vxc_ref, kxp_ref, vxp_ref, carry_ref, acc_ref,
                      *, tq, n_q, n_past, past_valid):
    tk = SB_KEYS
    per_tile = tq // tk
    head0 = lax.broadcasted_iota(jnp.int32, (tk, LANES), 1) < SB_HEAD_DIM
    key_in_block = lax.broadcasted_iota(jnp.int32, (tq, 2 * tk), 1) & (tk - 1)
    delta = key_in_block - lax.broadcasted_iota(jnp.int32, (tq, 2 * tk), 0)

    def expand(src_ref, dst_ref, n_blocks):
        def body(j, _):
            blk = src_ref[0, pl.ds(pl.multiple_of(j * tk, tk), tk), :]
            zero = jnp.zeros_like(blk)
            dst_ref[j, 0:tk, :] = jnp.where(head0, blk, zero)
            dst_ref[j, tk:2 * tk, :] = jnp.where(head0, zero, blk)
            return 0
        lax.fori_loop(0, n_blocks, body, 0)

    expand(kc_ref, kxc_ref, n_q * per_tile)
    expand(vc_ref, vxc_ref, n_q * per_tile)
    if n_past > 0:
        expand(kp_ref, kxp_ref, n_past)
        expand(vp_ref, vxp_ref, n_past)

    def add_blocks(kx_ref, vx_ref, i, j_last, masks, r0=0):
        q2 = q_ref[0, pl.ds(pl.multiple_of(i * tq + r0, tk), tq - r0), :]
        carry = carry_ref[r0:, :]
        for u, mask in enumerate(masks):
            j = j_last - u
            z = lax.dot_general(q2, kx_ref[j], (((1,), (1,)), ((), ())), preferred_element_type=F32)
            neg_abs = lax.bitcast_convert_type(
                lax.bitcast_convert_type(z, jnp.uint32) | jnp.uint32(0x80000000), F32)
            sp = jnp.maximum(z, 0.0) + jnp.log2(1.0 + jnp.exp2(neg_abs))
            spm = sp if mask is None else jnp.where(mask, sp, 0.0)
            hi = spm.astype(BF16)
            later = jnp.dot(hi, bdu_ref[...], preferred_element_type=F32)
            total = jnp.dot(hi, bdo_ref[...], preferred_element_type=F32)
            a = jnp.exp2((z - sp) - (later + carry))
            if mask is not None:
                a = jnp.where(mask, a, 0.0)
            acc_ref[r0:, :] += jnp.dot(a.astype(BF16), vx_ref[j], preferred_element_type=F32)
            carry = carry + total
        carry_ref[r0:, :] = carry
        return carry

    def stick_left(carry):
        return (jnp.min(carry) < SB_EXHAUSTED).astype(jnp.int32)

    def sweep(kx_ref, vx_ref, i, j_top, n_steps, unroll, alive):
        def cond(state):
            m, live = state
            return jnp.logical_and(m < n_steps, live > 0)

        def body(state):
            m, _ = state
            carry = add_blocks(kx_ref, vx_ref, i, j_top - m * unroll, [None] * unroll)
            return m + 1, stick_left(carry)

        return lax.while_loop(cond, body, (jnp.int32(0), alive))[1]

    def q_tile(i, _):
        rows = pl.ds(pl.multiple_of(i * tq, tq), tq)
        carry_ref[...] = jnp.zeros_like(carry_ref)
        acc_ref[...] = jnp.zeros_like(acc_ref)

        for jj in range(per_tile - 1, -1, -1):
            r0 = jj * tk
            add_blocks(kxc_ref, vxc_ref, i, i * per_tile + jj, [delta[r0:, :] < -r0], r0)

        unroll = SB_UNROLL if per_tile % SB_UNROLL == 0 else 1
        alive = sweep(kxc_ref, vxc_ref, i, i * per_tile - 1, (i * per_tile) // unroll, unroll,
                      stick_left(carry_ref[...]))

        if n_past > 0:
            n_full = n_past
            if past_valid < tk:
                @pl.when(alive > 0)
                def _():
                    add_blocks(kxp_ref, vxp_ref, i, n_past - 1, [key_in_block < past_valid])

                n_full = n_past - 1
                if n_full > 0:
                    alive = stick_left(carry_ref[...])
            if n_full // SB_UNROLL > 0:
                alive = sweep(kxp_ref, vxp_ref, i, n_full - 1, n_full // SB_UNROLL, SB_UNROLL, alive)
            if n_full % SB_UNROLL:
                alive = sweep(kxp_ref, vxp_ref, i, n_full % SB_UNROLL - 1, n_full % SB_UNROLL, 1, alive)

        o = acc_ref[...]
        hi, lo = _split_bf16(o * o)
        jm = j_ref[...]
        ms = (jnp.dot(hi, jm, preferred_element_type=F32)
              + jnp.dot(lo, jm, preferred_element_type=F32)) * (1.0 / SB_HEAD_DIM)
        y = o * lax.rsqrt(ms + EPS) * g_ref[...]
        o_ref[0, rows, :] = y.astype(BF16)
        return 0

    lax.fori_loop(0, n_q, q_tile, 0)


def _attention(q, k_cur, v_cur, k_past, v_past, past_len, g_sb_out):
    b, lq, _ = q.shape
    tk = SB_KEYS
    tq = min(SB_QUERY_TILE, lq)
    assert lq % tq == 0 and tq % tk == 0
    if k_past is None:
        k_past = jnp.zeros((1, tk, SB_WIDTH), BF16)
        v_past = jnp.zeros((1, tk, SB_WIDTH), BF16)
        n_past, past_valid = 0, tk
    else:
        assert k_past.shape[1] % tk == 0
        n_past = -(-past_len // tk)
        past_valid = past_len - (n_past - 1) * tk
        k_past = k_past[:, :n_past * tk]
        v_past = v_past[:, :n_past * tk]
    bp, p, _ = k_past.shape
    jj = jnp.arange(2 * tk)
    same_head = (jj[:, None] // tk) == (jj[None, :] // tk)
    bdu = (same_head & (jj[:, None] > jj[None, :])).astype(BF16)
    bdo = same_head.astype(BF16)
    ll = jnp.arange(LANES) // SB_HEAD_DIM
    jm = (ll[:, None] == ll[None, :]).astype(BF16)
    cur_spec = pl.BlockSpec((1, lq, PAIR), lambda bi, hp: (bi, 0, hp))
    past_spec = pl.BlockSpec((1, p, PAIR), (lambda bi, hp: (bi, 0, hp)) if bp > 1
                             else (lambda bi, hp: (0, 0, hp)))
    kernel = functools.partial(_attention_kernel, tq=tq, n_q=lq // tq, n_past=n_past,
                               past_valid=past_valid)
    stacked = lambda n: pltpu.VMEM((n, 2 * tk, PAIR), BF16)
    return pl.pallas_call(
        kernel,
        grid=(b, SB_HEADS // 2),
        in_specs=[cur_spec, cur_spec, cur_spec, past_spec, past_spec,
                  _const_spec(bdu.shape), _const_spec(bdo.shape), _const_spec(jm.shape),
                  pl.BlockSpec((1, PAIR), lambda bi, hp: (0, hp))],
        out_specs=cur_spec,
        out_shape=jax.ShapeDtypeStruct((b, lq, SB_WIDTH), BF16),
        scratch_shapes=[stacked(lq // tk), stacked(lq // tk), stacked(p // tk), stacked(p // tk),
                        pltpu.VMEM((tq, 2 * tk), F32), pltpu.VMEM((tq, PAIR), F32)],
        compiler_params=pltpu.CompilerParams(
            dimension_semantics=("arbitrary", "arbitrary"), vmem_limit_bytes=VMEM_LIMIT),
        name="attention",
    )(q, k_cur, v_cur, k_past, v_past, bdu, bdo, jm, g_sb_out.reshape(1, SB_WIDTH))


def _retention_kernel(q_ref, k_ref, v_ref, gate_ref, s0_ref, g_ref, o_ref, s_ref, *, chunk, n_chunks):
    c = chunk
    head = pl.program_id(1).astype(F32)
    log_g = jnp.log(1.0 - jnp.exp2(jnp.full((1, LANES), -5.0, F32) - head))
    n_row = lax.broadcasted_iota(jnp.int32, (c, LANES), 0).astype(F32)
    q_decay = jnp.exp((n_row + 1.0) * log_g)
    k_decay = jnp.exp((c - 1.0 - n_row) * log_g)
    s_decay = jnp.exp(float(c) * log_g)
    diff = (lax.broadcasted_iota(jnp.int32, (c, c), 0)
            - lax.broadcasted_iota(jnp.int32, (c, c), 1))
    log_g_cc = log_g if c == LANES else jnp.log(1.0 - jnp.exp2(jnp.full((1, c), -5.0, F32) - head))
    decay = jnp.where(diff >= 0, jnp.exp(jnp.maximum(diff, 0).astype(F32) * log_g_cc), 0.0)
    gain = g_ref[...]

    def body(ci, s):
        rows = pl.ds(pl.multiple_of(ci * c, c), c)
        q = q_ref[0, rows, :]
        k = k_ref[0, rows, :]
        v = v_ref[0, rows, :]
        qk = lax.dot_general(q, k, (((1,), (1,)), ((), ())), preferred_element_type=F32)
        inner = jnp.dot((qk * decay).astype(BF16), v, preferred_element_type=F32)
        cross = jnp.dot(q, s.astype(BF16), preferred_element_type=F32) * q_decay
        o = inner + cross
        k_dec = (k.astype(F32) * k_decay).astype(BF16)
        s_new = s_decay * s + lax.dot_general(k_dec, v, (((0,), (0,)), ((), ())),
                                              preferred_element_type=F32)
        y = o * lax.rsqrt(jnp.mean(o * o, axis=-1, keepdims=True) + EPS) * gain
        gate = gate_ref[0, rows, :].astype(F32)
        o_ref[0, rows, :] = (y * (gate * jax.nn.sigmoid(gate))).astype(BF16)
        return s_new

    s_ref[0, 0] = lax.fori_loop(0, n_chunks, body, s0_ref[0, 0])


def _retention(rq, rk, rv, rgate, s0, g_ret_out, chunk):
    b, l, _ = rq.shape
    assert l % chunk == 0
    bs = s0.shape[0]
    seq_spec = pl.BlockSpec((1, l, RET_DK), lambda bi, hh: (bi, 0, hh))
    state_spec = pl.BlockSpec((1, 1, RET_DK, RET_DV), lambda bi, hh: (bi, hh, 0, 0))
    s0_spec = state_spec if bs > 1 else pl.BlockSpec((1, 1, RET_DK, RET_DV), lambda bi, hh: (0, hh, 0, 0))
    kernel = functools.partial(_retention_kernel, chunk=chunk, n_chunks=l // chunk)
    return pl.pallas_call(
        kernel,
        grid=(b, RET_HEADS),
        in_specs=[seq_spec, seq_spec, seq_spec, seq_spec, s0_spec,
                  pl.BlockSpec((1, RET_DV), lambda bi, hh: (0, hh))],
        out_specs=[seq_spec, state_spec],
        out_shape=[jax.ShapeDtypeStruct((b, l, RET_WIDTH), BF16),
                   jax.ShapeDtypeStruct((b, RET_HEADS, RET_DK, RET_DV), F32)],
        compiler_params=pltpu.CompilerParams(
            dimension_semantics=("arbitrary", "arbitrary"), vmem_limit_bytes=VMEM_LIMIT),
        name="retention",
    )(rq, rk, rv, rgate, s0, g_ret_out.reshape(1, RET_WIDTH))


def _merge_ffn_kernel(x_ref, sb_ref, ret_ref, conv0_ref, wo_ref, gf_ref, wug_ref, wuv_ref,
                      cw_ref, cb_ref, wd_ref, gl_ref, y_ref, conv_ref,
                      h_ref, hn_ref, acc_ref, ubuf_ref, act_ref, carry_ref, *, tm):
    i = pl.program_id(1)

    @pl.when(i == 0)
    def _():
        carry_ref[...] = conv0_ref[0]

    mixed = jnp.concatenate([sb_ref[0], ret_ref[0]], axis=1)
    h = x_ref[0] + jnp.dot(mixed, wo_ref[...], preferred_element_type=F32)
    h_ref[...] = h
    ms = jnp.mean(h * h, axis=-1, keepdims=True)
    hn_ref[...] = (h * lax.rsqrt(ms + EPS) * gf_ref[...]).astype(BF16)
    acc_ref[...] = jnp.zeros_like(acc_ref)
    lo = SUBLANES - (CONV_W - 1)

    def up_project(j):
        ubuf = ubuf_ref.at[j % 2]
        hn = hn_ref[...]
        ubuf[0:SUBLANES, :] = carry_ref[j]
        ubuf[SUBLANES:, 0:FF_CHUNK] = jnp.dot(hn, wug_ref[j], preferred_element_type=F32)
        ubuf[SUBLANES:, FF_CHUNK:] = jnp.dot(hn, wuv_ref[j], preferred_element_type=F32)
        carry_ref[j] = ubuf[tm:tm + SUBLANES, :]

    def gated_conv(j):
        ubuf = ubuf_ref.at[j % 2]
        act = act_ref.at[j % 2]
        cw = cw_ref[j]
        cb = cb_ref[j]
        strip = min(tm, FF_STRIP)
        for r in range(0, tm, strip):
            c = cb
            for tap in range(CONV_W):
                c = c + cw[tap:tap + 1, :] * ubuf[lo + tap + r:lo + tap + r + strip, :]
            gate = c[:, :FF_CHUNK]
            act[r:r + strip, :] = ((gate * jax.nn.sigmoid(gate)) * c[:, FF_CHUNK:]).astype(BF16)

    up_project(0)
    for j in range(N_FF_CHUNKS):
        if j + 1 < N_FF_CHUNKS:
            up_project(j + 1)
        gated_conv(j)
        acc_ref[...] += jnp.dot(act_ref[j % 2], wd_ref[j], preferred_element_type=F32)

    hh = h_ref[...] + acc_ref[...]
    ms2 = jnp.mean(hh * hh, axis=-1, keepdims=True)
    y_ref[0] = hh * lax.rsqrt(ms2 + EPS) * gl_ref[...]

    @pl.when(i == pl.num_programs(1) - 1)
    def _():
        conv_ref[0] = carry_ref[...]


def _merge_ffn(x, sb_n, ret_n, conv0, w_out_bf, g_norm_ffn, wug, wuv, cw, cb, wd, g_norm_final, tm):
    b, l, d = x.shape
    bs = conv0.shape[0]
    row_spec = lambda w: pl.BlockSpec((1, tm, w), lambda bi, i: (bi, i, 0))
    conv_shape = (1, N_FF_CHUNKS, SUBLANES, 2 * FF_CHUNK)
    conv_spec = pl.BlockSpec(conv_shape, lambda bi, i: (bi, 0, 0, 0))
    conv0_spec = conv_spec if bs > 1 else pl.BlockSpec(conv_shape, lambda bi, i: (0, 0, 0, 0))
    kernel = functools.partial(_merge_ffn_kernel, tm=tm)
    return pl.pallas_call(
        kernel,
        grid=(b, l // tm),
        in_specs=[row_spec(d), row_spec(SB_WIDTH), row_spec(RET_WIDTH), conv0_spec,
                  _const_spec(w_out_bf.shape), _const_spec((1, d)),
                  _const_spec(wug.shape), _const_spec(wuv.shape),
                  _const_spec(cw.shape), _const_spec(cb.shape), _const_spec(wd.shape),
                  _const_spec((1, d))],
        out_specs=[row_spec(d), conv_spec],
        out_shape=[jax.ShapeDtypeStruct((b, l, d), F32),
                   jax.ShapeDtypeStruct((b,) + conv_shape[1:], F32)],
        scratch_shapes=[pltpu.VMEM((tm, d), F32), pltpu.VMEM((tm, d), BF16), pltpu.VMEM((tm, d), F32),
                        pltpu.VMEM((2, tm + SUBLANES, 2 * FF_CHUNK), F32),
                        pltpu.VMEM((2, tm, FF_CHUNK), BF16),
                        pltpu.VMEM(conv_shape[1:], F32)],
        compiler_params=pltpu.CompilerParams(
            dimension_semantics=("arbitrary", "arbitrary"), vmem_limit_bytes=VMEM_LIMIT),
        name="merge_ffn",
    )(x, sb_n, ret_n, conv0, w_out_bf, g_norm_ffn.reshape(1, d), wug, wuv, cw, cb, wd,
      g_norm_final.reshape(1, d))


def _rope_tables(pos):
    half = RET_DK // 2
    inv = ROPE_BASE ** (-jnp.arange(half, dtype=F32) / half)
    ang = pos.astype(F32)[:, None] * inv[None, :]
    cos, sin = jnp.cos(ang), jnp.sin(ang)
    return jnp.concatenate([cos, cos], axis=1), jnp.concatenate([-sin, sin], axis=1)


def _conv_state_to_chunks(state):
    b = state.shape[0]
    s = state.reshape(b, CONV_W - 1, 2, N_FF_CHUNKS, FF_CHUNK).transpose(0, 3, 1, 2, 4)
    s = s.reshape(b, N_FF_CHUNKS, CONV_W - 1, 2 * FF_CHUNK)
    return jnp.pad(s, ((0, 0), (0, 0), (SUBLANES - (CONV_W - 1), 0), (0, 0)))


def _conv_state_from_chunks(chunks):
    b = chunks.shape[0]
    s = chunks[:, :, SUBLANES - (CONV_W - 1):, :].reshape(b, N_FF_CHUNKS, CONV_W - 1, 2, FF_CHUNK)
    return s.transpose(0, 2, 3, 1, 4).reshape(b, CONV_W - 1, 2 * D_FF)


def _pad_rows(a, rows):
    return a if a.shape[1] == rows else jnp.pad(a, ((0, 0), (0, rows - a.shape[1]), (0, 0)))


def _prefix_rows_kernel(k_any, v_any, k_rows_ref, v_rows_ref, k_ref, v_ref):
    del k_any, v_any
    k_ref[...] = k_rows_ref[...]
    v_ref[...] = v_rows_ref[...]


def _write_prefix_rows(k_big, v_big, k_rows, v_rows):
    b = k_big.shape[0]
    blk = (1,) + k_rows.shape[1:]
    any_spec = pl.BlockSpec(memory_space=pl.ANY)
    rows_spec = pl.BlockSpec(blk, lambda bi: (0, 0, 0, 0))
    out_spec = pl.BlockSpec(blk, lambda bi: (bi, 0, 0, 0))
    big = jax.ShapeDtypeStruct(k_big.shape, k_big.dtype)
    return pl.pallas_call(
        _prefix_rows_kernel,
        grid=(b,),
        in_specs=[any_spec, any_spec, rows_spec, rows_spec],
        out_specs=[out_spec, out_spec],
        out_shape=[big, big],
        input_output_aliases={0: 0, 1: 1},
        name="prefix_rows",
    )(k_big, v_big, k_rows, v_rows)


def _stream_step(x, pos0, k_past, v_past, past_len, s0, conv0, wts, tm, chunk, head_row0=0):
    l = x.shape[1]
    cos2, sin2 = _rope_tables(pos0 + jnp.arange(l))
    q, k, v, k_out, v_out, rq, rk, rv, rgate = _project(x, wts["g_norm_mix"], wts["w_in"], cos2, sin2,
                                                       tm, head_row0)
    lq = -(-l // SB_KEYS) * SB_KEYS
    sb_n = _attention(_pad_rows(q, lq), _pad_rows(k, lq), _pad_rows(v, lq),
                      k_past, v_past, past_len, wts["g_sb_out"])[:, :l]
    ret_n, s_new = _retention(rq, rk, rv, rgate, s0, wts["g_ret_out"], chunk)
    y, conv_new = _merge_ffn(x, sb_n, ret_n, conv0, wts["w_out"], wts["g_norm_ffn"], wts["wug"],
                             wts["wuv"], wts["cw"], wts["cb"], wts["wd"], wts["g_norm_final"], tm)
    return y, k, v, k_out, v_out, s_new, conv_new


def kernel(x_prompt, x_sample, cache_sb_k, cache_sb_v, state_ret, state_conv, meta_tokens, g_norm_mix, w_in, g_sb_out, g_ret_out, w_out, g_norm_ffn, w_up, conv_w, conv_b, w_down, g_norm_final):
    b, seq, d = x_prompt.shape
    bd, ls, _ = x_sample.shape
    past = cache_sb_k.shape[2]

    def ff_cols(a):
        r = a.shape[0]
        return a.reshape(r, 2, N_FF_CHUNKS, FF_CHUNK).transpose(2, 0, 1, 3).reshape(N_FF_CHUNKS, r, 2 * FF_CHUNK)

    w_up_bf = w_up.astype(BF16)
    wts = dict(
        g_norm_mix=g_norm_mix, g_sb_out=g_sb_out, g_ret_out=g_ret_out, g_norm_ffn=g_norm_ffn,
        g_norm_final=g_norm_final,
        w_in=w_in.astype(BF16), w_out=w_out.astype(BF16),
        wug=w_up_bf[:, :D_FF].reshape(d, N_FF_CHUNKS, FF_CHUNK).transpose(1, 0, 2),
        wuv=w_up_bf[:, D_FF:].reshape(d, N_FF_CHUNKS, FF_CHUNK).transpose(1, 0, 2),
        cw=ff_cols(conv_w), cb=ff_cols(conv_b.reshape(1, 2 * D_FF)),
        wd=w_down.astype(BF16).reshape(N_FF_CHUNKS, FF_CHUNK, d),
    )

    zero_state = jnp.zeros((1, RET_HEADS, RET_DK, RET_DV), F32)
    zero_conv = jnp.zeros((1, N_FF_CHUNKS, SUBLANES, 2 * FF_CHUNK), F32)
    _, k_m, v_m, k_m_out, v_m_out, s_meta, conv_meta = _stream_step(
        meta_tokens[None], -N_META, None, None, 0, zero_state, zero_conv, wts, N_META, N_META)

    y_prompt, _, _, k_p_out, v_p_out, s_prompt, conv_prompt = _stream_step(
        x_prompt, 0, _pad_rows(k_m, SB_KEYS), _pad_rows(v_m, SB_KEYS), N_META, s_meta, conv_meta,
        wts, 512, 256, head_row0=N_META)
    new_k_prompt, new_v_prompt = _write_prefix_rows(k_p_out, v_p_out, k_m_out, v_m_out)

    to_rows = lambda c: c.transpose(0, 2, 1, 3).reshape(bd, past, SB_WIDTH).astype(BF16)
    y_sample, _, _, k_s_out, v_s_out, s_sample, conv_sample = _stream_step(
        x_sample, past, to_rows(cache_sb_k), to_rows(cache_sb_v), past, state_ret,
        _conv_state_to_chunks(state_conv), wts, ls, ls)

    return (y_prompt, y_sample, new_k_prompt, new_v_prompt, s_prompt,
            _conv_state_from_chunks(conv_prompt), k_s_out, v_s_out, s_sample,
            _conv_state_from_chunks(conv_sample))
```

```python
import functools
import math

import jax
import jax.numpy as jnp
from jax import lax
from jax.experimental import pallas as pl
from jax.experimental.pallas import tpu as pltpu

D_MODEL = 1024
N_META = 16
SB_HEADS = 8
SB_HEAD_DIM = 64
SB_WIDTH = SB_HEADS * SB_HEAD_DIM
RET_HEADS = 4
RET_DK = 128
RET_DV = 128
RET_WIDTH = RET_HEADS * RET_DV
MIX_WIDTH = SB_WIDTH + RET_WIDTH
GROUP = 512
N_GROUPS = 7
IN_WIDTH = N_GROUPS * GROUP
D_FF = 2816
CONV_W = 3
ROPE_BASE = 10000.0
EPS = 1e-5

LANES = 128
SUBLANES = 8
FF_CHUNK = 256
N_FF_CHUNKS = D_FF // FF_CHUNK
FF_STRIP = 64
SB_KEYS = 128
SB_QUERY_TILE = 512
SB_UNROLL = 2
PAIR = 2 * SB_HEAD_DIM
SB_EXHAUSTED = 152.0
SB_Q_SCALE = SB_HEAD_DIM ** -0.5 * math.log2(math.e)
VMEM_LIMIT = 56 * 1024 * 1024

BF16 = jnp.bfloat16
F32 = jnp.float32


def _const_spec(shape):
    zeros = (0,) * len(shape)
    return pl.BlockSpec(shape, lambda *_: zeros, pipeline_mode=pl.Buffered(1))


def _project_kernel(x_ref, g_ref, w_ref, cos_ref, sin_ref,
                    q_ref, k_ref, v_ref, ko_ref, vo_ref, rq_ref, rk_ref, rv_ref, rg_ref):
    x = x_ref[0]
    ms = jnp.mean(x * x, axis=-1, keepdims=True)
    h = (x * lax.rsqrt(ms + EPS) * g_ref[...]).astype(BF16)

    def group(i):
        return jnp.dot(h, w_ref[:, i * GROUP:(i + 1) * GROUP], preferred_element_type=F32)

    def split_heads(p, out_ref):
        for hh in range(SB_HEADS):
            out_ref[0, hh] = p[:, hh * SB_HEAD_DIM:(hh + 1) * SB_HEAD_DIM]

    def rope(p, out_ref, scale):
        cos = cos_ref[...]
        sin = sin_ref[...]
        for hh in range(RET_HEADS):
            t = p[:, hh * RET_DK:(hh + 1) * RET_DK]
            r = t * cos + pltpu.roll(t, RET_DK // 2, 1) * sin
            if scale is not None:
                r = r * scale
            out_ref[0, :, hh * RET_DK:(hh + 1) * RET_DK] = r.astype(BF16)

    q_ref[0] = (group(0) * SB_Q_SCALE).astype(BF16)
    pk = group(1)
    k_ref[0] = pk.astype(BF16)
    split_heads(pk, ko_ref)
    pv = group(2)
    v_ref[0] = pv.astype(BF16)
    split_heads(pv, vo_ref)
    rope(group(3), rq_ref, None)
    rope(group(4), rk_ref, RET_DK ** -0.5)
    rv_ref[0] = group(5).astype(BF16)
    rg_ref[0] = group(6).astype(BF16)


def _project(x, g_norm, w_in_bf, cos2, sin2, tm, head_row0=0):
    b, l, d = x.shape
    grid = (b, l // tm)
    row_spec = lambda w: pl.BlockSpec((1, tm, w), lambda bi, i: (bi, i, 0))
    head_blk = (1, SB_HEADS, tm, SB_HEAD_DIM)
    head_spec = pl.BlockSpec(tuple(pl.Element(n) for n in head_blk),
                             lambda bi, i: (bi, 0, pl.multiple_of(head_row0 + i * tm, SUBLANES), 0))
    tab_spec = pl.BlockSpec((tm, RET_DK), lambda bi, i: (i, 0))
    act = jax.ShapeDtypeStruct((b, l, GROUP), BF16)
    heads = jax.ShapeDtypeStruct((b, SB_HEADS, head_row0 + l, SB_HEAD_DIM), F32)
    return pl.pallas_call(
        _project_kernel,
        grid=grid,
        in_specs=[row_spec(d), _const_spec((1, d)), _const_spec((d, IN_WIDTH)), tab_spec, tab_spec],
        out_specs=[row_spec(GROUP)] * 3 + [head_spec] * 2 + [row_spec(GROUP)] * 4,
        out_shape=[act] * 3 + [heads] * 2 + [act] * 4,
        compiler_params=pltpu.CompilerParams(
            dimension_semantics=("arbitrary", "arbitrary"), vmem_limit_bytes=VMEM_LIMIT),
        name="project",
    )(x, g_norm.reshape(1, d), w_in_bf, cos2, sin2)


def _split_bf16(x):
    hi = x.astype(BF16)
    lo = (x - hi.astype(F32)).astype(BF16)
    return hi, lo


def _attention_kernel(q_ref, kc_ref, vc_ref, kp_ref, vp_ref, bdu_ref, bdo_ref, j_ref, g_ref, o_ref,
                      kxc_ref, vxc_ref, kxp_ref, vxp_ref, carry_ref, acc_ref,
                      *, tq, n_q, n_past, past_valid):
    tk = SB_KEYS
    per_tile = tq // tk
    head0 = lax.broadcasted_iota(jnp.int32, (tk, LANES), 1) < SB_HEAD_DIM
    key_in_block = lax.broadcasted_iota(jnp.int32, (tq, 2 * tk), 1) & (tk - 1)
    delta = key_in_block - lax.broadcasted_iota(jnp.int32, (tq, 2 * tk), 0)

    def expand(src_ref, dst_ref, n_blocks):
        def body(j, _):
            blk = src_ref[0, pl.ds(pl.multiple_of(j * tk, tk), tk), :]
            zero = jnp.zeros_like(blk)
            dst_ref[j, 0:tk, :] = jnp.where(head0, blk, zero)
            dst_ref[j, tk:2 * tk, :] = jnp.where(head0, zero, blk)
            return 0
        lax.fori_loop(0, n_blocks, body, 0)

    expand(kc_ref, kxc_ref, n_q * per_tile)
    expand(vc_ref, vxc_ref, n_q * per_tile)
    if n_past > 0:
        expand(kp_ref, kxp_ref, n_past)
        expand(vp_ref, vxp_ref, n_past)

    def add_blocks(kx_ref, vx_ref, i, j_last, masks, r0=0):
        q2 = q_ref[0, pl.ds(pl.multiple_of(i * tq + r0, tk), tq - r0), :]
        carry = carry_ref[r0:, :]
        for u, mask in enumerate(masks):
            j = j_last - u
            z = lax.dot_general(q2, kx_ref[j], (((1,), (1,)), ((), ())), preferred_element_type=F32)
            neg_abs = lax.bitcast_convert_type(
                lax.bitcast_convert_type(z, jnp.uint32) | jnp.uint32(0x80000000), F32)
            sp = jnp.maximum(z, 0.0) + jnp.log2(1.0 + jnp.exp2(neg_abs))
            spm = sp if mask is None else jnp.where(mask, sp, 0.0)
            hi = spm.astype(BF16)
            later = jnp.dot(hi, bdu_ref[...], preferred_element_type=F32)
            total = jnp.dot(hi, bdo_ref[...], preferred_element_type=F32)
            a = jnp.exp2((z - sp) - (later + carry))
            if mask is not None:
                a = jnp.where(mask, a, 0.0)
            acc_ref[r0:, :] += jnp.dot(a.astype(BF16), vx_ref[j], preferred_element_type=F32)
            carry = carry + total
        carry_ref[r0:, :] = carry
        return carry

    def stick_left(carry):
        return (jnp.min(carry) < SB_EXHAUSTED).astype(jnp.int32)

    def sweep(kx_ref, vx_ref, i, j_top, n_steps, unroll, alive):
        def cond(state):
            m, live = state
            return jnp.logical_and(m < n_steps, live > 0)

        def body(state):
            m, _ = state
            carry = add_blocks(kx_ref, vx_ref, i, j_top - m * unroll, [None] * unroll)
            return m + 1, stick_left(carry)

        return lax.while_loop(cond, body, (jnp.int32(0), alive))[1]

    def q_tile(i, _):
        rows = pl.ds(pl.multiple_of(i * tq, tq), tq)
        carry_ref[...] = jnp.zeros_like(carry_ref)
        acc_ref[...] = jnp.zeros_like(acc_ref)

        for jj in range(per_tile - 1, -1, -1):
            r0 = jj * tk
            add_blocks(kxc_ref, vxc_ref, i, i * per_tile + jj, [delta[r0:, :] < -r0], r0)

        unroll = SB_UNROLL if per_tile % SB_UNROLL == 0 else 1
        alive = sweep(kxc_ref, vxc_ref, i, i * per_tile - 1, (i * per_tile) // unroll, unroll,
                      stick_left(carry_ref[...]))

        if n_past > 0:
            n_full = n_past
            if past_valid < tk:
                @pl.when(alive > 0)
                def _():
                    add_blocks(kxp_ref, vxp_ref, i, n_past - 1, [key_in_block < past_valid])

                n_full = n_past - 1
                if n_full > 0:
                    alive = stick_left(carry_ref[...])
            if n_full // SB_UNROLL > 0:
                alive = sweep(kxp_ref, vxp_ref, i, n_full - 1, n_full // SB_UNROLL, SB_UNROLL, alive)
            if n_full % SB_UNROLL:
                alive = sweep(kxp_ref, vxp_ref, i, n_full % SB_UNROLL - 1, n_full % SB_UNROLL, 1, alive)

        o = acc_ref[...]
        hi, lo = _split_bf16(o * o)
        jm = j_ref[...]
        ms = (jnp.dot(hi, jm, preferred_element_type=F32)
              + jnp.dot(lo, jm, preferred_element_type=F32)) * (1.0 / SB_HEAD_DIM)
        y = o * lax.rsqrt(ms + EPS) * g_ref[...]
        o_ref[0, rows, :] = y.astype(BF16)
        return 0

    lax.fori_loop(0, n_q, q_tile, 0)


def _attention(q, k_cur, v_cur, k_past, v_past, past_len, g_sb_out):
    b, lq, _ = q.shape
    tk = SB_KEYS
    tq = min(SB_QUERY_TILE, lq)
    assert lq % tq == 0 and tq % tk == 0
    if k_past is None:
        k_past = jnp.zeros((1, tk, SB_WIDTH), BF16)
        v_past = jnp.zeros((1, tk, SB_WIDTH), BF16)
        n_past, past_valid = 0, tk
    else:
        assert k_past.shape[1] % tk == 0
        n_past = -(-past_len // tk)
        past_valid = past_len - (n_past - 1) * tk
        k_past = k_past[:, :n_past * tk]
        v_past = v_past[:, :n_past * tk]
    bp, p, _ = k_past.shape
    jj = jnp.arange(2 * tk)
    same_head = (jj[:, None] // tk) == (jj[None, :] // tk)
    bdu = (same_head & (jj[:, None] > jj[None, :])).astype(BF16)
    bdo = same_head.astype(BF16)
    ll = jnp.arange(LANES) // SB_HEAD_DIM
    jm = (ll[:, None] == ll[None, :]).astype(BF16)
    cur_spec = pl.BlockSpec((1, lq, PAIR), lambda bi, hp: (bi, 0, hp))
    past_spec = pl.BlockSpec((1, p, PAIR), (lambda bi, hp: (bi, 0, hp)) if bp > 1
                             else (lambda bi, hp: (0, 0, hp)))
    kernel = functools.partial(_attention_kernel, tq=tq, n_q=lq // tq, n_past=n_past,
                               past_valid=past_valid)
    stacked = lambda n: pltpu.VMEM((n, 2 * tk, PAIR), BF16)
    return pl.pallas_call(
        kernel,
        grid=(b, SB_HEADS // 2),
        in_specs=[cur_spec, cur_spec, cur_spec, past_spec, past_spec,
                  _const_spec(bdu.shape), _const_spec(bdo.shape), _const_spec(jm.shape),
                  pl.BlockSpec((1, PAIR), lambda bi, hp: (0, hp))],
        out_specs=cur_spec,
        out_shape=jax.ShapeDtypeStruct((b, lq, SB_WIDTH), BF16),
        scratch_shapes=[stacked(lq // tk), stacked(lq // tk), stacked(p // tk), stacked(p // tk),
                        pltpu.VMEM((tq, 2 * tk), F32), pltpu.VMEM((tq, PAIR), F32)],
        compiler_params=pltpu.CompilerParams(
            dimension_semantics=("arbitrary", "arbitrary"), vmem_limit_bytes=VMEM_LIMIT),
        name="attention",
    )(q, k_cur, v_cur, k_past, v_past, bdu, bdo, jm, g_sb_out.reshape(1, SB_WIDTH))


def _retention_kernel(q_ref, k_ref, v_ref, gate_ref, s0_ref, g_ref, o_ref, s_ref, *, chunk, n_chunks):
    c = chunk
    head = pl.program_id(1).astype(F32)
    log_g = jnp.log(1.0 - jnp.exp2(jnp.full((1, LANES), -5.0, F32) - head))
    n_row = lax.broadcasted_iota(jnp.int32, (c, LANES), 0).astype(F32)
    q_decay = jnp.exp((n_row + 1.0) * log_g)
    k_decay = jnp.exp((c - 1.0 - n_row) * log_g)
    s_decay = jnp.exp(float(c) * log_g)
    diff = (lax.broadcasted_iota(jnp.int32, (c, c), 0)
            - lax.broadcasted_iota(jnp.int32, (c, c), 1))
    log_g_cc = log_g if c == LANES else jnp.log(1.0 - jnp.exp2(jnp.full((1, c), -5.0, F32) - head))
    decay = jnp.where(diff >= 0, jnp.exp(jnp.maximum(diff, 0).astype(F32) * log_g_cc), 0.0)
    gain = g_ref[...]

    def body(ci, s):
        rows = pl.ds(pl.multiple_of(ci * c, c), c)
        q = q_ref[0, rows, :]
        k = k_ref[0, rows, :]
        v = v_ref[0, rows, :]
        qk = lax.dot_general(q, k, (((1,), (1,)), ((), ())), preferred_element_type=F32)
        inner = jnp.dot((qk * decay).astype(BF16), v, preferred_element_type=F32)
        cross = jnp.dot(q, s.astype(BF16), preferred_element_type=F32) * q_decay
        o = inner + cross
        k_dec = (k.astype(F32) * k_decay).astype(BF16)
        s_new = s_decay * s + lax.dot_general(k_dec, v, (((0,), (0,)), ((), ())),
                                              preferred_element_type=F32)
        y = o * lax.rsqrt(jnp.mean(o * o, axis=-1, keepdims=True) + EPS) * gain
        gate = gate_ref[0, rows, :].astype(F32)
        o_ref[0, rows, :] = (y * (gate * jax.nn.sigmoid(gate))).astype(BF16)
        return s_new

    s_ref[0, 0] = lax.fori_loop(0, n_chunks, body, s0_ref[0, 0])


def _retention(rq, rk, rv, rgate, s0, g_ret_out, chunk):
    b, l, _ = rq.shape
    assert l % chunk == 0
    bs = s0.shape[0]
    seq_spec = pl.BlockSpec((1, l, RET_DK), lambda bi, hh: (bi, 0, hh))
    state_spec = pl.BlockSpec((1, 1, RET_DK, RET_DV), lambda bi, hh: (bi, hh, 0, 0))
    s0_spec = state_spec if bs > 1 else pl.BlockSpec((1, 1, RET_DK, RET_DV), lambda bi, hh: (0, hh, 0, 0))
    kernel = functools.partial(_retention_kernel, chunk=chunk, n_chunks=l // chunk)
    return pl.pallas_call(
        kernel,
        grid=(b, RET_HEADS),
        in_specs=[seq_spec, seq_spec, seq_spec, seq_spec, s0_spec,
                  pl.BlockSpec((1, RET_DV), lambda bi, hh: (0, hh))],
        out_specs=[seq_spec, state_spec],
        out_shape=[jax.ShapeDtypeStruct((b, l, RET_WIDTH), BF16),
                   jax.ShapeDtypeStruct((b, RET_HEADS, RET_DK, RET_DV), F32)],
        compiler_params=pltpu.CompilerParams(
            dimension_semantics=("arbitrary", "arbitrary"), vmem_limit_bytes=VMEM_LIMIT),
        name="retention",
    )(rq, rk, rv, rgate, s0, g_ret_out.reshape(1, RET_WIDTH))


def _merge_ffn_kernel(x_ref, sb_ref, ret_ref, conv0_ref, wo_ref, gf_ref, wug_ref, wuv_ref,
                      cw_ref, cb_ref, wd_ref, gl_ref, y_ref, conv_ref,
                      h_ref, hn_ref, ubuf_ref, act_ref, carry_ref, *, tm):
    i = pl.program_id(1)

    @pl.when(i == 0)
    def _():
        carry_ref[...] = conv0_ref[0]

    mixed = jnp.concatenate([sb_ref[0], ret_ref[0]], axis=1)
    h = x_ref[0] + jnp.dot(mixed, wo_ref[...], preferred_element_type=F32)
    h_ref[...] = h
    ms = jnp.mean(h * h, axis=-1, keepdims=True)
    hn_ref[...] = (h * lax.rsqrt(ms + EPS) * gf_ref[...]).astype(BF16)
    lo = SUBLANES - (CONV_W - 1)

    def up_project(j):
        ubuf = ubuf_ref.at[j % 2]
        hn = hn_ref[...]
        ubuf[0:SUBLANES, :] = carry_ref[j]
        ubuf[SUBLANES:, 0:FF_CHUNK] = jnp.dot(hn, wug_ref[j], preferred_element_type=F32)
        ubuf[SUBLANES:, FF_CHUNK:] = jnp.dot(hn, wuv_ref[j], preferred_element_type=F32)
        carry_ref[j] = ubuf[tm:tm + SUBLANES, :]

    def gated_conv(j):
        ubuf = ubuf_ref.at[j % 2]
        cw = cw_ref[j]
        cb = cb_ref[j]
        strip = min(tm, FF_STRIP)
        for r in range(0, tm, strip):
            ext = ubuf[r:r + strip + SUBLANES, :]
            c = cb
            for tap in range(CONV_W):
                back = CONV_W - 1 - tap
                rows = ext if back == 0 else pltpu.roll(ext, back, 0)
                c = c + cw[tap:tap + 1, :] * rows[SUBLANES:, :]
            gate = c[:, :FF_CHUNK]
            act_ref[r:r + strip, j * FF_CHUNK:(j + 1) * FF_CHUNK] = (
                (gate * jax.nn.sigmoid(gate)) * c[:, FF_CHUNK:]).astype(BF16)

    up_project(0)
    for j in range(N_FF_CHUNKS):
        if j + 1 < N_FF_CHUNKS:
            up_project(j + 1)
        gated_conv(j)

    hh = h_ref[...] + jnp.dot(act_ref[...], wd_ref[...], preferred_element_type=F32)
    ms2 = jnp.mean(hh * hh, axis=-1, keepdims=True)
    y_ref[0] = hh * lax.rsqrt(ms2 + EPS) * gl_ref[...]

    @pl.when(i == pl.num_programs(1) - 1)
    def _():
        conv_ref[0] = carry_ref[...]


def _merge_ffn(x, sb_n, ret_n, conv0, w_out_bf, g_norm_ffn, wug, wuv, cw, cb, wd, g_norm_final, tm):
    b, l, d = x.shape
    bs = conv0.shape[0]
    row_spec = lambda w: pl.BlockSpec((1, tm, w), lambda bi, i: (bi, i, 0))
    conv_shape = (1, N_FF_CHUNKS, SUBLANES, 2 * FF_CHUNK)
    conv_spec = pl.BlockSpec(conv_shape, lambda bi, i: (bi, 0, 0, 0))
    conv0_spec = conv_spec if bs > 1 else pl.BlockSpec(conv_shape, lambda bi, i: (0, 0, 0, 0))
    kernel = functools.partial(_merge_ffn_kernel, tm=tm)
    return pl.pallas_call(
        kernel,
        grid=(b, l // tm),
        in_specs=[row_spec(d), row_spec(SB_WIDTH), row_spec(RET_WIDTH), conv0_spec,
                  _const_spec(w_out_bf.shape), _const_spec((1, d)),
                  _const_spec(wug.shape), _const_spec(wuv.shape),
                  _const_spec(cw.shape), _const_spec(cb.shape), _const_spec(wd.shape),
                  _const_spec((1, d))],
        out_specs=[row_spec(d), conv_spec],
        out_shape=[jax.ShapeDtypeStruct((b, l, d), F32),
                   jax.ShapeDtypeStruct((b,) + conv_shape[1:], F32)],
        scratch_shapes=[pltpu.VMEM((tm, d), F32), pltpu.VMEM((tm, d), BF16),
                        pltpu.VMEM((2, tm + SUBLANES, 2 * FF_CHUNK), F32),
                        pltpu.VMEM((tm, D_FF), BF16),
                        pltpu.VMEM(conv_shape[1:], F32)],
        compiler_params=pltpu.CompilerParams(
            dimension_semantics=("arbitrary", "arbitrary"), vmem_limit_bytes=VMEM_LIMIT),
        name="merge_ffn",
    )(x, sb_n, ret_n, conv0, w_out_bf, g_norm_ffn.reshape(1, d), wug, wuv, cw, cb, wd,
      g_norm_final.reshape(1, d))


def _rope_tables(pos):
    half = RET_DK // 2
    inv = ROPE_BASE ** (-jnp.arange(half, dtype=F32) / half)
    ang = pos.astype(F32)[:, None] * inv[None, :]
    cos, sin = jnp.cos(ang), jnp.sin(ang)
    return jnp.concatenate([cos, cos], axis=1), jnp.concatenate([-sin, sin], axis=1)


def _conv_state_to_chunks(state):
    b = state.shape[0]
    s = state.reshape(b, CONV_W - 1, 2, N_FF_CHUNKS, FF_CHUNK).transpose(0, 3, 1, 2, 4)
    s = s.reshape(b, N_FF_CHUNKS, CONV_W - 1, 2 * FF_CHUNK)
    return jnp.pad(s, ((0, 0), (0, 0), (SUBLANES - (CONV_W - 1), 0), (0, 0)))


def _conv_state_from_chunks(chunks):
    b = chunks.shape[0]
    s = chunks[:, :, SUBLANES - (CONV_W - 1):, :].reshape(b, N_FF_CHUNKS, CONV_W - 1, 2, FF_CHUNK)
    return s.transpose(0, 2, 3, 1, 4).reshape(b, CONV_W - 1, 2 * D_FF)


def _pad_rows(a, rows):
    return a if a.shape[1] == rows else jnp.pad(a, ((0, 0), (0, rows - a.shape[1]), (0, 0)))


def _prefix_rows_kernel(k_any, v_any, k_rows_ref, v_rows_ref, k_ref, v_ref):
    del k_any, v_any
    k_ref[...] = k_rows_ref[...]
    v_ref[...] = v_rows_ref[...]


def _write_prefix_rows(k_big, v_big, k_rows, v_rows):
    b = k_big.shape[0]
    blk = (1,) + k_rows.shape[1:]
    any_spec = pl.BlockSpec(memory_space=pl.ANY)
    rows_spec = pl.BlockSpec(blk, lambda bi: (0, 0, 0, 0))
    out_spec = pl.BlockSpec(blk, lambda bi: (bi, 0, 0, 0))
    big = jax.ShapeDtypeStruct(k_big.shape, k_big.dtype)
    return pl.pallas_call(
        _prefix_rows_kernel,
        grid=(b,),
        in_specs=[any_spec, any_spec, rows_spec, rows_spec],
        out_specs=[out_spec, out_spec],
        out_shape=[big, big],
        input_output_aliases={0: 0, 1: 1},
        name="prefix_rows",
    )(k_big, v_big, k_rows, v_rows)


def _stream_step(x, pos0, k_past, v_past, past_len, s0, conv0, wts, tm, chunk, head_row0=0):
    l = x.shape[1]
    cos2, sin2 = _rope_tables(pos0 + jnp.arange(l))
    q, k, v, k_out, v_out, rq, rk, rv, rgate = _project(x, wts["g_norm_mix"], wts["w_in"], cos2, sin2,
                                                       tm, head_row0)
    lq = -(-l // SB_KEYS) * SB_KEYS
    sb_n = _attention(_pad_rows(q, lq), _pad_rows(k, lq), _pad_rows(v, lq),
                      k_past, v_past, past_len, wts["g_sb_out"])[:, :l]
    ret_n, s_new = _retention(rq, rk, rv, rgate, s0, wts["g_ret_out"], chunk)
    y, conv_new = _merge_ffn(x, sb_n, ret_n, conv0, wts["w_out"], wts["g_norm_ffn"], wts["wug"],
                             wts["wuv"], wts["cw"], wts["cb"], wts["wd"], wts["g_norm_final"], tm)
    return y, k, v, k_out, v_out, s_new, conv_new


def kernel(x_prompt, x_sample, cache_sb_k, cache_sb_v, state_ret, state_conv, meta_tokens, g_norm_mix, w_in, g_sb_out, g_ret_out, w_out, g_norm_ffn, w_up, conv_w, conv_b, w_down, g_norm_final):
    b, seq, d = x_prompt.shape
    bd, ls, _ = x_sample.shape
    past = cache_sb_k.shape[2]

    def ff_cols(a):
        r = a.shape[0]
        return a.reshape(r, 2, N_FF_CHUNKS, FF_CHUNK).transpose(2, 0, 1, 3).reshape(N_FF_CHUNKS, r, 2 * FF_CHUNK)

    w_up_bf = w_up.astype(BF16)
    wts = dict(
        g_norm_mix=g_norm_mix, g_sb_out=g_sb_out, g_ret_out=g_ret_out, g_norm_ffn=g_norm_ffn,
        g_norm_final=g_norm_final,
        w_in=w_in.astype(BF16), w_out=w_out.astype(BF16),
        wug=w_up_bf[:, :D_FF].reshape(d, N_FF_CHUNKS, FF_CHUNK).transpose(1, 0, 2),
        wuv=w_up_bf[:, D_FF:].reshape(d, N_FF_CHUNKS, FF_CHUNK).transpose(1, 0, 2),
        cw=ff_cols(conv_w), cb=ff_cols(conv_b.reshape(1, 2 * D_FF)),
        wd=w_down.astype(BF16),
    )

    zero_state = jnp.zeros((1, RET_HEADS, RET_DK, RET_DV), F32)
    zero_conv = jnp.zeros((1, N_FF_CHUNKS, SUBLANES, 2 * FF_CHUNK), F32)
    _, k_m, v_m, k_m_out, v_m_out, s_meta, conv_meta = _stream_step(
        meta_tokens[None], -N_META, None, None, 0, zero_state, zero_conv, wts, N_META, N_META)

    y_prompt, _, _, k_p_out, v_p_out, s_prompt, conv_prompt = _stream_step(
        x_prompt, 0, _pad_rows(k_m, SB_KEYS), _pad_rows(v_m, SB_KEYS), N_META, s_meta, conv_meta,
        wts, 512, 256, head_row0=N_META)
    new_k_prompt, new_v_prompt = _write_prefix_rows(k_p_out, v_p_out, k_m_out, v_m_out)

    to_rows = lambda c: c.transpose(0, 2, 1, 3).reshape(bd, past, SB_WIDTH).astype(BF16)
    y_sample, _, _, k_s_out, v_s_out, s_sample, conv_sample = _stream_step(
        x_sample, past, to_rows(cache_sb_k), to_rows(cache_sb_v), past, state_ret,
        _conv_state_to_chunks(state_conv), wts, ls, ls)

    return (y_prompt, y_sample, new_k_prompt, new_v_prompt, s_prompt,
            _conv_state_from_chunks(conv_prompt), k_s_out, v_s_out, s_sample,
            _conv_state_from_chunks(conv_sample))
```

```python
import functools
import math

import jax
import jax.numpy as jnp
from jax import lax
from jax.experimental import pallas as pl
from jax.experimental.pallas import tpu as pltpu

D_MODEL = 1024
N_META = 16
SB_HEADS = 8
SB_HEAD_DIM = 64
SB_WIDTH = SB_HEADS * SB_HEAD_DIM
RET_HEADS = 4
RET_DK = 128
RET_DV = 128
RET_WIDTH = RET_HEADS * RET_DV
MIX_WIDTH = SB_WIDTH + RET_WIDTH
GROUP = 512
N_GROUPS = 7
IN_WIDTH = N_GROUPS * GROUP
D_FF = 2816
CONV_W = 3
ROPE_BASE = 10000.0
EPS = 1e-5

LANES = 128
SUBLANES = 8
FF_CHUNK = 256
N_FF_CHUNKS = D_FF // FF_CHUNK
FF_STRIP = 64
RET_UNROLL = 8
SB_KEYS = 128
SB_QUERY_TILE = 512
SB_UNROLL = 2
PAIR = 2 * SB_HEAD_DIM
SB_EXHAUSTED = 128.0
SB_Q_SCALE = SB_HEAD_DIM ** -0.5 * math.log2(math.e)
VMEM_LIMIT = 56 * 1024 * 1024

BF16 = jnp.bfloat16
F32 = jnp.float32


def _const_spec(shape):
    zeros = (0,) * len(shape)
    return pl.BlockSpec(shape, lambda *_: zeros, pipeline_mode=pl.Buffered(1))


def _project_kernel(x_ref, g_ref, w_ref, cos_ref, sin_ref,
                    q_ref, k_ref, v_ref, ko_ref, vo_ref, rq_ref, rk_ref, rv_ref, rg_ref):
    x = x_ref[0]
    ms = jnp.mean(x * x, axis=-1, keepdims=True)
    h = (x * lax.rsqrt(ms + EPS) * g_ref[...]).astype(BF16)

    def group(i):
        return jnp.dot(h, w_ref[:, i * GROUP:(i + 1) * GROUP], preferred_element_type=F32)

    def split_heads(p, out_ref):
        for hh in range(SB_HEADS):
            out_ref[0, hh] = p[:, hh * SB_HEAD_DIM:(hh + 1) * SB_HEAD_DIM]

    def rope(p, out_ref, scale):
        cos = cos_ref[...]
        sin = sin_ref[...]
        for hh in range(RET_HEADS):
            t = p[:, hh * RET_DK:(hh + 1) * RET_DK]
            r = t * cos + pltpu.roll(t, RET_DK // 2, 1) * sin
            if scale is not None:
                r = r * scale
            out_ref[0, :, hh * RET_DK:(hh + 1) * RET_DK] = r.astype(BF16)

    q_ref[0] = (group(0) * SB_Q_SCALE).astype(BF16)
    pk = group(1)
    k_ref[0] = pk.astype(BF16)
    split_heads(pk, ko_ref)
    pv = group(2)
    v_ref[0] = pv.astype(BF16)
    split_heads(pv, vo_ref)
    rope(group(3), rq_ref, None)
    rope(group(4), rk_ref, RET_DK ** -0.5)
    rv_ref[0] = group(5).astype(BF16)
    rg_ref[0] = group(6).astype(BF16)


def _project(x, g_norm, w_in_bf, cos2, sin2, tm, head_row0=0):
    b, l, d = x.shape
    grid = (b, l // tm)
    row_spec = lambda w: pl.BlockSpec((1, tm, w), lambda bi, i: (bi, i, 0))
    head_blk = (1, SB_HEADS, tm, SB_HEAD_DIM)
    head_spec = pl.BlockSpec(tuple(pl.Element(n) for n in head_blk),
                             lambda bi, i: (bi, 0, pl.multiple_of(head_row0 + i * tm, SUBLANES), 0))
    tab_spec = pl.BlockSpec((tm, RET_DK), lambda bi, i: (i, 0))
    act = jax.ShapeDtypeStruct((b, l, GROUP), BF16)
    heads = jax.ShapeDtypeStruct((b, SB_HEADS, head_row0 + l, SB_HEAD_DIM), F32)
    return pl.pallas_call(
        _project_kernel,
        grid=grid,
        in_specs=[row_spec(d), _const_spec((1, d)), _const_spec((d, IN_WIDTH)), tab_spec, tab_spec],
        out_specs=[row_spec(GROUP)] * 3 + [head_spec] * 2 + [row_spec(GROUP)] * 4,
        out_shape=[act] * 3 + [heads] * 2 + [act] * 4,
        compiler_params=pltpu.CompilerParams(
            dimension_semantics=("arbitrary", "arbitrary"), vmem_limit_bytes=VMEM_LIMIT),
        name="project",
    )(x, g_norm.reshape(1, d), w_in_bf, cos2, sin2)


def _split_bf16(x):
    hi = x.astype(BF16)
    lo = (x - hi.astype(F32)).astype(BF16)
    return hi, lo


def _attention_kernel(q_ref, kc_ref, vc_ref, kp_ref, vp_ref, bdu_ref, bdo_ref, j_ref, g_ref, o_ref,
                      kxc_ref, vxc_ref, kxp_ref, vxp_ref, carry_ref, acc_ref,
                      *, tq, n_q, n_past, past_valid):
    tk = SB_KEYS
    per_tile = tq // tk
    head0 = lax.broadcasted_iota(jnp.int32, (tk, LANES), 1) < SB_HEAD_DIM
    key_in_block = lax.broadcasted_iota(jnp.int32, (tq, 2 * tk), 1) & (tk - 1)
    delta = key_in_block - lax.broadcasted_iota(jnp.int32, (tq, 2 * tk), 0)

    def expand(src_ref, dst_ref, n_blocks):
        def body(j, _):
            blk = src_ref[0, pl.ds(pl.multiple_of(j * tk, tk), tk), :]
            zero = jnp.zeros_like(blk)
            dst_ref[j, 0:tk, :] = jnp.where(head0, blk, zero)
            dst_ref[j, tk:2 * tk, :] = jnp.where(head0, zero, blk)
            return 0
        lax.fori_loop(0, n_blocks, body, 0)

    expand(kc_ref, kxc_ref, n_q * per_tile)
    expand(vc_ref, vxc_ref, n_q * per_tile)
    if n_past > 0:
        expand(kp_ref, kxp_ref, n_past)
        expand(vp_ref, vxp_ref, n_past)

    def add_blocks(kx_ref, vx_ref, i, j_last, masks, r0=0):
        q2 = q_ref[0, pl.ds(pl.multiple_of(i * tq + r0, tk), tq - r0), :]
        carry = carry_ref[r0:, :]
        for u, mask in enumerate(masks):
            j = j_last - u
            z = lax.dot_general(q2, kx_ref[j], (((1,), (1,)), ((), ())), preferred_element_type=F32)
            neg_abs = lax.bitcast_convert_type(
                lax.bitcast_convert_type(z, jnp.uint32) | jnp.uint32(0x80000000), F32)
            sp = jnp.maximum(z, 0.0) + jnp.log2(1.0 + jnp.exp2(neg_abs))
            spm = sp if mask is None else jnp.where(mask, sp, 0.0)
            hi = spm.astype(BF16)
            later = jnp.dot(hi, bdu_ref[...], preferred_element_type=F32)
            total = jnp.dot(hi, bdo_ref[...], preferred_element_type=F32)
            a = jnp.exp2((z - sp) - (later + carry))
            if mask is not None:
                a = jnp.where(mask, a, 0.0)
            acc_ref[r0:, :] += jnp.dot(a.astype(BF16), vx_ref[j], preferred_element_type=F32)
            carry = carry + total
        carry_ref[r0:, :] = carry
        return carry

    def stick_left(carry):
        return (jnp.min(carry) < SB_EXHAUSTED).astype(jnp.int32)

    def sweep(kx_ref, vx_ref, i, j_top, n_steps, unroll, alive):
        def cond(state):
            m, live = state
            return jnp.logical_and(m < n_steps, live > 0)

        def body(state):
            m, _ = state
            carry = add_blocks(kx_ref, vx_ref, i, j_top - m * unroll, [None] * unroll)
            return m + 1, stick_left(carry)

        return lax.while_loop(cond, body, (jnp.int32(0), alive))[1]

    def q_tile(i, _):
        rows = pl.ds(pl.multiple_of(i * tq, tq), tq)
        carry_ref[...] = jnp.zeros_like(carry_ref)
        acc_ref[...] = jnp.zeros_like(acc_ref)

        for jj in range(per_tile - 1, -1, -1):
            r0 = jj * tk
            add_blocks(kxc_ref, vxc_ref, i, i * per_tile + jj, [delta[r0:, :] < -r0], r0)

        unroll = SB_UNROLL if per_tile % SB_UNROLL == 0 else 1
        alive = sweep(kxc_ref, vxc_ref, i, i * per_tile - 1, (i * per_tile) // unroll, unroll,
                      stick_left(carry_ref[...]))

        if n_past > 0:
            n_full = n_past
            if past_valid < tk:
                @pl.when(alive > 0)
                def _():
                    add_blocks(kxp_ref, vxp_ref, i, n_past - 1, [key_in_block < past_valid])

                n_full = n_past - 1
                if n_full > 0:
                    alive = stick_left(carry_ref[...])
            if n_full // SB_UNROLL > 0:
                alive = sweep(kxp_ref, vxp_ref, i, n_full - 1, n_full // SB_UNROLL, SB_UNROLL, alive)
            if n_full % SB_UNROLL:
                alive = sweep(kxp_ref, vxp_ref, i, n_full % SB_UNROLL - 1, n_full % SB_UNROLL, 1, alive)

        o = acc_ref[...]
        hi, lo = _split_bf16(o * o)
        jm = j_ref[...]
        ms = (jnp.dot(hi, jm, preferred_element_type=F32)
              + jnp.dot(lo, jm, preferred_element_type=F32)) * (1.0 / SB_HEAD_DIM)
        y = o * lax.rsqrt(ms + EPS) * g_ref[...]
        o_ref[0, rows, :] = y.astype(BF16)
        return 0

    lax.fori_loop(0, n_q, q_tile, 0)


def _attention(q, k_cur, v_cur, k_past, v_past, past_len, g_sb_out):
    b, lq, _ = q.shape
    tk = SB_KEYS
    tq = min(SB_QUERY_TILE, lq)
    assert lq % tq == 0 and tq % tk == 0
    if k_past is None:
        k_past = jnp.zeros((1, tk, SB_WIDTH), BF16)
        v_past = jnp.zeros((1, tk, SB_WIDTH), BF16)
        n_past, past_valid = 0, tk
    else:
        assert k_past.shape[1] % tk == 0
        n_past = -(-past_len // tk)
        past_valid = past_len - (n_past - 1) * tk
        k_past = k_past[:, :n_past * tk]
        v_past = v_past[:, :n_past * tk]
    bp, p, _ = k_past.shape
    jj = jnp.arange(2 * tk)
    same_head = (jj[:, None] // tk) == (jj[None, :] // tk)
    bdu = (same_head & (jj[:, None] > jj[None, :])).astype(BF16)
    bdo = same_head.astype(BF16)
    ll = jnp.arange(LANES) // SB_HEAD_DIM
    jm = (ll[:, None] == ll[None, :]).astype(BF16)
    cur_spec = pl.BlockSpec((1, lq, PAIR), lambda bi, hp: (bi, 0, hp))
    past_spec = pl.BlockSpec((1, p, PAIR), (lambda bi, hp: (bi, 0, hp)) if bp > 1
                             else (lambda bi, hp: (0, 0, hp)))
    kernel = functools.partial(_attention_kernel, tq=tq, n_q=lq // tq, n_past=n_past,
                               past_valid=past_valid)
    stacked = lambda n: pltpu.VMEM((n, 2 * tk, PAIR), BF16)
    return pl.pallas_call(
        kernel,
        grid=(b, SB_HEADS // 2),
        in_specs=[cur_spec, cur_spec, cur_spec, past_spec, past_spec,
                  _const_spec(bdu.shape), _const_spec(bdo.shape), _const_spec(jm.shape),
                  pl.BlockSpec((1, PAIR), lambda bi, hp: (0, hp))],
        out_specs=cur_spec,
        out_shape=jax.ShapeDtypeStruct((b, lq, SB_WIDTH), BF16),
        scratch_shapes=[stacked(lq // tk), stacked(lq // tk), stacked(p // tk), stacked(p // tk),
                        pltpu.VMEM((tq, 2 * tk), F32), pltpu.VMEM((tq, PAIR), F32)],
        compiler_params=pltpu.CompilerParams(
            dimension_semantics=("arbitrary", "arbitrary"), vmem_limit_bytes=VMEM_LIMIT),
        name="attention",
    )(q, k_cur, v_cur, k_past, v_past, bdu, bdo, jm, g_sb_out.reshape(1, SB_WIDTH))


def _retention_kernel(q_ref, k_ref, v_ref, gate_ref, s0_ref, g_ref, o_ref, s_ref, *, chunk, n_chunks):
    c = chunk
    head = pl.program_id(1).astype(F32)
    log_g = jnp.log(1.0 - jnp.exp2(jnp.full((1, LANES), -5.0, F32) - head))
    n_row = lax.broadcasted_iota(jnp.int32, (c, LANES), 0).astype(F32)
    q_decay = jnp.exp((n_row + 1.0) * log_g)
    k_decay = jnp.exp((c - 1.0 - n_row) * log_g)
    s_decay = jnp.exp(float(c) * log_g)
    diff = (lax.broadcasted_iota(jnp.int32, (c, c), 0)
            - lax.broadcasted_iota(jnp.int32, (c, c), 1))
    log_g_cc = log_g if c == LANES else jnp.log(1.0 - jnp.exp2(jnp.full((1, c), -5.0, F32) - head))
    decay = jnp.where(diff >= 0, jnp.exp(jnp.maximum(diff, 0).astype(F32) * log_g_cc), 0.0)
    gain = g_ref[...]

    def body(ci, s):
        rows = pl.ds(pl.multiple_of(ci * c, c), c)
        q = q_ref[0, rows, :]
        k = k_ref[0, rows, :]
        v = v_ref[0, rows, :]
        qk = lax.dot_general(q, k, (((1,), (1,)), ((), ())), preferred_element_type=F32)
        inner = jnp.dot((qk * decay).astype(BF16), v, preferred_element_type=F32)
        cross = jnp.dot(q, s.astype(BF16), preferred_element_type=F32) * q_decay
        o = inner + cross
        k_dec = (k.astype(F32) * k_decay).astype(BF16)
        s_new = s_decay * s + lax.dot_general(k_dec, v, (((0,), (0,)), ((), ())),
                                              preferred_element_type=F32)
        y = o * lax.rsqrt(jnp.mean(o * o, axis=-1, keepdims=True) + EPS) * gain
        gate = gate_ref[0, rows, :].astype(F32)
        o_ref[0, rows, :] = (y * (gate * jax.nn.sigmoid(gate))).astype(BF16)
        return s_new

    s_ref[0, 0] = lax.fori_loop(0, n_chunks, body, s0_ref[0, 0], unroll=math.gcd(n_chunks, RET_UNROLL))


def _retention(rq, rk, rv, rgate, s0, g_ret_out, chunk):
    b, l, _ = rq.shape
    assert l % chunk == 0
    bs = s0.shape[0]
    seq_spec = pl.BlockSpec((1, l, RET_DK), lambda bi, hh: (bi, 0, hh))
    state_spec = pl.BlockSpec((1, 1, RET_DK, RET_DV), lambda bi, hh: (bi, hh, 0, 0))
    s0_spec = state_spec if bs > 1 else pl.BlockSpec((1, 1, RET_DK, RET_DV), lambda bi, hh: (0, hh, 0, 0))
    kernel = functools.partial(_retention_kernel, chunk=chunk, n_chunks=l // chunk)
    return pl.pallas_call(
        kernel,
        grid=(b, RET_HEADS),
        in_specs=[seq_spec, seq_spec, seq_spec, seq_spec, s0_spec,
                  pl.BlockSpec((1, RET_DV), lambda bi, hh: (0, hh))],
        out_specs=[seq_spec, state_spec],
        out_shape=[jax.ShapeDtypeStruct((b, l, RET_WIDTH), BF16),
                   jax.ShapeDtypeStruct((b, RET_HEADS, RET_DK, RET_DV), F32)],
        compiler_params=pltpu.CompilerParams(
            dimension_semantics=("arbitrary", "arbitrary"), vmem_limit_bytes=VMEM_LIMIT),
        name="retention",
    )(rq, rk, rv, rgate, s0, g_ret_out.reshape(1, RET_WIDTH))


def _merge_ffn_kernel(x_ref, sb_ref, ret_ref, conv0_ref, wo_ref, gf_ref, wug_ref, wuv_ref,
                      cw_ref, cb_ref, wd_ref, gl_ref, y_ref, conv_ref,
                      h_ref, hn_ref, ubuf_ref, act_ref, carry_ref, *, tm):
    i = pl.program_id(1)

    @pl.when(i == 0)
    def _():
        carry_ref[...] = conv0_ref[0]

    mixed = jnp.concatenate([sb_ref[0], ret_ref[0]], axis=1)
    h = x_ref[0] + jnp.dot(mixed, wo_ref[...], preferred_element_type=F32)
    h_ref[...] = h
    ms = jnp.mean(h * h, axis=-1, keepdims=True)
    hn_ref[...] = (h * lax.rsqrt(ms + EPS) * gf_ref[...]).astype(BF16)
    lo = SUBLANES - (CONV_W - 1)

    def up_project(j):
        ubuf = ubuf_ref.at[j % 2]
        hn = hn_ref[...]
        ubuf[0:SUBLANES, :] = carry_ref[j]
        ubuf[SUBLANES:, 0:FF_CHUNK] = jnp.dot(hn, wug_ref[j], preferred_element_type=F32)
        ubuf[SUBLANES:, FF_CHUNK:] = jnp.dot(hn, wuv_ref[j], preferred_element_type=F32)
        carry_ref[j] = ubuf[tm:tm + SUBLANES, :]

    def gated_conv(j):
        ubuf = ubuf_ref.at[j % 2]
        cw = cw_ref[j]
        cb = cb_ref[j]
        strip = min(tm, FF_STRIP)
        for r in range(0, tm, strip):
            ext = ubuf[r:r + strip + SUBLANES, :]
            c = cb
            for tap in range(CONV_W):
                back = CONV_W - 1 - tap
                rows = ext if back == 0 else pltpu.roll(ext, back, 0)
                c = c + cw[tap:tap + 1, :] * rows[SUBLANES:, :]
            gate = c[:, :FF_CHUNK]
            act_ref[r:r + strip, j * FF_CHUNK:(j + 1) * FF_CHUNK] = (
                (gate * jax.nn.sigmoid(gate)) * c[:, FF_CHUNK:]).astype(BF16)

    up_project(0)
    for j in range(N_FF_CHUNKS):
        if j + 1 < N_FF_CHUNKS:
            up_project(j + 1)
        gated_conv(j)

    hh = h_ref[...] + jnp.dot(act_ref[...], wd_ref[...], preferred_element_type=F32)
    ms2 = jnp.mean(hh * hh, axis=-1, keepdims=True)
    y_ref[0] = hh * lax.rsqrt(ms2 + EPS) * gl_ref[...]

    @pl.when(i == pl.num_programs(1) - 1)
    def _():
        conv_ref[0] = carry_ref[...]


def _merge_ffn(x, sb_n, ret_n, conv0, w_out_bf, g_norm_ffn, wug, wuv, cw, cb, wd, g_norm_final, tm):
    b, l, d = x.shape
    bs = conv0.shape[0]
    row_spec = lambda w: pl.BlockSpec((1, tm, w), lambda bi, i: (bi, i, 0))
    conv_shape = (1, N_FF_CHUNKS, SUBLANES, 2 * FF_CHUNK)
    conv_spec = pl.BlockSpec(conv_shape, lambda bi, i: (bi, 0, 0, 0))
    conv0_spec = conv_spec if bs > 1 else pl.BlockSpec(conv_shape, lambda bi, i: (0, 0, 0, 0))
    kernel = functools.partial(_merge_ffn_kernel, tm=tm)
    return pl.pallas_call(
        kernel,
        grid=(b, l // tm),
        in_specs=[row_spec(d), row_spec(SB_WIDTH), row_spec(RET_WIDTH), conv0_spec,
                  _const_spec(w_out_bf.shape), _const_spec((1, d)),
                  _const_spec(wug.shape), _const_spec(wuv.shape),
                  _const_spec(cw.shape), _const_spec(cb.shape), _const_spec(wd.shape),
                  _const_spec((1, d))],
        out_specs=[row_spec(d), conv_spec],
        out_shape=[jax.ShapeDtypeStruct((b, l, d), F32),
                   jax.ShapeDtypeStruct((b,) + conv_shape[1:], F32)],
        scratch_shapes=[pltpu.VMEM((tm, d), F32), pltpu.VMEM((tm, d), BF16),
                        pltpu.VMEM((2, tm + SUBLANES, 2 * FF_CHUNK), F32),
                        pltpu.VMEM((tm, D_FF), BF16),
                        pltpu.VMEM(conv_shape[1:], F32)],
        compiler_params=pltpu.CompilerParams(
            dimension_semantics=("arbitrary", "arbitrary"), vmem_limit_bytes=VMEM_LIMIT),
        name="merge_ffn",
    )(x, sb_n, ret_n, conv0, w_out_bf, g_norm_ffn.reshape(1, d), wug, wuv, cw, cb, wd,
      g_norm_final.reshape(1, d))


def _rope_tables(pos):
    half = RET_DK // 2
    inv = ROPE_BASE ** (-jnp.arange(half, dtype=F32) / half)
    ang = pos.astype(F32)[:, None] * inv[None, :]
    cos, sin = jnp.cos(ang), jnp.sin(ang)
    return jnp.concatenate([cos, cos], axis=1), jnp.concatenate([-sin, sin], axis=1)


def _conv_state_to_chunks(state):
    b = state.shape[0]
    s = state.reshape(b, CONV_W - 1, 2, N_FF_CHUNKS, FF_CHUNK).transpose(0, 3, 1, 2, 4)
    s = s.reshape(b, N_FF_CHUNKS, CONV_W - 1, 2 * FF_CHUNK)
    return jnp.pad(s, ((0, 0), (0, 0), (SUBLANES - (CONV_W - 1), 0), (0, 0)))


def _conv_state_from_chunks(chunks):
    b = chunks.shape[0]
    s = chunks[:, :, SUBLANES - (CONV_W - 1):, :].reshape(b, N_FF_CHUNKS, CONV_W - 1, 2, FF_CHUNK)
    return s.transpose(0, 2, 3, 1, 4).reshape(b, CONV_W - 1, 2 * D_FF)


def _pad_rows(a, rows):
    return a if a.shape[1] == rows else jnp.pad(a, ((0, 0), (0, rows - a.shape[1]), (0, 0)))


def _prefix_rows_kernel(k_any, v_any, k_rows_ref, v_rows_ref, k_ref, v_ref):
    del k_any, v_any
    k_ref[...] = k_rows_ref[...]
    v_ref[...] = v_rows_ref[...]


def _write_prefix_rows(k_big, v_big, k_rows, v_rows):
    b = k_big.shape[0]
    blk = (1,) + k_rows.shape[1:]
    any_spec = pl.BlockSpec(memory_space=pl.ANY)
    rows_spec = pl.BlockSpec(blk, lambda bi: (0, 0, 0, 0))
    out_spec = pl.BlockSpec(blk, lambda bi: (bi, 0, 0, 0))
    big = jax.ShapeDtypeStruct(k_big.shape, k_big.dtype)
    return pl.pallas_call(
        _prefix_rows_kernel,
        grid=(b,),
        in_specs=[any_spec, any_spec, rows_spec, rows_spec],
        out_specs=[out_spec, out_spec],
        out_shape=[big, big],
        input_output_aliases={0: 0, 1: 1},
        name="prefix_rows",
    )(k_big, v_big, k_rows, v_rows)


def _stream_step(x, pos0, k_past, v_past, past_len, s0, conv0, wts, tm, chunk, head_row0=0):
    l = x.shape[1]
    cos2, sin2 = _rope_tables(pos0 + jnp.arange(l))
    q, k, v, k_out, v_out, rq, rk, rv, rgate = _project(x, wts["g_norm_mix"], wts["w_in"], cos2, sin2,
                                                       tm, head_row0)
    lq = -(-l // SB_KEYS) * SB_KEYS
    sb_n = _attention(_pad_rows(q, lq), _pad_rows(k, lq), _pad_rows(v, lq),
                      k_past, v_past, past_len, wts["g_sb_out"])[:, :l]
    ret_n, s_new = _retention(rq, rk, rv, rgate, s0, wts["g_ret_out"], chunk)
    y, conv_new = _merge_ffn(x, sb_n, ret_n, conv0, wts["w_out"], wts["g_norm_ffn"], wts["wug"],
                             wts["wuv"], wts["cw"], wts["cb"], wts["wd"], wts["g_norm_final"], tm)
    return y, k, v, k_out, v_out, s_new, conv_new


def kernel(x_prompt, x_sample, cache_sb_k, cache_sb_v, state_ret, state_conv, meta_tokens, g_norm_mix, w_in, g_sb_out, g_ret_out, w_out, g_norm_ffn, w_up, conv_w, conv_b, w_down, g_norm_final):
    b, seq, d = x_prompt.shape
    bd, ls, _ = x_sample.shape
    past = cache_sb_k.shape[2]

    def ff_cols(a):
        r = a.shape[0]
        return a.reshape(r, 2, N_FF_CHUNKS, FF_CHUNK).transpose(2, 0, 1, 3).reshape(N_FF_CHUNKS, r, 2 * FF_CHUNK)

    w_up_bf = w_up.astype(BF16)
    wts = dict(
        g_norm_mix=g_norm_mix, g_sb_out=g_sb_out, g_ret_out=g_ret_out, g_norm_ffn=g_norm_ffn,
        g_norm_final=g_norm_final,
        w_in=w_in.astype(BF16), w_out=w_out.astype(BF16),
        wug=w_up_bf[:, :D_FF].reshape(d, N_FF_CHUNKS, FF_CHUNK).transpose(1, 0, 2),
        wuv=w_up_bf[:, D_FF:].reshape(d, N_FF_CHUNKS, FF_CHUNK).transpose(1, 0, 2),
        cw=ff_cols(conv_w), cb=ff_cols(conv_b.reshape(1, 2 * D_FF)),
        wd=w_down.astype(BF16),
    )

    zero_state = jnp.zeros((1, RET_HEADS, RET_DK, RET_DV), F32)
    zero_conv = jnp.zeros((1, N_FF_CHUNKS, SUBLANES, 2 * FF_CHUNK), F32)
    _, k_m, v_m, k_m_out, v_m_out, s_meta, conv_meta = _stream_step(
        meta_tokens[None], -N_META, None, None, 0, zero_state, zero_conv, wts, N_META, N_META)

    y_prompt, _, _, k_p_out, v_p_out, s_prompt, conv_prompt = _stream_step(
        x_prompt, 0, _pad_rows(k_m, SB_KEYS), _pad_rows(v_m, SB_KEYS), N_META, s_meta, conv_meta,
        wts, 512, 256, head_row0=N_META)
    new_k_prompt, new_v_prompt = _write_prefix_rows(k_p_out, v_p_out, k_m_out, v_m_out)

    to_rows = lambda c: c.transpose(0, 2, 1, 3).reshape(bd, past, SB_WIDTH).astype(BF16)
    y_sample, _, _, k_s_out, v_s_out, s_sample, conv_sample = _stream_step(
        x_sample, past, to_rows(cache_sb_k), to_rows(cache_sb_v), past, state_ret,
        _conv_state_to_chunks(state_conv), wts, ls, ls)

    return (y_prompt, y_sample, new_k_prompt, new_v_prompt, s_prompt,
            _conv_state_from_chunks(conv_prompt), k_s_out, v_s_out, s_sample,
            _conv_state_from_chunks(conv_sample))
```

```python
import functools
import math

import jax
import jax.numpy as jnp
from jax import lax
from jax.experimental import pallas as pl
from jax.experimental.pallas import tpu as pltpu

D_MODEL = 1024
N_META = 16
SB_HEADS = 8
SB_HEAD_DIM = 64
SB_WIDTH = SB_HEADS * SB_HEAD_DIM
RET_HEADS = 4
RET_DK = 128
RET_DV = 128
RET_WIDTH = RET_HEADS * RET_DV
MIX_WIDTH = SB_WIDTH + RET_WIDTH
GROUP = 512
N_GROUPS = 7
IN_WIDTH = N_GROUPS * GROUP
D_FF = 2816
CONV_W = 3
ROPE_BASE = 10000.0
EPS = 1e-5

LANES = 128
SUBLANES = 8
FF_CHUNK = 256
N_FF_CHUNKS = D_FF // FF_CHUNK
FF_STRIP = 64
RET_UNROLL = 8
SB_KEYS = 128
SB_QUERY_TILE = 512
SB_UNROLL = 2
PAIR = 2 * SB_HEAD_DIM
SB_PAIRS = 2
SB_EXHAUSTED = 152.0
SB_Q_SCALE = SB_HEAD_DIM ** -0.5 * math.log2(math.e)
VMEM_LIMIT = 56 * 1024 * 1024

BF16 = jnp.bfloat16
F32 = jnp.float32


def _const_spec(shape):
    zeros = (0,) * len(shape)
    return pl.BlockSpec(shape, lambda *_: zeros, pipeline_mode=pl.Buffered(1))


def _project_kernel(x_ref, g_ref, w_ref, cos_ref, sin_ref,
                    q_ref, k_ref, v_ref, ko_ref, vo_ref, rq_ref, rk_ref, rv_ref, rg_ref):
    x = x_ref[0]
    ms = jnp.mean(x * x, axis=-1, keepdims=True)
    h = (x * lax.rsqrt(ms + EPS) * g_ref[...]).astype(BF16)

    def group(i):
        return jnp.dot(h, w_ref[:, i * GROUP:(i + 1) * GROUP], preferred_element_type=F32)

    def split_heads(p, out_ref):
        for hh in range(SB_HEADS):
            out_ref[0, hh] = p[:, hh * SB_HEAD_DIM:(hh + 1) * SB_HEAD_DIM]

    def rope(p, out_ref, scale):
        cos = cos_ref[...]
        sin = sin_ref[...]
        for hh in range(RET_HEADS):
            t = p[:, hh * RET_DK:(hh + 1) * RET_DK]
            r = t * cos + pltpu.roll(t, RET_DK // 2, 1) * sin
            if scale is not None:
                r = r * scale
            out_ref[0, :, hh * RET_DK:(hh + 1) * RET_DK] = r.astype(BF16)

    q_ref[0] = (group(0) * SB_Q_SCALE).astype(BF16)
    pk = group(1)
    k_ref[0] = pk.astype(BF16)
    split_heads(pk, ko_ref)
    pv = group(2)
    v_ref[0] = pv.astype(BF16)
    split_heads(pv, vo_ref)
    rope(group(3), rq_ref, None)
    rope(group(4), rk_ref, RET_DK ** -0.5)
    rv_ref[0] = group(5).astype(BF16)
    rg_ref[0] = group(6).astype(BF16)


def _project(x, g_norm, w_in_bf, cos2, sin2, tm, head_row0=0):
    b, l, d = x.shape
    grid = (b, l // tm)
    row_spec = lambda w: pl.BlockSpec((1, tm, w), lambda bi, i: (bi, i, 0))
    head_blk = (1, SB_HEADS, tm, SB_HEAD_DIM)
    head_spec = pl.BlockSpec(tuple(pl.Element(n) for n in head_blk),
                             lambda bi, i: (bi, 0, pl.multiple_of(head_row0 + i * tm, SUBLANES), 0))
    tab_spec = pl.BlockSpec((tm, RET_DK), lambda bi, i: (i, 0))
    act = jax.ShapeDtypeStruct((b, l, GROUP), BF16)
    heads = jax.ShapeDtypeStruct((b, SB_HEADS, head_row0 + l, SB_HEAD_DIM), F32)
    return pl.pallas_call(
        _project_kernel,
        grid=grid,
        in_specs=[row_spec(d), _const_spec((1, d)), _const_spec((d, IN_WIDTH)), tab_spec, tab_spec],
        out_specs=[row_spec(GROUP)] * 3 + [head_spec] * 2 + [row_spec(GROUP)] * 4,
        out_shape=[act] * 3 + [heads] * 2 + [act] * 4,
        compiler_params=pltpu.CompilerParams(
            dimension_semantics=("arbitrary", "arbitrary"), vmem_limit_bytes=VMEM_LIMIT),
        name="project",
    )(x, g_norm.reshape(1, d), w_in_bf, cos2, sin2)


def _split_bf16(x):
    hi = x.astype(BF16)
    lo = (x - hi.astype(F32)).astype(BF16)
    return hi, lo


def _attention_kernel(q_ref, kc_ref, vc_ref, kp_ref, vp_ref, bdu_ref, bdo_ref, j_ref, g_ref, o_ref,
                      kxc_ref, vxc_ref, kxp_ref, vxp_ref, carry_ref, acc_ref,
                      *, tq, n_q, n_past, past_valid):
    tk = SB_KEYS
    per_tile = tq // tk
    head0 = lax.broadcasted_iota(jnp.int32, (tk, LANES), 1) < SB_HEAD_DIM
    key_in_block = lax.broadcasted_iota(jnp.int32, (tq, 2 * tk), 1) & (tk - 1)
    delta = key_in_block - lax.broadcasted_iota(jnp.int32, (tq, 2 * tk), 0)

    pairs = range(SB_PAIRS)
    lanes = [slice(p * PAIR, (p + 1) * PAIR) for p in pairs]

    def expand(src_ref, dst_ref, n_blocks):
        def body(j, _):
            for p in pairs:
                blk = src_ref[0, pl.ds(pl.multiple_of(j * tk, tk), tk), lanes[p]]
                zero = jnp.zeros_like(blk)
                dst_ref[p, j, 0:tk, :] = jnp.where(head0, blk, zero)
                dst_ref[p, j, tk:2 * tk, :] = jnp.where(head0, zero, blk)
            return 0
        lax.fori_loop(0, n_blocks, body, 0)

    expand(kc_ref, kxc_ref, n_q * per_tile)
    expand(vc_ref, vxc_ref, n_q * per_tile)
    if n_past > 0:
        expand(kp_ref, kxp_ref, n_past)
        expand(vp_ref, vxp_ref, n_past)

    def add_blocks(kx_ref, vx_ref, i, j_last, masks, r0=0):
        rows = pl.ds(pl.multiple_of(i * tq + r0, tk), tq - r0)
        carry = [carry_ref[p, r0:, :] for p in pairs]
        for u, mask in enumerate(masks):
            j = j_last - u
            for p in pairs:
                z = lax.dot_general(q_ref[0, rows, lanes[p]], kx_ref[p, j], (((1,), (1,)), ((), ())),
                                    preferred_element_type=F32)
                neg_abs = lax.bitcast_convert_type(
                    lax.bitcast_convert_type(z, jnp.uint32) | jnp.uint32(0x80000000), F32)
                sp = jnp.maximum(z, 0.0) + jnp.log2(1.0 + jnp.exp2(neg_abs))
                spm = sp if mask is None else jnp.where(mask, sp, 0.0)
                hi = spm.astype(BF16)
                later = jnp.dot(hi, bdu_ref[...], preferred_element_type=F32)
                total = jnp.dot(hi, bdo_ref[...], preferred_element_type=F32)
                a = jnp.exp2((z - sp) - (later + carry[p]))
                if mask is not None:
                    a = jnp.where(mask, a, 0.0)
                acc_ref[p, r0:, :] += jnp.dot(a.astype(BF16), vx_ref[p, j], preferred_element_type=F32)
                carry[p] = carry[p] + total
        for p in pairs:
            carry_ref[p, r0:, :] = carry[p]
        return carry

    def stick_left(carry):
        least = functools.reduce(jnp.minimum, carry)
        return (jnp.min(least) < SB_EXHAUSTED).astype(jnp.int32)

    def sweep(kx_ref, vx_ref, i, j_top, n_steps, unroll, alive):
        def cond(state):
            m, live = state
            return jnp.logical_and(m < n_steps, live > 0)

        def body(state):
            m, _ = state
            carry = add_blocks(kx_ref, vx_ref, i, j_top - m * unroll, [None] * unroll)
            return m + 1, stick_left(carry)

        return lax.while_loop(cond, body, (jnp.int32(0), alive))[1]

    def q_tile(i, _):
        rows = pl.ds(pl.multiple_of(i * tq, tq), tq)
        carry_ref[...] = jnp.zeros_like(carry_ref)
        acc_ref[...] = jnp.zeros_like(acc_ref)

        for jj in range(per_tile - 1, -1, -1):
            r0 = jj * tk
            add_blocks(kxc_ref, vxc_ref, i, i * per_tile + jj, [delta[r0:, :] < -r0], r0)

        unroll = SB_UNROLL if per_tile % SB_UNROLL == 0 else 1
        alive = sweep(kxc_ref, vxc_ref, i, i * per_tile - 1, (i * per_tile) // unroll, unroll,
                      stick_left([carry_ref[p] for p in pairs]))

        if n_past > 0:
            n_full = n_past
            if past_valid < tk:
                @pl.when(alive > 0)
                def _():
                    add_blocks(kxp_ref, vxp_ref, i, n_past - 1, [key_in_block < past_valid])

                n_full = n_past - 1
                if n_full > 0:
                    alive = stick_left([carry_ref[p] for p in pairs])
            if n_full // SB_UNROLL > 0:
                alive = sweep(kxp_ref, vxp_ref, i, n_full - 1, n_full // SB_UNROLL, SB_UNROLL, alive)
            if n_full % SB_UNROLL:
                alive = sweep(kxp_ref, vxp_ref, i, n_full % SB_UNROLL - 1, n_full % SB_UNROLL, 1, alive)

        for p in pairs:
            o = acc_ref[p]
            hi, lo = _split_bf16(o * o)
            jm = j_ref[...]
            ms = (jnp.dot(hi, jm, preferred_element_type=F32)
                  + jnp.dot(lo, jm, preferred_element_type=F32)) * (1.0 / SB_HEAD_DIM)
            y = o * lax.rsqrt(ms + EPS) * g_ref[:, lanes[p]]
            o_ref[0, rows, lanes[p]] = y.astype(BF16)
        return 0

    lax.fori_loop(0, n_q, q_tile, 0)


def _attention(q, k_cur, v_cur, k_past, v_past, past_len, g_sb_out):
    b, lq, _ = q.shape
    tk = SB_KEYS
    tq = min(SB_QUERY_TILE, lq)
    assert lq % tq == 0 and tq % tk == 0
    if k_past is None:
        k_past = jnp.zeros((1, tk, SB_WIDTH), BF16)
        v_past = jnp.zeros((1, tk, SB_WIDTH), BF16)
        n_past, past_valid = 0, tk
    else:
        assert k_past.shape[1] % tk == 0
        n_past = -(-past_len // tk)
        past_valid = past_len - (n_past - 1) * tk
        k_past = k_past[:, :n_past * tk]
        v_past = v_past[:, :n_past * tk]
    bp, p, _ = k_past.shape
    jj = jnp.arange(2 * tk)
    same_head = (jj[:, None] // tk) == (jj[None, :] // tk)
    bdu = (same_head & (jj[:, None] > jj[None, :])).astype(BF16)
    bdo = same_head.astype(BF16)
    ll = jnp.arange(LANES) // SB_HEAD_DIM
    jm = (ll[:, None] == ll[None, :]).astype(BF16)
    width = SB_PAIRS * PAIR
    cur_spec = pl.BlockSpec((1, lq, width), lambda bi, hp: (bi, 0, hp))
    past_spec = pl.BlockSpec((1, p, width), (lambda bi, hp: (bi, 0, hp)) if bp > 1
                             else (lambda bi, hp: (0, 0, hp)))
    kernel = functools.partial(_attention_kernel, tq=tq, n_q=lq // tq, n_past=n_past,
                               past_valid=past_valid)
    stacked = lambda n: pltpu.VMEM((SB_PAIRS, n, 2 * tk, PAIR), BF16)
    return pl.pallas_call(
        kernel,
        grid=(b, SB_WIDTH // width),
        in_specs=[cur_spec, cur_spec, cur_spec, past_spec, past_spec,
                  _const_spec(bdu.shape), _const_spec(bdo.shape), _const_spec(jm.shape),
                  pl.BlockSpec((1, width), lambda bi, hp: (0, hp))],
        out_specs=cur_spec,
        out_shape=jax.ShapeDtypeStruct((b, lq, SB_WIDTH), BF16),
        scratch_shapes=[stacked(lq // tk), stacked(lq // tk), stacked(p // tk), stacked(p // tk),
                        pltpu.VMEM((SB_PAIRS, tq, 2 * tk), F32), pltpu.VMEM((SB_PAIRS, tq, PAIR), F32)],
        compiler_params=pltpu.CompilerParams(
            dimension_semantics=("arbitrary", "arbitrary"), vmem_limit_bytes=VMEM_LIMIT),
        name="attention",
    )(q, k_cur, v_cur, k_past, v_past, bdu, bdo, jm, g_sb_out.reshape(1, SB_WIDTH))


def _retention_kernel(q_ref, k_ref, v_ref, gate_ref, s0_ref, g_ref, o_ref, s_ref, *, chunk, n_chunks):
    c = chunk
    head = pl.program_id(1).astype(F32)
    log_g = jnp.log(1.0 - jnp.exp2(jnp.full((1, LANES), -5.0, F32) - head))
    n_row = lax.broadcasted_iota(jnp.int32, (c, LANES), 0).astype(F32)
    q_decay = jnp.exp((n_row + 1.0) * log_g)
    k_decay = jnp.exp((c - 1.0 - n_row) * log_g)
    s_decay = jnp.exp(float(c) * log_g)
    diff = (lax.broadcasted_iota(jnp.int32, (c, c), 0)
            - lax.broadcasted_iota(jnp.int32, (c, c), 1))
    log_g_cc = log_g if c == LANES else jnp.log(1.0 - jnp.exp2(jnp.full((1, c), -5.0, F32) - head))
    decay = jnp.where(diff >= 0, jnp.exp(jnp.maximum(diff, 0).astype(F32) * log_g_cc), 0.0)
    gain = g_ref[...]

    def body(ci, s):
        rows = pl.ds(pl.multiple_of(ci * c, c), c)
        q = q_ref[0, rows, :]
        k = k_ref[0, rows, :]
        v = v_ref[0, rows, :]
        qk = lax.dot_general(q, k, (((1,), (1,)), ((), ())), preferred_element_type=F32)
        inner = jnp.dot((qk * decay).astype(BF16), v, preferred_element_type=F32)
        cross = jnp.dot(q, s.astype(BF16), preferred_element_type=F32) * q_decay
        o = inner + cross
        k_dec = (k.astype(F32) * k_decay).astype(BF16)
        s_new = s_decay * s + lax.dot_general(k_dec, v, (((0,), (0,)), ((), ())),
                                              preferred_element_type=F32)
        y = o * lax.rsqrt(jnp.mean(o * o, axis=-1, keepdims=True) + EPS) * gain
        gate = gate_ref[0, rows, :].astype(F32)
        o_ref[0, rows, :] = (y * (gate * jax.nn.sigmoid(gate))).astype(BF16)
        return s_new

    s_ref[0, 0] = lax.fori_loop(0, n_chunks, body, s0_ref[0, 0], unroll=math.gcd(n_chunks, RET_UNROLL))


def _retention(rq, rk, rv, rgate, s0, g_ret_out, chunk):
    b, l, _ = rq.shape
    assert l % chunk == 0
    bs = s0.shape[0]
    seq_spec = pl.BlockSpec((1, l, RET_DK), lambda bi, hh: (bi, 0, hh))
    state_spec = pl.BlockSpec((1, 1, RET_DK, RET_DV), lambda bi, hh: (bi, hh, 0, 0))
    s0_spec = state_spec if bs > 1 else pl.BlockSpec((1, 1, RET_DK, RET_DV), lambda bi, hh: (0, hh, 0, 0))
    kernel = functools.partial(_retention_kernel, chunk=chunk, n_chunks=l // chunk)
    return pl.pallas_call(
        kernel,
        grid=(b, RET_HEADS),
        in_specs=[seq_spec, seq_spec, seq_spec, seq_spec, s0_spec,
                  pl.BlockSpec((1, RET_DV), lambda bi, hh: (0, hh))],
        out_specs=[seq_spec, state_spec],
        out_shape=[jax.ShapeDtypeStruct((b, l, RET_WIDTH), BF16),
                   jax.ShapeDtypeStruct((b, RET_HEADS, RET_DK, RET_DV), F32)],
        compiler_params=pltpu.CompilerParams(
            dimension_semantics=("arbitrary", "arbitrary"), vmem_limit_bytes=VMEM_LIMIT),
        name="retention",
    )(rq, rk, rv, rgate, s0, g_ret_out.reshape(1, RET_WIDTH))


def _merge_ffn_kernel(x_ref, sb_ref, ret_ref, conv0_ref, wo_ref, gf_ref, wug_ref, wuv_ref,
                      cw_ref, cb_ref, wd_ref, gl_ref, y_ref, conv_ref,
                      h_ref, hn_ref, ubuf_ref, act_ref, carry_ref, *, tm):
    i = pl.program_id(1)

    @pl.when(i == 0)
    def _():
        carry_ref[...] = conv0_ref[0]

    mixed = jnp.concatenate([sb_ref[0], ret_ref[0]], axis=1)
    h = x_ref[0] + jnp.dot(mixed, wo_ref[...], preferred_element_type=F32)
    h_ref[...] = h
    ms = jnp.mean(h * h, axis=-1, keepdims=True)
    hn_ref[...] = (h * lax.rsqrt(ms + EPS) * gf_ref[...]).astype(BF16)
    lo = SUBLANES - (CONV_W - 1)

    def up_project(j):
        ubuf = ubuf_ref.at[j % 2]
        hn = hn_ref[...]
        ubuf[0:SUBLANES, :] = carry_ref[j]
        ubuf[SUBLANES:, 0:FF_CHUNK] = jnp.dot(hn, wug_ref[j], preferred_element_type=F32)
        ubuf[SUBLANES:, FF_CHUNK:] = jnp.dot(hn, wuv_ref[j], preferred_element_type=F32)
        carry_ref[j] = ubuf[tm:tm + SUBLANES, :]

    def gated_conv(j):
        ubuf = ubuf_ref.at[j % 2]
        cw = cw_ref[j]
        cb = cb_ref[j]
        strip = min(tm, FF_STRIP)
        for r in range(0, tm, strip):
            ext = ubuf[r:r + strip + SUBLANES, :]
            c = cb
            for tap in range(CONV_W):
                back = CONV_W - 1 - tap
                rows = ext if back == 0 else pltpu.roll(ext, back, 0)
                c = c + cw[tap:tap + 1, :] * rows[SUBLANES:, :]
            gate = c[:, :FF_CHUNK]
            act_ref[r:r + strip, j * FF_CHUNK:(j + 1) * FF_CHUNK] = (
                (gate * jax.nn.sigmoid(gate)) * c[:, FF_CHUNK:]).astype(BF16)

    up_project(0)
    for j in range(N_FF_CHUNKS):
        if j + 1 < N_FF_CHUNKS:
            up_project(j + 1)
        gated_conv(j)

    hh = h_ref[...] + jnp.dot(act_ref[...], wd_ref[...], preferred_element_type=F32)
    ms2 = jnp.mean(hh * hh, axis=-1, keepdims=True)
    y_ref[0] = hh * lax.rsqrt(ms2 + EPS) * gl_ref[...]

    @pl.when(i == pl.num_programs(1) - 1)
    def _():
        conv_ref[0] = carry_ref[...]


def _merge_ffn(x, sb_n, ret_n, conv0, w_out_bf, g_norm_ffn, wug, wuv, cw, cb, wd, g_norm_final, tm):
    b, l, d = x.shape
    bs = conv0.shape[0]
    row_spec = lambda w: pl.BlockSpec((1, tm, w), lambda bi, i: (bi, i, 0))
    conv_shape = (1, N_FF_CHUNKS, SUBLANES, 2 * FF_CHUNK)
    conv_spec = pl.BlockSpec(conv_shape, lambda bi, i: (bi, 0, 0, 0))
    conv0_spec = conv_spec if bs > 1 else pl.BlockSpec(conv_shape, lambda bi, i: (0, 0, 0, 0))
    kernel = functools.partial(_merge_ffn_kernel, tm=tm)
    return pl.pallas_call(
        kernel,
        grid=(b, l // tm),
        in_specs=[row_spec(d), row_spec(SB_WIDTH), row_spec(RET_WIDTH), conv0_spec,
                  _const_spec(w_out_bf.shape), _const_spec((1, d)),
                  _const_spec(wug.shape), _const_spec(wuv.shape),
                  _const_spec(cw.shape), _const_spec(cb.shape), _const_spec(wd.shape),
                  _const_spec((1, d))],
        out_specs=[row_spec(d), conv_spec],
        out_shape=[jax.ShapeDtypeStruct((b, l, d), F32),
                   jax.ShapeDtypeStruct((b,) + conv_shape[1:], F32)],
        scratch_shapes=[pltpu.VMEM((tm, d), F32), pltpu.VMEM((tm, d), BF16),
                        pltpu.VMEM((2, tm + SUBLANES, 2 * FF_CHUNK), F32),
                        pltpu.VMEM((tm, D_FF), BF16),
                        pltpu.VMEM(conv_shape[1:], F32)],
        compiler_params=pltpu.CompilerParams(
            dimension_semantics=("arbitrary", "arbitrary"), vmem_limit_bytes=VMEM_LIMIT),
        name="merge_ffn",
    )(x, sb_n, ret_n, conv0, w_out_bf, g_norm_ffn.reshape(1, d), wug, wuv, cw, cb, wd,
      g_norm_final.reshape(1, d))


def _rope_tables(pos):
    half = RET_DK // 2
    inv = ROPE_BASE ** (-jnp.arange(half, dtype=F32) / half)
    ang = pos.astype(F32)[:, None] * inv[None, :]
    cos, sin = jnp.cos(ang), jnp.sin(ang)
    return jnp.concatenate([cos, cos], axis=1), jnp.concatenate([-sin, sin], axis=1)


def _conv_state_to_chunks(state):
    b = state.shape[0]
    s = state.reshape(b, CONV_W - 1, 2, N_FF_CHUNKS, FF_CHUNK).transpose(0, 3, 1, 2, 4)
    s = s.reshape(b, N_FF_CHUNKS, CONV_W - 1, 2 * FF_CHUNK)
    return jnp.pad(s, ((0, 0), (0, 0), (SUBLANES - (CONV_W - 1), 0), (0, 0)))


def _conv_state_from_chunks(chunks):
    b = chunks.shape[0]
    s = chunks[:, :, SUBLANES - (CONV_W - 1):, :].reshape(b, N_FF_CHUNKS, CONV_W - 1, 2, FF_CHUNK)
    return s.transpose(0, 2, 3, 1, 4).reshape(b, CONV_W - 1, 2 * D_FF)


def _pad_rows(a, rows):
    return a if a.shape[1] == rows else jnp.pad(a, ((0, 0), (0, rows - a.shape[1]), (0, 0)))


def _prefix_rows_kernel(k_any, v_any, k_rows_ref, v_rows_ref, k_ref, v_ref):
    del k_any, v_any
    k_ref[...] = k_rows_ref[...]
    v_ref[...] = v_rows_ref[...]


def _write_prefix_rows(k_big, v_big, k_rows, v_rows):
    b = k_big.shape[0]
    blk = (1,) + k_rows.shape[1:]
    any_spec = pl.BlockSpec(memory_space=pl.ANY)
    rows_spec = pl.BlockSpec(blk, lambda bi: (0, 0, 0, 0))
    out_spec = pl.BlockSpec(blk, lambda bi: (bi, 0, 0, 0))
    big = jax.ShapeDtypeStruct(k_big.shape, k_big.dtype)
    return pl.pallas_call(
        _prefix_rows_kernel,
        grid=(b,),
        in_specs=[any_spec, any_spec, rows_spec, rows_spec],
        out_specs=[out_spec, out_spec],
        out_shape=[big, big],
        input_output_aliases={0: 0, 1: 1},
        name="prefix_rows",
    )(k_big, v_big, k_rows, v_rows)


def _stream_step(x, pos0, k_past, v_past, past_len, s0, conv0, wts, tm, chunk, head_row0=0):
    l = x.shape[1]
    cos2, sin2 = _rope_tables(pos0 + jnp.arange(l))
    q, k, v, k_out, v_out, rq, rk, rv, rgate = _project(x, wts["g_norm_mix"], wts["w_in"], cos2, sin2,
                                                       tm, head_row0)
    lq = -(-l // SB_KEYS) * SB_KEYS
    sb_n = _attention(_pad_rows(q, lq), _pad_rows(k, lq), _pad_rows(v, lq),
                      k_past, v_past, past_len, wts["g_sb_out"])[:, :l]
    ret_n, s_new = _retention(rq, rk, rv, rgate, s0, wts["g_ret_out"], chunk)
    y, conv_new = _merge_ffn(x, sb_n, ret_n, conv0, wts["w_out"], wts["g_norm_ffn"], wts["wug"],
                             wts["wuv"], wts["cw"], wts["cb"], wts["wd"], wts["g_norm_final"], tm)
    return y, k, v, k_out, v_out, s_new, conv_new


def kernel(x_prompt, x_sample, cache_sb_k, cache_sb_v, state_ret, state_conv, meta_tokens, g_norm_mix, w_in, g_sb_out, g_ret_out, w_out, g_norm_ffn, w_up, conv_w, conv_b, w_down, g_norm_final):
    b, seq, d = x_prompt.shape
    bd, ls, _ = x_sample.shape
    past = cache_sb_k.shape[2]

    def ff_cols(a):
        r = a.shape[0]
        return a.reshape(r, 2, N_FF_CHUNKS, FF_CHUNK).transpose(2, 0, 1, 3).reshape(N_FF_CHUNKS, r, 2 * FF_CHUNK)

    w_up_bf = w_up.astype(BF16)
    wts = dict(
        g_norm_mix=g_norm_mix, g_sb_out=g_sb_out, g_ret_out=g_ret_out, g_norm_ffn=g_norm_ffn,
        g_norm_final=g_norm_final,
        w_in=w_in.astype(BF16), w_out=w_out.astype(BF16),
        wug=w_up_bf[:, :D_FF].reshape(d, N_FF_CHUNKS, FF_CHUNK).transpose(1, 0, 2),
        wuv=w_up_bf[:, D_FF:].reshape(d, N_FF_CHUNKS, FF_CHUNK).transpose(1, 0, 2),
        cw=ff_cols(conv_w), cb=ff_cols(conv_b.reshape(1, 2 * D_FF)),
        wd=w_down.astype(BF16),
    )

    zero_state = jnp.zeros((1, RET_HEADS, RET_DK, RET_DV), F32)
    zero_conv = jnp.zeros((1, N_FF_CHUNKS, SUBLANES, 2 * FF_CHUNK), F32)
    _, k_m, v_m, k_m_out, v_m_out, s_meta, conv_meta = _stream_step(
        meta_tokens[None], -N_META, None, None, 0, zero_state, zero_conv, wts, N_META, N_META)

    y_prompt, _, _, k_p_out, v_p_out, s_prompt, conv_prompt = _stream_step(
        x_prompt, 0, _pad_rows(k_m, SB_KEYS), _pad_rows(v_m, SB_KEYS), N_META, s_meta, conv_meta,
        wts, 512, 256, head_row0=N_META)
    new_k_prompt, new_v_prompt = _write_prefix_rows(k_p_out, v_p_out, k_m_out, v_m_out)

    to_rows = lambda c: c.transpose(0, 2, 1, 3).reshape(bd, past, SB_WIDTH).astype(BF16)
    y_sample, _, _, k_s_out, v_s_out, s_sample, conv_sample = _stream_step(
        x_sample, past, to_rows(cache_sb_k), to_rows(cache_sb_v), past, state_ret,
        _conv_state_to_chunks(state_conv), wts, ls, ls)

    return (y_prompt, y_sample, new_k_prompt, new_v_prompt, s_prompt,
            _conv_state_from_chunks(conv_prompt), k_s_out, v_s_out, s_sample,
            _conv_state_from_chunks(conv_sample))
```

```python
import functools
import math

import jax
import jax.numpy as jnp
from jax import lax
from jax.experimental import pallas as pl
from jax.experimental.pallas import tpu as pltpu

D_MODEL = 1024
N_META = 16
SB_HEADS = 8
SB_HEAD_DIM = 64
SB_WIDTH = SB_HEADS * SB_HEAD_DIM
RET_HEADS = 4
RET_DK = 128
RET_DV = 128
RET_WIDTH = RET_HEADS * RET_DV
MIX_WIDTH = SB_WIDTH + RET_WIDTH
GROUP = 512
N_GROUPS = 7
IN_WIDTH = N_GROUPS * GROUP
D_FF = 2816
CONV_W = 3
ROPE_BASE = 10000.0
EPS = 1e-5

LANES = 128
SUBLANES = 8
FF_CHUNK = 256
N_FF_CHUNKS = D_FF // FF_CHUNK
FF_STRIP = 64
RET_UNROLL = 8
SB_KEYS = 128
SB_QUERY_TILE = 512
SB_UNROLL = 2
PAIR = 2 * SB_HEAD_DIM
SB_PAIRS = 2
SB_EXHAUSTED = 152.0
SB_Q_SCALE = SB_HEAD_DIM ** -0.5 * math.log2(math.e)
VMEM_LIMIT = 56 * 1024 * 1024

BF16 = jnp.bfloat16
F32 = jnp.float32


def _const_spec(shape):
    zeros = (0,) * len(shape)
    return pl.BlockSpec(shape, lambda *_: zeros, pipeline_mode=pl.Buffered(1))


def _project_kernel(x_ref, g_ref, w_ref, cos_ref, sin_ref,
                    q_ref, k_ref, v_ref, ko_ref, vo_ref, rq_ref, rk_ref, rv_ref, rg_ref):
    x = x_ref[0]
    ms = jnp.mean(x * x, axis=-1, keepdims=True)
    h = (x * lax.rsqrt(ms + EPS) * g_ref[...]).astype(BF16)

    def group(i):
        return jnp.dot(h, w_ref[:, i * GROUP:(i + 1) * GROUP], preferred_element_type=F32)

    def split_heads(p, out_ref):
        for hh in range(SB_HEADS):
            out_ref[0, hh] = p[:, hh * SB_HEAD_DIM:(hh + 1) * SB_HEAD_DIM]

    def rope(p, out_ref, scale):
        cos = cos_ref[...]
        sin = sin_ref[...]
        for hh in range(RET_HEADS):
            t = p[:, hh * RET_DK:(hh + 1) * RET_DK]
            r = t * cos + pltpu.roll(t, RET_DK // 2, 1) * sin
            if scale is not None:
                r = r * scale
            out_ref[0, :, hh * RET_DK:(hh + 1) * RET_DK] = r.astype(BF16)

    q_ref[0] = (group(0) * SB_Q_SCALE).astype(BF16)
    pk = group(1)
    k_ref[0] = pk.astype(BF16)
    split_heads(pk, ko_ref)
    pv = group(2)
    v_ref[0] = pv.astype(BF16)
    split_heads(pv, vo_ref)
    rope(group(3), rq_ref, None)
    rope(group(4), rk_ref, RET_DK ** -0.5)
    rv_ref[0] = group(5).astype(BF16)
    rg_ref[0] = group(6).astype(BF16)


def _project(x, g_norm, w_in_bf, cos2, sin2, tm, head_row0=0):
    b, l, d = x.shape
    grid = (b, l // tm)
    row_spec = lambda w: pl.BlockSpec((1, tm, w), lambda bi, i: (bi, i, 0))
    head_blk = (1, SB_HEADS, tm, SB_HEAD_DIM)
    head_spec = pl.BlockSpec(tuple(pl.Element(n) for n in head_blk),
                             lambda bi, i: (bi, 0, pl.multiple_of(head_row0 + i * tm, SUBLANES), 0))
    tab_spec = pl.BlockSpec((tm, RET_DK), lambda bi, i: (i, 0))
    act = jax.ShapeDtypeStruct((b, l, GROUP), BF16)
    heads = jax.ShapeDtypeStruct((b, SB_HEADS, head_row0 + l, SB_HEAD_DIM), F32)
    return pl.pallas_call(
        _project_kernel,
        grid=grid,
        in_specs=[row_spec(d), _const_spec((1, d)), _const_spec((d, IN_WIDTH)), tab_spec, tab_spec],
        out_specs=[row_spec(GROUP)] * 3 + [head_spec] * 2 + [row_spec(GROUP)] * 4,
        out_shape=[act] * 3 + [heads] * 2 + [act] * 4,
        compiler_params=pltpu.CompilerParams(
            dimension_semantics=("arbitrary", "arbitrary"), vmem_limit_bytes=VMEM_LIMIT),
        name="project",
    )(x, g_norm.reshape(1, d), w_in_bf, cos2, sin2)


def _split_bf16(x):
    hi = x.astype(BF16)
    lo = (x - hi.astype(F32)).astype(BF16)
    return hi, lo


def _attention_kernel(q_ref, kc_ref, vc_ref, kp_ref, vp_ref, bdu_ref, bdo_ref, j_ref, g_ref, o_ref,
                      kxc_ref, vxc_ref, kxp_ref, vxp_ref, carry_ref, acc_ref,
                      *, tq, n_q, n_past, past_valid, past_heads):
    tk = SB_KEYS
    per_tile = tq // tk
    head0 = lax.broadcasted_iota(jnp.int32, (tk, LANES), 1) < SB_HEAD_DIM
    key_in_block = lax.broadcasted_iota(jnp.int32, (tq, 2 * tk), 1) & (tk - 1)
    delta = key_in_block - lax.broadcasted_iota(jnp.int32, (tq, 2 * tk), 0)

    pairs = range(SB_PAIRS)
    lanes = [slice(p * PAIR, (p + 1) * PAIR) for p in pairs]

    def expand(src_ref, dst_ref, n_blocks):
        def body(j, _):
            for p in pairs:
                blk = src_ref[0, pl.ds(pl.multiple_of(j * tk, tk), tk), lanes[p]]
                zero = jnp.zeros_like(blk)
                dst_ref[p, j, 0:tk, :] = jnp.where(head0, blk, zero)
                dst_ref[p, j, tk:2 * tk, :] = jnp.where(head0, zero, blk)
            return 0
        lax.fori_loop(0, n_blocks, body, 0)

    def expand_heads(src_ref, dst_ref, n_blocks):
        zero = jnp.zeros((tk, SB_HEAD_DIM), F32)

        def body(j, _):
            rows = pl.ds(pl.multiple_of(j * tk, tk), tk)
            for p in pairs:
                dst_ref[p, j, 0:tk, :] = jnp.concatenate(
                    [src_ref[0, 2 * p, rows, :], zero], axis=1).astype(BF16)
                dst_ref[p, j, tk:2 * tk, :] = jnp.concatenate(
                    [zero, src_ref[0, 2 * p + 1, rows, :]], axis=1).astype(BF16)
            return 0
        lax.fori_loop(0, n_blocks, body, 0)

    expand(kc_ref, kxc_ref, n_q * per_tile)
    expand(vc_ref, vxc_ref, n_q * per_tile)
    if n_past > 0:
        expand_past = expand_heads if past_heads else expand
        expand_past(kp_ref, kxp_ref, n_past)
        expand_past(vp_ref, vxp_ref, n_past)

    def add_blocks(kx_ref, vx_ref, i, j_last, masks, r0=0):
        rows = pl.ds(pl.multiple_of(i * tq + r0, tk), tq - r0)
        carry = [carry_ref[p, r0:, :] for p in pairs]
        for u, mask in enumerate(masks):
            j = j_last - u
            for p in pairs:
                z = lax.dot_general(q_ref[0, rows, lanes[p]], kx_ref[p, j], (((1,), (1,)), ((), ())),
                                    preferred_element_type=F32)
                neg_abs = lax.bitcast_convert_type(
                    lax.bitcast_convert_type(z, jnp.uint32) | jnp.uint32(0x80000000), F32)
                sp = jnp.maximum(z, 0.0) + jnp.log2(1.0 + jnp.exp2(neg_abs))
                spm = sp if mask is None else jnp.where(mask, sp, 0.0)
                hi = spm.astype(BF16)
                later = jnp.dot(hi, bdu_ref[...], preferred_element_type=F32)
                total = jnp.dot(hi, bdo_ref[...], preferred_element_type=F32)
                a = jnp.exp2((z - sp) - (later + carry[p]))
                if mask is not None:
                    a = jnp.where(mask, a, 0.0)
                acc_ref[p, r0:, :] += jnp.dot(a.astype(BF16), vx_ref[p, j], preferred_element_type=F32)
                carry[p] = carry[p] + total
        for p in pairs:
            carry_ref[p, r0:, :] = carry[p]
        return carry

    def stick_left(carry):
        least = functools.reduce(jnp.minimum, carry)
        return (jnp.min(least) < SB_EXHAUSTED).astype(jnp.int32)

    def sweep(kx_ref, vx_ref, i, j_top, n_steps, unroll, alive):
        def cond(state):
            m, live = state
            return jnp.logical_and(m < n_steps, live > 0)

        def body(state):
            m, _ = state
            carry = add_blocks(kx_ref, vx_ref, i, j_top - m * unroll, [None] * unroll)
            return m + 1, stick_left(carry)

        return lax.while_loop(cond, body, (jnp.int32(0), alive))[1]

    def q_tile(i, _):
        rows = pl.ds(pl.multiple_of(i * tq, tq), tq)
        carry_ref[...] = jnp.zeros_like(carry_ref)
        acc_ref[...] = jnp.zeros_like(acc_ref)

        for jj in range(per_tile - 1, -1, -1):
            r0 = jj * tk
            add_blocks(kxc_ref, vxc_ref, i, i * per_tile + jj, [delta[r0:, :] < -r0], r0)

        unroll = SB_UNROLL if per_tile % SB_UNROLL == 0 else 1
        alive = sweep(kxc_ref, vxc_ref, i, i * per_tile - 1, (i * per_tile) // unroll, unroll,
                      stick_left([carry_ref[p] for p in pairs]))

        if n_past > 0:
            n_full = n_past
            if past_valid < tk:
                @pl.when(alive > 0)
                def _():
                    add_blocks(kxp_ref, vxp_ref, i, n_past - 1, [key_in_block < past_valid])

                n_full = n_past - 1
                if n_full > 0:
                    alive = stick_left([carry_ref[p] for p in pairs])
            if n_full // SB_UNROLL > 0:
                alive = sweep(kxp_ref, vxp_ref, i, n_full - 1, n_full // SB_UNROLL, SB_UNROLL, alive)
            if n_full % SB_UNROLL:
                alive = sweep(kxp_ref, vxp_ref, i, n_full % SB_UNROLL - 1, n_full % SB_UNROLL, 1, alive)

        for p in pairs:
            o = acc_ref[p]
            hi, lo = _split_bf16(o * o)
            jm = j_ref[...]
            ms = (jnp.dot(hi, jm, preferred_element_type=F32)
                  + jnp.dot(lo, jm, preferred_element_type=F32)) * (1.0 / SB_HEAD_DIM)
            y = o * lax.rsqrt(ms + EPS) * g_ref[:, lanes[p]]
            o_ref[0, rows, lanes[p]] = y.astype(BF16)
        return 0

    lax.fori_loop(0, n_q, q_tile, 0)


def _attention(q, k_cur, v_cur, k_past, v_past, past_len, g_sb_out):
    b, lq, _ = q.shape
    tk = SB_KEYS
    tq = min(SB_QUERY_TILE, lq)
    assert lq % tq == 0 and tq % tk == 0
    past_heads = k_past is not None and k_past.ndim == 4
    if k_past is None:
        k_past = jnp.zeros((1, tk, SB_WIDTH), BF16)
        v_past = jnp.zeros((1, tk, SB_WIDTH), BF16)
        n_past, past_valid = 0, tk
    else:
        assert k_past.shape[-2] % tk == 0 and (not past_heads or past_len == k_past.shape[2])
        n_past = -(-past_len // tk)
        past_valid = past_len - (n_past - 1) * tk
    bp, p = k_past.shape[0], k_past.shape[-2]
    jj = jnp.arange(2 * tk)
    same_head = (jj[:, None] // tk) == (jj[None, :] // tk)
    bdu = (same_head & (jj[:, None] > jj[None, :])).astype(BF16)
    bdo = same_head.astype(BF16)
    ll = jnp.arange(LANES) // SB_HEAD_DIM
    jm = (ll[:, None] == ll[None, :]).astype(BF16)
    width = SB_PAIRS * PAIR
    cur_spec = pl.BlockSpec((1, lq, width), lambda bi, hp: (bi, 0, hp))
    if past_heads:
        past_spec = pl.BlockSpec((1, 2 * SB_PAIRS, p, SB_HEAD_DIM), lambda bi, hp: (bi, hp, 0, 0))
    else:
        past_spec = pl.BlockSpec((1, p, width), (lambda bi, hp: (bi, 0, hp)) if bp > 1
                                 else (lambda bi, hp: (0, 0, hp)))
    kernel = functools.partial(_attention_kernel, tq=tq, n_q=lq // tq, n_past=n_past,
                               past_valid=past_valid, past_heads=past_heads)
    stacked = lambda n: pltpu.VMEM((SB_PAIRS, n, 2 * tk, PAIR), BF16)
    return pl.pallas_call(
        kernel,
        grid=(b, SB_WIDTH // width),
        in_specs=[cur_spec, cur_spec, cur_spec, past_spec, past_spec,
                  _const_spec(bdu.shape), _const_spec(bdo.shape), _const_spec(jm.shape),
                  pl.BlockSpec((1, width), lambda bi, hp: (0, hp))],
        out_specs=cur_spec,
        out_shape=jax.ShapeDtypeStruct((b, lq, SB_WIDTH), BF16),
        scratch_shapes=[stacked(lq // tk), stacked(lq // tk), stacked(p // tk), stacked(p // tk),
                        pltpu.VMEM((SB_PAIRS, tq, 2 * tk), F32), pltpu.VMEM((SB_PAIRS, tq, PAIR), F32)],
        compiler_params=pltpu.CompilerParams(
            dimension_semantics=("arbitrary", "arbitrary"), vmem_limit_bytes=VMEM_LIMIT),
        name="attention",
    )(q, k_cur, v_cur, k_past, v_past, bdu, bdo, jm, g_sb_out.reshape(1, SB_WIDTH))


def _retention_kernel(q_ref, k_ref, v_ref, gate_ref, s0_ref, g_ref, o_ref, s_ref, *, chunk, n_chunks):
    c = chunk
    head = pl.program_id(1).astype(F32)
    log_g = jnp.log(1.0 - jnp.exp2(jnp.full((1, LANES), -5.0, F32) - head))
    n_row = lax.broadcasted_iota(jnp.int32, (c, LANES), 0).astype(F32)
    q_decay = jnp.exp((n_row + 1.0) * log_g)
    k_decay = jnp.exp((c - 1.0 - n_row) * log_g)
    s_decay = jnp.exp(float(c) * log_g)
    diff = (lax.broadcasted_iota(jnp.int32, (c, c), 0)
            - lax.broadcasted_iota(jnp.int32, (c, c), 1))
    log_g_cc = log_g if c == LANES else jnp.log(1.0 - jnp.exp2(jnp.full((1, c), -5.0, F32) - head))
    decay = jnp.where(diff >= 0, jnp.exp(jnp.maximum(diff, 0).astype(F32) * log_g_cc), 0.0)
    gain = g_ref[...]

    def body(ci, s):
        rows = pl.ds(pl.multiple_of(ci * c, c), c)
        q = q_ref[0, rows, :]
        k = k_ref[0, rows, :]
        v = v_ref[0, rows, :]
        qk = lax.dot_general(q, k, (((1,), (1,)), ((), ())), preferred_element_type=F32)
        inner = jnp.dot((qk * decay).astype(BF16), v, preferred_element_type=F32)
        cross = jnp.dot(q, s.astype(BF16), preferred_element_type=F32) * q_decay
        o = inner + cross
        k_dec = (k.astype(F32) * k_decay).astype(BF16)
        s_new = s_decay * s + lax.dot_general(k_dec, v, (((0,), (0,)), ((), ())),
                                              preferred_element_type=F32)
        y = o * lax.rsqrt(jnp.mean(o * o, axis=-1, keepdims=True) + EPS) * gain
        gate = gate_ref[0, rows, :].astype(F32)
        o_ref[0, rows, :] = (y * (gate * jax.nn.sigmoid(gate))).astype(BF16)
        return s_new

    s_ref[0, 0] = lax.fori_loop(0, n_chunks, body, s0_ref[0, 0], unroll=math.gcd(n_chunks, RET_UNROLL))


def _retention(rq, rk, rv, rgate, s0, g_ret_out, chunk):
    b, l, _ = rq.shape
    assert l % chunk == 0
    bs = s0.shape[0]
    seq_spec = pl.BlockSpec((1, l, RET_DK), lambda bi, hh: (bi, 0, hh))
    state_spec = pl.BlockSpec((1, 1, RET_DK, RET_DV), lambda bi, hh: (bi, hh, 0, 0))
    s0_spec = state_spec if bs > 1 else pl.BlockSpec((1, 1, RET_DK, RET_DV), lambda bi, hh: (0, hh, 0, 0))
    kernel = functools.partial(_retention_kernel, chunk=chunk, n_chunks=l // chunk)
    return pl.pallas_call(
        kernel,
        grid=(b, RET_HEADS),
        in_specs=[seq_spec, seq_spec, seq_spec, seq_spec, s0_spec,
                  pl.BlockSpec((1, RET_DV), lambda bi, hh: (0, hh))],
        out_specs=[seq_spec, state_spec],
        out_shape=[jax.ShapeDtypeStruct((b, l, RET_WIDTH), BF16),
                   jax.ShapeDtypeStruct((b, RET_HEADS, RET_DK, RET_DV), F32)],
        compiler_params=pltpu.CompilerParams(
            dimension_semantics=("arbitrary", "arbitrary"), vmem_limit_bytes=VMEM_LIMIT),
        name="retention",
    )(rq, rk, rv, rgate, s0, g_ret_out.reshape(1, RET_WIDTH))


def _merge_ffn_kernel(x_ref, sb_ref, ret_ref, conv0_ref, wo_ref, gf_ref, wug_ref, wuv_ref,
                      cw_ref, cb_ref, wd_ref, gl_ref, y_ref, conv_ref,
                      h_ref, hn_ref, ubuf_ref, act_ref, carry_ref, *, tm):
    i = pl.program_id(1)

    @pl.when(i == 0)
    def _():
        carry_ref[...] = conv0_ref[0]

    mixed = jnp.concatenate([sb_ref[0], ret_ref[0]], axis=1)
    h = x_ref[0] + jnp.dot(mixed, wo_ref[...], preferred_element_type=F32)
    h_ref[...] = h
    ms = jnp.mean(h * h, axis=-1, keepdims=True)
    hn_ref[...] = (h * lax.rsqrt(ms + EPS) * gf_ref[...]).astype(BF16)
    lo = SUBLANES - (CONV_W - 1)

    def up_project(j):
        ubuf = ubuf_ref.at[j % 2]
        hn = hn_ref[...]
        ubuf[0:SUBLANES, :] = carry_ref[j]
        ubuf[SUBLANES:, 0:FF_CHUNK] = jnp.dot(hn, wug_ref[j], preferred_element_type=F32)
        ubuf[SUBLANES:, FF_CHUNK:] = jnp.dot(hn, wuv_ref[j], preferred_element_type=F32)
        carry_ref[j] = ubuf[tm:tm + SUBLANES, :]

    def gated_conv(j):
        ubuf = ubuf_ref.at[j % 2]
        cw = cw_ref[j]
        cb = cb_ref[j]
        strip = min(tm, FF_STRIP)
        for r in range(0, tm, strip):
            ext = ubuf[r:r + strip + SUBLANES, :]
            c = cb
            for tap in range(CONV_W):
                back = CONV_W - 1 - tap
                rows = ext if back == 0 else pltpu.roll(ext, back, 0)
                c = c + cw[tap:tap + 1, :] * rows[SUBLANES:, :]
            gate = c[:, :FF_CHUNK]
            act_ref[r:r + strip, j * FF_CHUNK:(j + 1) * FF_CHUNK] = (
                (gate * jax.nn.sigmoid(gate)) * c[:, FF_CHUNK:]).astype(BF16)

    up_project(0)
    for j in range(N_FF_CHUNKS):
        if j + 1 < N_FF_CHUNKS:
            up_project(j + 1)
        gated_conv(j)

    hh = h_ref[...] + jnp.dot(act_ref[...], wd_ref[...], preferred_element_type=F32)
    ms2 = jnp.mean(hh * hh, axis=-1, keepdims=True)
    y_ref[0] = hh * lax.rsqrt(ms2 + EPS) * gl_ref[...]

    @pl.when(i == pl.num_programs(1) - 1)
    def _():
        conv_ref[0] = carry_ref[...]


def _merge_ffn(x, sb_n, ret_n, conv0, w_out_bf, g_norm_ffn, wug, wuv, cw, cb, wd, g_norm_final, tm):
    b, l, d = x.shape
    bs = conv0.shape[0]
    row_spec = lambda w: pl.BlockSpec((1, tm, w), lambda bi, i: (bi, i, 0))
    conv_shape = (1, N_FF_CHUNKS, SUBLANES, 2 * FF_CHUNK)
    conv_spec = pl.BlockSpec(conv_shape, lambda bi, i: (bi, 0, 0, 0))
    conv0_spec = conv_spec if bs > 1 else pl.BlockSpec(conv_shape, lambda bi, i: (0, 0, 0, 0))
    kernel = functools.partial(_merge_ffn_kernel, tm=tm)
    return pl.pallas_call(
        kernel,
        grid=(b, l // tm),
        in_specs=[row_spec(d), row_spec(SB_WIDTH), row_spec(RET_WIDTH), conv0_spec,
                  _const_spec(w_out_bf.shape), _const_spec((1, d)),
                  _const_spec(wug.shape), _const_spec(wuv.shape),
                  _const_spec(cw.shape), _const_spec(cb.shape), _const_spec(wd.shape),
                  _const_spec((1, d))],
        out_specs=[row_spec(d), conv_spec],
        out_shape=[jax.ShapeDtypeStruct((b, l, d), F32),
                   jax.ShapeDtypeStruct((b,) + conv_shape[1:], F32)],
        scratch_shapes=[pltpu.VMEM((tm, d), F32), pltpu.VMEM((tm, d), BF16),
                        pltpu.VMEM((2, tm + SUBLANES, 2 * FF_CHUNK), F32),
                        pltpu.VMEM((tm, D_FF), BF16),
                        pltpu.VMEM(conv_shape[1:], F32)],
        compiler_params=pltpu.CompilerParams(
            dimension_semantics=("arbitrary", "arbitrary"), vmem_limit_bytes=VMEM_LIMIT),
        name="merge_ffn",
    )(x, sb_n, ret_n, conv0, w_out_bf, g_norm_ffn.reshape(1, d), wug, wuv, cw, cb, wd,
      g_norm_final.reshape(1, d))


def _rope_tables(pos):
    half = RET_DK // 2
    inv = ROPE_BASE ** (-jnp.arange(half, dtype=F32) / half)
    ang = pos.astype(F32)[:, None] * inv[None, :]
    cos, sin = jnp.cos(ang), jnp.sin(ang)
    return jnp.concatenate([cos, cos], axis=1), jnp.concatenate([-sin, sin], axis=1)


def _conv_state_to_chunks(state):
    b = state.shape[0]
    s = state.reshape(b, CONV_W - 1, 2, N_FF_CHUNKS, FF_CHUNK).transpose(0, 3, 1, 2, 4)
    s = s.reshape(b, N_FF_CHUNKS, CONV_W - 1, 2 * FF_CHUNK)
    return jnp.pad(s, ((0, 0), (0, 0), (SUBLANES - (CONV_W - 1), 0), (0, 0)))


def _conv_state_from_chunks(chunks):
    b = chunks.shape[0]
    s = chunks[:, :, SUBLANES - (CONV_W - 1):, :].reshape(b, N_FF_CHUNKS, CONV_W - 1, 2, FF_CHUNK)
    return s.transpose(0, 2, 3, 1, 4).reshape(b, CONV_W - 1, 2 * D_FF)


def _pad_rows(a, rows):
    return a if a.shape[1] == rows else jnp.pad(a, ((0, 0), (0, rows - a.shape[1]), (0, 0)))


def _prefix_rows_kernel(k_any, v_any, k_rows_ref, v_rows_ref, k_ref, v_ref):
    del k_any, v_any
    k_ref[...] = k_rows_ref[...]
    v_ref[...] = v_rows_ref[...]


def _write_prefix_rows(k_big, v_big, k_rows, v_rows):
    b = k_big.shape[0]
    blk = (1,) + k_rows.shape[1:]
    any_spec = pl.BlockSpec(memory_space=pl.ANY)
    rows_spec = pl.BlockSpec(blk, lambda bi: (0, 0, 0, 0))
    out_spec = pl.BlockSpec(blk, lambda bi: (bi, 0, 0, 0))
    big = jax.ShapeDtypeStruct(k_big.shape, k_big.dtype)
    return pl.pallas_call(
        _prefix_rows_kernel,
        grid=(b,),
        in_specs=[any_spec, any_spec, rows_spec, rows_spec],
        out_specs=[out_spec, out_spec],
        out_shape=[big, big],
        input_output_aliases={0: 0, 1: 1},
        name="prefix_rows",
    )(k_big, v_big, k_rows, v_rows)


def _stream_step(x, pos0, k_past, v_past, past_len, s0, conv0, wts, tm, chunk, head_row0=0):
    l = x.shape[1]
    cos2, sin2 = _rope_tables(pos0 + jnp.arange(l))
    q, k, v, k_out, v_out, rq, rk, rv, rgate = _project(x, wts["g_norm_mix"], wts["w_in"], cos2, sin2,
                                                       tm, head_row0)
    lq = -(-l // SB_KEYS) * SB_KEYS
    sb_n = _attention(_pad_rows(q, lq), _pad_rows(k, lq), _pad_rows(v, lq),
                      k_past, v_past, past_len, wts["g_sb_out"])[:, :l]
    ret_n, s_new = _retention(rq, rk, rv, rgate, s0, wts["g_ret_out"], chunk)
    y, conv_new = _merge_ffn(x, sb_n, ret_n, conv0, wts["w_out"], wts["g_norm_ffn"], wts["wug"],
                             wts["wuv"], wts["cw"], wts["cb"], wts["wd"], wts["g_norm_final"], tm)
    return y, k, v, k_out, v_out, s_new, conv_new


def kernel(x_prompt, x_sample, cache_sb_k, cache_sb_v, state_ret, state_conv, meta_tokens, g_norm_mix, w_in, g_sb_out, g_ret_out, w_out, g_norm_ffn, w_up, conv_w, conv_b, w_down, g_norm_final):
    b, seq, d = x_prompt.shape
    bd, ls, _ = x_sample.shape
    past = cache_sb_k.shape[2]

    def ff_cols(a):
        r = a.shape[0]
        return a.reshape(r, 2, N_FF_CHUNKS, FF_CHUNK).transpose(2, 0, 1, 3).reshape(N_FF_CHUNKS, r, 2 * FF_CHUNK)

    w_up_bf = w_up.astype(BF16)
    wts = dict(
        g_norm_mix=g_norm_mix, g_sb_out=g_sb_out, g_ret_out=g_ret_out, g_norm_ffn=g_norm_ffn,
        g_norm_final=g_norm_final,
        w_in=w_in.astype(BF16), w_out=w_out.astype(BF16),
        wug=w_up_bf[:, :D_FF].reshape(d, N_FF_CHUNKS, FF_CHUNK).transpose(1, 0, 2),
        wuv=w_up_bf[:, D_FF:].reshape(d, N_FF_CHUNKS, FF_CHUNK).transpose(1, 0, 2),
        cw=ff_cols(conv_w), cb=ff_cols(conv_b.reshape(1, 2 * D_FF)),
        wd=w_down.astype(BF16),
    )

    zero_state = jnp.zeros((1, RET_HEADS, RET_DK, RET_DV), F32)
    zero_conv = jnp.zeros((1, N_FF_CHUNKS, SUBLANES, 2 * FF_CHUNK), F32)
    _, k_m, v_m, k_m_out, v_m_out, s_meta, conv_meta = _stream_step(
        meta_tokens[None], -N_META, None, None, 0, zero_state, zero_conv, wts, N_META, N_META)

    y_prompt, _, _, k_p_out, v_p_out, s_prompt, conv_prompt = _stream_step(
        x_prompt, 0, _pad_rows(k_m, SB_KEYS), _pad_rows(v_m, SB_KEYS), N_META, s_meta, conv_meta,
        wts, 512, 256, head_row0=N_META)
    new_k_prompt, new_v_prompt = _write_prefix_rows(k_p_out, v_p_out, k_m_out, v_m_out)

    y_sample, _, _, k_s_out, v_s_out, s_sample, conv_sample = _stream_step(
        x_sample, past, cache_sb_k, cache_sb_v, past, state_ret,
        _conv_state_to_chunks(state_conv), wts, ls, ls)

    return (y_prompt, y_sample, new_k_prompt, new_v_prompt, s_prompt,
            _conv_state_from_chunks(conv_prompt), k_s_out, v_s_out, s_sample,
            _conv_state_from_chunks(conv_sample))
```

```python
import functools
import math

import jax
import jax.numpy as jnp
from jax import lax
from jax.experimental import pallas as pl
from jax.experimental.pallas import tpu as pltpu

D_MODEL = 1024
N_META = 16
SB_HEADS = 8
SB_HEAD_DIM = 64
SB_WIDTH = SB_HEADS * SB_HEAD_DIM
RET_HEADS = 4
RET_DK = 128
RET_DV = 128
RET_WIDTH = RET_HEADS * RET_DV
MIX_WIDTH = SB_WIDTH + RET_WIDTH
GROUP = 512
N_GROUPS = 7
IN_WIDTH = N_GROUPS * GROUP
D_FF = 2816
CONV_W = 3
ROPE_BASE = 10000.0
EPS = 1e-5

LANES = 128
SUBLANES = 8
FF_CHUNK = 256
N_FF_CHUNKS = D_FF // FF_CHUNK
FF_STRIP = 64
RET_UNROLL = 8
SB_KEYS = 128
SB_QUERY_TILE = 512
SB_UNROLL = 2
PAIR = 2 * SB_HEAD_DIM
SB_PAIRS = 2
SB_RECENT = 512
SB_EXHAUSTED = 152.0
SB_Q_SCALE = SB_HEAD_DIM ** -0.5 * math.log2(math.e)
VMEM_LIMIT = 56 * 1024 * 1024

BF16 = jnp.bfloat16
F32 = jnp.float32


def _const_spec(shape):
    zeros = (0,) * len(shape)
    return pl.BlockSpec(shape, lambda *_: zeros, pipeline_mode=pl.Buffered(1))


def _project_kernel(x_ref, g_ref, w_ref, cos_ref, sin_ref,
                    q_ref, k_ref, v_ref, ko_ref, vo_ref, rq_ref, rk_ref, rv_ref, rg_ref):
    x = x_ref[0]
    ms = jnp.mean(x * x, axis=-1, keepdims=True)
    h = (x * lax.rsqrt(ms + EPS) * g_ref[...]).astype(BF16)

    def group(i):
        return jnp.dot(h, w_ref[:, i * GROUP:(i + 1) * GROUP], preferred_element_type=F32)

    def split_heads(p, out_ref):
        for hh in range(SB_HEADS):
            out_ref[0, hh] = p[:, hh * SB_HEAD_DIM:(hh + 1) * SB_HEAD_DIM]

    def rope(p, out_ref, scale):
        cos = cos_ref[...]
        sin = sin_ref[...]
        for hh in range(RET_HEADS):
            t = p[:, hh * RET_DK:(hh + 1) * RET_DK]
            r = t * cos + pltpu.roll(t, RET_DK // 2, 1) * sin
            if scale is not None:
                r = r * scale
            out_ref[0, :, hh * RET_DK:(hh + 1) * RET_DK] = r.astype(BF16)

    q_ref[0] = (group(0) * SB_Q_SCALE).astype(BF16)
    pk = group(1)
    k_ref[0] = pk.astype(BF16)
    split_heads(pk, ko_ref)
    pv = group(2)
    v_ref[0] = pv.astype(BF16)
    split_heads(pv, vo_ref)
    rope(group(3), rq_ref, None)
    rope(group(4), rk_ref, RET_DK ** -0.5)
    rv_ref[0] = group(5).astype(BF16)
    rg_ref[0] = group(6).astype(BF16)


def _project(x, g_norm, w_in_bf, cos2, sin2, tm, head_row0=0):
    b, l, d = x.shape
    grid = (b, l // tm)
    row_spec = lambda w: pl.BlockSpec((1, tm, w), lambda bi, i: (bi, i, 0))
    head_blk = (1, SB_HEADS, tm, SB_HEAD_DIM)
    head_spec = pl.BlockSpec(tuple(pl.Element(n) for n in head_blk),
                             lambda bi, i: (bi, 0, pl.multiple_of(head_row0 + i * tm, SUBLANES), 0))
    tab_spec = pl.BlockSpec((tm, RET_DK), lambda bi, i: (i, 0))
    act = jax.ShapeDtypeStruct((b, l, GROUP), BF16)
    heads = jax.ShapeDtypeStruct((b, SB_HEADS, head_row0 + l, SB_HEAD_DIM), F32)
    return pl.pallas_call(
        _project_kernel,
        grid=grid,
        in_specs=[row_spec(d), _const_spec((1, d)), _const_spec((d, IN_WIDTH)), tab_spec, tab_spec],
        out_specs=[row_spec(GROUP)] * 3 + [head_spec] * 2 + [row_spec(GROUP)] * 4,
        out_shape=[act] * 3 + [heads] * 2 + [act] * 4,
        compiler_params=pltpu.CompilerParams(
            dimension_semantics=("arbitrary", "arbitrary"), vmem_limit_bytes=VMEM_LIMIT),
        name="project",
    )(x, g_norm.reshape(1, d), w_in_bf, cos2, sin2)


def _split_bf16(x):
    hi = x.astype(BF16)
    lo = (x - hi.astype(F32)).astype(BF16)
    return hi, lo


def _attention_kernel(q_ref, kc_ref, vc_ref, kp_ref, vp_ref, bdu_ref, bdo_ref, j_ref, g_ref,
                      o_ref, left_ref, kxc_ref, vxc_ref, kxp_ref, vxp_ref, carry_ref, acc_ref,
                      *, tq, n_q, n_past, past_valid, report_left):
    tk = SB_KEYS
    per_tile = tq // tk
    head0 = lax.broadcasted_iota(jnp.int32, (tk, LANES), 1) < SB_HEAD_DIM
    key_in_block = lax.broadcasted_iota(jnp.int32, (tq, 2 * tk), 1) & (tk - 1)
    delta = key_in_block - lax.broadcasted_iota(jnp.int32, (tq, 2 * tk), 0)

    pairs = range(SB_PAIRS)
    lanes = [slice(p * PAIR, (p + 1) * PAIR) for p in pairs]

    def expand(src_ref, dst_ref, n_blocks):
        def body(j, _):
            for p in pairs:
                blk = src_ref[0, pl.ds(pl.multiple_of(j * tk, tk), tk), lanes[p]]
                zero = jnp.zeros_like(blk)
                dst_ref[p, j, 0:tk, :] = jnp.where(head0, blk, zero)
                dst_ref[p, j, tk:2 * tk, :] = jnp.where(head0, zero, blk)
            return 0
        lax.fori_loop(0, n_blocks, body, 0)

    expand(kc_ref, kxc_ref, n_q * per_tile)
    expand(vc_ref, vxc_ref, n_q * per_tile)
    if n_past > 0:
        expand(kp_ref, kxp_ref, n_past)
        expand(vp_ref, vxp_ref, n_past)

    def add_blocks(kx_ref, vx_ref, i, j_last, masks, r0=0):
        rows = pl.ds(pl.multiple_of(i * tq + r0, tk), tq - r0)
        carry = [carry_ref[p, r0:, :] for p in pairs]
        for u, mask in enumerate(masks):
            j = j_last - u
            for p in pairs:
                z = lax.dot_general(q_ref[0, rows, lanes[p]], kx_ref[p, j], (((1,), (1,)), ((), ())),
                                    preferred_element_type=F32)
                neg_abs = lax.bitcast_convert_type(
                    lax.bitcast_convert_type(z, jnp.uint32) | jnp.uint32(0x80000000), F32)
                sp = jnp.maximum(z, 0.0) + jnp.log2(1.0 + jnp.exp2(neg_abs))
                spm = sp if mask is None else jnp.where(mask, sp, 0.0)
                hi = spm.astype(BF16)
                later = jnp.dot(hi, bdu_ref[...], preferred_element_type=F32)
                total = jnp.dot(hi, bdo_ref[...], preferred_element_type=F32)
                a = jnp.exp2((z - sp) - (later + carry[p]))
                if mask is not None:
                    a = jnp.where(mask, a, 0.0)
                acc_ref[p, r0:, :] += jnp.dot(a.astype(BF16), vx_ref[p, j], preferred_element_type=F32)
                carry[p] = carry[p] + total
        for p in pairs:
            carry_ref[p, r0:, :] = carry[p]
        return carry

    def stick_left(carry):
        least = functools.reduce(jnp.minimum, carry)
        return (jnp.min(least) < SB_EXHAUSTED).astype(jnp.int32)

    def sweep(kx_ref, vx_ref, i, j_top, n_steps, unroll, alive):
        def cond(state):
            m, live = state
            return jnp.logical_and(m < n_steps, live > 0)

        def body(state):
            m, _ = state
            carry = add_blocks(kx_ref, vx_ref, i, j_top - m * unroll, [None] * unroll)
            return m + 1, stick_left(carry)

        return lax.while_loop(cond, body, (jnp.int32(0), alive))[1]

    def q_tile(i, any_left):
        rows = pl.ds(pl.multiple_of(i * tq, tq), tq)
        carry_ref[...] = jnp.zeros_like(carry_ref)
        acc_ref[...] = jnp.zeros_like(acc_ref)

        for jj in range(per_tile - 1, -1, -1):
            r0 = jj * tk
            add_blocks(kxc_ref, vxc_ref, i, i * per_tile + jj, [delta[r0:, :] < -r0], r0)

        unroll = SB_UNROLL if per_tile % SB_UNROLL == 0 else 1
        alive = sweep(kxc_ref, vxc_ref, i, i * per_tile - 1, (i * per_tile) // unroll, unroll,
                      stick_left([carry_ref[p] for p in pairs]))

        if n_past > 0:
            n_full = n_past
            if past_valid < tk:
                @pl.when(alive > 0)
                def _():
                    add_blocks(kxp_ref, vxp_ref, i, n_past - 1, [key_in_block < past_valid])

                n_full = n_past - 1
                if n_full > 0:
                    alive = stick_left([carry_ref[p] for p in pairs])
            if n_full // SB_UNROLL > 0:
                alive = sweep(kxp_ref, vxp_ref, i, n_full - 1, n_full // SB_UNROLL, SB_UNROLL, alive)
            if n_full % SB_UNROLL:
                alive = sweep(kxp_ref, vxp_ref, i, n_full % SB_UNROLL - 1, n_full % SB_UNROLL, 1, alive)

        for p in pairs:
            o = acc_ref[p]
            hi, lo = _split_bf16(o * o)
            jm = j_ref[...]
            ms = (jnp.dot(hi, jm, preferred_element_type=F32)
                  + jnp.dot(lo, jm, preferred_element_type=F32)) * (1.0 / SB_HEAD_DIM)
            y = o * lax.rsqrt(ms + EPS) * g_ref[:, lanes[p]]
            o_ref[0, rows, lanes[p]] = y.astype(BF16)
        if not report_left:
            return any_left
        return jnp.maximum(any_left, stick_left([carry_ref[p] for p in pairs]))

    left_ref[...] = jnp.full(left_ref.shape, lax.fori_loop(0, n_q, q_tile, jnp.int32(0)), jnp.int32)


def _attention(q, k_cur, v_cur, k_past, v_past, past_len, g_sb_out, report_left=False):
    b, lq, _ = q.shape
    tk = SB_KEYS
    tq = min(SB_QUERY_TILE, lq)
    assert lq % tq == 0 and tq % tk == 0
    if k_past is None:
        k_past = jnp.zeros((1, tk, SB_WIDTH), BF16)
        v_past = jnp.zeros((1, tk, SB_WIDTH), BF16)
        n_past, past_valid = 0, tk
    else:
        assert k_past.shape[1] % tk == 0
        n_past = -(-past_len // tk)
        past_valid = past_len - (n_past - 1) * tk
        k_past = k_past[:, :n_past * tk]
        v_past = v_past[:, :n_past * tk]
    bp, p, _ = k_past.shape
    jj = jnp.arange(2 * tk)
    same_head = (jj[:, None] // tk) == (jj[None, :] // tk)
    bdu = (same_head & (jj[:, None] > jj[None, :])).astype(BF16)
    bdo = same_head.astype(BF16)
    ll = jnp.arange(LANES) // SB_HEAD_DIM
    jm = (ll[:, None] == ll[None, :]).astype(BF16)
    width = SB_PAIRS * PAIR
    cur_spec = pl.BlockSpec((1, lq, width), lambda bi, hp: (bi, 0, hp))
    past_spec = pl.BlockSpec((1, p, width), (lambda bi, hp: (bi, 0, hp)) if bp > 1
                             else (lambda bi, hp: (0, 0, hp)))
    kernel = functools.partial(_attention_kernel, tq=tq, n_q=lq // tq, n_past=n_past,
                               past_valid=past_valid, report_left=report_left)
    stacked = lambda n: pltpu.VMEM((SB_PAIRS, n, 2 * tk, PAIR), BF16)
    n_groups = SB_WIDTH // width
    out, left = pl.pallas_call(
        kernel,
        grid=(b, n_groups),
        in_specs=[cur_spec, cur_spec, cur_spec, past_spec, past_spec,
                  _const_spec(bdu.shape), _const_spec(bdo.shape), _const_spec(jm.shape),
                  pl.BlockSpec((1, width), lambda bi, hp: (0, hp))],
        out_specs=[cur_spec, pl.BlockSpec((1, 1, SUBLANES, LANES), lambda bi, hp: (bi, hp, 0, 0))],
        out_shape=[jax.ShapeDtypeStruct((b, lq, SB_WIDTH), BF16),
                   jax.ShapeDtypeStruct((b, n_groups, SUBLANES, LANES), jnp.int32)],
        scratch_shapes=[stacked(lq // tk), stacked(lq // tk), stacked(p // tk), stacked(p // tk),
                        pltpu.VMEM((SB_PAIRS, tq, 2 * tk), F32), pltpu.VMEM((SB_PAIRS, tq, PAIR), F32)],
        compiler_params=pltpu.CompilerParams(
            dimension_semantics=("arbitrary", "arbitrary"), vmem_limit_bytes=VMEM_LIMIT),
        name="attention",
    )(q, k_cur, v_cur, k_past, v_past, bdu, bdo, jm, g_sb_out.reshape(1, SB_WIDTH))
    return out, left


def _attention_over_cache(q, k_cur, v_cur, cache_k, cache_v, g_sb_out):
    bd, _, past, _ = cache_k.shape
    to_rows = lambda c: c.transpose(0, 2, 1, 3).reshape(bd, c.shape[2], SB_WIDTH).astype(BF16)
    recent = min(past, SB_RECENT)
    out, left = _attention(q, k_cur, v_cur, to_rows(cache_k[:, :, past - recent:]),
                           to_rows(cache_v[:, :, past - recent:]), recent, g_sb_out, report_left=True)
    if recent == past:
        return out
    return lax.cond(
        jnp.any(left > 0),
        lambda: _attention(q, k_cur, v_cur, to_rows(cache_k), to_rows(cache_v), past, g_sb_out)[0],
        lambda: out)


def _retention_kernel(q_ref, k_ref, v_ref, gate_ref, s0_ref, g_ref, o_ref, s_ref, *, chunk, n_chunks):
    c = chunk
    head = pl.program_id(1).astype(F32)
    log_g = jnp.log(1.0 - jnp.exp2(jnp.full((1, LANES), -5.0, F32) - head))
    n_row = lax.broadcasted_iota(jnp.int32, (c, LANES), 0).astype(F32)
    q_decay = jnp.exp((n_row + 1.0) * log_g)
    k_decay = jnp.exp((c - 1.0 - n_row) * log_g)
    s_decay = jnp.exp(float(c) * log_g)
    diff = (lax.broadcasted_iota(jnp.int32, (c, c), 0)
            - lax.broadcasted_iota(jnp.int32, (c, c), 1))
    log_g_cc = log_g if c == LANES else jnp.log(1.0 - jnp.exp2(jnp.full((1, c), -5.0, F32) - head))
    decay = jnp.where(diff >= 0, jnp.exp(jnp.maximum(diff, 0).astype(F32) * log_g_cc), 0.0)
    gain = g_ref[...]

    def body(ci, s):
        rows = pl.ds(pl.multiple_of(ci * c, c), c)
        q = q_ref[0, rows, :]
        k = k_ref[0, rows, :]
        v = v_ref[0, rows, :]
        qk = lax.dot_general(q, k, (((1,), (1,)), ((), ())), preferred_element_type=F32)
        inner = jnp.dot((qk * decay).astype(BF16), v, preferred_element_type=F32)
        cross = jnp.dot(q, s.astype(BF16), preferred_element_type=F32) * q_decay
        o = inner + cross
        k_dec = (k.astype(F32) * k_decay).astype(BF16)
        s_new = s_decay * s + lax.dot_general(k_dec, v, (((0,), (0,)), ((), ())),
                                              preferred_element_type=F32)
        y = o * lax.rsqrt(jnp.mean(o * o, axis=-1, keepdims=True) + EPS) * gain
        gate = gate_ref[0, rows, :].astype(F32)
        o_ref[0, rows, :] = (y * (gate * jax.nn.sigmoid(gate))).astype(BF16)
        return s_new

    s_ref[0, 0] = lax.fori_loop(0, n_chunks, body, s0_ref[0, 0], unroll=math.gcd(n_chunks, RET_UNROLL))


def _retention(rq, rk, rv, rgate, s0, g_ret_out, chunk):
    b, l, _ = rq.shape
    assert l % chunk == 0
    bs = s0.shape[0]
    seq_spec = pl.BlockSpec((1, l, RET_DK), lambda bi, hh: (bi, 0, hh))
    state_spec = pl.BlockSpec((1, 1, RET_DK, RET_DV), lambda bi, hh: (bi, hh, 0, 0))
    s0_spec = state_spec if bs > 1 else pl.BlockSpec((1, 1, RET_DK, RET_DV), lambda bi, hh: (0, hh, 0, 0))
    kernel = functools.partial(_retention_kernel, chunk=chunk, n_chunks=l // chunk)
    return pl.pallas_call(
        kernel,
        grid=(b, RET_HEADS),
        in_specs=[seq_spec, seq_spec, seq_spec, seq_spec, s0_spec,
                  pl.BlockSpec((1, RET_DV), lambda bi, hh: (0, hh))],
        out_specs=[seq_spec, state_spec],
        out_shape=[jax.ShapeDtypeStruct((b, l, RET_WIDTH), BF16),
                   jax.ShapeDtypeStruct((b, RET_HEADS, RET_DK, RET_DV), F32)],
        compiler_params=pltpu.CompilerParams(
            dimension_semantics=("arbitrary", "arbitrary"), vmem_limit_bytes=VMEM_LIMIT),
        name="retention",
    )(rq, rk, rv, rgate, s0, g_ret_out.reshape(1, RET_WIDTH))


def _merge_ffn_kernel(x_ref, sb_ref, ret_ref, conv0_ref, wo_ref, gf_ref, wug_ref, wuv_ref,
                      cw_ref, cb_ref, wd_ref, gl_ref, y_ref, conv_ref,
                      h_ref, hn_ref, ubuf_ref, act_ref, carry_ref, *, tm):
    i = pl.program_id(1)

    @pl.when(i == 0)
    def _():
        carry_ref[...] = conv0_ref[0]

    mixed = jnp.concatenate([sb_ref[0], ret_ref[0]], axis=1)
    h = x_ref[0] + jnp.dot(mixed, wo_ref[...], preferred_element_type=F32)
    h_ref[...] = h
    ms = jnp.mean(h * h, axis=-1, keepdims=True)
    hn_ref[...] = (h * lax.rsqrt(ms + EPS) * gf_ref[...]).astype(BF16)
    lo = SUBLANES - (CONV_W - 1)

    def up_project(j):
        ubuf = ubuf_ref.at[j % 2]
        hn = hn_ref[...]
        ubuf[0:SUBLANES, :] = carry_ref[j]
        ubuf[SUBLANES:, 0:FF_CHUNK] = jnp.dot(hn, wug_ref[j], preferred_element_type=F32)
        ubuf[SUBLANES:, FF_CHUNK:] = jnp.dot(hn, wuv_ref[j], preferred_element_type=F32)
        carry_ref[j] = ubuf[tm:tm + SUBLANES, :]

    def gated_conv(j):
        ubuf = ubuf_ref.at[j % 2]
        cw = cw_ref[j]
        cb = cb_ref[j]
        strip = min(tm, FF_STRIP)
        for r in range(0, tm, strip):
            ext = ubuf[r:r + strip + SUBLANES, :]
            c = cb
            for tap in range(CONV_W):
                back = CONV_W - 1 - tap
                rows = ext if back == 0 else pltpu.roll(ext, back, 0)
                c = c + cw[tap:tap + 1, :] * rows[SUBLANES:, :]
            gate = c[:, :FF_CHUNK]
            act_ref[r:r + strip, j * FF_CHUNK:(j + 1) * FF_CHUNK] = (
                (gate * jax.nn.sigmoid(gate)) * c[:, FF_CHUNK:]).astype(BF16)

    up_project(0)
    for j in range(N_FF_CHUNKS):
        if j + 1 < N_FF_CHUNKS:
            up_project(j + 1)
        gated_conv(j)

    hh = h_ref[...] + jnp.dot(act_ref[...], wd_ref[...], preferred_element_type=F32)
    ms2 = jnp.mean(hh * hh, axis=-1, keepdims=True)
    y_ref[0] = hh * lax.rsqrt(ms2 + EPS) * gl_ref[...]

    @pl.when(i == pl.num_programs(1) - 1)
    def _():
        conv_ref[0] = carry_ref[...]


def _merge_ffn(x, sb_n, ret_n, conv0, w_out_bf, g_norm_ffn, wug, wuv, cw, cb, wd, g_norm_final, tm):
    b, l, d = x.shape
    bs = conv0.shape[0]
    row_spec = lambda w: pl.BlockSpec((1, tm, w), lambda bi, i: (bi, i, 0))
    conv_shape = (1, N_FF_CHUNKS, SUBLANES, 2 * FF_CHUNK)
    conv_spec = pl.BlockSpec(conv_shape, lambda bi, i: (bi, 0, 0, 0))
    conv0_spec = conv_spec if bs > 1 else pl.BlockSpec(conv_shape, lambda bi, i: (0, 0, 0, 0))
    kernel = functools.partial(_merge_ffn_kernel, tm=tm)
    return pl.pallas_call(
        kernel,
        grid=(b, l // tm),
        in_specs=[row_spec(d), row_spec(SB_WIDTH), row_spec(RET_WIDTH), conv0_spec,
                  _const_spec(w_out_bf.shape), _const_spec((1, d)),
                  _const_spec(wug.shape), _const_spec(wuv.shape),
                  _const_spec(cw.shape), _const_spec(cb.shape), _const_spec(wd.shape),
                  _const_spec((1, d))],
        out_specs=[row_spec(d), conv_spec],
        out_shape=[jax.ShapeDtypeStruct((b, l, d), F32),
                   jax.ShapeDtypeStruct((b,) + conv_shape[1:], F32)],
        scratch_shapes=[pltpu.VMEM((tm, d), F32), pltpu.VMEM((tm, d), BF16),
                        pltpu.VMEM((2, tm + SUBLANES, 2 * FF_CHUNK), F32),
                        pltpu.VMEM((tm, D_FF), BF16),
                        pltpu.VMEM(conv_shape[1:], F32)],
        compiler_params=pltpu.CompilerParams(
            dimension_semantics=("arbitrary", "arbitrary"), vmem_limit_bytes=VMEM_LIMIT),
        name="merge_ffn",
    )(x, sb_n, ret_n, conv0, w_out_bf, g_norm_ffn.reshape(1, d), wug, wuv, cw, cb, wd,
      g_norm_final.reshape(1, d))


def _rope_tables(pos):
    half = RET_DK // 2
    inv = ROPE_BASE ** (-jnp.arange(half, dtype=F32) / half)
    ang = pos.astype(F32)[:, None] * inv[None, :]
    cos, sin = jnp.cos(ang), jnp.sin(ang)
    return jnp.concatenate([cos, cos], axis=1), jnp.concatenate([-sin, sin], axis=1)


def _conv_state_to_chunks(state):
    b = state.shape[0]
    s = state.reshape(b, CONV_W - 1, 2, N_FF_CHUNKS, FF_CHUNK).transpose(0, 3, 1, 2, 4)
    s = s.reshape(b, N_FF_CHUNKS, CONV_W - 1, 2 * FF_CHUNK)
    return jnp.pad(s, ((0, 0), (0, 0), (SUBLANES - (CONV_W - 1), 0), (0, 0)))


def _conv_state_from_chunks(chunks):
    b = chunks.shape[0]
    s = chunks[:, :, SUBLANES - (CONV_W - 1):, :].reshape(b, N_FF_CHUNKS, CONV_W - 1, 2, FF_CHUNK)
    return s.transpose(0, 2, 3, 1, 4).reshape(b, CONV_W - 1, 2 * D_FF)


def _pad_rows(a, rows):
    return a if a.shape[1] == rows else jnp.pad(a, ((0, 0), (0, rows - a.shape[1]), (0, 0)))


def _prefix_rows_kernel(k_any, v_any, k_rows_ref, v_rows_ref, k_ref, v_ref):
    del k_any, v_any
    k_ref[...] = k_rows_ref[...]
    v_ref[...] = v_rows_ref[...]


def _write_prefix_rows(k_big, v_big, k_rows, v_rows):
    b = k_big.shape[0]
    blk = (1,) + k_rows.shape[1:]
    any_spec = pl.BlockSpec(memory_space=pl.ANY)
    rows_spec = pl.BlockSpec(blk, lambda bi: (0, 0, 0, 0))
    out_spec = pl.BlockSpec(blk, lambda bi: (bi, 0, 0, 0))
    big = jax.ShapeDtypeStruct(k_big.shape, k_big.dtype)
    return pl.pallas_call(
        _prefix_rows_kernel,
        grid=(b,),
        in_specs=[any_spec, any_spec, rows_spec, rows_spec],
        out_specs=[out_spec, out_spec],
        out_shape=[big, big],
        input_output_aliases={0: 0, 1: 1},
        name="prefix_rows",
    )(k_big, v_big, k_rows, v_rows)


def _stream_step(x, pos0, attend, s0, conv0, wts, tm, chunk, head_row0=0):
    l = x.shape[1]
    cos2, sin2 = _rope_tables(pos0 + jnp.arange(l))
    q, k, v, k_out, v_out, rq, rk, rv, rgate = _project(x, wts["g_norm_mix"], wts["w_in"], cos2, sin2,
                                                       tm, head_row0)
    lq = -(-l // SB_KEYS) * SB_KEYS
    sb_n = attend(_pad_rows(q, lq), _pad_rows(k, lq), _pad_rows(v, lq))[:, :l]
    ret_n, s_new = _retention(rq, rk, rv, rgate, s0, wts["g_ret_out"], chunk)
    y, conv_new = _merge_ffn(x, sb_n, ret_n, conv0, wts["w_out"], wts["g_norm_ffn"], wts["wug"],
                             wts["wuv"], wts["cw"], wts["cb"], wts["wd"], wts["g_norm_final"], tm)
    return y, k, v, k_out, v_out, s_new, conv_new


def kernel(x_prompt, x_sample, cache_sb_k, cache_sb_v, state_ret, state_conv, meta_tokens, g_norm_mix, w_in, g_sb_out, g_ret_out, w_out, g_norm_ffn, w_up, conv_w, conv_b, w_down, g_norm_final):
    b, seq, d = x_prompt.shape
    bd, ls, _ = x_sample.shape
    past = cache_sb_k.shape[2]

    def ff_cols(a):
        r = a.shape[0]
        return a.reshape(r, 2, N_FF_CHUNKS, FF_CHUNK).transpose(2, 0, 1, 3).reshape(N_FF_CHUNKS, r, 2 * FF_CHUNK)

    w_up_bf = w_up.astype(BF16)
    wts = dict(
        g_norm_mix=g_norm_mix, g_sb_out=g_sb_out, g_ret_out=g_ret_out, g_norm_ffn=g_norm_ffn,
        g_norm_final=g_norm_final,
        w_in=w_in.astype(BF16), w_out=w_out.astype(BF16),
        wug=w_up_bf[:, :D_FF].reshape(d, N_FF_CHUNKS, FF_CHUNK).transpose(1, 0, 2),
        wuv=w_up_bf[:, D_FF:].reshape(d, N_FF_CHUNKS, FF_CHUNK).transpose(1, 0, 2),
        cw=ff_cols(conv_w), cb=ff_cols(conv_b.reshape(1, 2 * D_FF)),
        wd=w_down.astype(BF16),
    )

    zero_state = jnp.zeros((1, RET_HEADS, RET_DK, RET_DV), F32)
    zero_conv = jnp.zeros((1, N_FF_CHUNKS, SUBLANES, 2 * FF_CHUNK), F32)
    gain = wts["g_sb_out"]
    _, k_m, v_m, k_m_out, v_m_out, s_meta, conv_meta = _stream_step(
        meta_tokens[None], -N_META, lambda q, k, v: _attention(q, k, v, None, None, 0, gain)[0],
        zero_state, zero_conv, wts, N_META, N_META)

    k_m, v_m = _pad_rows(k_m, SB_KEYS), _pad_rows(v_m, SB_KEYS)
    y_prompt, _, _, k_p_out, v_p_out, s_prompt, conv_prompt = _stream_step(
        x_prompt, 0, lambda q, k, v: _attention(q, k, v, k_m, v_m, N_META, gain)[0],
        s_meta, conv_meta, wts, 512, 256, head_row0=N_META)
    new_k_prompt, new_v_prompt = _write_prefix_rows(k_p_out, v_p_out, k_m_out, v_m_out)

    y_sample, _, _, k_s_out, v_s_out, s_sample, conv_sample = _stream_step(
        x_sample, past, lambda q, k, v: _attention_over_cache(q, k, v, cache_sb_k, cache_sb_v, gain),
        state_ret, _conv_state_to_chunks(state_conv), wts, ls, ls)

    return (y_prompt, y_sample, new_k_prompt, new_v_prompt, s_prompt,
            _conv_state_from_chunks(conv_prompt), k_s_out, v_s_out, s_sample,
            _conv_state_from_chunks(conv_sample))
```

```python
import functools
import math

import jax
import jax.numpy as jnp
from jax import lax
from jax.experimental import pallas as pl
from jax.experimental.pallas import tpu as pltpu

D_MODEL = 1024
N_META = 16
SB_HEADS = 8
SB_HEAD_DIM = 64
SB_WIDTH = SB_HEADS * SB_HEAD_DIM
RET_HEADS = 4
RET_DK = 128
RET_DV = 128
RET_WIDTH = RET_HEADS * RET_DV
MIX_WIDTH = SB_WIDTH + RET_WIDTH
GROUP = 512
N_GROUPS = 7
IN_WIDTH = N_GROUPS * GROUP
D_FF = 2816
CONV_W = 3
ROPE_BASE = 10000.0
EPS = 1e-5

LANES = 128
SUBLANES = 8
FF_CHUNK = 256
N_FF_CHUNKS = D_FF // FF_CHUNK
FF_STRIP = 64
RET_UNROLL = 8
SB_KEYS = 128
SB_QUERY_TILE = 512
SB_UNROLL = 2
PAIR = 2 * SB_HEAD_DIM
SB_PAIRS = 2
SB_RECENT = 512
SB_EXHAUSTED = 152.0
SB_Q_SCALE = SB_HEAD_DIM ** -0.5 * math.log2(math.e)
VMEM_LIMIT = 56 * 1024 * 1024

BF16 = jnp.bfloat16
F32 = jnp.float32


def _const_spec(shape):
    zeros = (0,) * len(shape)
    return pl.BlockSpec(shape, lambda *_: zeros, pipeline_mode=pl.Buffered(1))


def _project_kernel(x_ref, g_ref, w_ref, cos_ref, sin_ref,
                    q_ref, k_ref, v_ref, ko_ref, vo_ref, rq_ref, rk_ref, rv_ref, rg_ref):
    x = x_ref[0]
    ms = jnp.mean(x * x, axis=-1, keepdims=True)
    h = (x * lax.rsqrt(ms + EPS) * g_ref[...]).astype(BF16)

    def group(i):
        return jnp.dot(h, w_ref[:, i * GROUP:(i + 1) * GROUP], preferred_element_type=F32)

    def split_heads(p, out_ref):
        for hh in range(SB_HEADS):
            out_ref[0, hh] = p[:, hh * SB_HEAD_DIM:(hh + 1) * SB_HEAD_DIM]

    def rope(p, out_ref, scale):
        cos = cos_ref[...]
        sin = sin_ref[...]
        for hh in range(RET_HEADS):
            t = p[:, hh * RET_DK:(hh + 1) * RET_DK]
            r = t * cos + pltpu.roll(t, RET_DK // 2, 1) * sin
            if scale is not None:
                r = r * scale
            out_ref[0, :, hh * RET_DK:(hh + 1) * RET_DK] = r.astype(BF16)

    q_ref[0] = (group(0) * SB_Q_SCALE).astype(BF16)
    pk = group(1)
    k_ref[0] = pk.astype(BF16)
    split_heads(pk, ko_ref)
    pv = group(2)
    v_ref[0] = pv.astype(BF16)
    split_heads(pv, vo_ref)
    rope(group(3), rq_ref, None)
    rope(group(4), rk_ref, RET_DK ** -0.5)
    rv_ref[0] = group(5).astype(BF16)
    rg_ref[0] = group(6).astype(BF16)


def _project(x, g_norm, w_in_bf, cos2, sin2, tm, head_row0=0):
    b, l, d = x.shape
    grid = (b, l // tm)
    row_spec = lambda w: pl.BlockSpec((1, tm, w), lambda bi, i: (bi, i, 0))
    head_blk = (1, SB_HEADS, tm, SB_HEAD_DIM)
    head_spec = pl.BlockSpec(tuple(pl.Element(n) for n in head_blk),
                             lambda bi, i: (bi, 0, pl.multiple_of(head_row0 + i * tm, SUBLANES), 0))
    tab_spec = pl.BlockSpec((tm, RET_DK), lambda bi, i: (i, 0))
    act = jax.ShapeDtypeStruct((b, l, GROUP), BF16)
    heads = jax.ShapeDtypeStruct((b, SB_HEADS, head_row0 + l, SB_HEAD_DIM), F32)
    return pl.pallas_call(
        _project_kernel,
        grid=grid,
        in_specs=[row_spec(d), _const_spec((1, d)), _const_spec((d, IN_WIDTH)), tab_spec, tab_spec],
        out_specs=[row_spec(GROUP)] * 3 + [head_spec] * 2 + [row_spec(GROUP)] * 4,
        out_shape=[act] * 3 + [heads] * 2 + [act] * 4,
        compiler_params=pltpu.CompilerParams(
            dimension_semantics=("arbitrary", "arbitrary"), vmem_limit_bytes=VMEM_LIMIT),
        name="project",
    )(x, g_norm.reshape(1, d), w_in_bf, cos2, sin2)


def _split_bf16(x):
    hi = x.astype(BF16)
    lo = (x - hi.astype(F32)).astype(BF16)
    return hi, lo


def _attention_kernel(q_ref, kc_ref, vc_ref, kp_ref, vp_ref, bdu_ref, bdo_ref, j_ref, g_ref,
                      o_ref, left_ref, kxc_ref, vxc_ref, kxp_ref, vxp_ref, carry_ref, acc_ref,
                      *, tq, n_q, n_past, past_valid, report_left):
    tk = SB_KEYS
    per_tile = tq // tk
    head0 = lax.broadcasted_iota(jnp.int32, (tk, LANES), 1) < SB_HEAD_DIM
    key_in_block = lax.broadcasted_iota(jnp.int32, (tq, 2 * tk), 1) & (tk - 1)
    delta = key_in_block - lax.broadcasted_iota(jnp.int32, (tq, 2 * tk), 0)

    pairs = range(SB_PAIRS)
    lanes = [slice(p * PAIR, (p + 1) * PAIR) for p in pairs]

    def expand(src_ref, dst_ref, n_blocks):
        def body(j, _):
            for p in pairs:
                blk = src_ref[0, pl.ds(pl.multiple_of(j * tk, tk), tk), lanes[p]]
                zero = jnp.zeros_like(blk)
                dst_ref[p, j, 0:tk, :] = jnp.where(head0, blk, zero)
                dst_ref[p, j, tk:2 * tk, :] = jnp.where(head0, zero, blk)
            return 0
        lax.fori_loop(0, n_blocks, body, 0)

    expand(kc_ref, kxc_ref, n_q * per_tile)
    expand(vc_ref, vxc_ref, n_q * per_tile)
    if n_past > 0:
        expand(kp_ref, kxp_ref, n_past)
        expand(vp_ref, vxp_ref, n_past)

    def add_blocks(kx_ref, vx_ref, i, j_last, masks, r0=0, r1=tq):
        rows = pl.ds(pl.multiple_of(i * tq + r0, tk), r1 - r0)
        carry = [carry_ref[p, r0:r1, :] for p in pairs]
        for u, mask in enumerate(masks):
            j = j_last - u
            for p in pairs:
                z = lax.dot_general(q_ref[0, rows, lanes[p]], kx_ref[p, j], (((1,), (1,)), ((), ())),
                                    preferred_element_type=F32)
                neg_abs = lax.bitcast_convert_type(
                    lax.bitcast_convert_type(z, jnp.uint32) | jnp.uint32(0x80000000), F32)
                sp = jnp.maximum(z, 0.0) + jnp.log2(1.0 + jnp.exp2(neg_abs))
                spm = sp if mask is None else jnp.where(mask, sp, 0.0)
                hi = spm.astype(BF16)
                later = jnp.dot(hi, bdu_ref[...], preferred_element_type=F32)
                total = jnp.dot(hi, bdo_ref[...], preferred_element_type=F32)
                a = jnp.exp2((z - sp) - (later + carry[p]))
                if mask is not None:
                    a = jnp.where(mask, a, 0.0)
                acc_ref[p, r0:r1, :] += jnp.dot(a.astype(BF16), vx_ref[p, j], preferred_element_type=F32)
                carry[p] = carry[p] + total
        for p in pairs:
            carry_ref[p, r0:r1, :] = carry[p]
        return carry

    def stick_left(carry):
        least = functools.reduce(jnp.minimum, carry)
        return (jnp.min(least) < SB_EXHAUSTED).astype(jnp.int32)

    def sweep(kx_ref, vx_ref, i, j_top, n_steps, unroll, alive):
        def cond(state):
            m, live = state
            return jnp.logical_and(m < n_steps, live > 0)

        def body(state):
            m, _ = state
            carry = add_blocks(kx_ref, vx_ref, i, j_top - m * unroll, [None] * unroll)
            return m + 1, stick_left(carry)

        return lax.while_loop(cond, body, (jnp.int32(0), alive))[1]

    def all_carries(r0=0):
        return [carry_ref[p, r0:, :] for p in pairs]

    def q_tile(i, any_left, first):
        rows = pl.ds(pl.multiple_of(i * tq, tq), tq)
        carry_ref[...] = jnp.zeros_like(carry_ref)
        acc_ref[...] = jnp.zeros_like(acc_ref)

        for jj in range(per_tile - 1, -1, -1):
            r0 = jj * tk
            add_blocks(kxc_ref, vxc_ref, i, i * per_tile + jj, [delta[r0:, :] < -r0], r0)

        unroll = SB_UNROLL if per_tile % SB_UNROLL == 0 else 1
        top = unroll * tk
        if first:
            alive = stick_left(all_carries())
        elif top < tq:
            j_prev = i * per_tile - 1
            add_blocks(kxc_ref, vxc_ref, i, j_prev, [None] * unroll, 0, top)

            @pl.when(stick_left(all_carries(top)) > 0)
            def _():
                add_blocks(kxc_ref, vxc_ref, i, j_prev, [None] * unroll, top, tq)

            alive = sweep(kxc_ref, vxc_ref, i, j_prev - unroll, (i * per_tile) // unroll - 1, unroll,
                          stick_left(all_carries()))
        else:
            alive = sweep(kxc_ref, vxc_ref, i, i * per_tile - 1, (i * per_tile) // unroll, unroll,
                          stick_left(all_carries()))

        if n_past > 0:
            n_full = n_past
            if past_valid < tk:
                @pl.when(alive > 0)
                def _():
                    add_blocks(kxp_ref, vxp_ref, i, n_past - 1, [key_in_block < past_valid])

                n_full = n_past - 1
                if n_full > 0:
                    alive = stick_left(all_carries())
            if n_full // SB_UNROLL > 0:
                alive = sweep(kxp_ref, vxp_ref, i, n_full - 1, n_full // SB_UNROLL, SB_UNROLL, alive)
            if n_full % SB_UNROLL:
                alive = sweep(kxp_ref, vxp_ref, i, n_full % SB_UNROLL - 1, n_full % SB_UNROLL, 1, alive)

        for p in pairs:
            o = acc_ref[p]
            hi, lo = _split_bf16(o * o)
            jm = j_ref[...]
            ms = (jnp.dot(hi, jm, preferred_element_type=F32)
                  + jnp.dot(lo, jm, preferred_element_type=F32)) * (1.0 / SB_HEAD_DIM)
            y = o * lax.rsqrt(ms + EPS) * g_ref[:, lanes[p]]
            o_ref[0, rows, lanes[p]] = y.astype(BF16)
        if not report_left:
            return any_left
        return jnp.maximum(any_left, stick_left(all_carries()))

    any_left = q_tile(jnp.int32(0), jnp.int32(0), True)
    any_left = lax.fori_loop(1, n_q, lambda i, left: q_tile(i, left, False), any_left)
    left_ref[...] = jnp.full(left_ref.shape, any_left, jnp.int32)


def _attention(q, k_cur, v_cur, k_past, v_past, past_len, g_sb_out, report_left=False):
    b, lq, _ = q.shape
    tk = SB_KEYS
    tq = min(SB_QUERY_TILE, lq)
    assert lq % tq == 0 and tq % tk == 0
    if k_past is None:
        k_past = jnp.zeros((1, tk, SB_WIDTH), BF16)
        v_past = jnp.zeros((1, tk, SB_WIDTH), BF16)
        n_past, past_valid = 0, tk
    else:
        assert k_past.shape[1] % tk == 0
        n_past = -(-past_len // tk)
        past_valid = past_len - (n_past - 1) * tk
        k_past = k_past[:, :n_past * tk]
        v_past = v_past[:, :n_past * tk]
    bp, p, _ = k_past.shape
    jj = jnp.arange(2 * tk)
    same_head = (jj[:, None] // tk) == (jj[None, :] // tk)
    bdu = (same_head & (jj[:, None] > jj[None, :])).astype(BF16)
    bdo = same_head.astype(BF16)
    ll = jnp.arange(LANES) // SB_HEAD_DIM
    jm = (ll[:, None] == ll[None, :]).astype(BF16)
    width = SB_PAIRS * PAIR
    cur_spec = pl.BlockSpec((1, lq, width), lambda bi, hp: (bi, 0, hp))
    past_spec = pl.BlockSpec((1, p, width), (lambda bi, hp: (bi, 0, hp)) if bp > 1
                             else (lambda bi, hp: (0, 0, hp)))
    kernel = functools.partial(_attention_kernel, tq=tq, n_q=lq // tq, n_past=n_past,
                               past_valid=past_valid, report_left=report_left)
    stacked = lambda n: pltpu.VMEM((SB_PAIRS, n, 2 * tk, PAIR), BF16)
    n_groups = SB_WIDTH // width
    out, left = pl.pallas_call(
        kernel,
        grid=(b, n_groups),
        in_specs=[cur_spec, cur_spec, cur_spec, past_spec, past_spec,
                  _const_spec(bdu.shape), _const_spec(bdo.shape), _const_spec(jm.shape),
                  pl.BlockSpec((1, width), lambda bi, hp: (0, hp))],
        out_specs=[cur_spec, pl.BlockSpec((1, 1, SUBLANES, LANES), lambda bi, hp: (bi, hp, 0, 0))],
        out_shape=[jax.ShapeDtypeStruct((b, lq, SB_WIDTH), BF16),
                   jax.ShapeDtypeStruct((b, n_groups, SUBLANES, LANES), jnp.int32)],
        scratch_shapes=[stacked(lq // tk), stacked(lq // tk), stacked(p // tk), stacked(p // tk),
                        pltpu.VMEM((SB_PAIRS, tq, 2 * tk), F32), pltpu.VMEM((SB_PAIRS, tq, PAIR), F32)],
        compiler_params=pltpu.CompilerParams(
            dimension_semantics=("arbitrary", "arbitrary"), vmem_limit_bytes=VMEM_LIMIT),
        name="attention",
    )(q, k_cur, v_cur, k_past, v_past, bdu, bdo, jm, g_sb_out.reshape(1, SB_WIDTH))
    return out, left


def _attention_over_cache(q, k_cur, v_cur, cache_k, cache_v, g_sb_out):
    bd, _, past, _ = cache_k.shape
    to_rows = lambda c: c.transpose(0, 2, 1, 3).reshape(bd, c.shape[2], SB_WIDTH).astype(BF16)
    recent = min(past, SB_RECENT)
    out, left = _attention(q, k_cur, v_cur, to_rows(cache_k[:, :, past - recent:]),
                           to_rows(cache_v[:, :, past - recent:]), recent, g_sb_out, report_left=True)
    if recent == past:
        return out
    return lax.cond(
        jnp.any(left > 0),
        lambda: _attention(q, k_cur, v_cur, to_rows(cache_k), to_rows(cache_v), past, g_sb_out)[0],
        lambda: out)


def _retention_kernel(q_ref, k_ref, v_ref, gate_ref, s0_ref, g_ref, o_ref, s_ref, *, chunk, n_chunks):
    c = chunk
    head = pl.program_id(1).astype(F32)
    log_g = jnp.log(1.0 - jnp.exp2(jnp.full((1, LANES), -5.0, F32) - head))
    n_row = lax.broadcasted_iota(jnp.int32, (c, LANES), 0).astype(F32)
    q_decay = jnp.exp((n_row + 1.0) * log_g)
    k_decay = jnp.exp((c - 1.0 - n_row) * log_g)
    s_decay = jnp.exp(float(c) * log_g)
    diff = (lax.broadcasted_iota(jnp.int32, (c, c), 0)
            - lax.broadcasted_iota(jnp.int32, (c, c), 1))
    log_g_cc = log_g if c == LANES else jnp.log(1.0 - jnp.exp2(jnp.full((1, c), -5.0, F32) - head))
    decay = jnp.where(diff >= 0, jnp.exp(jnp.maximum(diff, 0).astype(F32) * log_g_cc), 0.0)
    gain = g_ref[...]

    def body(ci, s):
        rows = pl.ds(pl.multiple_of(ci * c, c), c)
        q = q_ref[0, rows, :]
        k = k_ref[0, rows, :]
        v = v_ref[0, rows, :]
        qk = lax.dot_general(q, k, (((1,), (1,)), ((), ())), preferred_element_type=F32)
        inner = jnp.dot((qk * decay).astype(BF16), v, preferred_element_type=F32)
        cross = jnp.dot(q, s.astype(BF16), preferred_element_type=F32) * q_decay
        o = inner + cross
        k_dec = (k.astype(F32) * k_decay).astype(BF16)
        s_new = s_decay * s + lax.dot_general(k_dec, v, (((0,), (0,)), ((), ())),
                                              preferred_element_type=F32)
        y = o * lax.rsqrt(jnp.mean(o * o, axis=-1, keepdims=True) + EPS) * gain
        gate = gate_ref[0, rows, :].astype(F32)
        o_ref[0, rows, :] = (y * (gate * jax.nn.sigmoid(gate))).astype(BF16)
        return s_new

    s_ref[0, 0] = lax.fori_loop(0, n_chunks, body, s0_ref[0, 0], unroll=math.gcd(n_chunks, RET_UNROLL))


def _retention(rq, rk, rv, rgate, s0, g_ret_out, chunk):
    b, l, _ = rq.shape
    assert l % chunk == 0
    bs = s0.shape[0]
    seq_spec = pl.BlockSpec((1, l, RET_DK), lambda bi, hh: (bi, 0, hh))
    state_spec = pl.BlockSpec((1, 1, RET_DK, RET_DV), lambda bi, hh: (bi, hh, 0, 0))
    s0_spec = state_spec if bs > 1 else pl.BlockSpec((1, 1, RET_DK, RET_DV), lambda bi, hh: (0, hh, 0, 0))
    kernel = functools.partial(_retention_kernel, chunk=chunk, n_chunks=l // chunk)
    return pl.pallas_call(
        kernel,
        grid=(b, RET_HEADS),
        in_specs=[seq_spec, seq_spec, seq_spec, seq_spec, s0_spec,
                  pl.BlockSpec((1, RET_DV), lambda bi, hh: (0, hh))],
        out_specs=[seq_spec, state_spec],
        out_shape=[jax.ShapeDtypeStruct((b, l, RET_WIDTH), BF16),
                   jax.ShapeDtypeStruct((b, RET_HEADS, RET_DK, RET_DV), F32)],
        compiler_params=pltpu.CompilerParams(
            dimension_semantics=("arbitrary", "arbitrary"), vmem_limit_bytes=VMEM_LIMIT),
        name="retention",
    )(rq, rk, rv, rgate, s0, g_ret_out.reshape(1, RET_WIDTH))


def _merge_ffn_kernel(x_ref, sb_ref, ret_ref, conv0_ref, wo_ref, gf_ref, wug_ref, wuv_ref,
                      cw_ref, cb_ref, wd_ref, gl_ref, y_ref, conv_ref,
                      h_ref, hn_ref, ubuf_ref, act_ref, carry_ref, *, tm):
    i = pl.program_id(1)

    @pl.when(i == 0)
    def _():
        carry_ref[...] = conv0_ref[0]

    mixed = jnp.concatenate([sb_ref[0], ret_ref[0]], axis=1)
    h = x_ref[0] + jnp.dot(mixed, wo_ref[...], preferred_element_type=F32)
    h_ref[...] = h
    ms = jnp.mean(h * h, axis=-1, keepdims=True)
    hn_ref[...] = (h * lax.rsqrt(ms + EPS) * gf_ref[...]).astype(BF16)
    lo = SUBLANES - (CONV_W - 1)

    def up_project(j):
        ubuf = ubuf_ref.at[j % 2]
        hn = hn_ref[...]
        ubuf[0:SUBLANES, :] = carry_ref[j]
        ubuf[SUBLANES:, 0:FF_CHUNK] = jnp.dot(hn, wug_ref[j], preferred_element_type=F32)
        ubuf[SUBLANES:, FF_CHUNK:] = jnp.dot(hn, wuv_ref[j], preferred_element_type=F32)
        carry_ref[j] = ubuf[tm:tm + SUBLANES, :]

    def gated_conv(j):
        ubuf = ubuf_ref.at[j % 2]
        cw = cw_ref[j]
        cb = cb_ref[j]
        strip = min(tm, FF_STRIP)
        for r in range(0, tm, strip):
            ext = ubuf[r:r + strip + SUBLANES, :]
            c = cb
            for tap in range(CONV_W):
                back = CONV_W - 1 - tap
                rows = ext if back == 0 else pltpu.roll(ext, back, 0)
                c = c + cw[tap:tap + 1, :] * rows[SUBLANES:, :]
            gate = c[:, :FF_CHUNK]
            act_ref[r:r + strip, j * FF_CHUNK:(j + 1) * FF_CHUNK] = (
                (gate * jax.nn.sigmoid(gate)) * c[:, FF_CHUNK:]).astype(BF16)

    up_project(0)
    for j in range(N_FF_CHUNKS):
        if j + 1 < N_FF_CHUNKS:
            up_project(j + 1)
        gated_conv(j)

    hh = h_ref[...] + jnp.dot(act_ref[...], wd_ref[...], preferred_element_type=F32)
    ms2 = jnp.mean(hh * hh, axis=-1, keepdims=True)
    y_ref[0] = hh * lax.rsqrt(ms2 + EPS) * gl_ref[...]

    @pl.when(i == pl.num_programs(1) - 1)
    def _():
        conv_ref[0] = carry_ref[...]


def _merge_ffn(x, sb_n, ret_n, conv0, w_out_bf, g_norm_ffn, wug, wuv, cw, cb, wd, g_norm_final, tm):
    b, l, d = x.shape
    bs = conv0.shape[0]
    row_spec = lambda w: pl.BlockSpec((1, tm, w), lambda bi, i: (bi, i, 0))
    conv_shape = (1, N_FF_CHUNKS, SUBLANES, 2 * FF_CHUNK)
    conv_spec = pl.BlockSpec(conv_shape, lambda bi, i: (bi, 0, 0, 0))
    conv0_spec = conv_spec if bs > 1 else pl.BlockSpec(conv_shape, lambda bi, i: (0, 0, 0, 0))
    kernel = functools.partial(_merge_ffn_kernel, tm=tm)
    return pl.pallas_call(
        kernel,
        grid=(b, l // tm),
        in_specs=[row_spec(d), row_spec(SB_WIDTH), row_spec(RET_WIDTH), conv0_spec,
                  _const_spec(w_out_bf.shape), _const_spec((1, d)),
                  _const_spec(wug.shape), _const_spec(wuv.shape),
                  _const_spec(cw.shape), _const_spec(cb.shape), _const_spec(wd.shape),
                  _const_spec((1, d))],
        out_specs=[row_spec(d), conv_spec],
        out_shape=[jax.ShapeDtypeStruct((b, l, d), F32),
                   jax.ShapeDtypeStruct((b,) + conv_shape[1:], F32)],
        scratch_shapes=[pltpu.VMEM((tm, d), F32), pltpu.VMEM((tm, d), BF16),
                        pltpu.VMEM((2, tm + SUBLANES, 2 * FF_CHUNK), F32),
                        pltpu.VMEM((tm, D_FF), BF16),
                        pltpu.VMEM(conv_shape[1:], F32)],
        compiler_params=pltpu.CompilerParams(
            dimension_semantics=("arbitrary", "arbitrary"), vmem_limit_bytes=VMEM_LIMIT),
        name="merge_ffn",
    )(x, sb_n, ret_n, conv0, w_out_bf, g_norm_ffn.reshape(1, d), wug, wuv, cw, cb, wd,
      g_norm_final.reshape(1, d))


def _rope_tables(pos):
    half = RET_DK // 2
    inv = ROPE_BASE ** (-jnp.arange(half, dtype=F32) / half)
    ang = pos.astype(F32)[:, None] * inv[None, :]
    cos, sin = jnp.cos(ang), jnp.sin(ang)
    return jnp.concatenate([cos, cos], axis=1), jnp.concatenate([-sin, sin], axis=1)


def _conv_state_to_chunks(state):
    b = state.shape[0]
    s = state.reshape(b, CONV_W - 1, 2, N_FF_CHUNKS, FF_CHUNK).transpose(0, 3, 1, 2, 4)
    s = s.reshape(b, N_FF_CHUNKS, CONV_W - 1, 2 * FF_CHUNK)
    return jnp.pad(s, ((0, 0), (0, 0), (SUBLANES - (CONV_W - 1), 0), (0, 0)))


def _conv_state_from_chunks(chunks):
    b = chunks.shape[0]
    s = chunks[:, :, SUBLANES - (CONV_W - 1):, :].reshape(b, N_FF_CHUNKS, CONV_W - 1, 2, FF_CHUNK)
    return s.transpose(0, 2, 3, 1, 4).reshape(b, CONV_W - 1, 2 * D_FF)


def _pad_rows(a, rows):
    return a if a.shape[1] == rows else jnp.pad(a, ((0, 0), (0, rows - a.shape[1]), (0, 0)))


def _prefix_rows_kernel(k_any, v_any, k_rows_ref, v_rows_ref, k_ref, v_ref):
    del k_any, v_any
    k_ref[...] = k_rows_ref[...]
    v_ref[...] = v_rows_ref[...]


def _write_prefix_rows(k_big, v_big, k_rows, v_rows):
    b = k_big.shape[0]
    blk = (1,) + k_rows.shape[1:]
    any_spec = pl.BlockSpec(memory_space=pl.ANY)
    rows_spec = pl.BlockSpec(blk, lambda bi: (0, 0, 0, 0))
    out_spec = pl.BlockSpec(blk, lambda bi: (bi, 0, 0, 0))
    big = jax.ShapeDtypeStruct(k_big.shape, k_big.dtype)
    return pl.pallas_call(
        _prefix_rows_kernel,
        grid=(b,),
        in_specs=[any_spec, any_spec, rows_spec, rows_spec],
        out_specs=[out_spec, out_spec],
        out_shape=[big, big],
        input_output_aliases={0: 0, 1: 1},
        name="prefix_rows",
    )(k_big, v_big, k_rows, v_rows)


def _stream_step(x, pos0, attend, s0, conv0, wts, tm, chunk, head_row0=0):
    l = x.shape[1]
    cos2, sin2 = _rope_tables(pos0 + jnp.arange(l))
    q, k, v, k_out, v_out, rq, rk, rv, rgate = _project(x, wts["g_norm_mix"], wts["w_in"], cos2, sin2,
                                                       tm, head_row0)
    lq = -(-l // SB_KEYS) * SB_KEYS
    sb_n = attend(_pad_rows(q, lq), _pad_rows(k, lq), _pad_rows(v, lq))[:, :l]
    ret_n, s_new = _retention(rq, rk, rv, rgate, s0, wts["g_ret_out"], chunk)
    y, conv_new = _merge_ffn(x, sb_n, ret_n, conv0, wts["w_out"], wts["g_norm_ffn"], wts["wug"],
                             wts["wuv"], wts["cw"], wts["cb"], wts["wd"], wts["g_norm_final"], tm)
    return y, k, v, k_out, v_out, s_new, conv_new


def kernel(x_prompt, x_sample, cache_sb_k, cache_sb_v, state_ret, state_conv, meta_tokens, g_norm_mix, w_in, g_sb_out, g_ret_out, w_out, g_norm_ffn, w_up, conv_w, conv_b, w_down, g_norm_final):
    b, seq, d = x_prompt.shape
    bd, ls, _ = x_sample.shape
    past = cache_sb_k.shape[2]

    def ff_cols(a):
        r = a.shape[0]
        return a.reshape(r, 2, N_FF_CHUNKS, FF_CHUNK).transpose(2, 0, 1, 3).reshape(N_FF_CHUNKS, r, 2 * FF_CHUNK)

    w_up_bf = w_up.astype(BF16)
    wts = dict(
        g_norm_mix=g_norm_mix, g_sb_out=g_sb_out, g_ret_out=g_ret_out, g_norm_ffn=g_norm_ffn,
        g_norm_final=g_norm_final,
        w_in=w_in.astype(BF16), w_out=w_out.astype(BF16),
        wug=w_up_bf[:, :D_FF].reshape(d, N_FF_CHUNKS, FF_CHUNK).transpose(1, 0, 2),
        wuv=w_up_bf[:, D_FF:].reshape(d, N_FF_CHUNKS, FF_CHUNK).transpose(1, 0, 2),
        cw=ff_cols(conv_w), cb=ff_cols(conv_b.reshape(1, 2 * D_FF)),
        wd=w_down.astype(BF16),
    )

    zero_state = jnp.zeros((1, RET_HEADS, RET_DK, RET_DV), F32)
    zero_conv = jnp.zeros((1, N_FF_CHUNKS, SUBLANES, 2 * FF_CHUNK), F32)
    gain = wts["g_sb_out"]
    _, k_m, v_m, k_m_out, v_m_out, s_meta, conv_meta = _stream_step(
        meta_tokens[None], -N_META, lambda q, k, v: _attention(q, k, v, None, None, 0, gain)[0],
        zero_state, zero_conv, wts, N_META, N_META)

    k_m, v_m = _pad_rows(k_m, SB_KEYS), _pad_rows(v_m, SB_KEYS)
    y_prompt, _, _, k_p_out, v_p_out, s_prompt, conv_prompt = _stream_step(
        x_prompt, 0, lambda q, k, v: _attention(q, k, v, k_m, v_m, N_META, gain)[0],
        s_meta, conv_meta, wts, 512, 256, head_row0=N_META)
    new_k_prompt, new_v_prompt = _write_prefix_rows(k_p_out, v_p_out, k_m_out, v_m_out)

    y_sample, _, _, k_s_out, v_s_out, s_sample, conv_sample = _stream_step(
        x_sample, past, lambda q, k, v: _attention_over_cache(q, k, v, cache_sb_k, cache_sb_v, gain),
        state_ret, _conv_state_to_chunks(state_conv), wts, ls, ls)

    return (y_prompt, y_sample, new_k_prompt, new_v_prompt, s_prompt,
            _conv_state_from_chunks(conv_prompt), k_s_out, v_s_out, s_sample,
            _conv_state_from_chunks(conv_sample))
```

```python
import functools
import math

import jax
import jax.numpy as jnp
from jax import lax
from jax.experimental import pallas as pl
from jax.experimental.pallas import tpu as pltpu

D_MODEL = 1024
N_META = 16
SB_HEADS = 8
SB_HEAD_DIM = 64
SB_WIDTH = SB_HEADS * SB_HEAD_DIM
RET_HEADS = 4
RET_DK = 128
RET_DV = 128
RET_WIDTH = RET_HEADS * RET_DV
MIX_WIDTH = SB_WIDTH + RET_WIDTH
GROUP = 512
N_GROUPS = 7
IN_WIDTH = N_GROUPS * GROUP
D_FF = 2816
CONV_W = 3
ROPE_BASE = 10000.0
EPS = 1e-5

LANES = 128
SUBLANES = 8
FF_CHUNK = 256
N_FF_CHUNKS = D_FF // FF_CHUNK
FF_STRIP = 64
RET_UNROLL = 8
SB_KEYS = 128
SB_QUERY_TILE = 512
SB_UNROLL = 2
PAIR = 2 * SB_HEAD_DIM
SB_PAIRS = 2
SB_RECENT = 512
SB_EXHAUSTED = 152.0
SB_Q_SCALE = SB_HEAD_DIM ** -0.5 * math.log2(math.e)
VMEM_LIMIT = 56 * 1024 * 1024

BF16 = jnp.bfloat16
F32 = jnp.float32


def _const_spec(shape):
    zeros = (0,) * len(shape)
    return pl.BlockSpec(shape, lambda *_: zeros, pipeline_mode=pl.Buffered(1))


def _project_kernel(x_ref, g_ref, w_ref, cos_ref, sin_ref,
                    q_ref, k_ref, v_ref, ko_ref, vo_ref, rq_ref, rk_ref, rv_ref, rg_ref):
    x = x_ref[0]
    ms = jnp.mean(x * x, axis=-1, keepdims=True)
    h = (x * lax.rsqrt(ms + EPS) * g_ref[...]).astype(BF16)

    def group(i):
        return jnp.dot(h, w_ref[:, i * GROUP:(i + 1) * GROUP], preferred_element_type=F32)

    def split_heads(p, out_ref):
        for hh in range(SB_HEADS):
            out_ref[0, hh] = p[:, hh * SB_HEAD_DIM:(hh + 1) * SB_HEAD_DIM]

    def rope(p, out_ref, scale):
        cos = cos_ref[...]
        sin = sin_ref[...]
        for hh in range(RET_HEADS):
            t = p[:, hh * RET_DK:(hh + 1) * RET_DK]
            r = t * cos + pltpu.roll(t, RET_DK // 2, 1) * sin
            if scale is not None:
                r = r * scale
            out_ref[0, :, hh * RET_DK:(hh + 1) * RET_DK] = r.astype(BF16)

    q_ref[0] = (group(0) * SB_Q_SCALE).astype(BF16)
    pk = group(1)
    k_ref[0] = pk.astype(BF16)
    split_heads(pk, ko_ref)
    pv = group(2)
    v_ref[0] = pv.astype(BF16)
    split_heads(pv, vo_ref)
    rope(group(3), rq_ref, None)
    rope(group(4), rk_ref, RET_DK ** -0.5)
    rv_ref[0] = group(5).astype(BF16)
    rg_ref[0] = group(6).astype(BF16)


def _project(x, g_norm, w_in_bf, cos2, sin2, tm, head_row0=0):
    b, l, d = x.shape
    grid = (b, l // tm)
    row_spec = lambda w: pl.BlockSpec((1, tm, w), lambda bi, i: (bi, i, 0))
    head_blk = (1, SB_HEADS, tm, SB_HEAD_DIM)
    head_spec = pl.BlockSpec(tuple(pl.Element(n) for n in head_blk),
                             lambda bi, i: (bi, 0, pl.multiple_of(head_row0 + i * tm, SUBLANES), 0))
    tab_spec = pl.BlockSpec((tm, RET_DK), lambda bi, i: (i, 0))
    act = jax.ShapeDtypeStruct((b, l, GROUP), BF16)
    heads = jax.ShapeDtypeStruct((b, SB_HEADS, head_row0 + l, SB_HEAD_DIM), F32)
    return pl.pallas_call(
        _project_kernel,
        grid=grid,
        in_specs=[row_spec(d), _const_spec((1, d)), _const_spec((d, IN_WIDTH)), tab_spec, tab_spec],
        out_specs=[row_spec(GROUP)] * 3 + [head_spec] * 2 + [row_spec(GROUP)] * 4,
        out_shape=[act] * 3 + [heads] * 2 + [act] * 4,
        compiler_params=pltpu.CompilerParams(
            dimension_semantics=("arbitrary", "arbitrary"), vmem_limit_bytes=VMEM_LIMIT),
        name="project",
    )(x, g_norm.reshape(1, d), w_in_bf, cos2, sin2)


def _split_bf16(x):
    hi = x.astype(BF16)
    lo = (x - hi.astype(F32)).astype(BF16)
    return hi, lo


def _attention_kernel(q_ref, kc_ref, vc_ref, kp_ref, vp_ref, bdu_ref, bdo_ref, j_ref, g_ref,
                      o_ref, left_ref, kxc_ref, vxc_ref, kxp_ref, vxp_ref, carry_ref, acc_ref,
                      *, tq, n_q, n_past, past_valid, report_left):
    tk = SB_KEYS
    per_tile = tq // tk
    head0 = lax.broadcasted_iota(jnp.int32, (tk, LANES), 1) < SB_HEAD_DIM
    key_in_block = lax.broadcasted_iota(jnp.int32, (tq, 2 * tk), 1) & (tk - 1)
    delta = key_in_block - lax.broadcasted_iota(jnp.int32, (tq, 2 * tk), 0)

    pairs = range(SB_PAIRS)
    lanes = [slice(p * PAIR, (p + 1) * PAIR) for p in pairs]

    def expand(src_ref, dst_ref, n_blocks):
        def body(j, _):
            for p in pairs:
                blk = src_ref[0, pl.ds(pl.multiple_of(j * tk, tk), tk), lanes[p]]
                zero = jnp.zeros_like(blk)
                dst_ref[p, j, 0:tk, :] = jnp.where(head0, blk, zero)
                dst_ref[p, j, tk:2 * tk, :] = jnp.where(head0, zero, blk)
            return 0
        lax.fori_loop(0, n_blocks, body, 0)

    expand(kc_ref, kxc_ref, n_q * per_tile)
    expand(vc_ref, vxc_ref, n_q * per_tile)
    if n_past > 0:
        expand(kp_ref, kxp_ref, n_past)
        expand(vp_ref, vxp_ref, n_past)

    def add_blocks(kx_ref, vx_ref, i, j_last, masks, r0=0, r1=tq):
        rows = pl.ds(pl.multiple_of(i * tq + r0, tk), r1 - r0)
        carry = [carry_ref[p, r0:r1, :] for p in pairs]
        for u, mask in enumerate(masks):
            j = j_last - u
            for p in pairs:
                z = lax.dot_general(q_ref[0, rows, lanes[p]], kx_ref[p, j], (((1,), (1,)), ((), ())),
                                    preferred_element_type=F32)
                neg_abs = lax.bitcast_convert_type(
                    lax.bitcast_convert_type(z, jnp.uint32) | jnp.uint32(0x80000000), F32)
                sp = jnp.maximum(z, 0.0) + jnp.log2(1.0 + jnp.exp2(neg_abs))
                spm = sp if mask is None else jnp.where(mask, sp, 0.0)
                hi = spm.astype(BF16)
                later = jnp.dot(hi, bdu_ref[...], preferred_element_type=F32)
                total = jnp.dot(hi, bdo_ref[...], preferred_element_type=F32)
                a = jnp.exp2((z - sp) - (later + carry[p]))
                if mask is not None:
                    a = jnp.where(mask, a, 0.0)
                acc_ref[p, r0:r1, :] += jnp.dot(a.astype(BF16), vx_ref[p, j], preferred_element_type=F32)
                carry[p] = carry[p] + total
        for p in pairs:
            carry_ref[p, r0:r1, :] = carry[p]
        return carry

    def stick_left(carry):
        least = functools.reduce(jnp.minimum, carry)
        return (jnp.min(least) < SB_EXHAUSTED).astype(jnp.int32)

    def sweep(kx_ref, vx_ref, i, j_top, n_steps, unroll, alive):
        def cond(state):
            m, live = state
            return jnp.logical_and(m < n_steps, live > 0)

        def body(state):
            m, _ = state
            carry = add_blocks(kx_ref, vx_ref, i, j_top - m * unroll, [None] * unroll)
            return m + 1, stick_left(carry)

        return lax.while_loop(cond, body, (jnp.int32(0), alive))[1]

    def all_carries(r0=0):
        return [carry_ref[p, r0:, :] for p in pairs]

    def q_tile(i, any_left, first):
        rows = pl.ds(pl.multiple_of(i * tq, tq), tq)
        carry_ref[...] = jnp.zeros_like(carry_ref)
        acc_ref[...] = jnp.zeros_like(acc_ref)

        for jj in range(per_tile - 1, -1, -1):
            r0 = jj * tk
            add_blocks(kxc_ref, vxc_ref, i, i * per_tile + jj, [delta[r0:, :] < -r0], r0)

        unroll = SB_UNROLL if per_tile % SB_UNROLL == 0 else 1
        top = unroll * tk
        if first:
            alive = stick_left(all_carries())
        elif top < tq:
            j_prev = i * per_tile - 1
            add_blocks(kxc_ref, vxc_ref, i, j_prev, [None] * unroll, 0, top)

            @pl.when(stick_left(all_carries(top)) > 0)
            def _():
                add_blocks(kxc_ref, vxc_ref, i, j_prev, [None] * unroll, top, tq)

            alive = sweep(kxc_ref, vxc_ref, i, j_prev - unroll, (i * per_tile) // unroll - 1, unroll,
                          stick_left(all_carries()))
        else:
            alive = sweep(kxc_ref, vxc_ref, i, i * per_tile - 1, (i * per_tile) // unroll, unroll,
                          stick_left(all_carries()))

        if n_past > 0:
            n_full = n_past
            if past_valid < tk:
                @pl.when(alive > 0)
                def _():
                    add_blocks(kxp_ref, vxp_ref, i, n_past - 1, [key_in_block < past_valid])

                n_full = n_past - 1
                if n_full > 0:
                    alive = stick_left(all_carries())
            if n_full // SB_UNROLL > 0:
                alive = sweep(kxp_ref, vxp_ref, i, n_full - 1, n_full // SB_UNROLL, SB_UNROLL, alive)
            if n_full % SB_UNROLL:
                alive = sweep(kxp_ref, vxp_ref, i, n_full % SB_UNROLL - 1, n_full % SB_UNROLL, 1, alive)

        for p in pairs:
            o = acc_ref[p]
            hi, lo = _split_bf16(o * o)
            jm = j_ref[...]
            ms = (jnp.dot(hi, jm, preferred_element_type=F32)
                  + jnp.dot(lo, jm, preferred_element_type=F32)) * (1.0 / SB_HEAD_DIM)
            y = o * lax.rsqrt(ms + EPS) * g_ref[:, lanes[p]]
            o_ref[0, rows, lanes[p]] = y.astype(BF16)
        if not report_left:
            return any_left
        return jnp.maximum(any_left, stick_left(all_carries()))

    any_left = q_tile(jnp.int32(0), jnp.int32(0), True)
    any_left = lax.fori_loop(1, n_q, lambda i, left: q_tile(i, left, False), any_left)
    left_ref[...] = jnp.full(left_ref.shape, any_left, jnp.int32)


def _attention(q, k_cur, v_cur, k_past, v_past, past_len, g_sb_out, report_left=False):
    b, lq, _ = q.shape
    tk = SB_KEYS
    tq = min(SB_QUERY_TILE, lq)
    assert lq % tq == 0 and tq % tk == 0
    if k_past is None:
        k_past = jnp.zeros((1, tk, SB_WIDTH), BF16)
        v_past = jnp.zeros((1, tk, SB_WIDTH), BF16)
        n_past, past_valid = 0, tk
    else:
        assert k_past.shape[1] % tk == 0
        n_past = -(-past_len // tk)
        past_valid = past_len - (n_past - 1) * tk
        k_past = k_past[:, :n_past * tk]
        v_past = v_past[:, :n_past * tk]
    bp, p, _ = k_past.shape
    jj = jnp.arange(2 * tk)
    same_head = (jj[:, None] // tk) == (jj[None, :] // tk)
    bdu = (same_head & (jj[:, None] > jj[None, :])).astype(BF16)
    bdo = same_head.astype(BF16)
    ll = jnp.arange(LANES) // SB_HEAD_DIM
    jm = (ll[:, None] == ll[None, :]).astype(BF16)
    width = SB_PAIRS * PAIR
    cur_spec = pl.BlockSpec((1, lq, width), lambda bi, hp: (bi, 0, hp))
    past_spec = pl.BlockSpec((1, p, width), (lambda bi, hp: (bi, 0, hp)) if bp > 1
                             else (lambda bi, hp: (0, 0, hp)))
    kernel = functools.partial(_attention_kernel, tq=tq, n_q=lq // tq, n_past=n_past,
                               past_valid=past_valid, report_left=report_left)
    stacked = lambda n: pltpu.VMEM((SB_PAIRS, n, 2 * tk, PAIR), BF16)
    n_groups = SB_WIDTH // width
    out, left = pl.pallas_call(
        kernel,
        grid=(b, n_groups),
        in_specs=[cur_spec, cur_spec, cur_spec, past_spec, past_spec,
                  _const_spec(bdu.shape), _const_spec(bdo.shape), _const_spec(jm.shape),
                  pl.BlockSpec((1, width), lambda bi, hp: (0, hp))],
        out_specs=[cur_spec, pl.BlockSpec((1, 1, SUBLANES, LANES), lambda bi, hp: (bi, hp, 0, 0))],
        out_shape=[jax.ShapeDtypeStruct((b, lq, SB_WIDTH), BF16),
                   jax.ShapeDtypeStruct((b, n_groups, SUBLANES, LANES), jnp.int32)],
        scratch_shapes=[stacked(lq // tk), stacked(lq // tk), stacked(p // tk), stacked(p // tk),
                        pltpu.VMEM((SB_PAIRS, tq, 2 * tk), F32), pltpu.VMEM((SB_PAIRS, tq, PAIR), F32)],
        compiler_params=pltpu.CompilerParams(
            dimension_semantics=("arbitrary", "arbitrary"), vmem_limit_bytes=VMEM_LIMIT),
        name="attention",
    )(q, k_cur, v_cur, k_past, v_past, bdu, bdo, jm, g_sb_out.reshape(1, SB_WIDTH))
    return out, left


def _attention_over_cache(q, k_cur, v_cur, cache_k, cache_v, g_sb_out):
    bd, _, past, _ = cache_k.shape
    to_rows = lambda c: c.transpose(0, 2, 1, 3).reshape(bd, c.shape[2], SB_WIDTH).astype(BF16)
    recent = min(past, SB_RECENT)
    out, left = _attention(q, k_cur, v_cur, to_rows(cache_k[:, :, past - recent:]),
                           to_rows(cache_v[:, :, past - recent:]), recent, g_sb_out, report_left=True)
    if recent == past:
        return out
    return lax.cond(
        jnp.any(left > 0),
        lambda: _attention(q, k_cur, v_cur, to_rows(cache_k), to_rows(cache_v), past, g_sb_out)[0],
        lambda: out)


def _retention_kernel(q_ref, k_ref, v_ref, gate_ref, s0_ref, g_ref, o_ref, s_ref, *, chunk, n_chunks):
    c = chunk
    head = pl.program_id(1).astype(F32)
    log_g = jnp.log(1.0 - jnp.exp2(jnp.full((1, LANES), -5.0, F32) - head))
    n_row = lax.broadcasted_iota(jnp.int32, (c, LANES), 0).astype(F32)
    q_decay = jnp.exp((n_row + 1.0) * log_g)
    k_decay = jnp.exp((c - 1.0 - n_row) * log_g)
    s_decay = jnp.exp(float(c) * log_g)
    diff = (lax.broadcasted_iota(jnp.int32, (c, c), 0)
            - lax.broadcasted_iota(jnp.int32, (c, c), 1))
    log_g_cc = log_g if c == LANES else jnp.log(1.0 - jnp.exp2(jnp.full((1, c), -5.0, F32) - head))
    decay = jnp.where(diff >= 0, jnp.exp(jnp.maximum(diff, 0).astype(F32) * log_g_cc), 0.0)
    gain = g_ref[...]

    def body(ci, s):
        rows = pl.ds(pl.multiple_of(ci * c, c), c)
        q = q_ref[0, rows, :]
        k = k_ref[0, rows, :]
        v = v_ref[0, rows, :]
        qk = lax.dot_general(q, k, (((1,), (1,)), ((), ())), preferred_element_type=F32)
        inner = jnp.dot((qk * decay).astype(BF16), v, preferred_element_type=F32)
        cross = jnp.dot(q, s.astype(BF16), preferred_element_type=F32) * q_decay
        o = inner + cross
        k_dec = (k.astype(F32) * k_decay).astype(BF16)
        s_new = s_decay * s + lax.dot_general(k_dec, v, (((0,), (0,)), ((), ())),
                                              preferred_element_type=F32)
        y = o * lax.rsqrt(jnp.mean(o * o, axis=-1, keepdims=True) + EPS) * gain
        gate = gate_ref[0, rows, :].astype(F32)
        o_ref[0, rows, :] = (y * (gate * jax.nn.sigmoid(gate))).astype(BF16)
        return s_new

    s_ref[0, 0] = lax.fori_loop(0, n_chunks, body, s0_ref[0, 0], unroll=math.gcd(n_chunks, RET_UNROLL))


def _retention(rq, rk, rv, rgate, s0, g_ret_out, chunk):
    b, l, _ = rq.shape
    assert l % chunk == 0
    bs = s0.shape[0]
    seq_spec = pl.BlockSpec((1, l, RET_DK), lambda bi, hh: (bi, 0, hh))
    state_spec = pl.BlockSpec((1, 1, RET_DK, RET_DV), lambda bi, hh: (bi, hh, 0, 0))
    s0_spec = state_spec if bs > 1 else pl.BlockSpec((1, 1, RET_DK, RET_DV), lambda bi, hh: (0, hh, 0, 0))
    kernel = functools.partial(_retention_kernel, chunk=chunk, n_chunks=l // chunk)
    return pl.pallas_call(
        kernel,
        grid=(b, RET_HEADS),
        in_specs=[seq_spec, seq_spec, seq_spec, seq_spec, s0_spec,
                  pl.BlockSpec((1, RET_DV), lambda bi, hh: (0, hh))],
        out_specs=[seq_spec, state_spec],
        out_shape=[jax.ShapeDtypeStruct((b, l, RET_WIDTH), BF16),
                   jax.ShapeDtypeStruct((b, RET_HEADS, RET_DK, RET_DV), F32)],
        compiler_params=pltpu.CompilerParams(
            dimension_semantics=("arbitrary", "arbitrary"), vmem_limit_bytes=VMEM_LIMIT),
        name="retention",
    )(rq, rk, rv, rgate, s0, g_ret_out.reshape(1, RET_WIDTH))


def _merge_ffn_kernel(x_ref, sb_ref, ret_ref, conv0_ref, wo_ref, gf_ref, wu_ref,
                      cw_ref, cb_ref, wd_ref, gl_ref, y_ref, conv_ref,
                      h_ref, hn_ref, ubuf_ref, act_ref, carry_ref, *, tm):
    i = pl.program_id(1)

    @pl.when(i == 0)
    def _():
        carry_ref[...] = conv0_ref[0]

    mixed = jnp.concatenate([sb_ref[0], ret_ref[0]], axis=1)
    h = x_ref[0] + jnp.dot(mixed, wo_ref[...], preferred_element_type=F32)
    h_ref[...] = h
    ms = jnp.mean(h * h, axis=-1, keepdims=True)
    hn_ref[...] = (h * lax.rsqrt(ms + EPS) * gf_ref[...]).astype(BF16)
    lo = SUBLANES - (CONV_W - 1)

    def up_project(j):
        ubuf = ubuf_ref.at[j % 2]
        hn = hn_ref[...]
        ubuf[0:SUBLANES, :] = carry_ref[j]
        gate_cols = slice(j * FF_CHUNK, (j + 1) * FF_CHUNK)
        val_cols = slice(D_FF + j * FF_CHUNK, D_FF + (j + 1) * FF_CHUNK)
        ubuf[SUBLANES:, 0:FF_CHUNK] = jnp.dot(hn, wu_ref[:, gate_cols], preferred_element_type=F32)
        ubuf[SUBLANES:, FF_CHUNK:] = jnp.dot(hn, wu_ref[:, val_cols], preferred_element_type=F32)
        carry_ref[j] = ubuf[tm:tm + SUBLANES, :]

    def gated_conv(j):
        ubuf = ubuf_ref.at[j % 2]
        cw = cw_ref[j]
        cb = cb_ref[j]
        strip = min(tm, FF_STRIP)
        for r in range(0, tm, strip):
            ext = ubuf[r:r + strip + SUBLANES, :]
            c = cb
            for tap in range(CONV_W):
                back = CONV_W - 1 - tap
                rows = ext if back == 0 else pltpu.roll(ext, back, 0)
                c = c + cw[tap:tap + 1, :] * rows[SUBLANES:, :]
            gate = c[:, :FF_CHUNK]
            act_ref[r:r + strip, j * FF_CHUNK:(j + 1) * FF_CHUNK] = (
                (gate * jax.nn.sigmoid(gate)) * c[:, FF_CHUNK:]).astype(BF16)

    up_project(0)
    for j in range(N_FF_CHUNKS):
        if j + 1 < N_FF_CHUNKS:
            up_project(j + 1)
        gated_conv(j)

    hh = h_ref[...] + jnp.dot(act_ref[...], wd_ref[...], preferred_element_type=F32)
    ms2 = jnp.mean(hh * hh, axis=-1, keepdims=True)
    y_ref[0] = hh * lax.rsqrt(ms2 + EPS) * gl_ref[...]

    @pl.when(i == pl.num_programs(1) - 1)
    def _():
        conv_ref[0] = carry_ref[...]


def _merge_ffn(x, sb_n, ret_n, conv0, w_out_bf, g_norm_ffn, w_up_bf, cw, cb, wd, g_norm_final, tm):
    b, l, d = x.shape
    bs = conv0.shape[0]
    row_spec = lambda w: pl.BlockSpec((1, tm, w), lambda bi, i: (bi, i, 0))
    conv_shape = (1, N_FF_CHUNKS, SUBLANES, 2 * FF_CHUNK)
    conv_spec = pl.BlockSpec(conv_shape, lambda bi, i: (bi, 0, 0, 0))
    conv0_spec = conv_spec if bs > 1 else pl.BlockSpec(conv_shape, lambda bi, i: (0, 0, 0, 0))
    kernel = functools.partial(_merge_ffn_kernel, tm=tm)
    return pl.pallas_call(
        kernel,
        grid=(b, l // tm),
        in_specs=[row_spec(d), row_spec(SB_WIDTH), row_spec(RET_WIDTH), conv0_spec,
                  _const_spec(w_out_bf.shape), _const_spec((1, d)),
                  _const_spec(w_up_bf.shape),
                  _const_spec(cw.shape), _const_spec(cb.shape), _const_spec(wd.shape),
                  _const_spec((1, d))],
        out_specs=[row_spec(d), conv_spec],
        out_shape=[jax.ShapeDtypeStruct((b, l, d), F32),
                   jax.ShapeDtypeStruct((b,) + conv_shape[1:], F32)],
        scratch_shapes=[pltpu.VMEM((tm, d), F32), pltpu.VMEM((tm, d), BF16),
                        pltpu.VMEM((2, tm + SUBLANES, 2 * FF_CHUNK), F32),
                        pltpu.VMEM((tm, D_FF), BF16),
                        pltpu.VMEM(conv_shape[1:], F32)],
        compiler_params=pltpu.CompilerParams(
            dimension_semantics=("arbitrary", "arbitrary"), vmem_limit_bytes=VMEM_LIMIT),
        name="merge_ffn",
    )(x, sb_n, ret_n, conv0, w_out_bf, g_norm_ffn.reshape(1, d), w_up_bf, cw, cb, wd,
      g_norm_final.reshape(1, d))


def _rope_tables(pos):
    half = RET_DK // 2
    inv = ROPE_BASE ** (-jnp.arange(half, dtype=F32) / half)
    ang = pos.astype(F32)[:, None] * inv[None, :]
    cos, sin = jnp.cos(ang), jnp.sin(ang)
    return jnp.concatenate([cos, cos], axis=1), jnp.concatenate([-sin, sin], axis=1)


def _conv_state_to_chunks(state):
    b = state.shape[0]
    s = state.reshape(b, CONV_W - 1, 2, N_FF_CHUNKS, FF_CHUNK).transpose(0, 3, 1, 2, 4)
    s = s.reshape(b, N_FF_CHUNKS, CONV_W - 1, 2 * FF_CHUNK)
    return jnp.pad(s, ((0, 0), (0, 0), (SUBLANES - (CONV_W - 1), 0), (0, 0)))


def _conv_state_from_chunks(chunks):
    b = chunks.shape[0]
    s = chunks[:, :, SUBLANES - (CONV_W - 1):, :].reshape(b, N_FF_CHUNKS, CONV_W - 1, 2, FF_CHUNK)
    return s.transpose(0, 2, 3, 1, 4).reshape(b, CONV_W - 1, 2 * D_FF)


def _pad_rows(a, rows):
    return a if a.shape[1] == rows else jnp.pad(a, ((0, 0), (0, rows - a.shape[1]), (0, 0)))


def _prefix_rows_kernel(k_any, v_any, k_rows_ref, v_rows_ref, k_ref, v_ref):
    del k_any, v_any
    k_ref[...] = k_rows_ref[...]
    v_ref[...] = v_rows_ref[...]


def _write_prefix_rows(k_big, v_big, k_rows, v_rows):
    b = k_big.shape[0]
    blk = (1,) + k_rows.shape[1:]
    any_spec = pl.BlockSpec(memory_space=pl.ANY)
    rows_spec = pl.BlockSpec(blk, lambda bi: (0, 0, 0, 0))
    out_spec = pl.BlockSpec(blk, lambda bi: (bi, 0, 0, 0))
    big = jax.ShapeDtypeStruct(k_big.shape, k_big.dtype)
    return pl.pallas_call(
        _prefix_rows_kernel,
        grid=(b,),
        in_specs=[any_spec, any_spec, rows_spec, rows_spec],
        out_specs=[out_spec, out_spec],
        out_shape=[big, big],
        input_output_aliases={0: 0, 1: 1},
        name="prefix_rows",
    )(k_big, v_big, k_rows, v_rows)


def _stream_step(x, pos0, attend, s0, conv0, wts, tm, chunk, head_row0=0):
    l = x.shape[1]
    cos2, sin2 = _rope_tables(pos0 + jnp.arange(l))
    q, k, v, k_out, v_out, rq, rk, rv, rgate = _project(x, wts["g_norm_mix"], wts["w_in"], cos2, sin2,
                                                       tm, head_row0)
    lq = -(-l // SB_KEYS) * SB_KEYS
    sb_n = attend(_pad_rows(q, lq), _pad_rows(k, lq), _pad_rows(v, lq))[:, :l]
    ret_n, s_new = _retention(rq, rk, rv, rgate, s0, wts["g_ret_out"], chunk)
    y, conv_new = _merge_ffn(x, sb_n, ret_n, conv0, wts["w_out"], wts["g_norm_ffn"], wts["w_up"],
                             wts["cw"], wts["cb"], wts["wd"], wts["g_norm_final"], tm)
    return y, k, v, k_out, v_out, s_new, conv_new


def kernel(x_prompt, x_sample, cache_sb_k, cache_sb_v, state_ret, state_conv, meta_tokens, g_norm_mix, w_in, g_sb_out, g_ret_out, w_out, g_norm_ffn, w_up, conv_w, conv_b, w_down, g_norm_final):
    b, seq, d = x_prompt.shape
    bd, ls, _ = x_sample.shape
    past = cache_sb_k.shape[2]

    def ff_cols(a):
        r = a.shape[0]
        return a.reshape(r, 2, N_FF_CHUNKS, FF_CHUNK).transpose(2, 0, 1, 3).reshape(N_FF_CHUNKS, r, 2 * FF_CHUNK)

    wts = dict(
        g_norm_mix=g_norm_mix, g_sb_out=g_sb_out, g_ret_out=g_ret_out, g_norm_ffn=g_norm_ffn,
        g_norm_final=g_norm_final,
        w_in=w_in.astype(BF16), w_out=w_out.astype(BF16),
        w_up=w_up.astype(BF16),
        cw=ff_cols(conv_w), cb=ff_cols(conv_b.reshape(1, 2 * D_FF)),
        wd=w_down.astype(BF16),
    )

    zero_state = jnp.zeros((1, RET_HEADS, RET_DK, RET_DV), F32)
    zero_conv = jnp.zeros((1, N_FF_CHUNKS, SUBLANES, 2 * FF_CHUNK), F32)
    gain = wts["g_sb_out"]
    _, k_m, v_m, k_m_out, v_m_out, s_meta, conv_meta = _stream_step(
        meta_tokens[None], -N_META, lambda q, k, v: _attention(q, k, v, None, None, 0, gain)[0],
        zero_state, zero_conv, wts, N_META, N_META)

    k_m, v_m = _pad_rows(k_m, SB_KEYS), _pad_rows(v_m, SB_KEYS)
    y_prompt, _, _, k_p_out, v_p_out, s_prompt, conv_prompt = _stream_step(
        x_prompt, 0, lambda q, k, v: _attention(q, k, v, k_m, v_m, N_META, gain)[0],
        s_meta, conv_meta, wts, 512, 256, head_row0=N_META)
    new_k_prompt, new_v_prompt = _write_prefix_rows(k_p_out, v_p_out, k_m_out, v_m_out)

    y_sample, _, _, k_s_out, v_s_out, s_sample, conv_sample = _stream_step(
        x_sample, past, lambda q, k, v: _attention_over_cache(q, k, v, cache_sb_k, cache_sb_v, gain),
        state_ret, _conv_state_to_chunks(state_conv), wts, ls, ls)

    return (y_prompt, y_sample, new_k_prompt, new_v_prompt, s_prompt,
            _conv_state_from_chunks(conv_prompt), k_s_out, v_s_out, s_sample,
            _conv_state_from_chunks(conv_sample))
```

```python
import functools
import math

import jax
import jax.numpy as jnp
from jax import lax
from jax.experimental import pallas as pl
from jax.experimental.pallas import tpu as pltpu

D_MODEL = 1024
N_META = 16
SB_HEADS = 8
SB_HEAD_DIM = 64
SB_WIDTH = SB_HEADS * SB_HEAD_DIM
RET_HEADS = 4
RET_DK = 128
RET_DV = 128
RET_WIDTH = RET_HEADS * RET_DV
MIX_WIDTH = SB_WIDTH + RET_WIDTH
GROUP = 512
N_GROUPS = 7
IN_WIDTH = N_GROUPS * GROUP
D_FF = 2816
CONV_W = 3
ROPE_BASE = 10000.0
EPS = 1e-5

LANES = 128
SUBLANES = 8
FF_CHUNK = 256
N_FF_CHUNKS = D_FF // FF_CHUNK
FF_STRIP = 64
RET_UNROLL = 8
SB_KEYS = 128
SB_QUERY_TILE = 512
SB_UNROLL = 2
PAIR = 2 * SB_HEAD_DIM
SB_PAIRS = 2
SB_RECENT = 512
SB_EXHAUSTED = 152.0
SB_Q_SCALE = SB_HEAD_DIM ** -0.5 * math.log2(math.e)
VMEM_LIMIT = 56 * 1024 * 1024

BF16 = jnp.bfloat16
F32 = jnp.float32


def _const_spec(shape):
    zeros = (0,) * len(shape)
    return pl.BlockSpec(shape, lambda *_: zeros, pipeline_mode=pl.Buffered(1))


def _project_kernel(x_ref, g_ref, w_ref, cos_ref, sin_ref,
                    q_ref, k_ref, v_ref, ko_ref, vo_ref, rq_ref, rk_ref, rv_ref, rg_ref):
    x = x_ref[0]
    ms = jnp.mean(x * x, axis=-1, keepdims=True)
    h = (x * lax.rsqrt(ms + EPS) * g_ref[...]).astype(BF16)

    def group(i):
        return jnp.dot(h, w_ref[:, i * GROUP:(i + 1) * GROUP], preferred_element_type=F32)

    def split_heads(p, out_ref):
        for hh in range(SB_HEADS):
            out_ref[0, hh] = p[:, hh * SB_HEAD_DIM:(hh + 1) * SB_HEAD_DIM]

    def rope(p, out_ref, scale):
        cos = cos_ref[...]
        sin = sin_ref[...]
        for hh in range(RET_HEADS):
            t = p[:, hh * RET_DK:(hh + 1) * RET_DK]
            r = t * cos + pltpu.roll(t, RET_DK // 2, 1) * sin
            if scale is not None:
                r = r * scale
            out_ref[0, :, hh * RET_DK:(hh + 1) * RET_DK] = r.astype(BF16)

    q_ref[0] = (group(0) * SB_Q_SCALE).astype(BF16)
    pk = group(1)
    k_ref[0] = pk.astype(BF16)
    split_heads(pk, ko_ref)
    pv = group(2)
    v_ref[0] = pv.astype(BF16)
    split_heads(pv, vo_ref)
    rope(group(3), rq_ref, None)
    rope(group(4), rk_ref, RET_DK ** -0.5)
    rv_ref[0] = group(5).astype(BF16)
    rg_ref[0] = group(6).astype(BF16)


def _project(x, g_norm, w_in_bf, cos2, sin2, tm, head_row0=0):
    b, l, d = x.shape
    grid = (b, l // tm)
    row_spec = lambda w: pl.BlockSpec((1, tm, w), lambda bi, i: (bi, i, 0))
    head_blk = (1, SB_HEADS, tm, SB_HEAD_DIM)
    head_spec = pl.BlockSpec(tuple(pl.Element(n) for n in head_blk),
                             lambda bi, i: (bi, 0, pl.multiple_of(head_row0 + i * tm, SUBLANES), 0))
    tab_spec = pl.BlockSpec((tm, RET_DK), lambda bi, i: (i, 0))
    act = jax.ShapeDtypeStruct((b, l, GROUP), BF16)
    heads = jax.ShapeDtypeStruct((b, SB_HEADS, head_row0 + l, SB_HEAD_DIM), F32)
    return pl.pallas_call(
        _project_kernel,
        grid=grid,
        in_specs=[row_spec(d), _const_spec((1, d)), _const_spec((d, IN_WIDTH)), tab_spec, tab_spec],
        out_specs=[row_spec(GROUP)] * 3 + [head_spec] * 2 + [row_spec(GROUP)] * 4,
        out_shape=[act] * 3 + [heads] * 2 + [act] * 4,
        compiler_params=pltpu.CompilerParams(
            dimension_semantics=("arbitrary", "arbitrary"), vmem_limit_bytes=VMEM_LIMIT),
        name="project",
    )(x, g_norm.reshape(1, d), w_in_bf, cos2, sin2)


def _split_bf16(x):
    hi = x.astype(BF16)
    lo = (x - hi.astype(F32)).astype(BF16)
    return hi, lo


def _attention_kernel(q_ref, kc_ref, vc_ref, kp_ref, vp_ref, bdu_ref, bdo_ref, j_ref, g_ref,
                      o_ref, left_ref, kxc_ref, vxc_ref, kxp_ref, vxp_ref, carry_ref, acc_ref,
                      *, tq, n_q, n_past, past_valid, report_left):
    tk = SB_KEYS
    per_tile = tq // tk
    head0 = lax.broadcasted_iota(jnp.int32, (tk, LANES), 1) < SB_HEAD_DIM
    key_in_block = lax.broadcasted_iota(jnp.int32, (tq, 2 * tk), 1) & (tk - 1)
    delta = key_in_block - lax.broadcasted_iota(jnp.int32, (tq, 2 * tk), 0)

    pairs = range(SB_PAIRS)
    lanes = [slice(p * PAIR, (p + 1) * PAIR) for p in pairs]

    def expand(src_ref, dst_ref, n_blocks):
        def body(j, _):
            for p in pairs:
                blk = src_ref[0, pl.ds(pl.multiple_of(j * tk, tk), tk), lanes[p]]
                zero = jnp.zeros_like(blk)
                dst_ref[p, j, 0:tk, :] = jnp.where(head0, blk, zero)
                dst_ref[p, j, tk:2 * tk, :] = jnp.where(head0, zero, blk)
            return 0
        lax.fori_loop(0, n_blocks, body, 0)

    expand(kc_ref, kxc_ref, n_q * per_tile)
    expand(vc_ref, vxc_ref, n_q * per_tile)
    if n_past > 0:
        expand(kp_ref, kxp_ref, n_past)
        expand(vp_ref, vxp_ref, n_past)

    def add_blocks(kx_ref, vx_ref, i, j_last, masks, r0=0, r1=tq):
        rows = pl.ds(pl.multiple_of(i * tq + r0, tk), r1 - r0)
        carry = [carry_ref[p, r0:r1, :] for p in pairs]
        for u, mask in enumerate(masks):
            j = j_last - u
            for p in pairs:
                z = lax.dot_general(q_ref[0, rows, lanes[p]], kx_ref[p, j], (((1,), (1,)), ((), ())),
                                    preferred_element_type=F32)
                sp = jnp.maximum(z, 0.0) + jnp.log2(1.0 + jnp.exp2(-jnp.abs(z)))
                spm = sp if mask is None else jnp.where(mask, sp, 0.0)
                hi = spm.astype(BF16)
                later = jnp.dot(hi, bdu_ref[...], preferred_element_type=F32)
                total = jnp.dot(hi, bdo_ref[...], preferred_element_type=F32)
                a = jnp.exp2((z - sp) - (later + carry[p]))
                if mask is not None:
                    a = jnp.where(mask, a, 0.0)
                acc_ref[p, r0:r1, :] += jnp.dot(a.astype(BF16), vx_ref[p, j], preferred_element_type=F32)
                carry[p] = carry[p] + total
        for p in pairs:
            carry_ref[p, r0:r1, :] = carry[p]
        return carry

    def stick_left(carry):
        least = functools.reduce(jnp.minimum, carry)
        return (jnp.min(least) < SB_EXHAUSTED).astype(jnp.int32)

    def sweep(kx_ref, vx_ref, i, j_top, n_steps, unroll, alive):
        def cond(state):
            m, live = state
            return jnp.logical_and(m < n_steps, live > 0)

        def body(state):
            m, _ = state
            carry = add_blocks(kx_ref, vx_ref, i, j_top - m * unroll, [None] * unroll)
            return m + 1, stick_left(carry)

        return lax.while_loop(cond, body, (jnp.int32(0), alive))[1]

    def all_carries(r0=0):
        return [carry_ref[p, r0:, :] for p in pairs]

    def q_tile(i, any_left, first):
        rows = pl.ds(pl.multiple_of(i * tq, tq), tq)
        carry_ref[...] = jnp.zeros_like(carry_ref)
        acc_ref[...] = jnp.zeros_like(acc_ref)

        for jj in range(per_tile - 1, -1, -1):
            r0 = jj * tk
            add_blocks(kxc_ref, vxc_ref, i, i * per_tile + jj, [delta[r0:, :] < -r0], r0)

        unroll = SB_UNROLL if per_tile % SB_UNROLL == 0 else 1
        top = unroll * tk
        if first:
            alive = stick_left(all_carries())
        elif top < tq:
            j_prev = i * per_tile - 1
            add_blocks(kxc_ref, vxc_ref, i, j_prev, [None] * unroll, 0, top)

            @pl.when(stick_left(all_carries(top)) > 0)
            def _():
                add_blocks(kxc_ref, vxc_ref, i, j_prev, [None] * unroll, top, tq)

            alive = sweep(kxc_ref, vxc_ref, i, j_prev - unroll, (i * per_tile) // unroll - 1, unroll,
                          stick_left(all_carries()))
        else:
            alive = sweep(kxc_ref, vxc_ref, i, i * per_tile - 1, (i * per_tile) // unroll, unroll,
                          stick_left(all_carries()))

        if n_past > 0:
            n_full = n_past
            if past_valid < tk:
                @pl.when(alive > 0)
                def _():
                    add_blocks(kxp_ref, vxp_ref, i, n_past - 1, [key_in_block < past_valid])

                n_full = n_past - 1
                if n_full > 0:
                    alive = stick_left(all_carries())
            if n_full // SB_UNROLL > 0:
                alive = sweep(kxp_ref, vxp_ref, i, n_full - 1, n_full // SB_UNROLL, SB_UNROLL, alive)
            if n_full % SB_UNROLL:
                alive = sweep(kxp_ref, vxp_ref, i, n_full % SB_UNROLL - 1, n_full % SB_UNROLL, 1, alive)

        for p in pairs:
            o = acc_ref[p]
            hi, lo = _split_bf16(o * o)
            jm = j_ref[...]
            ms = (jnp.dot(hi, jm, preferred_element_type=F32)
                  + jnp.dot(lo, jm, preferred_element_type=F32)) * (1.0 / SB_HEAD_DIM)
            y = o * lax.rsqrt(ms + EPS) * g_ref[:, lanes[p]]
            o_ref[0, rows, lanes[p]] = y.astype(BF16)
        if not report_left:
            return any_left
        return jnp.maximum(any_left, stick_left(all_carries()))

    any_left = q_tile(jnp.int32(0), jnp.int32(0), True)
    any_left = lax.fori_loop(1, n_q, lambda i, left: q_tile(i, left, False), any_left)
    left_ref[...] = jnp.full(left_ref.shape, any_left, jnp.int32)


def _attention(q, k_cur, v_cur, k_past, v_past, past_len, g_sb_out, report_left=False):
    b, lq, _ = q.shape
    tk = SB_KEYS
    tq = min(SB_QUERY_TILE, lq)
    assert lq % tq == 0 and tq % tk == 0
    if k_past is None:
        k_past = jnp.zeros((1, tk, SB_WIDTH), BF16)
        v_past = jnp.zeros((1, tk, SB_WIDTH), BF16)
        n_past, past_valid = 0, tk
    else:
        assert k_past.shape[1] % tk == 0
        n_past = -(-past_len // tk)
        past_valid = past_len - (n_past - 1) * tk
        k_past = k_past[:, :n_past * tk]
        v_past = v_past[:, :n_past * tk]
    bp, p, _ = k_past.shape
    jj = jnp.arange(2 * tk)
    same_head = (jj[:, None] // tk) == (jj[None, :] // tk)
    bdu = (same_head & (jj[:, None] > jj[None, :])).astype(BF16)
    bdo = same_head.astype(BF16)
    ll = jnp.arange(LANES) // SB_HEAD_DIM
    jm = (ll[:, None] == ll[None, :]).astype(BF16)
    width = SB_PAIRS * PAIR
    cur_spec = pl.BlockSpec((1, lq, width), lambda bi, hp: (bi, 0, hp))
    past_spec = pl.BlockSpec((1, p, width), (lambda bi, hp: (bi, 0, hp)) if bp > 1
                             else (lambda bi, hp: (0, 0, hp)))
    kernel = functools.partial(_attention_kernel, tq=tq, n_q=lq // tq, n_past=n_past,
                               past_valid=past_valid, report_left=report_left)
    stacked = lambda n: pltpu.VMEM((SB_PAIRS, n, 2 * tk, PAIR), BF16)
    n_groups = SB_WIDTH // width
    out, left = pl.pallas_call(
        kernel,
        grid=(b, n_groups),
        in_specs=[cur_spec, cur_spec, cur_spec, past_spec, past_spec,
                  _const_spec(bdu.shape), _const_spec(bdo.shape), _const_spec(jm.shape),
                  pl.BlockSpec((1, width), lambda bi, hp: (0, hp))],
        out_specs=[cur_spec, pl.BlockSpec((1, 1, SUBLANES, LANES), lambda bi, hp: (bi, hp, 0, 0))],
        out_shape=[jax.ShapeDtypeStruct((b, lq, SB_WIDTH), BF16),
                   jax.ShapeDtypeStruct((b, n_groups, SUBLANES, LANES), jnp.int32)],
        scratch_shapes=[stacked(lq // tk), stacked(lq // tk), stacked(p // tk), stacked(p // tk),
                        pltpu.VMEM((SB_PAIRS, tq, 2 * tk), F32), pltpu.VMEM((SB_PAIRS, tq, PAIR), F32)],
        compiler_params=pltpu.CompilerParams(
            dimension_semantics=("arbitrary", "arbitrary"), vmem_limit_bytes=VMEM_LIMIT),
        name="attention",
    )(q, k_cur, v_cur, k_past, v_past, bdu, bdo, jm, g_sb_out.reshape(1, SB_WIDTH))
    return out, left


def _attention_over_cache(q, k_cur, v_cur, cache_k, cache_v, g_sb_out):
    bd, _, past, _ = cache_k.shape
    to_rows = lambda c: c.transpose(0, 2, 1, 3).reshape(bd, c.shape[2], SB_WIDTH).astype(BF16)
    recent = min(past, SB_RECENT)
    out, left = _attention(q, k_cur, v_cur, to_rows(cache_k[:, :, past - recent:]),
                           to_rows(cache_v[:, :, past - recent:]), recent, g_sb_out, report_left=True)
    if recent == past:
        return out
    return lax.cond(
        jnp.any(left > 0),
        lambda: _attention(q, k_cur, v_cur, to_rows(cache_k), to_rows(cache_v), past, g_sb_out)[0],
        lambda: out)


def _retention_kernel(q_ref, k_ref, v_ref, gate_ref, s0_ref, g_ref, o_ref, s_ref, *, chunk, n_chunks):
    c = chunk
    head = pl.program_id(1).astype(F32)
    log_g = jnp.log(1.0 - jnp.exp2(jnp.full((1, LANES), -5.0, F32) - head))
    n_row = lax.broadcasted_iota(jnp.int32, (c, LANES), 0).astype(F32)
    q_decay = jnp.exp((n_row + 1.0) * log_g)
    k_decay = jnp.exp((c - 1.0 - n_row) * log_g)
    s_decay = jnp.exp(float(c) * log_g)
    diff = (lax.broadcasted_iota(jnp.int32, (c, c), 0)
            - lax.broadcasted_iota(jnp.int32, (c, c), 1))
    log_g_cc = log_g if c == LANES else jnp.log(1.0 - jnp.exp2(jnp.full((1, c), -5.0, F32) - head))
    decay = jnp.where(diff >= 0, jnp.exp(jnp.maximum(diff, 0).astype(F32) * log_g_cc), 0.0)
    gain = g_ref[...]

    def body(ci, s):
        rows = pl.ds(pl.multiple_of(ci * c, c), c)
        q = q_ref[0, rows, :]
        k = k_ref[0, rows, :]
        v = v_ref[0, rows, :]
        qk = lax.dot_general(q, k, (((1,), (1,)), ((), ())), preferred_element_type=F32)
        inner = jnp.dot((qk * decay).astype(BF16), v, preferred_element_type=F32)
        cross = jnp.dot(q, s.astype(BF16), preferred_element_type=F32) * q_decay
        o = inner + cross
        k_dec = (k.astype(F32) * k_decay).astype(BF16)
        s_new = s_decay * s + lax.dot_general(k_dec, v, (((0,), (0,)), ((), ())),
                                              preferred_element_type=F32)
        y = o * lax.rsqrt(jnp.mean(o * o, axis=-1, keepdims=True) + EPS) * gain
        gate = gate_ref[0, rows, :].astype(F32)
        o_ref[0, rows, :] = (y * (gate * jax.nn.sigmoid(gate))).astype(BF16)
        return s_new

    s_ref[0, 0] = lax.fori_loop(0, n_chunks, body, s0_ref[0, 0], unroll=math.gcd(n_chunks, RET_UNROLL))


def _retention(rq, rk, rv, rgate, s0, g_ret_out, chunk):
    b, l, _ = rq.shape
    assert l % chunk == 0
    bs = s0.shape[0]
    seq_spec = pl.BlockSpec((1, l, RET_DK), lambda bi, hh: (bi, 0, hh))
    state_spec = pl.BlockSpec((1, 1, RET_DK, RET_DV), lambda bi, hh: (bi, hh, 0, 0))
    s0_spec = state_spec if bs > 1 else pl.BlockSpec((1, 1, RET_DK, RET_DV), lambda bi, hh: (0, hh, 0, 0))
    kernel = functools.partial(_retention_kernel, chunk=chunk, n_chunks=l // chunk)
    return pl.pallas_call(
        kernel,
        grid=(b, RET_HEADS),
        in_specs=[seq_spec, seq_spec, seq_spec, seq_spec, s0_spec,
                  pl.BlockSpec((1, RET_DV), lambda bi, hh: (0, hh))],
        out_specs=[seq_spec, state_spec],
        out_shape=[jax.ShapeDtypeStruct((b, l, RET_WIDTH), BF16),
                   jax.ShapeDtypeStruct((b, RET_HEADS, RET_DK, RET_DV), F32)],
        compiler_params=pltpu.CompilerParams(
            dimension_semantics=("arbitrary", "arbitrary"), vmem_limit_bytes=VMEM_LIMIT),
        name="retention",
    )(rq, rk, rv, rgate, s0, g_ret_out.reshape(1, RET_WIDTH))


def _merge_ffn_kernel(x_ref, sb_ref, ret_ref, conv0_ref, wo_ref, gf_ref, wu_ref,
                      cw_ref, cb_ref, wd_ref, gl_ref, y_ref, conv_ref,
                      h_ref, hn_ref, ubuf_ref, act_ref, carry_ref, *, tm):
    i = pl.program_id(1)

    @pl.when(i == 0)
    def _():
        carry_ref[...] = conv0_ref[0]

    mixed = jnp.concatenate([sb_ref[0], ret_ref[0]], axis=1)
    h = x_ref[0] + jnp.dot(mixed, wo_ref[...], preferred_element_type=F32)
    h_ref[...] = h
    ms = jnp.mean(h * h, axis=-1, keepdims=True)
    hn_ref[...] = (h * lax.rsqrt(ms + EPS) * gf_ref[...]).astype(BF16)
    lo = SUBLANES - (CONV_W - 1)

    def up_project(j):
        ubuf = ubuf_ref.at[j % 2]
        hn = hn_ref[...]
        ubuf[0:SUBLANES, :] = carry_ref[j]
        gate_cols = slice(j * FF_CHUNK, (j + 1) * FF_CHUNK)
        val_cols = slice(D_FF + j * FF_CHUNK, D_FF + (j + 1) * FF_CHUNK)
        ubuf[SUBLANES:, 0:FF_CHUNK] = jnp.dot(hn, wu_ref[:, gate_cols], preferred_element_type=F32)
        ubuf[SUBLANES:, FF_CHUNK:] = jnp.dot(hn, wu_ref[:, val_cols], preferred_element_type=F32)
        carry_ref[j] = ubuf[tm:tm + SUBLANES, :]

    def gated_conv(j):
        ubuf = ubuf_ref.at[j % 2]
        cw = cw_ref[j]
        cb = cb_ref[j]
        strip = min(tm, FF_STRIP)
        for r in range(0, tm, strip):
            ext = ubuf[r:r + strip + SUBLANES, :]
            c = cb
            for tap in range(CONV_W):
                back = CONV_W - 1 - tap
                rows = ext if back == 0 else pltpu.roll(ext, back, 0)
                c = c + cw[tap:tap + 1, :] * rows[SUBLANES:, :]
            gate = c[:, :FF_CHUNK]
            act_ref[r:r + strip, j * FF_CHUNK:(j + 1) * FF_CHUNK] = (
                (gate * jax.nn.sigmoid(gate)) * c[:, FF_CHUNK:]).astype(BF16)

    up_project(0)
    for j in range(N_FF_CHUNKS):
        if j + 1 < N_FF_CHUNKS:
            up_project(j + 1)
        gated_conv(j)

    hh = h_ref[...] + jnp.dot(act_ref[...], wd_ref[...], preferred_element_type=F32)
    ms2 = jnp.mean(hh * hh, axis=-1, keepdims=True)
    y_ref[0] = hh * lax.rsqrt(ms2 + EPS) * gl_ref[...]

    @pl.when(i == pl.num_programs(1) - 1)
    def _():
        conv_ref[0] = carry_ref[...]


def _merge_ffn(x, sb_n, ret_n, conv0, w_out_bf, g_norm_ffn, w_up_bf, cw, cb, wd, g_norm_final, tm):
    b, l, d = x.shape
    bs = conv0.shape[0]
    row_spec = lambda w: pl.BlockSpec((1, tm, w), lambda bi, i: (bi, i, 0))
    conv_shape = (1, N_FF_CHUNKS, SUBLANES, 2 * FF_CHUNK)
    conv_spec = pl.BlockSpec(conv_shape, lambda bi, i: (bi, 0, 0, 0))
    conv0_spec = conv_spec if bs > 1 else pl.BlockSpec(conv_shape, lambda bi, i: (0, 0, 0, 0))
    kernel = functools.partial(_merge_ffn_kernel, tm=tm)
    return pl.pallas_call(
        kernel,
        grid=(b, l // tm),
        in_specs=[row_spec(d), row_spec(SB_WIDTH), row_spec(RET_WIDTH), conv0_spec,
                  _const_spec(w_out_bf.shape), _const_spec((1, d)),
                  _const_spec(w_up_bf.shape),
                  _const_spec(cw.shape), _const_spec(cb.shape), _const_spec(wd.shape),
                  _const_spec((1, d))],
        out_specs=[row_spec(d), conv_spec],
        out_shape=[jax.ShapeDtypeStruct((b, l, d), F32),
                   jax.ShapeDtypeStruct((b,) + conv_shape[1:], F32)],
        scratch_shapes=[pltpu.VMEM((tm, d), F32), pltpu.VMEM((tm, d), BF16),
                        pltpu.VMEM((2, tm + SUBLANES, 2 * FF_CHUNK), F32),
                        pltpu.VMEM((tm, D_FF), BF16),
                        pltpu.VMEM(conv_shape[1:], F32)],
        compiler_params=pltpu.CompilerParams(
            dimension_semantics=("arbitrary", "arbitrary"), vmem_limit_bytes=VMEM_LIMIT),
        name="merge_ffn",
    )(x, sb_n, ret_n, conv0, w_out_bf, g_norm_ffn.reshape(1, d), w_up_bf, cw, cb, wd,
      g_norm_final.reshape(1, d))


def _rope_tables(pos):
    half = RET_DK // 2
    inv = ROPE_BASE ** (-jnp.arange(half, dtype=F32) / half)
    ang = pos.astype(F32)[:, None] * inv[None, :]
    cos, sin = jnp.cos(ang), jnp.sin(ang)
    return jnp.concatenate([cos, cos], axis=1), jnp.concatenate([-sin, sin], axis=1)


def _conv_state_to_chunks(state):
    b = state.shape[0]
    s = state.reshape(b, CONV_W - 1, 2, N_FF_CHUNKS, FF_CHUNK).transpose(0, 3, 1, 2, 4)
    s = s.reshape(b, N_FF_CHUNKS, CONV_W - 1, 2 * FF_CHUNK)
    return jnp.pad(s, ((0, 0), (0, 0), (SUBLANES - (CONV_W - 1), 0), (0, 0)))


def _conv_state_from_chunks(chunks):
    b = chunks.shape[0]
    s = chunks[:, :, SUBLANES - (CONV_W - 1):, :].reshape(b, N_FF_CHUNKS, CONV_W - 1, 2, FF_CHUNK)
    return s.transpose(0, 2, 3, 1, 4).reshape(b, CONV_W - 1, 2 * D_FF)


def _pad_rows(a, rows):
    return a if a.shape[1] == rows else jnp.pad(a, ((0, 0), (0, rows - a.shape[1]), (0, 0)))


def _prefix_rows_kernel(k_any, v_any, k_rows_ref, v_rows_ref, k_ref, v_ref):
    del k_any, v_any
    k_ref[...] = k_rows_ref[...]
    v_ref[...] = v_rows_ref[...]


def _write_prefix_rows(k_big, v_big, k_rows, v_rows):
    b = k_big.shape[0]
    blk = (1,) + k_rows.shape[1:]
    any_spec = pl.BlockSpec(memory_space=pl.ANY)
    rows_spec = pl.BlockSpec(blk, lambda bi: (0, 0, 0, 0))
    out_spec = pl.BlockSpec(blk, lambda bi: (bi, 0, 0, 0))
    big = jax.ShapeDtypeStruct(k_big.shape, k_big.dtype)
    return pl.pallas_call(
        _prefix_rows_kernel,
        grid=(b,),
        in_specs=[any_spec, any_spec, rows_spec, rows_spec],
        out_specs=[out_spec, out_spec],
        out_shape=[big, big],
        input_output_aliases={0: 0, 1: 1},
        name="prefix_rows",
    )(k_big, v_big, k_rows, v_rows)


def _stream_step(x, pos0, attend, s0, conv0, wts, tm, chunk, head_row0=0):
    l = x.shape[1]
    cos2, sin2 = _rope_tables(pos0 + jnp.arange(l))
    q, k, v, k_out, v_out, rq, rk, rv, rgate = _project(x, wts["g_norm_mix"], wts["w_in"], cos2, sin2,
                                                       tm, head_row0)
    lq = -(-l // SB_KEYS) * SB_KEYS
    sb_n = attend(_pad_rows(q, lq), _pad_rows(k, lq), _pad_rows(v, lq))[:, :l]
    ret_n, s_new = _retention(rq, rk, rv, rgate, s0, wts["g_ret_out"], chunk)
    y, conv_new = _merge_ffn(x, sb_n, ret_n, conv0, wts["w_out"], wts["g_norm_ffn"], wts["w_up"],
                             wts["cw"], wts["cb"], wts["wd"], wts["g_norm_final"], tm)
    return y, k, v, k_out, v_out, s_new, conv_new


def kernel(x_prompt, x_sample, cache_sb_k, cache_sb_v, state_ret, state_conv, meta_tokens, g_norm_mix, w_in, g_sb_out, g_ret_out, w_out, g_norm_ffn, w_up, conv_w, conv_b, w_down, g_norm_final):
    b, seq, d = x_prompt.shape
    bd, ls, _ = x_sample.shape
    past = cache_sb_k.shape[2]

    def ff_cols(a):
        r = a.shape[0]
        return a.reshape(r, 2, N_FF_CHUNKS, FF_CHUNK).transpose(2, 0, 1, 3).reshape(N_FF_CHUNKS, r, 2 * FF_CHUNK)

    wts = dict(
        g_norm_mix=g_norm_mix, g_sb_out=g_sb_out, g_ret_out=g_ret_out, g_norm_ffn=g_norm_ffn,
        g_norm_final=g_norm_final,
        w_in=w_in.astype(BF16), w_out=w_out.astype(BF16),
        w_up=w_up.astype(BF16),
        cw=ff_cols(conv_w), cb=ff_cols(conv_b.reshape(1, 2 * D_FF)),
        wd=w_down.astype(BF16),
    )

    zero_state = jnp.zeros((1, RET_HEADS, RET_DK, RET_DV), F32)
    zero_conv = jnp.zeros((1, N_FF_CHUNKS, SUBLANES, 2 * FF_CHUNK), F32)
    gain = wts["g_sb_out"]
    _, k_m, v_m, k_m_out, v_m_out, s_meta, conv_meta = _stream_step(
        meta_tokens[None], -N_META, lambda q, k, v: _attention(q, k, v, None, None, 0, gain)[0],
        zero_state, zero_conv, wts, N_META, N_META)

    k_m, v_m = _pad_rows(k_m, SB_KEYS), _pad_rows(v_m, SB_KEYS)
    y_prompt, _, _, k_p_out, v_p_out, s_prompt, conv_prompt = _stream_step(
        x_prompt, 0, lambda q, k, v: _attention(q, k, v, k_m, v_m, N_META, gain)[0],
        s_meta, conv_meta, wts, 512, 256, head_row0=N_META)
    new_k_prompt, new_v_prompt = _write_prefix_rows(k_p_out, v_p_out, k_m_out, v_m_out)

    y_sample, _, _, k_s_out, v_s_out, s_sample, conv_sample = _stream_step(
        x_sample, past, lambda q, k, v: _attention_over_cache(q, k, v, cache_sb_k, cache_sb_v, gain),
        state_ret, _conv_state_to_chunks(state_conv), wts, ls, ls)

    return (y_prompt, y_sample, new_k_prompt, new_v_prompt, s_prompt,
            _conv_state_from_chunks(conv_prompt), k_s_out, v_s_out, s_sample,
            _conv_state_from_chunks(conv_sample))
```

```python
import functools
import math

import jax
import jax.numpy as jnp
from jax import lax
from jax.experimental import pallas as pl
from jax.experimental.pallas import tpu as pltpu

D_MODEL = 1024
N_META = 16
SB_HEADS = 8
SB_HEAD_DIM = 64
SB_WIDTH = SB_HEADS * SB_HEAD_DIM
RET_HEADS = 4
RET_DK = 128
RET_DV = 128
RET_WIDTH = RET_HEADS * RET_DV
MIX_WIDTH = SB_WIDTH + RET_WIDTH
GROUP = 512
N_GROUPS = 7
IN_WIDTH = N_GROUPS * GROUP
D_FF = 2816
CONV_W = 3
ROPE_BASE = 10000.0
EPS = 1e-5

LANES = 128
SUBLANES = 8
FF_CHUNK = 256
N_FF_CHUNKS = D_FF // FF_CHUNK
FF_STRIP = 64
RET_UNROLL = 8
SB_KEYS = 128
SB_QUERY_TILE = 512
SB_UNROLL = 2
PAIR = 2 * SB_HEAD_DIM
SB_PAIRS = 2
SB_RECENT = 512
SB_EXHAUSTED = 152.0
SB_Q_SCALE = SB_HEAD_DIM ** -0.5 * math.log2(math.e)
VMEM_LIMIT = 56 * 1024 * 1024

BF16 = jnp.bfloat16
F32 = jnp.float32


def _const_spec(shape):
    zeros = (0,) * len(shape)
    return pl.BlockSpec(shape, lambda *_: zeros, pipeline_mode=pl.Buffered(1))


def _project_kernel(x_ref, g_ref, w_ref, cos_ref, sin_ref,
                    q_ref, k_ref, v_ref, ko_ref, vo_ref, rq_ref, rk_ref, rv_ref, rg_ref):
    x = x_ref[0]
    ms = jnp.mean(x * x, axis=-1, keepdims=True)
    h = (x * lax.rsqrt(ms + EPS) * g_ref[...]).astype(BF16)

    def group(i):
        return jnp.dot(h, w_ref[:, i * GROUP:(i + 1) * GROUP], preferred_element_type=F32)

    def split_heads(p, out_ref):
        for hh in range(SB_HEADS):
            out_ref[0, hh] = p[:, hh * SB_HEAD_DIM:(hh + 1) * SB_HEAD_DIM]

    def rope(p, out_ref, scale):
        cos = cos_ref[...]
        sin = sin_ref[...]
        for hh in range(RET_HEADS):
            t = p[:, hh * RET_DK:(hh + 1) * RET_DK]
            r = t * cos + pltpu.roll(t, RET_DK // 2, 1) * sin
            if scale is not None:
                r = r * scale
            out_ref[0, :, hh * RET_DK:(hh + 1) * RET_DK] = r.astype(BF16)

    q_ref[0] = (group(0) * SB_Q_SCALE).astype(BF16)
    pk = group(1)
    k_ref[0] = pk.astype(BF16)
    split_heads(pk, ko_ref)
    pv = group(2)
    v_ref[0] = pv.astype(BF16)
    split_heads(pv, vo_ref)
    rope(group(3), rq_ref, None)
    rope(group(4), rk_ref, RET_DK ** -0.5)
    rv_ref[0] = group(5).astype(BF16)
    rg_ref[0] = group(6).astype(BF16)


def _project(x, g_norm, w_in_bf, cos2, sin2, tm, head_row0=0):
    b, l, d = x.shape
    grid = (b, l // tm)
    row_spec = lambda w: pl.BlockSpec((1, tm, w), lambda bi, i: (bi, i, 0))
    head_blk = (1, SB_HEADS, tm, SB_HEAD_DIM)
    head_spec = pl.BlockSpec(tuple(pl.Element(n) for n in head_blk),
                             lambda bi, i: (bi, 0, pl.multiple_of(head_row0 + i * tm, SUBLANES), 0))
    tab_spec = pl.BlockSpec((tm, RET_DK), lambda bi, i: (i, 0))
    act = jax.ShapeDtypeStruct((b, l, GROUP), BF16)
    heads = jax.ShapeDtypeStruct((b, SB_HEADS, head_row0 + l, SB_HEAD_DIM), F32)
    return pl.pallas_call(
        _project_kernel,
        grid=grid,
        in_specs=[row_spec(d), _const_spec((1, d)), _const_spec((d, IN_WIDTH)), tab_spec, tab_spec],
        out_specs=[row_spec(GROUP)] * 3 + [head_spec] * 2 + [row_spec(GROUP)] * 4,
        out_shape=[act] * 3 + [heads] * 2 + [act] * 4,
        compiler_params=pltpu.CompilerParams(
            dimension_semantics=("arbitrary", "arbitrary"), vmem_limit_bytes=VMEM_LIMIT),
        name="project",
    )(x, g_norm.reshape(1, d), w_in_bf, cos2, sin2)


def _split_bf16(x):
    hi = x.astype(BF16)
    lo = (x - hi.astype(F32)).astype(BF16)
    return hi, lo


def _attention_kernel(q_ref, kc_ref, vc_ref, kp_ref, vp_ref, bdu_ref, bdo_ref, j_ref, g_ref,
                      o_ref, left_ref, kxc_ref, vxc_ref, kxp_ref, vxp_ref, carry_ref, acc_ref,
                      *, tq, n_q, n_past, past_valid, report_left):
    tk = SB_KEYS
    per_tile = tq // tk
    head0 = lax.broadcasted_iota(jnp.int32, (tk, LANES), 1) < SB_HEAD_DIM
    key_in_block = lax.broadcasted_iota(jnp.int32, (tq, 2 * tk), 1) & (tk - 1)
    delta = key_in_block - lax.broadcasted_iota(jnp.int32, (tq, 2 * tk), 0)

    pairs = range(SB_PAIRS)
    lanes = [slice(p * PAIR, (p + 1) * PAIR) for p in pairs]

    def expand(src_ref, dst_ref, n_blocks):
        def body(j, _):
            for p in pairs:
                blk = src_ref[0, pl.ds(pl.multiple_of(j * tk, tk), tk), lanes[p]]
                zero = jnp.zeros_like(blk)
                dst_ref[p, j, 0:tk, :] = jnp.where(head0, blk, zero)
                dst_ref[p, j, tk:2 * tk, :] = jnp.where(head0, zero, blk)
            return 0
        lax.fori_loop(0, n_blocks, body, 0)

    expand(kc_ref, kxc_ref, n_q * per_tile)
    expand(vc_ref, vxc_ref, n_q * per_tile)
    if n_past > 0:
        expand(kp_ref, kxp_ref, n_past)
        expand(vp_ref, vxp_ref, n_past)

    def keep(x, mask):
        if mask is None:
            return x
        n = mask.shape[0]
        top = jnp.where(mask, x[:n], 0.0)
        return top if n == x.shape[0] else jnp.concatenate([top, x[n:]], axis=0)

    def add_blocks(kx_ref, vx_ref, i, j_last, masks, r0=0, r1=tq):
        rows = pl.ds(pl.multiple_of(i * tq + r0, tk), r1 - r0)
        carry = [carry_ref[p, r0:r1, :] for p in pairs]
        for u, mask in enumerate(masks):
            j = j_last - u
            for p in pairs:
                z = lax.dot_general(q_ref[0, rows, lanes[p]], kx_ref[p, j], (((1,), (1,)), ((), ())),
                                    preferred_element_type=F32)
                sp = jnp.maximum(z, 0.0) + jnp.log2(1.0 + jnp.exp2(-jnp.abs(z)))
                hi = keep(sp, mask).astype(BF16)
                later = jnp.dot(hi, bdu_ref[...], preferred_element_type=F32)
                total = jnp.dot(hi, bdo_ref[...], preferred_element_type=F32)
                a = keep(jnp.exp2((z - sp) - (later + carry[p])), mask)
                acc_ref[p, r0:r1, :] += jnp.dot(a.astype(BF16), vx_ref[p, j], preferred_element_type=F32)
                carry[p] = carry[p] + total
        for p in pairs:
            carry_ref[p, r0:r1, :] = carry[p]
        return carry

    def stick_left(carry):
        least = functools.reduce(jnp.minimum, carry)
        return (jnp.min(least) < SB_EXHAUSTED).astype(jnp.int32)

    def sweep(kx_ref, vx_ref, i, j_top, n_steps, unroll, alive):
        def cond(state):
            m, live = state
            return jnp.logical_and(m < n_steps, live > 0)

        def body(state):
            m, _ = state
            carry = add_blocks(kx_ref, vx_ref, i, j_top - m * unroll, [None] * unroll)
            return m + 1, stick_left(carry)

        return lax.while_loop(cond, body, (jnp.int32(0), alive))[1]

    def all_carries(r0=0):
        return [carry_ref[p, r0:, :] for p in pairs]

    def q_tile(i, any_left, first):
        rows = pl.ds(pl.multiple_of(i * tq, tq), tq)
        carry_ref[...] = jnp.zeros_like(carry_ref)
        acc_ref[...] = jnp.zeros_like(acc_ref)

        for jj in range(per_tile - 1, -1, -1):
            r0 = jj * tk
            add_blocks(kxc_ref, vxc_ref, i, i * per_tile + jj, [delta[r0:r0 + tk, :] < -r0], r0)

        unroll = SB_UNROLL if per_tile % SB_UNROLL == 0 else 1
        top = unroll * tk
        if first:
            alive = stick_left(all_carries())
        elif top < tq:
            j_prev = i * per_tile - 1
            add_blocks(kxc_ref, vxc_ref, i, j_prev, [None] * unroll, 0, top)

            @pl.when(stick_left(all_carries(top)) > 0)
            def _():
                add_blocks(kxc_ref, vxc_ref, i, j_prev, [None] * unroll, top, tq)

            alive = sweep(kxc_ref, vxc_ref, i, j_prev - unroll, (i * per_tile) // unroll - 1, unroll,
                          stick_left(all_carries()))
        else:
            alive = sweep(kxc_ref, vxc_ref, i, i * per_tile - 1, (i * per_tile) // unroll, unroll,
                          stick_left(all_carries()))

        if n_past > 0:
            n_full = n_past
            if past_valid < tk:
                @pl.when(alive > 0)
                def _():
                    add_blocks(kxp_ref, vxp_ref, i, n_past - 1, [key_in_block < past_valid])

                n_full = n_past - 1
                if n_full > 0:
                    alive = stick_left(all_carries())
            if n_full // SB_UNROLL > 0:
                alive = sweep(kxp_ref, vxp_ref, i, n_full - 1, n_full // SB_UNROLL, SB_UNROLL, alive)
            if n_full % SB_UNROLL:
                alive = sweep(kxp_ref, vxp_ref, i, n_full % SB_UNROLL - 1, n_full % SB_UNROLL, 1, alive)

        for p in pairs:
            o = acc_ref[p]
            hi, lo = _split_bf16(o * o)
            jm = j_ref[...]
            ms = (jnp.dot(hi, jm, preferred_element_type=F32)
                  + jnp.dot(lo, jm, preferred_element_type=F32)) * (1.0 / SB_HEAD_DIM)
            y = o * lax.rsqrt(ms + EPS) * g_ref[:, lanes[p]]
            o_ref[0, rows, lanes[p]] = y.astype(BF16)
        if not report_left:
            return any_left
        return jnp.maximum(any_left, stick_left(all_carries()))

    any_left = q_tile(jnp.int32(0), jnp.int32(0), True)
    any_left = lax.fori_loop(1, n_q, lambda i, left: q_tile(i, left, False), any_left)
    left_ref[...] = jnp.full(left_ref.shape, any_left, jnp.int32)


def _attention(q, k_cur, v_cur, k_past, v_past, past_len, g_sb_out, report_left=False):
    b, lq, _ = q.shape
    tk = SB_KEYS
    tq = min(SB_QUERY_TILE, lq)
    assert lq % tq == 0 and tq % tk == 0
    if k_past is None:
        k_past = jnp.zeros((1, tk, SB_WIDTH), BF16)
        v_past = jnp.zeros((1, tk, SB_WIDTH), BF16)
        n_past, past_valid = 0, tk
    else:
        assert k_past.shape[1] % tk == 0
        n_past = -(-past_len // tk)
        past_valid = past_len - (n_past - 1) * tk
        k_past = k_past[:, :n_past * tk]
        v_past = v_past[:, :n_past * tk]
    bp, p, _ = k_past.shape
    jj = jnp.arange(2 * tk)
    same_head = (jj[:, None] // tk) == (jj[None, :] // tk)
    bdu = (same_head & (jj[:, None] > jj[None, :])).astype(BF16)
    bdo = same_head.astype(BF16)
    ll = jnp.arange(LANES) // SB_HEAD_DIM
    jm = (ll[:, None] == ll[None, :]).astype(BF16)
    width = SB_PAIRS * PAIR
    cur_spec = pl.BlockSpec((1, lq, width), lambda bi, hp: (bi, 0, hp))
    past_spec = pl.BlockSpec((1, p, width), (lambda bi, hp: (bi, 0, hp)) if bp > 1
                             else (lambda bi, hp: (0, 0, hp)))
    kernel = functools.partial(_attention_kernel, tq=tq, n_q=lq // tq, n_past=n_past,
                               past_valid=past_valid, report_left=report_left)
    stacked = lambda n: pltpu.VMEM((SB_PAIRS, n, 2 * tk, PAIR), BF16)
    n_groups = SB_WIDTH // width
    out, left = pl.pallas_call(
        kernel,
        grid=(b, n_groups),
        in_specs=[cur_spec, cur_spec, cur_spec, past_spec, past_spec,
                  _const_spec(bdu.shape), _const_spec(bdo.shape), _const_spec(jm.shape),
                  pl.BlockSpec((1, width), lambda bi, hp: (0, hp))],
        out_specs=[cur_spec, pl.BlockSpec((1, 1, SUBLANES, LANES), lambda bi, hp: (bi, hp, 0, 0))],
        out_shape=[jax.ShapeDtypeStruct((b, lq, SB_WIDTH), BF16),
                   jax.ShapeDtypeStruct((b, n_groups, SUBLANES, LANES), jnp.int32)],
        scratch_shapes=[stacked(lq // tk), stacked(lq // tk), stacked(p // tk), stacked(p // tk),
                        pltpu.VMEM((SB_PAIRS, tq, 2 * tk), F32), pltpu.VMEM((SB_PAIRS, tq, PAIR), F32)],
        compiler_params=pltpu.CompilerParams(
            dimension_semantics=("arbitrary", "arbitrary"), vmem_limit_bytes=VMEM_LIMIT),
        name="attention",
    )(q, k_cur, v_cur, k_past, v_past, bdu, bdo, jm, g_sb_out.reshape(1, SB_WIDTH))
    return out, left


def _attention_over_cache(q, k_cur, v_cur, cache_k, cache_v, g_sb_out):
    bd, _, past, _ = cache_k.shape
    to_rows = lambda c: c.transpose(0, 2, 1, 3).reshape(bd, c.shape[2], SB_WIDTH).astype(BF16)
    recent = min(past, SB_RECENT)
    out, left = _attention(q, k_cur, v_cur, to_rows(cache_k[:, :, past - recent:]),
                           to_rows(cache_v[:, :, past - recent:]), recent, g_sb_out, report_left=True)
    if recent == past:
        return out
    return lax.cond(
        jnp.any(left > 0),
        lambda: _attention(q, k_cur, v_cur, to_rows(cache_k), to_rows(cache_v), past, g_sb_out)[0],
        lambda: out)


def _retention_kernel(q_ref, k_ref, v_ref, gate_ref, s0_ref, g_ref, o_ref, s_ref, *, chunk, n_chunks):
    c = chunk
    head = pl.program_id(1).astype(F32)
    log_g = jnp.log(1.0 - jnp.exp2(jnp.full((1, LANES), -5.0, F32) - head))
    n_row = lax.broadcasted_iota(jnp.int32, (c, LANES), 0).astype(F32)
    q_decay = jnp.exp((n_row + 1.0) * log_g)
    k_decay = jnp.exp((c - 1.0 - n_row) * log_g)
    s_decay = jnp.exp(float(c) * log_g)
    diff = (lax.broadcasted_iota(jnp.int32, (c, c), 0)
            - lax.broadcasted_iota(jnp.int32, (c, c), 1))
    log_g_cc = log_g if c == LANES else jnp.log(1.0 - jnp.exp2(jnp.full((1, c), -5.0, F32) - head))
    decay = jnp.where(diff >= 0, jnp.exp(jnp.maximum(diff, 0).astype(F32) * log_g_cc), 0.0)
    gain = g_ref[...]

    def body(ci, s):
        rows = pl.ds(pl.multiple_of(ci * c, c), c)
        q = q_ref[0, rows, :]
        k = k_ref[0, rows, :]
        v = v_ref[0, rows, :]
        qk = lax.dot_general(q, k, (((1,), (1,)), ((), ())), preferred_element_type=F32)
        inner = jnp.dot((qk * decay).astype(BF16), v, preferred_element_type=F32)
        cross = jnp.dot(q, s.astype(BF16), preferred_element_type=F32) * q_decay
        o = inner + cross
        k_dec = (k.astype(F32) * k_decay).astype(BF16)
        s_new = s_decay * s + lax.dot_general(k_dec, v, (((0,), (0,)), ((), ())),
                                              preferred_element_type=F32)
        y = o * lax.rsqrt(jnp.mean(o * o, axis=-1, keepdims=True) + EPS) * gain
        gate = gate_ref[0, rows, :].astype(F32)
        o_ref[0, rows, :] = (y * (gate * jax.nn.sigmoid(gate))).astype(BF16)
        return s_new

    s_ref[0, 0] = lax.fori_loop(0, n_chunks, body, s0_ref[0, 0], unroll=math.gcd(n_chunks, RET_UNROLL))


def _retention(rq, rk, rv, rgate, s0, g_ret_out, chunk):
    b, l, _ = rq.shape
    assert l % chunk == 0
    bs = s0.shape[0]
    seq_spec = pl.BlockSpec((1, l, RET_DK), lambda bi, hh: (bi, 0, hh))
    state_spec = pl.BlockSpec((1, 1, RET_DK, RET_DV), lambda bi, hh: (bi, hh, 0, 0))
    s0_spec = state_spec if bs > 1 else pl.BlockSpec((1, 1, RET_DK, RET_DV), lambda bi, hh: (0, hh, 0, 0))
    kernel = functools.partial(_retention_kernel, chunk=chunk, n_chunks=l // chunk)
    return pl.pallas_call(
        kernel,
        grid=(b, RET_HEADS),
        in_specs=[seq_spec, seq_spec, seq_spec, seq_spec, s0_spec,
                  pl.BlockSpec((1, RET_DV), lambda bi, hh: (0, hh))],
        out_specs=[seq_spec, state_spec],
        out_shape=[jax.ShapeDtypeStruct((b, l, RET_WIDTH), BF16),
                   jax.ShapeDtypeStruct((b, RET_HEADS, RET_DK, RET_DV), F32)],
        compiler_params=pltpu.CompilerParams(
            dimension_semantics=("arbitrary", "arbitrary"), vmem_limit_bytes=VMEM_LIMIT),
        name="retention",
    )(rq, rk, rv, rgate, s0, g_ret_out.reshape(1, RET_WIDTH))


def _merge_ffn_kernel(x_ref, sb_ref, ret_ref, conv0_ref, wo_ref, gf_ref, wu_ref,
                      cw_ref, cb_ref, wd_ref, gl_ref, y_ref, conv_ref,
                      h_ref, hn_ref, ubuf_ref, act_ref, carry_ref, *, tm):
    i = pl.program_id(1)

    @pl.when(i == 0)
    def _():
        carry_ref[...] = conv0_ref[0]

    mixed = jnp.concatenate([sb_ref[0], ret_ref[0]], axis=1)
    h = x_ref[0] + jnp.dot(mixed, wo_ref[...], preferred_element_type=F32)
    h_ref[...] = h
    ms = jnp.mean(h * h, axis=-1, keepdims=True)
    hn_ref[...] = (h * lax.rsqrt(ms + EPS) * gf_ref[...]).astype(BF16)
    lo = SUBLANES - (CONV_W - 1)

    def up_project(j):
        ubuf = ubuf_ref.at[j % 2]
        hn = hn_ref[...]
        ubuf[0:SUBLANES, :] = carry_ref[j]
        gate_cols = slice(j * FF_CHUNK, (j + 1) * FF_CHUNK)
        val_cols = slice(D_FF + j * FF_CHUNK, D_FF + (j + 1) * FF_CHUNK)
        ubuf[SUBLANES:, 0:FF_CHUNK] = jnp.dot(hn, wu_ref[:, gate_cols], preferred_element_type=F32)
        ubuf[SUBLANES:, FF_CHUNK:] = jnp.dot(hn, wu_ref[:, val_cols], preferred_element_type=F32)
        carry_ref[j] = ubuf[tm:tm + SUBLANES, :]

    def gated_conv(j):
        ubuf = ubuf_ref.at[j % 2]
        cw = cw_ref[j]
        cb = cb_ref[j]
        strip = min(tm, FF_STRIP)
        for r in range(0, tm, strip):
            ext = ubuf[r:r + strip + SUBLANES, :]
            c = cb
            for tap in range(CONV_W):
                back = CONV_W - 1 - tap
                rows = ext if back == 0 else pltpu.roll(ext, back, 0)
                c = c + cw[tap:tap + 1, :] * rows[SUBLANES:, :]
            gate = c[:, :FF_CHUNK]
            act_ref[r:r + strip, j * FF_CHUNK:(j + 1) * FF_CHUNK] = (
                (gate * jax.nn.sigmoid(gate)) * c[:, FF_CHUNK:]).astype(BF16)

    up_project(0)
    for j in range(N_FF_CHUNKS):
        if j + 1 < N_FF_CHUNKS:
            up_project(j + 1)
        gated_conv(j)

    hh = h_ref[...] + jnp.dot(act_ref[...], wd_ref[...], preferred_element_type=F32)
    ms2 = jnp.mean(hh * hh, axis=-1, keepdims=True)
    y_ref[0] = hh * lax.rsqrt(ms2 + EPS) * gl_ref[...]

    @pl.when(i == pl.num_programs(1) - 1)
    def _():
        conv_ref[0] = carry_ref[...]


def _merge_ffn(x, sb_n, ret_n, conv0, w_out_bf, g_norm_ffn, w_up_bf, cw, cb, wd, g_norm_final, tm):
    b, l, d = x.shape
    bs = conv0.shape[0]
    row_spec = lambda w: pl.BlockSpec((1, tm, w), lambda bi, i: (bi, i, 0))
    conv_shape = (1, N_FF_CHUNKS, SUBLANES, 2 * FF_CHUNK)
    conv_spec = pl.BlockSpec(conv_shape, lambda bi, i: (bi, 0, 0, 0))
    conv0_spec = conv_spec if bs > 1 else pl.BlockSpec(conv_shape, lambda bi, i: (0, 0, 0, 0))
    kernel = functools.partial(_merge_ffn_kernel, tm=tm)
    return pl.pallas_call(
        kernel,
        grid=(b, l // tm),
        in_specs=[row_spec(d), row_spec(SB_WIDTH), row_spec(RET_WIDTH), conv0_spec,
                  _const_spec(w_out_bf.shape), _const_spec((1, d)),
                  _const_spec(w_up_bf.shape),
                  _const_spec(cw.shape), _const_spec(cb.shape), _const_spec(wd.shape),
                  _const_spec((1, d))],
        out_specs=[row_spec(d), conv_spec],
        out_shape=[jax.ShapeDtypeStruct((b, l, d), F32),
                   jax.ShapeDtypeStruct((b,) + conv_shape[1:], F32)],
        scratch_shapes=[pltpu.VMEM((tm, d), F32), pltpu.VMEM((tm, d), BF16),
                        pltpu.VMEM((2, tm + SUBLANES, 2 * FF_CHUNK), F32),
                        pltpu.VMEM((tm, D_FF), BF16),
                        pltpu.VMEM(conv_shape[1:], F32)],
        compiler_params=pltpu.CompilerParams(
            dimension_semantics=("arbitrary", "arbitrary"), vmem_limit_bytes=VMEM_LIMIT),
        name="merge_ffn",
    )(x, sb_n, ret_n, conv0, w_out_bf, g_norm_ffn.reshape(1, d), w_up_bf, cw, cb, wd,
      g_norm_final.reshape(1, d))


def _rope_tables(pos):
    half = RET_DK // 2
    inv = ROPE_BASE ** (-jnp.arange(half, dtype=F32) / half)
    ang = pos.astype(F32)[:, None] * inv[None, :]
    cos, sin = jnp.cos(ang), jnp.sin(ang)
    return jnp.concatenate([cos, cos], axis=1), jnp.concatenate([-sin, sin], axis=1)


def _conv_state_to_chunks(state):
    b = state.shape[0]
    s = state.reshape(b, CONV_W - 1, 2, N_FF_CHUNKS, FF_CHUNK).transpose(0, 3, 1, 2, 4)
    s = s.reshape(b, N_FF_CHUNKS, CONV_W - 1, 2 * FF_CHUNK)
    return jnp.pad(s, ((0, 0), (0, 0), (SUBLANES - (CONV_W - 1), 0), (0, 0)))


def _conv_state_from_chunks(chunks):
    b = chunks.shape[0]
    s = chunks[:, :, SUBLANES - (CONV_W - 1):, :].reshape(b, N_FF_CHUNKS, CONV_W - 1, 2, FF_CHUNK)
    return s.transpose(0, 2, 3, 1, 4).reshape(b, CONV_W - 1, 2 * D_FF)


def _pad_rows(a, rows):
    return a if a.shape[1] == rows else jnp.pad(a, ((0, 0), (0, rows - a.shape[1]), (0, 0)))


def _prefix_rows_kernel(k_any, v_any, k_rows_ref, v_rows_ref, k_ref, v_ref):
    del k_any, v_any
    k_ref[...] = k_rows_ref[...]
    v_ref[...] = v_rows_ref[...]


def _write_prefix_rows(k_big, v_big, k_rows, v_rows):
    b = k_big.shape[0]
    blk = (1,) + k_rows.shape[1:]
    any_spec = pl.BlockSpec(memory_space=pl.ANY)
    rows_spec = pl.BlockSpec(blk, lambda bi: (0, 0, 0, 0))
    out_spec = pl.BlockSpec(blk, lambda bi: (bi, 0, 0, 0))
    big = jax.ShapeDtypeStruct(k_big.shape, k_big.dtype)
    return pl.pallas_call(
        _prefix_rows_kernel,
        grid=(b,),
        in_specs=[any_spec, any_spec, rows_spec, rows_spec],
        out_specs=[out_spec, out_spec],
        out_shape=[big, big],
        input_output_aliases={0: 0, 1: 1},
        name="prefix_rows",
    )(k_big, v_big, k_rows, v_rows)


def _stream_step(x, pos0, attend, s0, conv0, wts, tm, chunk, head_row0=0):
    l = x.shape[1]
    cos2, sin2 = _rope_tables(pos0 + jnp.arange(l))
    q, k, v, k_out, v_out, rq, rk, rv, rgate = _project(x, wts["g_norm_mix"], wts["w_in"], cos2, sin2,
                                                       tm, head_row0)
    lq = -(-l // SB_KEYS) * SB_KEYS
    sb_n = attend(_pad_rows(q, lq), _pad_rows(k, lq), _pad_rows(v, lq))[:, :l]
    ret_n, s_new = _retention(rq, rk, rv, rgate, s0, wts["g_ret_out"], chunk)
    y, conv_new = _merge_ffn(x, sb_n, ret_n, conv0, wts["w_out"], wts["g_norm_ffn"], wts["w_up"],
                             wts["cw"], wts["cb"], wts["wd"], wts["g_norm_final"], tm)
    return y, k, v, k_out, v_out, s_new, conv_new


def kernel(x_prompt, x_sample, cache_sb_k, cache_sb_v, state_ret, state_conv, meta_tokens, g_norm_mix, w_in, g_sb_out, g_ret_out, w_out, g_norm_ffn, w_up, conv_w, conv_b, w_down, g_norm_final):
    b, seq, d = x_prompt.shape
    bd, ls, _ = x_sample.shape
    past = cache_sb_k.shape[2]

    def ff_cols(a):
        r = a.shape[0]
        return a.reshape(r, 2, N_FF_CHUNKS, FF_CHUNK).transpose(2, 0, 1, 3).reshape(N_FF_CHUNKS, r, 2 * FF_CHUNK)

    wts = dict(
        g_norm_mix=g_norm_mix, g_sb_out=g_sb_out, g_ret_out=g_ret_out, g_norm_ffn=g_norm_ffn,
        g_norm_final=g_norm_final,
        w_in=w_in.astype(BF16), w_out=w_out.astype(BF16),
        w_up=w_up.astype(BF16),
        cw=ff_cols(conv_w), cb=ff_cols(conv_b.reshape(1, 2 * D_FF)),
        wd=w_down.astype(BF16),
    )

    zero_state = jnp.zeros((1, RET_HEADS, RET_DK, RET_DV), F32)
    zero_conv = jnp.zeros((1, N_FF_CHUNKS, SUBLANES, 2 * FF_CHUNK), F32)
    gain = wts["g_sb_out"]
    _, k_m, v_m, k_m_out, v_m_out, s_meta, conv_meta = _stream_step(
        meta_tokens[None], -N_META, lambda q, k, v: _attention(q, k, v, None, None, 0, gain)[0],
        zero_state, zero_conv, wts, N_META, N_META)

    k_m, v_m = _pad_rows(k_m, SB_KEYS), _pad_rows(v_m, SB_KEYS)
    y_prompt, _, _, k_p_out, v_p_out, s_prompt, conv_prompt = _stream_step(
        x_prompt, 0, lambda q, k, v: _attention(q, k, v, k_m, v_m, N_META, gain)[0],
        s_meta, conv_meta, wts, 512, 256, head_row0=N_META)
    new_k_prompt, new_v_prompt = _write_prefix_rows(k_p_out, v_p_out, k_m_out, v_m_out)

    y_sample, _, _, k_s_out, v_s_out, s_sample, conv_sample = _stream_step(
        x_sample, past, lambda q, k, v: _attention_over_cache(q, k, v, cache_sb_k, cache_sb_v, gain),
        state_ret, _conv_state_to_chunks(state_conv), wts, ls, ls)

    return (y_prompt, y_sample, new_k_prompt, new_v_prompt, s_prompt,
            _conv_state_from_chunks(conv_prompt), k_s_out, v_s_out, s_sample,
            _conv_state_from_chunks(conv_sample))
```

```python
import functools
import math

import jax
import jax.numpy as jnp
from jax import lax
from jax.experimental import pallas as pl
from jax.experimental.pallas import tpu as pltpu

D_MODEL = 1024
N_META = 16
SB_HEADS = 8
SB_HEAD_DIM = 64
SB_WIDTH = SB_HEADS * SB_HEAD_DIM
RET_HEADS = 4
RET_DK = 128
RET_DV = 128
RET_WIDTH = RET_HEADS * RET_DV
MIX_WIDTH = SB_WIDTH + RET_WIDTH
GROUP = 512
N_GROUPS = 7
IN_WIDTH = N_GROUPS * GROUP
D_FF = 2816
CONV_W = 3
ROPE_BASE = 10000.0
EPS = 1e-5

LANES = 128
SUBLANES = 8
FF_CHUNK = 256
N_FF_CHUNKS = D_FF // FF_CHUNK
FF_STRIP = 64
RET_UNROLL = 8
PROJECT_ROWS = 1024
SB_KEYS = 128
SB_QUERY_TILE = 512
SB_UNROLL = 2
PAIR = 2 * SB_HEAD_DIM
SB_PAIRS = 2
SB_RECENT = 512
SB_EXHAUSTED = 152.0
SB_Q_SCALE = SB_HEAD_DIM ** -0.5 * math.log2(math.e)
VMEM_LIMIT = 56 * 1024 * 1024

BF16 = jnp.bfloat16
F32 = jnp.float32


def _const_spec(shape):
    zeros = (0,) * len(shape)
    return pl.BlockSpec(shape, lambda *_: zeros, pipeline_mode=pl.Buffered(1))


def _project_kernel(x_ref, g_ref, w_ref, cos_ref, sin_ref,
                    q_ref, k_ref, v_ref, ko_ref, vo_ref, rq_ref, rk_ref, rv_ref, rg_ref):
    x = x_ref[0]
    ms = jnp.mean(x * x, axis=-1, keepdims=True)
    h = (x * lax.rsqrt(ms + EPS) * g_ref[...]).astype(BF16)

    def group(i):
        return jnp.dot(h, w_ref[:, i * GROUP:(i + 1) * GROUP], preferred_element_type=F32)

    def split_heads(p, out_ref):
        for hh in range(SB_HEADS):
            out_ref[0, hh] = p[:, hh * SB_HEAD_DIM:(hh + 1) * SB_HEAD_DIM]

    def rope(p, out_ref, scale):
        cos = cos_ref[...]
        sin = sin_ref[...]
        for hh in range(RET_HEADS):
            t = p[:, hh * RET_DK:(hh + 1) * RET_DK]
            r = t * cos + pltpu.roll(t, RET_DK // 2, 1) * sin
            if scale is not None:
                r = r * scale
            out_ref[0, :, hh * RET_DK:(hh + 1) * RET_DK] = r.astype(BF16)

    q_ref[0] = (group(0) * SB_Q_SCALE).astype(BF16)
    pk = group(1)
    k_ref[0] = pk.astype(BF16)
    split_heads(pk, ko_ref)
    pv = group(2)
    v_ref[0] = pv.astype(BF16)
    split_heads(pv, vo_ref)
    rope(group(3), rq_ref, None)
    rope(group(4), rk_ref, RET_DK ** -0.5)
    rv_ref[0] = group(5).astype(BF16)
    rg_ref[0] = group(6).astype(BF16)


def _project(x, g_norm, w_in_bf, cos2, sin2, tm, head_row0=0):
    b, l, d = x.shape
    grid = (b, l // tm)
    row_spec = lambda w: pl.BlockSpec((1, tm, w), lambda bi, i: (bi, i, 0))
    head_blk = (1, SB_HEADS, tm, SB_HEAD_DIM)
    head_spec = pl.BlockSpec(tuple(pl.Element(n) for n in head_blk),
                             lambda bi, i: (bi, 0, pl.multiple_of(head_row0 + i * tm, SUBLANES), 0))
    tab_spec = pl.BlockSpec((tm, RET_DK), lambda bi, i: (i, 0))
    act = jax.ShapeDtypeStruct((b, l, GROUP), BF16)
    heads = jax.ShapeDtypeStruct((b, SB_HEADS, head_row0 + l, SB_HEAD_DIM), F32)
    return pl.pallas_call(
        _project_kernel,
        grid=grid,
        in_specs=[row_spec(d), _const_spec((1, d)), _const_spec((d, IN_WIDTH)), tab_spec, tab_spec],
        out_specs=[row_spec(GROUP)] * 3 + [head_spec] * 2 + [row_spec(GROUP)] * 4,
        out_shape=[act] * 3 + [heads] * 2 + [act] * 4,
        compiler_params=pltpu.CompilerParams(
            dimension_semantics=("arbitrary", "arbitrary"), vmem_limit_bytes=VMEM_LIMIT),
        name="project",
    )(x, g_norm.reshape(1, d), w_in_bf, cos2, sin2)


def _split_bf16(x):
    hi = x.astype(BF16)
    lo = (x - hi.astype(F32)).astype(BF16)
    return hi, lo


def _attention_kernel(q_ref, kc_ref, vc_ref, kp_ref, vp_ref, bdu_ref, bdo_ref, j_ref, g_ref,
                      o_ref, left_ref, kxc_ref, vxc_ref, kxp_ref, vxp_ref, carry_ref, acc_ref,
                      *, tq, n_q, n_past, past_valid, report_left):
    tk = SB_KEYS
    per_tile = tq // tk
    head0 = lax.broadcasted_iota(jnp.int32, (tk, LANES), 1) < SB_HEAD_DIM
    key_in_block = lax.broadcasted_iota(jnp.int32, (tq, 2 * tk), 1) & (tk - 1)
    delta = key_in_block - lax.broadcasted_iota(jnp.int32, (tq, 2 * tk), 0)

    pairs = range(SB_PAIRS)
    lanes = [slice(p * PAIR, (p + 1) * PAIR) for p in pairs]

    def expand(src_ref, dst_ref, n_blocks):
        def body(j, _):
            for p in pairs:
                blk = src_ref[0, pl.ds(pl.multiple_of(j * tk, tk), tk), lanes[p]]
                zero = jnp.zeros_like(blk)
                dst_ref[p, j, 0:tk, :] = jnp.where(head0, blk, zero)
                dst_ref[p, j, tk:2 * tk, :] = jnp.where(head0, zero, blk)
            return 0
        lax.fori_loop(0, n_blocks, body, 0)

    expand(kc_ref, kxc_ref, n_q * per_tile)
    expand(vc_ref, vxc_ref, n_q * per_tile)
    if n_past > 0:
        expand(kp_ref, kxp_ref, n_past)
        expand(vp_ref, vxp_ref, n_past)

    def add_blocks(kx_ref, vx_ref, i, j_last, masks, r0=0, r1=tq):
        rows = pl.ds(pl.multiple_of(i * tq + r0, tk), r1 - r0)
        carry = [carry_ref[p, r0:r1, :] for p in pairs]
        for u, mask in enumerate(masks):
            j = j_last - u
            for p in pairs:
                z = lax.dot_general(q_ref[0, rows, lanes[p]], kx_ref[p, j], (((1,), (1,)), ((), ())),
                                    preferred_element_type=F32)
                sp = jnp.maximum(z, 0.0) + jnp.log2(1.0 + jnp.exp2(-jnp.abs(z)))
                spm = sp if mask is None else jnp.where(mask, sp, 0.0)
                hi = spm.astype(BF16)
                later = jnp.dot(hi, bdu_ref[...], preferred_element_type=F32)
                total = jnp.dot(hi, bdo_ref[...], preferred_element_type=F32)
                a = jnp.exp2((z - sp) - (later + carry[p]))
                if mask is not None:
                    a = jnp.where(mask, a, 0.0)
                acc_ref[p, r0:r1, :] += jnp.dot(a.astype(BF16), vx_ref[p, j], preferred_element_type=F32)
                carry[p] = carry[p] + total
        for p in pairs:
            carry_ref[p, r0:r1, :] = carry[p]
        return carry

    def stick_left(carry):
        least = functools.reduce(jnp.minimum, carry)
        return (jnp.min(least) < SB_EXHAUSTED).astype(jnp.int32)

    def sweep(kx_ref, vx_ref, i, j_top, n_steps, unroll, alive):
        def cond(state):
            m, live = state
            return jnp.logical_and(m < n_steps, live > 0)

        def body(state):
            m, _ = state
            carry = add_blocks(kx_ref, vx_ref, i, j_top - m * unroll, [None] * unroll)
            return m + 1, stick_left(carry)

        return lax.while_loop(cond, body, (jnp.int32(0), alive))[1]

    def all_carries(r0=0):
        return [carry_ref[p, r0:, :] for p in pairs]

    def q_tile(i, any_left, first):
        rows = pl.ds(pl.multiple_of(i * tq, tq), tq)
        carry_ref[...] = jnp.zeros_like(carry_ref)
        acc_ref[...] = jnp.zeros_like(acc_ref)

        for jj in range(per_tile - 1, -1, -1):
            r0 = jj * tk
            add_blocks(kxc_ref, vxc_ref, i, i * per_tile + jj, [delta[r0:, :] < -r0], r0)

        unroll = SB_UNROLL if per_tile % SB_UNROLL == 0 else 1
        top = unroll * tk
        if first:
            alive = stick_left(all_carries())
        elif top < tq:
            j_prev = i * per_tile - 1
            add_blocks(kxc_ref, vxc_ref, i, j_prev, [None] * unroll, 0, top)

            @pl.when(stick_left(all_carries(top)) > 0)
            def _():
                add_blocks(kxc_ref, vxc_ref, i, j_prev, [None] * unroll, top, tq)

            alive = sweep(kxc_ref, vxc_ref, i, j_prev - unroll, (i * per_tile) // unroll - 1, unroll,
                          stick_left(all_carries()))
        else:
            alive = sweep(kxc_ref, vxc_ref, i, i * per_tile - 1, (i * per_tile) // unroll, unroll,
                          stick_left(all_carries()))

        if n_past > 0:
            n_full = n_past
            if past_valid < tk:
                @pl.when(alive > 0)
                def _():
                    add_blocks(kxp_ref, vxp_ref, i, n_past - 1, [key_in_block < past_valid])

                n_full = n_past - 1
                if n_full > 0:
                    alive = stick_left(all_carries())
            if n_full // SB_UNROLL > 0:
                alive = sweep(kxp_ref, vxp_ref, i, n_full - 1, n_full // SB_UNROLL, SB_UNROLL, alive)
            if n_full % SB_UNROLL:
                alive = sweep(kxp_ref, vxp_ref, i, n_full % SB_UNROLL - 1, n_full % SB_UNROLL, 1, alive)

        for p in pairs:
            o = acc_ref[p]
            hi, lo = _split_bf16(o * o)
            jm = j_ref[...]
            ms = (jnp.dot(hi, jm, preferred_element_type=F32)
                  + jnp.dot(lo, jm, preferred_element_type=F32)) * (1.0 / SB_HEAD_DIM)
            y = o * lax.rsqrt(ms + EPS) * g_ref[:, lanes[p]]
            o_ref[0, rows, lanes[p]] = y.astype(BF16)
        if not report_left:
            return any_left
        return jnp.maximum(any_left, stick_left(all_carries()))

    any_left = q_tile(jnp.int32(0), jnp.int32(0), True)
    any_left = lax.fori_loop(1, n_q, lambda i, left: q_tile(i, left, False), any_left)
    left_ref[...] = jnp.full(left_ref.shape, any_left, jnp.int32)


def _attention(q, k_cur, v_cur, k_past, v_past, past_len, g_sb_out, report_left=False):
    b, lq, _ = q.shape
    tk = SB_KEYS
    tq = min(SB_QUERY_TILE, lq)
    assert lq % tq == 0 and tq % tk == 0
    if k_past is None:
        k_past = jnp.zeros((1, tk, SB_WIDTH), BF16)
        v_past = jnp.zeros((1, tk, SB_WIDTH), BF16)
        n_past, past_valid = 0, tk
    else:
        assert k_past.shape[1] % tk == 0
        n_past = -(-past_len // tk)
        past_valid = past_len - (n_past - 1) * tk
        k_past = k_past[:, :n_past * tk]
        v_past = v_past[:, :n_past * tk]
    bp, p, _ = k_past.shape
    jj = jnp.arange(2 * tk)
    same_head = (jj[:, None] // tk) == (jj[None, :] // tk)
    bdu = (same_head & (jj[:, None] > jj[None, :])).astype(BF16)
    bdo = same_head.astype(BF16)
    ll = jnp.arange(LANES) // SB_HEAD_DIM
    jm = (ll[:, None] == ll[None, :]).astype(BF16)
    width = SB_PAIRS * PAIR
    cur_spec = pl.BlockSpec((1, lq, width), lambda bi, hp: (bi, 0, hp))
    past_spec = pl.BlockSpec((1, p, width), (lambda bi, hp: (bi, 0, hp)) if bp > 1
                             else (lambda bi, hp: (0, 0, hp)))
    kernel = functools.partial(_attention_kernel, tq=tq, n_q=lq // tq, n_past=n_past,
                               past_valid=past_valid, report_left=report_left)
    stacked = lambda n: pltpu.VMEM((SB_PAIRS, n, 2 * tk, PAIR), BF16)
    n_groups = SB_WIDTH // width
    out, left = pl.pallas_call(
        kernel,
        grid=(b, n_groups),
        in_specs=[cur_spec, cur_spec, cur_spec, past_spec, past_spec,
                  _const_spec(bdu.shape), _const_spec(bdo.shape), _const_spec(jm.shape),
                  pl.BlockSpec((1, width), lambda bi, hp: (0, hp))],
        out_specs=[cur_spec, pl.BlockSpec((1, 1, SUBLANES, LANES), lambda bi, hp: (bi, hp, 0, 0))],
        out_shape=[jax.ShapeDtypeStruct((b, lq, SB_WIDTH), BF16),
                   jax.ShapeDtypeStruct((b, n_groups, SUBLANES, LANES), jnp.int32)],
        scratch_shapes=[stacked(lq // tk), stacked(lq // tk), stacked(p // tk), stacked(p // tk),
                        pltpu.VMEM((SB_PAIRS, tq, 2 * tk), F32), pltpu.VMEM((SB_PAIRS, tq, PAIR), F32)],
        compiler_params=pltpu.CompilerParams(
            dimension_semantics=("arbitrary", "arbitrary"), vmem_limit_bytes=VMEM_LIMIT),
        name="attention",
    )(q, k_cur, v_cur, k_past, v_past, bdu, bdo, jm, g_sb_out.reshape(1, SB_WIDTH))
    return out, left


def _attention_over_cache(q, k_cur, v_cur, cache_k, cache_v, g_sb_out):
    bd, _, past, _ = cache_k.shape
    to_rows = lambda c: c.transpose(0, 2, 1, 3).reshape(bd, c.shape[2], SB_WIDTH).astype(BF16)
    recent = min(past, SB_RECENT)
    out, left = _attention(q, k_cur, v_cur, to_rows(cache_k[:, :, past - recent:]),
                           to_rows(cache_v[:, :, past - recent:]), recent, g_sb_out, report_left=True)
    if recent == past:
        return out
    return lax.cond(
        jnp.any(left > 0),
        lambda: _attention(q, k_cur, v_cur, to_rows(cache_k), to_rows(cache_v), past, g_sb_out)[0],
        lambda: out)


def _retention_kernel(q_ref, k_ref, v_ref, gate_ref, s0_ref, g_ref, o_ref, s_ref, *, chunk, n_chunks):
    c = chunk
    head = pl.program_id(1).astype(F32)
    log_g = jnp.log(1.0 - jnp.exp2(jnp.full((1, LANES), -5.0, F32) - head))
    n_row = lax.broadcasted_iota(jnp.int32, (c, LANES), 0).astype(F32)
    q_decay = jnp.exp((n_row + 1.0) * log_g)
    k_decay = jnp.exp((c - 1.0 - n_row) * log_g)
    s_decay = jnp.exp(float(c) * log_g)
    diff = (lax.broadcasted_iota(jnp.int32, (c, c), 0)
            - lax.broadcasted_iota(jnp.int32, (c, c), 1))
    log_g_cc = log_g if c == LANES else jnp.log(1.0 - jnp.exp2(jnp.full((1, c), -5.0, F32) - head))
    decay = jnp.where(diff >= 0, jnp.exp(jnp.maximum(diff, 0).astype(F32) * log_g_cc), 0.0)
    gain = g_ref[...]

    def body(ci, s):
        rows = pl.ds(pl.multiple_of(ci * c, c), c)
        q = q_ref[0, rows, :]
        k = k_ref[0, rows, :]
        v = v_ref[0, rows, :]
        qk = lax.dot_general(q, k, (((1,), (1,)), ((), ())), preferred_element_type=F32)
        inner = jnp.dot((qk * decay).astype(BF16), v, preferred_element_type=F32)
        cross = jnp.dot(q, s.astype(BF16), preferred_element_type=F32) * q_decay
        o = inner + cross
        k_dec = (k.astype(F32) * k_decay).astype(BF16)
        s_new = s_decay * s + lax.dot_general(k_dec, v, (((0,), (0,)), ((), ())),
                                              preferred_element_type=F32)
        y = o * lax.rsqrt(jnp.mean(o * o, axis=-1, keepdims=True) + EPS) * gain
        half = 0.5 * gate_ref[0, rows, :].astype(F32)
        o_ref[0, rows, :] = (y * (half * (1.0 + jnp.tanh(half)))).astype(BF16)
        return s_new

    s_ref[0, 0] = lax.fori_loop(0, n_chunks, body, s0_ref[0, 0], unroll=math.gcd(n_chunks, RET_UNROLL))


def _retention(rq, rk, rv, rgate, s0, g_ret_out, chunk):
    b, l, _ = rq.shape
    assert l % chunk == 0
    bs = s0.shape[0]
    seq_spec = pl.BlockSpec((1, l, RET_DK), lambda bi, hh: (bi, 0, hh))
    state_spec = pl.BlockSpec((1, 1, RET_DK, RET_DV), lambda bi, hh: (bi, hh, 0, 0))
    s0_spec = state_spec if bs > 1 else pl.BlockSpec((1, 1, RET_DK, RET_DV), lambda bi, hh: (0, hh, 0, 0))
    kernel = functools.partial(_retention_kernel, chunk=chunk, n_chunks=l // chunk)
    return pl.pallas_call(
        kernel,
        grid=(b, RET_HEADS),
        in_specs=[seq_spec, seq_spec, seq_spec, seq_spec, s0_spec,
                  pl.BlockSpec((1, RET_DV), lambda bi, hh: (0, hh))],
        out_specs=[seq_spec, state_spec],
        out_shape=[jax.ShapeDtypeStruct((b, l, RET_WIDTH), BF16),
                   jax.ShapeDtypeStruct((b, RET_HEADS, RET_DK, RET_DV), F32)],
        compiler_params=pltpu.CompilerParams(
            dimension_semantics=("arbitrary", "arbitrary"), vmem_limit_bytes=VMEM_LIMIT),
        name="retention",
    )(rq, rk, rv, rgate, s0, g_ret_out.reshape(1, RET_WIDTH))


def _merge_ffn_kernel(x_ref, sb_ref, ret_ref, conv0_ref, wo_ref, gf_ref, wu_ref,
                      cw_ref, cb_ref, wd_ref, gl_ref, y_ref, conv_ref,
                      h_ref, hn_ref, ubuf_ref, act_ref, carry_ref, *, tm):
    i = pl.program_id(1)

    @pl.when(i == 0)
    def _():
        carry_ref[...] = conv0_ref[0]

    mixed = jnp.concatenate([sb_ref[0], ret_ref[0]], axis=1)
    h = x_ref[0] + jnp.dot(mixed, wo_ref[...], preferred_element_type=F32)
    h_ref[...] = h
    ms = jnp.mean(h * h, axis=-1, keepdims=True)
    hn_ref[...] = (h * lax.rsqrt(ms + EPS) * gf_ref[...]).astype(BF16)
    lo = SUBLANES - (CONV_W - 1)

    def up_project(j):
        ubuf = ubuf_ref.at[j % 2]
        hn = hn_ref[...]
        ubuf[0:SUBLANES, :] = carry_ref[j]
        gate_cols = slice(j * FF_CHUNK, (j + 1) * FF_CHUNK)
        val_cols = slice(D_FF + j * FF_CHUNK, D_FF + (j + 1) * FF_CHUNK)
        ubuf[SUBLANES:, 0:FF_CHUNK] = jnp.dot(hn, wu_ref[:, gate_cols], preferred_element_type=F32)
        ubuf[SUBLANES:, FF_CHUNK:] = jnp.dot(hn, wu_ref[:, val_cols], preferred_element_type=F32)
        carry_ref[j] = ubuf[tm:tm + SUBLANES, :]

    def gated_conv(j):
        ubuf = ubuf_ref.at[j % 2]
        cw = cw_ref[j]
        cb = cb_ref[j]
        strip = min(tm, FF_STRIP)
        for r in range(0, tm, strip):
            ext = ubuf[r:r + strip + SUBLANES, :]
            c = cb
            for tap in range(CONV_W):
                back = CONV_W - 1 - tap
                rows = ext if back == 0 else pltpu.roll(ext, back, 0)
                c = c + cw[tap:tap + 1, :] * rows[SUBLANES:, :]
            half = 0.5 * c[:, :FF_CHUNK]
            act_ref[r:r + strip, j * FF_CHUNK:(j + 1) * FF_CHUNK] = (
                (half * (1.0 + jnp.tanh(half))) * c[:, FF_CHUNK:]).astype(BF16)

    up_project(0)
    for j in range(N_FF_CHUNKS):
        if j + 1 < N_FF_CHUNKS:
            up_project(j + 1)
        gated_conv(j)

    hh = h_ref[...] + jnp.dot(act_ref[...], wd_ref[...], preferred_element_type=F32)
    ms2 = jnp.mean(hh * hh, axis=-1, keepdims=True)
    y_ref[0] = hh * lax.rsqrt(ms2 + EPS) * gl_ref[...]

    @pl.when(i == pl.num_programs(1) - 1)
    def _():
        conv_ref[0] = carry_ref[...]


def _merge_ffn(x, sb_n, ret_n, conv0, w_out_bf, g_norm_ffn, w_up_bf, cw, cb, wd, g_norm_final, tm):
    b, l, d = x.shape
    bs = conv0.shape[0]
    row_spec = lambda w: pl.BlockSpec((1, tm, w), lambda bi, i: (bi, i, 0))
    conv_shape = (1, N_FF_CHUNKS, SUBLANES, 2 * FF_CHUNK)
    conv_spec = pl.BlockSpec(conv_shape, lambda bi, i: (bi, 0, 0, 0))
    conv0_spec = conv_spec if bs > 1 else pl.BlockSpec(conv_shape, lambda bi, i: (0, 0, 0, 0))
    kernel = functools.partial(_merge_ffn_kernel, tm=tm)
    return pl.pallas_call(
        kernel,
        grid=(b, l // tm),
        in_specs=[row_spec(d), row_spec(SB_WIDTH), row_spec(RET_WIDTH), conv0_spec,
                  _const_spec(w_out_bf.shape), _const_spec((1, d)),
                  _const_spec(w_up_bf.shape),
                  _const_spec(cw.shape), _const_spec(cb.shape), _const_spec(wd.shape),
                  _const_spec((1, d))],
        out_specs=[row_spec(d), conv_spec],
        out_shape=[jax.ShapeDtypeStruct((b, l, d), F32),
                   jax.ShapeDtypeStruct((b,) + conv_shape[1:], F32)],
        scratch_shapes=[pltpu.VMEM((tm, d), F32), pltpu.VMEM((tm, d), BF16),
                        pltpu.VMEM((2, tm + SUBLANES, 2 * FF_CHUNK), F32),
                        pltpu.VMEM((tm, D_FF), BF16),
                        pltpu.VMEM(conv_shape[1:], F32)],
        compiler_params=pltpu.CompilerParams(
            dimension_semantics=("arbitrary", "arbitrary"), vmem_limit_bytes=VMEM_LIMIT),
        name="merge_ffn",
    )(x, sb_n, ret_n, conv0, w_out_bf, g_norm_ffn.reshape(1, d), w_up_bf, cw, cb, wd,
      g_norm_final.reshape(1, d))


def _rope_tables(pos):
    half = RET_DK // 2
    inv = ROPE_BASE ** (-jnp.arange(half, dtype=F32) / half)
    ang = pos.astype(F32)[:, None] * inv[None, :]
    cos, sin = jnp.cos(ang), jnp.sin(ang)
    return jnp.concatenate([cos, cos], axis=1), jnp.concatenate([-sin, sin], axis=1)


def _conv_state_to_chunks(state):
    b = state.shape[0]
    s = state.reshape(b, CONV_W - 1, 2, N_FF_CHUNKS, FF_CHUNK).transpose(0, 3, 1, 2, 4)
    s = s.reshape(b, N_FF_CHUNKS, CONV_W - 1, 2 * FF_CHUNK)
    return jnp.pad(s, ((0, 0), (0, 0), (SUBLANES - (CONV_W - 1), 0), (0, 0)))


def _conv_state_from_chunks(chunks):
    b = chunks.shape[0]
    s = chunks[:, :, SUBLANES - (CONV_W - 1):, :].reshape(b, N_FF_CHUNKS, CONV_W - 1, 2, FF_CHUNK)
    return s.transpose(0, 2, 3, 1, 4).reshape(b, CONV_W - 1, 2 * D_FF)


def _pad_rows(a, rows):
    return a if a.shape[1] == rows else jnp.pad(a, ((0, 0), (0, rows - a.shape[1]), (0, 0)))


def _prefix_rows_kernel(k_any, v_any, k_rows_ref, v_rows_ref, k_ref, v_ref):
    del k_any, v_any
    k_ref[...] = k_rows_ref[...]
    v_ref[...] = v_rows_ref[...]


def _write_prefix_rows(k_big, v_big, k_rows, v_rows):
    b = k_big.shape[0]
    blk = (1,) + k_rows.shape[1:]
    any_spec = pl.BlockSpec(memory_space=pl.ANY)
    rows_spec = pl.BlockSpec(blk, lambda bi: (0, 0, 0, 0))
    out_spec = pl.BlockSpec(blk, lambda bi: (bi, 0, 0, 0))
    big = jax.ShapeDtypeStruct(k_big.shape, k_big.dtype)
    return pl.pallas_call(
        _prefix_rows_kernel,
        grid=(b,),
        in_specs=[any_spec, any_spec, rows_spec, rows_spec],
        out_specs=[out_spec, out_spec],
        out_shape=[big, big],
        input_output_aliases={0: 0, 1: 1},
        name="prefix_rows",
    )(k_big, v_big, k_rows, v_rows)


def _stream_step(x, pos0, attend, s0, conv0, wts, tm, chunk, head_row0=0):
    l = x.shape[1]
    cos2, sin2 = _rope_tables(pos0 + jnp.arange(l))
    tm_project = PROJECT_ROWS if l % PROJECT_ROWS == 0 else tm
    q, k, v, k_out, v_out, rq, rk, rv, rgate = _project(x, wts["g_norm_mix"], wts["w_in"], cos2, sin2,
                                                       tm_project, head_row0)
    lq = -(-l // SB_KEYS) * SB_KEYS
    sb_n = attend(_pad_rows(q, lq), _pad_rows(k, lq), _pad_rows(v, lq))[:, :l]
    ret_n, s_new = _retention(rq, rk, rv, rgate, s0, wts["g_ret_out"], chunk)
    y, conv_new = _merge_ffn(x, sb_n, ret_n, conv0, wts["w_out"], wts["g_norm_ffn"], wts["w_up"],
                             wts["cw"], wts["cb"], wts["wd"], wts["g_norm_final"], tm)
    return y, k, v, k_out, v_out, s_new, conv_new


def kernel(x_prompt, x_sample, cache_sb_k, cache_sb_v, state_ret, state_conv, meta_tokens, g_norm_mix, w_in, g_sb_out, g_ret_out, w_out, g_norm_ffn, w_up, conv_w, conv_b, w_down, g_norm_final):
    b, seq, d = x_prompt.shape
    bd, ls, _ = x_sample.shape
    past = cache_sb_k.shape[2]

    def ff_cols(a):
        r = a.shape[0]
        return a.reshape(r, 2, N_FF_CHUNKS, FF_CHUNK).transpose(2, 0, 1, 3).reshape(N_FF_CHUNKS, r, 2 * FF_CHUNK)

    wts = dict(
        g_norm_mix=g_norm_mix, g_sb_out=g_sb_out, g_ret_out=g_ret_out, g_norm_ffn=g_norm_ffn,
        g_norm_final=g_norm_final,
        w_in=w_in.astype(BF16), w_out=w_out.astype(BF16),
        w_up=w_up.astype(BF16),
        cw=ff_cols(conv_w), cb=ff_cols(conv_b.reshape(1, 2 * D_FF)),
        wd=w_down.astype(BF16),
    )

    zero_state = jnp.zeros((1, RET_HEADS, RET_DK, RET_DV), F32)
    zero_conv = jnp.zeros((1, N_FF_CHUNKS, SUBLANES, 2 * FF_CHUNK), F32)
    gain = wts["g_sb_out"]
    _, k_m, v_m, k_m_out, v_m_out, s_meta, conv_meta = _stream_step(
        meta_tokens[None], -N_META, lambda q, k, v: _attention(q, k, v, None, None, 0, gain)[0],
        zero_state, zero_conv, wts, N_META, N_META)

    k_m, v_m = _pad_rows(k_m, SB_KEYS), _pad_rows(v_m, SB_KEYS)
    y_prompt, _, _, k_p_out, v_p_out, s_prompt, conv_prompt = _stream_step(
        x_prompt, 0, lambda q, k, v: _attention(q, k, v, k_m, v_m, N_META, gain)[0],
        s_meta, conv_meta, wts, 512, 256, head_row0=N_META)
    new_k_prompt, new_v_prompt = _write_prefix_rows(k_p_out, v_p_out, k_m_out, v_m_out)

    y_sample, _, _, k_s_out, v_s_out, s_sample, conv_sample = _stream_step(
        x_sample, past, lambda q, k, v: _attention_over_cache(q, k, v, cache_sb_k, cache_sb_v, gain),
        state_ret, _conv_state_to_chunks(state_conv), wts, ls, ls)

    return (y_prompt, y_sample, new_k_prompt, new_v_prompt, s_prompt,
            _conv_state_from_chunks(conv_prompt), k_s_out, v_s_out, s_sample,
            _conv_state_from_chunks(conv_sample))
```

```python
import functools
import math

import jax
import jax.numpy as jnp
from jax import lax
from jax.experimental import pallas as pl
from jax.experimental.pallas import tpu as pltpu

D_MODEL = 1024
N_META = 16
SB_HEADS = 8
SB_HEAD_DIM = 64
SB_WIDTH = SB_HEADS * SB_HEAD_DIM
RET_HEADS = 4
RET_DK = 128
RET_DV = 128
RET_WIDTH = RET_HEADS * RET_DV
MIX_WIDTH = SB_WIDTH + RET_WIDTH
GROUP = 512
N_GROUPS = 7
IN_WIDTH = N_GROUPS * GROUP
D_FF = 2816
CONV_W = 3
ROPE_BASE = 10000.0
EPS = 1e-5

LANES = 128
SUBLANES = 8
FF_CHUNK = 256
N_FF_CHUNKS = D_FF // FF_CHUNK
FF_STRIP = 64
RET_UNROLL = 8
PROJECT_ROWS = 1024
SB_KEYS = 128
SB_QUERY_TILE = 512
SB_UNROLL = 2
PAIR = 2 * SB_HEAD_DIM
SB_PAIRS = 2
SB_RECENT = 512
SB_EXHAUSTED = 152.0
SB_Q_SCALE = SB_HEAD_DIM ** -0.5 * math.log2(math.e)
VMEM_LIMIT = 56 * 1024 * 1024

BF16 = jnp.bfloat16
F32 = jnp.float32


def _const_spec(shape):
    zeros = (0,) * len(shape)
    return pl.BlockSpec(shape, lambda *_: zeros, pipeline_mode=pl.Buffered(1))


def _project_kernel(x_ref, g_ref, w_ref, cos_ref, sin_ref,
                    q_ref, k_ref, v_ref, ko_ref, vo_ref, rq_ref, rk_ref, rv_ref, rg_ref):
    x = x_ref[0]
    ms = jnp.mean(x * x, axis=-1, keepdims=True)
    h = (x * lax.rsqrt(ms + EPS) * g_ref[...]).astype(BF16)

    def group(i):
        return jnp.dot(h, w_ref[:, i * GROUP:(i + 1) * GROUP], preferred_element_type=F32)

    def split_heads(p, out_ref):
        for hh in range(SB_HEADS):
            out_ref[0, hh] = p[:, hh * SB_HEAD_DIM:(hh + 1) * SB_HEAD_DIM]

    def rope(p, out_ref, scale):
        cos = cos_ref[...]
        sin = sin_ref[...]
        for hh in range(RET_HEADS):
            t = p[:, hh * RET_DK:(hh + 1) * RET_DK]
            r = t * cos + pltpu.roll(t, RET_DK // 2, 1) * sin
            if scale is not None:
                r = r * scale
            out_ref[0, :, hh * RET_DK:(hh + 1) * RET_DK] = r.astype(BF16)

    q_ref[0] = (group(0) * SB_Q_SCALE).astype(BF16)
    pk = group(1)
    k_ref[0] = pk.astype(BF16)
    split_heads(pk, ko_ref)
    pv = group(2)
    v_ref[0] = pv.astype(BF16)
    split_heads(pv, vo_ref)
    rope(group(3), rq_ref, None)
    rope(group(4), rk_ref, RET_DK ** -0.5)
    rv_ref[0] = group(5).astype(BF16)
    rg_ref[0] = group(6).astype(BF16)


def _project(x, g_norm, w_in_bf, cos2, sin2, tm, head_row0=0):
    b, l, d = x.shape
    grid = (b, l // tm)
    row_spec = lambda w: pl.BlockSpec((1, tm, w), lambda bi, i: (bi, i, 0))
    head_blk = (1, SB_HEADS, tm, SB_HEAD_DIM)
    head_spec = pl.BlockSpec(tuple(pl.Element(n) for n in head_blk),
                             lambda bi, i: (bi, 0, pl.multiple_of(head_row0 + i * tm, SUBLANES), 0))
    tab_spec = pl.BlockSpec((tm, RET_DK), lambda bi, i: (i, 0))
    act = jax.ShapeDtypeStruct((b, l, GROUP), BF16)
    heads = jax.ShapeDtypeStruct((b, SB_HEADS, head_row0 + l, SB_HEAD_DIM), F32)
    return pl.pallas_call(
        _project_kernel,
        grid=grid,
        in_specs=[row_spec(d), _const_spec((1, d)), _const_spec((d, IN_WIDTH)), tab_spec, tab_spec],
        out_specs=[row_spec(GROUP)] * 3 + [head_spec] * 2 + [row_spec(GROUP)] * 4,
        out_shape=[act] * 3 + [heads] * 2 + [act] * 4,
        compiler_params=pltpu.CompilerParams(
            dimension_semantics=("arbitrary", "arbitrary"), vmem_limit_bytes=VMEM_LIMIT),
        name="project",
    )(x, g_norm.reshape(1, d), w_in_bf, cos2, sin2)


def _split_bf16(x):
    hi = x.astype(BF16)
    lo = (x - hi.astype(F32)).astype(BF16)
    return hi, lo


def _attention_kernel(q_ref, kc_ref, vc_ref, kp_ref, vp_ref, bdu_ref, bdo_ref, j_ref, g_ref,
                      o_ref, left_ref, kxc_ref, vxc_ref, kxp_ref, vxp_ref, carry_ref, acc_ref,
                      *, tq, n_q, n_past, past_valid, report_left):
    tk = SB_KEYS
    per_tile = tq // tk
    head0 = lax.broadcasted_iota(jnp.int32, (tk, LANES), 1) < SB_HEAD_DIM
    key_in_block = lax.broadcasted_iota(jnp.int32, (tq, 2 * tk), 1) & (tk - 1)
    delta = key_in_block - lax.broadcasted_iota(jnp.int32, (tq, 2 * tk), 0)

    pairs = range(SB_PAIRS)
    lanes = [slice(p * PAIR, (p + 1) * PAIR) for p in pairs]

    def expand(src_ref, dst_ref, n_blocks):
        def body(j, _):
            for p in pairs:
                blk = src_ref[0, pl.ds(pl.multiple_of(j * tk, tk), tk), lanes[p]]
                zero = jnp.zeros_like(blk)
                dst_ref[p, j, 0:tk, :] = jnp.where(head0, blk, zero)
                dst_ref[p, j, tk:2 * tk, :] = jnp.where(head0, zero, blk)
            return 0
        lax.fori_loop(0, n_blocks, body, 0)

    expand(kc_ref, kxc_ref, n_q * per_tile)
    expand(vc_ref, vxc_ref, n_q * per_tile)
    if n_past > 0:
        expand(kp_ref, kxp_ref, n_past)
        expand(vp_ref, vxp_ref, n_past)

    def add_blocks(kx_ref, vx_ref, i, j_last, masks, r0=0, r1=tq):
        rows = pl.ds(pl.multiple_of(i * tq + r0, tk), r1 - r0)
        carry = [carry_ref[p, r0:r1, :] for p in pairs]
        for u, mask in enumerate(masks):
            j = j_last - u
            for p in pairs:
                z = lax.dot_general(q_ref[0, rows, lanes[p]], kx_ref[p, j], (((1,), (1,)), ((), ())),
                                    preferred_element_type=F32)
                sp = jnp.maximum(z, 0.0) + jnp.log2(1.0 + jnp.exp2(-jnp.abs(z)))
                spm = sp if mask is None else jnp.where(mask, sp, 0.0)
                hi = spm.astype(BF16)
                later = jnp.dot(hi, bdu_ref[...], preferred_element_type=F32)
                total = jnp.dot(hi, bdo_ref[...], preferred_element_type=F32)
                a = jnp.exp2((z - sp) - (later + carry[p]))
                if mask is not None:
                    a = jnp.where(mask, a, 0.0)
                acc_ref[p, r0:r1, :] += jnp.dot(a.astype(BF16), vx_ref[p, j], preferred_element_type=F32)
                carry[p] = carry[p] + total
        for p in pairs:
            carry_ref[p, r0:r1, :] = carry[p]
        return carry

    def stick_left(carry):
        least = functools.reduce(jnp.minimum, carry)
        return (jnp.min(least) < SB_EXHAUSTED).astype(jnp.int32)

    def sweep(kx_ref, vx_ref, i, j_top, n_steps, unroll, alive):
        def cond(state):
            m, live = state
            return jnp.logical_and(m < n_steps, live > 0)

        def body(state):
            m, _ = state
            carry = add_blocks(kx_ref, vx_ref, i, j_top - m * unroll, [None] * unroll)
            return m + 1, stick_left(carry)

        return lax.while_loop(cond, body, (jnp.int32(0), alive))[1]

    def all_carries(r0=0):
        return [carry_ref[p, r0:, :] for p in pairs]

    def q_tile(i, any_left, first):
        rows = pl.ds(pl.multiple_of(i * tq, tq), tq)
        carry_ref[...] = jnp.zeros_like(carry_ref)
        acc_ref[...] = jnp.zeros_like(acc_ref)

        for jj in range(per_tile - 1, -1, -1):
            r0 = jj * tk
            add_blocks(kxc_ref, vxc_ref, i, i * per_tile + jj, [delta[r0:, :] < -r0], r0)

        unroll = SB_UNROLL if per_tile % SB_UNROLL == 0 else 1
        top = unroll * tk
        if first:
            alive = stick_left(all_carries())
        elif top < tq:
            j_prev = i * per_tile - 1
            add_blocks(kxc_ref, vxc_ref, i, j_prev, [None] * unroll, 0, top)

            @pl.when(stick_left(all_carries(top)) > 0)
            def _():
                add_blocks(kxc_ref, vxc_ref, i, j_prev, [None] * unroll, top, tq)

            alive = sweep(kxc_ref, vxc_ref, i, j_prev - unroll, (i * per_tile) // unroll - 1, unroll,
                          stick_left(all_carries()))
        else:
            alive = sweep(kxc_ref, vxc_ref, i, i * per_tile - 1, (i * per_tile) // unroll, unroll,
                          stick_left(all_carries()))

        if n_past > 0:
            n_full = n_past
            if past_valid < tk:
                @pl.when(alive > 0)
                def _():
                    add_blocks(kxp_ref, vxp_ref, i, n_past - 1, [key_in_block < past_valid])

                n_full = n_past - 1
                if n_full > 0:
                    alive = stick_left(all_carries())
            if n_full // SB_UNROLL > 0:
                alive = sweep(kxp_ref, vxp_ref, i, n_full - 1, n_full // SB_UNROLL, SB_UNROLL, alive)
            if n_full % SB_UNROLL:
                alive = sweep(kxp_ref, vxp_ref, i, n_full % SB_UNROLL - 1, n_full % SB_UNROLL, 1, alive)

        for p in pairs:
            o = acc_ref[p]
            hi, lo = _split_bf16(o * o)
            jm = j_ref[...]
            ms = (jnp.dot(hi, jm, preferred_element_type=F32)
                  + jnp.dot(lo, jm, preferred_element_type=F32)) * (1.0 / SB_HEAD_DIM)
            y = o * lax.rsqrt(ms + EPS) * g_ref[:, lanes[p]]
            o_ref[0, rows, lanes[p]] = y.astype(BF16)
        if not report_left:
            return any_left
        return jnp.maximum(any_left, stick_left(all_carries()))

    any_left = q_tile(jnp.int32(0), jnp.int32(0), True)
    any_left = lax.fori_loop(1, n_q, lambda i, left: q_tile(i, left, False), any_left)
    left_ref[...] = jnp.full(left_ref.shape, any_left, jnp.int32)


def _attention(q, k_cur, v_cur, k_past, v_past, past_len, g_sb_out, report_left=False):
    b, lq, _ = q.shape
    tk = SB_KEYS
    tq = min(SB_QUERY_TILE, lq)
    assert lq % tq == 0 and tq % tk == 0
    if k_past is None:
        k_past = jnp.zeros((1, tk, SB_WIDTH), BF16)
        v_past = jnp.zeros((1, tk, SB_WIDTH), BF16)
        n_past, past_valid = 0, tk
    else:
        assert k_past.shape[1] % tk == 0
        n_past = -(-past_len // tk)
        past_valid = past_len - (n_past - 1) * tk
        k_past = k_past[:, :n_past * tk]
        v_past = v_past[:, :n_past * tk]
    bp, p, _ = k_past.shape
    jj = jnp.arange(2 * tk)
    same_head = (jj[:, None] // tk) == (jj[None, :] // tk)
    bdu = (same_head & (jj[:, None] > jj[None, :])).astype(BF16)
    bdo = same_head.astype(BF16)
    ll = jnp.arange(LANES) // SB_HEAD_DIM
    jm = (ll[:, None] == ll[None, :]).astype(BF16)
    width = SB_PAIRS * PAIR
    cur_spec = pl.BlockSpec((1, lq, width), lambda bi, hp: (bi, 0, hp))
    past_spec = pl.BlockSpec((1, p, width), (lambda bi, hp: (bi, 0, hp)) if bp > 1
                             else (lambda bi, hp: (0, 0, hp)))
    kernel = functools.partial(_attention_kernel, tq=tq, n_q=lq // tq, n_past=n_past,
                               past_valid=past_valid, report_left=report_left)
    stacked = lambda n: pltpu.VMEM((SB_PAIRS, n, 2 * tk, PAIR), BF16)
    n_groups = SB_WIDTH // width
    out, left = pl.pallas_call(
        kernel,
        grid=(b, n_groups),
        in_specs=[cur_spec, cur_spec, cur_spec, past_spec, past_spec,
                  _const_spec(bdu.shape), _const_spec(bdo.shape), _const_spec(jm.shape),
                  pl.BlockSpec((1, width), lambda bi, hp: (0, hp))],
        out_specs=[cur_spec, pl.BlockSpec((1, 1, SUBLANES, LANES), lambda bi, hp: (bi, hp, 0, 0))],
        out_shape=[jax.ShapeDtypeStruct((b, lq, SB_WIDTH), BF16),
                   jax.ShapeDtypeStruct((b, n_groups, SUBLANES, LANES), jnp.int32)],
        scratch_shapes=[stacked(lq // tk), stacked(lq // tk), stacked(p // tk), stacked(p // tk),
                        pltpu.VMEM((SB_PAIRS, tq, 2 * tk), F32), pltpu.VMEM((SB_PAIRS, tq, PAIR), F32)],
        compiler_params=pltpu.CompilerParams(
            dimension_semantics=("arbitrary", "arbitrary"), vmem_limit_bytes=VMEM_LIMIT),
        name="attention",
    )(q, k_cur, v_cur, k_past, v_past, bdu, bdo, jm, g_sb_out.reshape(1, SB_WIDTH))
    return out, left


def _attention_over_cache(q, k_cur, v_cur, cache_k, cache_v, g_sb_out):
    bd, _, past, _ = cache_k.shape
    to_rows = lambda c: c.transpose(0, 2, 1, 3).reshape(bd, c.shape[2], SB_WIDTH).astype(BF16)
    recent = min(past, SB_RECENT)
    out, left = _attention(q, k_cur, v_cur, to_rows(cache_k[:, :, past - recent:]),
                           to_rows(cache_v[:, :, past - recent:]), recent, g_sb_out, report_left=True)
    if recent == past:
        return out
    return lax.cond(
        jnp.any(left > 0),
        lambda: _attention(q, k_cur, v_cur, to_rows(cache_k), to_rows(cache_v), past, g_sb_out)[0],
        lambda: out)


def _retention_kernel(q_ref, k_ref, v_ref, gate_ref, s0_ref, g_ref, o_ref, s_ref, *, chunk, n_chunks):
    c = chunk
    head = pl.program_id(1).astype(F32)
    log_g = jnp.log(1.0 - jnp.exp2(jnp.full((1, LANES), -5.0, F32) - head))
    n_row = lax.broadcasted_iota(jnp.int32, (c, LANES), 0).astype(F32)
    q_decay = jnp.exp((n_row + 1.0) * log_g)
    k_decay = jnp.exp((c - 1.0 - n_row) * log_g)
    s_decay = jnp.exp(float(c) * log_g)
    diff = (lax.broadcasted_iota(jnp.int32, (c, c), 0)
            - lax.broadcasted_iota(jnp.int32, (c, c), 1))
    log_g_cc = log_g if c == LANES else jnp.log(1.0 - jnp.exp2(jnp.full((1, c), -5.0, F32) - head))
    decay = jnp.where(diff >= 0, jnp.exp(jnp.maximum(diff, 0).astype(F32) * log_g_cc), 0.0)
    gain = g_ref[...]

    def body(ci, s):
        rows = pl.ds(pl.multiple_of(ci * c, c), c)
        q = q_ref[0, rows, :]
        k = k_ref[0, rows, :]
        v = v_ref[0, rows, :]
        qk = lax.dot_general(q, k, (((1,), (1,)), ((), ())), preferred_element_type=F32)
        inner = jnp.dot((qk * decay).astype(BF16), v, preferred_element_type=F32)
        cross = jnp.dot(q, s.astype(BF16), preferred_element_type=F32) * q_decay
        o = inner + cross
        k_dec = (k.astype(F32) * k_decay).astype(BF16)
        s_new = s_decay * s + lax.dot_general(k_dec, v, (((0,), (0,)), ((), ())),
                                              preferred_element_type=F32)
        y = o * lax.rsqrt(jnp.mean(o * o, axis=-1, keepdims=True) + EPS) * gain
        half = 0.5 * gate_ref[0, rows, :].astype(F32)
        o_ref[0, rows, :] = (y * (half * (1.0 + jnp.tanh(half)))).astype(BF16)
        return s_new

    s_ref[0, 0] = lax.fori_loop(0, n_chunks, body, s0_ref[0, 0], unroll=math.gcd(n_chunks, RET_UNROLL))


def _retention(rq, rk, rv, rgate, s0, g_ret_out, chunk):
    b, l, _ = rq.shape
    assert l % chunk == 0
    bs = s0.shape[0]
    seq_spec = pl.BlockSpec((1, l, RET_DK), lambda bi, hh: (bi, 0, hh))
    state_spec = pl.BlockSpec((1, 1, RET_DK, RET_DV), lambda bi, hh: (bi, hh, 0, 0))
    s0_spec = state_spec if bs > 1 else pl.BlockSpec((1, 1, RET_DK, RET_DV), lambda bi, hh: (0, hh, 0, 0))
    kernel = functools.partial(_retention_kernel, chunk=chunk, n_chunks=l // chunk)
    return pl.pallas_call(
        kernel,
        grid=(b, RET_HEADS),
        in_specs=[seq_spec, seq_spec, seq_spec, seq_spec, s0_spec,
                  pl.BlockSpec((1, RET_DV), lambda bi, hh: (0, hh))],
        out_specs=[seq_spec, state_spec],
        out_shape=[jax.ShapeDtypeStruct((b, l, RET_WIDTH), BF16),
                   jax.ShapeDtypeStruct((b, RET_HEADS, RET_DK, RET_DV), F32)],
        compiler_params=pltpu.CompilerParams(
            dimension_semantics=("arbitrary", "arbitrary"), vmem_limit_bytes=VMEM_LIMIT),
        name="retention",
    )(rq, rk, rv, rgate, s0, g_ret_out.reshape(1, RET_WIDTH))


def _merge_ffn_kernel(x_ref, sb_ref, ret_ref, conv0_ref, wo_ref, gf_ref, wu_ref,
                      cw_ref, cb_ref, wd_ref, gl_ref, y_ref, conv_ref,
                      h_ref, hn_ref, ubuf_ref, act_ref, carry_ref, *, tm, seg):
    i = pl.program_id(1)

    @pl.when(i == 0)
    def _():
        carry_ref[...] = conv0_ref[0]

    mixed = jnp.concatenate([sb_ref[0], ret_ref[0]], axis=1)
    h = x_ref[0] + jnp.dot(mixed, wo_ref[...], preferred_element_type=F32)
    h_ref[...] = h
    ms = jnp.mean(h * h, axis=-1, keepdims=True)
    hn_ref[...] = (h * lax.rsqrt(ms + EPS) * gf_ref[...]).astype(BF16)

    n_seg = tm // seg
    ext_rows = seg + SUBLANES

    def up_project(j):
        ubuf = ubuf_ref.at[j % 2]
        hn = hn_ref[...]
        gate_cols = slice(j * FF_CHUNK, (j + 1) * FF_CHUNK)
        val_cols = slice(D_FF + j * FF_CHUNK, D_FF + (j + 1) * FF_CHUNK)
        gate = jnp.dot(hn, wu_ref[:, gate_cols], preferred_element_type=F32)
        val = jnp.dot(hn, wu_ref[:, val_cols], preferred_element_type=F32)
        for s in range(n_seg):
            base = s * ext_rows
            prev = slice(s * SUBLANES, (s + 1) * SUBLANES)
            ubuf[base:base + SUBLANES, :] = carry_ref[j, prev, :]
            ubuf[base + SUBLANES:base + ext_rows, 0:FF_CHUNK] = gate[s * seg:(s + 1) * seg]
            ubuf[base + SUBLANES:base + ext_rows, FF_CHUNK:] = val[s * seg:(s + 1) * seg]
            carry_ref[j, prev, :] = ubuf[base + seg:base + ext_rows, :]

    def gated_conv(j):
        ubuf = ubuf_ref.at[j % 2]
        cw = cw_ref[j]
        cb = cb_ref[j]
        strip = min(seg, FF_STRIP)
        for s in range(n_seg):
            for r in range(0, seg, strip):
                first = s * ext_rows + r
                ext = ubuf[first:first + strip + SUBLANES, :]
                c = cb
                for tap in range(CONV_W):
                    back = CONV_W - 1 - tap
                    rows = ext if back == 0 else pltpu.roll(ext, back, 0)
                    c = c + cw[tap:tap + 1, :] * rows[SUBLANES:, :]
                half = 0.5 * c[:, :FF_CHUNK]
                act_ref[s * seg + r:s * seg + r + strip, j * FF_CHUNK:(j + 1) * FF_CHUNK] = (
                    (half * (1.0 + jnp.tanh(half))) * c[:, FF_CHUNK:]).astype(BF16)

    up_project(0)
    for j in range(N_FF_CHUNKS):
        if j + 1 < N_FF_CHUNKS:
            up_project(j + 1)
        gated_conv(j)

    hh = h_ref[...] + jnp.dot(act_ref[...], wd_ref[...], preferred_element_type=F32)
    ms2 = jnp.mean(hh * hh, axis=-1, keepdims=True)
    y_ref[0] = hh * lax.rsqrt(ms2 + EPS) * gl_ref[...]

    @pl.when(i == pl.num_programs(1) - 1)
    def _():
        conv_ref[0] = carry_ref[...]


def _merge_ffn(x, sb_n, ret_n, conv0, w_out_bf, g_norm_ffn, w_up_bf, cw, cb, wd, g_norm_final, tm, seg=None):
    b, l, d = x.shape
    bs = conv0.shape[0]
    seg = tm if seg is None else seg
    assert tm % seg == 0 and (seg == tm or l == tm)
    row_spec = lambda w: pl.BlockSpec((1, tm, w), lambda bi, i: (bi, i, 0))
    conv_shape = (1, N_FF_CHUNKS, SUBLANES * (tm // seg), 2 * FF_CHUNK)
    conv_spec = pl.BlockSpec(conv_shape, lambda bi, i: (bi, 0, 0, 0))
    conv0_spec = conv_spec if bs > 1 else pl.BlockSpec(conv_shape, lambda bi, i: (0, 0, 0, 0))
    kernel = functools.partial(_merge_ffn_kernel, tm=tm, seg=seg)
    return pl.pallas_call(
        kernel,
        grid=(b, l // tm),
        in_specs=[row_spec(d), row_spec(SB_WIDTH), row_spec(RET_WIDTH), conv0_spec,
                  _const_spec(w_out_bf.shape), _const_spec((1, d)),
                  _const_spec(w_up_bf.shape),
                  _const_spec(cw.shape), _const_spec(cb.shape), _const_spec(wd.shape),
                  _const_spec((1, d))],
        out_specs=[row_spec(d), conv_spec],
        out_shape=[jax.ShapeDtypeStruct((b, l, d), F32),
                   jax.ShapeDtypeStruct((b,) + conv_shape[1:], F32)],
        scratch_shapes=[pltpu.VMEM((tm, d), F32), pltpu.VMEM((tm, d), BF16),
                        pltpu.VMEM((2, tm + SUBLANES * (tm // seg), 2 * FF_CHUNK), F32),
                        pltpu.VMEM((tm, D_FF), BF16),
                        pltpu.VMEM(conv_shape[1:], F32)],
        compiler_params=pltpu.CompilerParams(
            dimension_semantics=("arbitrary", "arbitrary"), vmem_limit_bytes=VMEM_LIMIT),
        name="merge_ffn",
    )(x, sb_n, ret_n, conv0, w_out_bf, g_norm_ffn.reshape(1, d), w_up_bf, cw, cb, wd,
      g_norm_final.reshape(1, d))


def _rope_tables(pos):
    half = RET_DK // 2
    inv = ROPE_BASE ** (-jnp.arange(half, dtype=F32) / half)
    ang = pos.astype(F32)[:, None] * inv[None, :]
    cos, sin = jnp.cos(ang), jnp.sin(ang)
    return jnp.concatenate([cos, cos], axis=1), jnp.concatenate([-sin, sin], axis=1)


def _conv_state_to_chunks(state):
    b = state.shape[0]
    s = state.reshape(b, CONV_W - 1, 2, N_FF_CHUNKS, FF_CHUNK).transpose(0, 3, 1, 2, 4)
    s = s.reshape(b, N_FF_CHUNKS, CONV_W - 1, 2 * FF_CHUNK)
    return jnp.pad(s, ((0, 0), (0, 0), (SUBLANES - (CONV_W - 1), 0), (0, 0)))


def _conv_state_from_chunks(chunks):
    b = chunks.shape[0]
    s = chunks[:, :, SUBLANES - (CONV_W - 1):, :].reshape(b, N_FF_CHUNKS, CONV_W - 1, 2, FF_CHUNK)
    return s.transpose(0, 2, 3, 1, 4).reshape(b, CONV_W - 1, 2 * D_FF)


def _pad_rows(a, rows):
    return a if a.shape[1] == rows else jnp.pad(a, ((0, 0), (0, rows - a.shape[1]), (0, 0)))


def _prefix_rows_kernel(k_any, v_any, k_rows_ref, v_rows_ref, k_ref, v_ref):
    del k_any, v_any
    k_ref[...] = k_rows_ref[...]
    v_ref[...] = v_rows_ref[...]


def _write_prefix_rows(k_big, v_big, k_rows, v_rows):
    b = k_big.shape[0]
    blk = (1,) + k_rows.shape[1:]
    any_spec = pl.BlockSpec(memory_space=pl.ANY)
    rows_spec = pl.BlockSpec(blk, lambda bi: (0, 0, 0, 0))
    out_spec = pl.BlockSpec(blk, lambda bi: (bi, 0, 0, 0))
    big = jax.ShapeDtypeStruct(k_big.shape, k_big.dtype)
    return pl.pallas_call(
        _prefix_rows_kernel,
        grid=(b,),
        in_specs=[any_spec, any_spec, rows_spec, rows_spec],
        out_specs=[out_spec, out_spec],
        out_shape=[big, big],
        input_output_aliases={0: 0, 1: 1},
        name="prefix_rows",
    )(k_big, v_big, k_rows, v_rows)


def _stream_step(x, pos0, attend, s0, conv0, wts, tm, chunk, head_row0=0, stack=False):
    b, l, d = x.shape
    cos2, sin2 = _rope_tables(pos0 + jnp.arange(l))
    tm_project = PROJECT_ROWS if l % PROJECT_ROWS == 0 else tm
    flat = lambda a: a.reshape(1, b * l, a.shape[-1])
    if stack:
        q, k, v, k_out, v_out, rq, rk, rv, rgate = _project(
            flat(x), wts["g_norm_mix"], wts["w_in"], jnp.tile(cos2, (b, 1)), jnp.tile(sin2, (b, 1)), b * l)
        q, k, v, rq, rk, rv, rgate = (a.reshape(b, l, a.shape[-1]) for a in (q, k, v, rq, rk, rv, rgate))
        k_out, v_out = (a.reshape(SB_HEADS, b, l, SB_HEAD_DIM).transpose(1, 0, 2, 3) for a in (k_out, v_out))
    else:
        q, k, v, k_out, v_out, rq, rk, rv, rgate = _project(x, wts["g_norm_mix"], wts["w_in"], cos2, sin2,
                                                           tm_project, head_row0)
    lq = -(-l // SB_KEYS) * SB_KEYS
    sb_n = attend(_pad_rows(q, lq), _pad_rows(k, lq), _pad_rows(v, lq))[:, :l]
    ret_n, s_new = _retention(rq, rk, rv, rgate, s0, wts["g_ret_out"], chunk)
    ffn = functools.partial(_merge_ffn, w_out_bf=wts["w_out"], g_norm_ffn=wts["g_norm_ffn"], w_up_bf=wts["w_up"],
                            cw=wts["cw"], cb=wts["cb"], wd=wts["wd"], g_norm_final=wts["g_norm_final"])
    if stack:
        groups = conv0.shape[2]
        conv0 = conv0.transpose(1, 0, 2, 3).reshape(1, N_FF_CHUNKS, b * groups, 2 * FF_CHUNK)
        y, conv_new = ffn(flat(x), flat(sb_n), flat(ret_n), conv0, tm=b * l, seg=l)
        y = y.reshape(b, l, d)
        conv_new = conv_new.reshape(N_FF_CHUNKS, b, groups, 2 * FF_CHUNK).transpose(1, 0, 2, 3)
    else:
        y, conv_new = ffn(x, sb_n, ret_n, conv0, tm=tm)
    return y, k, v, k_out, v_out, s_new, conv_new


def kernel(x_prompt, x_sample, cache_sb_k, cache_sb_v, state_ret, state_conv, meta_tokens, g_norm_mix, w_in, g_sb_out, g_ret_out, w_out, g_norm_ffn, w_up, conv_w, conv_b, w_down, g_norm_final):
    b, seq, d = x_prompt.shape
    bd, ls, _ = x_sample.shape
    past = cache_sb_k.shape[2]

    def ff_cols(a):
        r = a.shape[0]
        return a.reshape(r, 2, N_FF_CHUNKS, FF_CHUNK).transpose(2, 0, 1, 3).reshape(N_FF_CHUNKS, r, 2 * FF_CHUNK)

    wts = dict(
        g_norm_mix=g_norm_mix, g_sb_out=g_sb_out, g_ret_out=g_ret_out, g_norm_ffn=g_norm_ffn,
        g_norm_final=g_norm_final,
        w_in=w_in.astype(BF16), w_out=w_out.astype(BF16),
        w_up=w_up.astype(BF16),
        cw=ff_cols(conv_w), cb=ff_cols(conv_b.reshape(1, 2 * D_FF)),
        wd=w_down.astype(BF16),
    )

    zero_state = jnp.zeros((1, RET_HEADS, RET_DK, RET_DV), F32)
    zero_conv = jnp.zeros((1, N_FF_CHUNKS, SUBLANES, 2 * FF_CHUNK), F32)
    gain = wts["g_sb_out"]
    _, k_m, v_m, k_m_out, v_m_out, s_meta, conv_meta = _stream_step(
        meta_tokens[None], -N_META, lambda q, k, v: _attention(q, k, v, None, None, 0, gain)[0],
        zero_state, zero_conv, wts, N_META, N_META)

    k_m, v_m = _pad_rows(k_m, SB_KEYS), _pad_rows(v_m, SB_KEYS)
    y_prompt, _, _, k_p_out, v_p_out, s_prompt, conv_prompt = _stream_step(
        x_prompt, 0, lambda q, k, v: _attention(q, k, v, k_m, v_m, N_META, gain)[0],
        s_meta, conv_meta, wts, 512, 256, head_row0=N_META)
    new_k_prompt, new_v_prompt = _write_prefix_rows(k_p_out, v_p_out, k_m_out, v_m_out)

    y_sample, _, _, k_s_out, v_s_out, s_sample, conv_sample = _stream_step(
        x_sample, past, lambda q, k, v: _attention_over_cache(q, k, v, cache_sb_k, cache_sb_v, gain),
        state_ret, _conv_state_to_chunks(state_conv), wts, ls, ls, stack=True)

    return (y_prompt, y_sample, new_k_prompt, new_v_prompt, s_prompt,
            _conv_state_from_chunks(conv_prompt), k_s_out, v_s_out, s_sample,
            _conv_state_from_chunks(conv_sample))
```

```python
import functools
import math

import jax
import jax.numpy as jnp
from jax import lax
from jax.experimental import pallas as pl
from jax.experimental.pallas import tpu as pltpu

D_MODEL = 1024
N_META = 16
SB_HEADS = 8
SB_HEAD_DIM = 64
SB_WIDTH = SB_HEADS * SB_HEAD_DIM
RET_HEADS = 4
RET_DK = 128
RET_DV = 128
RET_WIDTH = RET_HEADS * RET_DV
MIX_WIDTH = SB_WIDTH + RET_WIDTH
GROUP = 512
N_GROUPS = 7
IN_WIDTH = N_GROUPS * GROUP
D_FF = 2816
CONV_W = 3
ROPE_BASE = 10000.0
EPS = 1e-5

LANES = 128
SUBLANES = 8
FF_CHUNK = 256
N_FF_CHUNKS = D_FF // FF_CHUNK
FF_STRIP = 64
RET_UNROLL = 8
PROJECT_ROWS = 1024
SB_KEYS = 128
SB_QUERY_TILE = 512
SB_UNROLL = 2
PAIR = 2 * SB_HEAD_DIM
SB_PAIRS = 2
SB_RECENT = 512
SB_EXHAUSTED = 152.0
SB_Q_SCALE = SB_HEAD_DIM ** -0.5 * math.log2(math.e)
VMEM_LIMIT = 56 * 1024 * 1024

BF16 = jnp.bfloat16
F32 = jnp.float32


def _const_spec(shape):
    zeros = (0,) * len(shape)
    return pl.BlockSpec(shape, lambda *_: zeros, pipeline_mode=pl.Buffered(1))


def _project_kernel(x_ref, g_ref, w_ref, cos_ref, sin_ref,
                    q_ref, k_ref, v_ref, ko_ref, vo_ref, rq_ref, rk_ref, rv_ref, rg_ref):
    x = x_ref[0]
    ms = jnp.mean(x * x, axis=-1, keepdims=True)
    h = (x * lax.rsqrt(ms + EPS) * g_ref[...]).astype(BF16)

    def group(i):
        return jnp.dot(h, w_ref[:, i * GROUP:(i + 1) * GROUP], preferred_element_type=F32)

    def split_heads(p, out_ref):
        for hh in range(SB_HEADS):
            out_ref[0, hh] = p[:, hh * SB_HEAD_DIM:(hh + 1) * SB_HEAD_DIM]

    def rope(p, out_ref, scale):
        cos = cos_ref[...]
        sin = sin_ref[...]
        for hh in range(RET_HEADS):
            t = p[:, hh * RET_DK:(hh + 1) * RET_DK]
            r = t * cos + pltpu.roll(t, RET_DK // 2, 1) * sin
            if scale is not None:
                r = r * scale
            out_ref[0, :, hh * RET_DK:(hh + 1) * RET_DK] = r.astype(BF16)

    q_ref[0] = (group(0) * SB_Q_SCALE).astype(BF16)
    pk = group(1)
    k_ref[0] = pk.astype(BF16)
    split_heads(pk, ko_ref)
    pv = group(2)
    v_ref[0] = pv.astype(BF16)
    split_heads(pv, vo_ref)
    rope(group(3), rq_ref, None)
    rope(group(4), rk_ref, RET_DK ** -0.5)
    rv_ref[0] = group(5).astype(BF16)
    rg_ref[0] = group(6).astype(BF16)


def _project(x, g_norm, w_in_bf, cos2, sin2, tm, head_row0=0):
    b, l, d = x.shape
    grid = (b, l // tm)
    row_spec = lambda w: pl.BlockSpec((1, tm, w), lambda bi, i: (bi, i, 0))
    head_blk = (1, SB_HEADS, tm, SB_HEAD_DIM)
    head_spec = pl.BlockSpec(tuple(pl.Element(n) for n in head_blk),
                             lambda bi, i: (bi, 0, pl.multiple_of(head_row0 + i * tm, SUBLANES), 0))
    tab_spec = pl.BlockSpec((tm, RET_DK), lambda bi, i: (i, 0))
    act = jax.ShapeDtypeStruct((b, l, GROUP), BF16)
    heads = jax.ShapeDtypeStruct((b, SB_HEADS, head_row0 + l, SB_HEAD_DIM), F32)
    return pl.pallas_call(
        _project_kernel,
        grid=grid,
        in_specs=[row_spec(d), _const_spec((1, d)), _const_spec((d, IN_WIDTH)), tab_spec, tab_spec],
        out_specs=[row_spec(GROUP)] * 3 + [head_spec] * 2 + [row_spec(GROUP)] * 4,
        out_shape=[act] * 3 + [heads] * 2 + [act] * 4,
        compiler_params=pltpu.CompilerParams(
            dimension_semantics=("arbitrary", "arbitrary"), vmem_limit_bytes=VMEM_LIMIT),
        name="project",
    )(x, g_norm.reshape(1, d), w_in_bf, cos2, sin2)


def _split_bf16(x):
    hi = x.astype(BF16)
    lo = (x - hi.astype(F32)).astype(BF16)
    return hi, lo


def _attention_kernel(q_ref, kc_ref, vc_ref, kp_ref, vp_ref, bdu_ref, bdo_ref, j_ref, g_ref,
                      o_ref, left_ref, kxc_ref, vxc_ref, kxp_ref, vxp_ref, carry_ref, acc_ref,
                      *, tq, n_q, n_past, past_valid, report_left):
    tk = SB_KEYS
    per_tile = tq // tk
    head0 = lax.broadcasted_iota(jnp.int32, (tk, LANES), 1) < SB_HEAD_DIM
    key_in_block = lax.broadcasted_iota(jnp.int32, (tq, 2 * tk), 1) & (tk - 1)
    delta = key_in_block - lax.broadcasted_iota(jnp.int32, (tq, 2 * tk), 0)

    pairs = range(SB_PAIRS)
    lanes = [slice(p * PAIR, (p + 1) * PAIR) for p in pairs]

    def expand(src_ref, dst_ref, n_blocks):
        def body(j, _):
            for p in pairs:
                blk = src_ref[0, pl.ds(pl.multiple_of(j * tk, tk), tk), lanes[p]]
                zero = jnp.zeros_like(blk)
                dst_ref[p, j, 0:tk, :] = jnp.where(head0, blk, zero)
                dst_ref[p, j, tk:2 * tk, :] = jnp.where(head0, zero, blk)
            return 0
        lax.fori_loop(0, n_blocks, body, 0)

    expand(kc_ref, kxc_ref, n_q * per_tile)
    expand(vc_ref, vxc_ref, n_q * per_tile)
    if n_past > 0:
        expand(kp_ref, kxp_ref, n_past)
        expand(vp_ref, vxp_ref, n_past)

    def add_blocks(kx_ref, vx_ref, i, j_last, masks, r0=0, r1=tq):
        rows = pl.ds(pl.multiple_of(i * tq + r0, tk), r1 - r0)
        carry = [carry_ref[p, r0:r1, :] for p in pairs]
        for u, mask in enumerate(masks):
            j = j_last - u
            for p in pairs:
                z = lax.dot_general(q_ref[0, rows, lanes[p]], kx_ref[p, j], (((1,), (1,)), ((), ())),
                                    preferred_element_type=F32)
                sp = jnp.maximum(z, 0.0) + jnp.log2(1.0 + jnp.exp2(-jnp.abs(z)))
                spm = sp if mask is None else jnp.where(mask, sp, 0.0)
                hi = spm.astype(BF16)
                later = jnp.dot(hi, bdu_ref[...], preferred_element_type=F32)
                total = jnp.dot(hi, bdo_ref[...], preferred_element_type=F32)
                a = jnp.exp2((z - sp) - (later + carry[p]))
                if mask is not None:
                    a = jnp.where(mask, a, 0.0)
                acc_ref[p, r0:r1, :] += jnp.dot(a.astype(BF16), vx_ref[p, j], preferred_element_type=F32)
                carry[p] = carry[p] + total
        for p in pairs:
            carry_ref[p, r0:r1, :] = carry[p]
        return carry

    def stick_left(carry):
        least = functools.reduce(jnp.minimum, carry)
        return (jnp.min(least) < SB_EXHAUSTED).astype(jnp.int32)

    def sweep(kx_ref, vx_ref, i, j_top, n_steps, unroll, alive):
        def cond(state):
            m, live = state
            return jnp.logical_and(m < n_steps, live > 0)

        def body(state):
            m, _ = state
            carry = add_blocks(kx_ref, vx_ref, i, j_top - m * unroll, [None] * unroll)
            return m + 1, stick_left(carry)

        return lax.while_loop(cond, body, (jnp.int32(0), alive))[1]

    def all_carries(r0=0):
        return [carry_ref[p, r0:, :] for p in pairs]

    def q_tile(i, any_left, first):
        rows = pl.ds(pl.multiple_of(i * tq, tq), tq)
        carry_ref[...] = jnp.zeros_like(carry_ref)
        acc_ref[...] = jnp.zeros_like(acc_ref)

        for jj in range(per_tile - 1, -1, -1):
            r0 = jj * tk
            add_blocks(kxc_ref, vxc_ref, i, i * per_tile + jj, [delta[r0:, :] < -r0], r0)

        unroll = SB_UNROLL if per_tile % SB_UNROLL == 0 else 1
        top = unroll * tk
        if first:
            alive = stick_left(all_carries())
        elif top < tq:
            j_prev = i * per_tile - 1
            top_left = stick_left(add_blocks(kxc_ref, vxc_ref, i, j_prev, [None] * unroll, 0, top))
            low_left = lax.cond(
                stick_left(all_carries(top)) > 0,
                lambda: stick_left(add_blocks(kxc_ref, vxc_ref, i, j_prev, [None] * unroll, top, tq)),
                lambda: jnp.int32(0))
            alive = sweep(kxc_ref, vxc_ref, i, j_prev - unroll, (i * per_tile) // unroll - 1, unroll,
                          jnp.maximum(top_left, low_left))
        else:
            alive = sweep(kxc_ref, vxc_ref, i, i * per_tile - 1, (i * per_tile) // unroll, unroll,
                          stick_left(all_carries()))

        if n_past > 0:
            n_full = n_past
            if past_valid < tk:
                @pl.when(alive > 0)
                def _():
                    add_blocks(kxp_ref, vxp_ref, i, n_past - 1, [key_in_block < past_valid])

                n_full = n_past - 1
                if n_full > 0:
                    alive = stick_left(all_carries())
            if n_full // SB_UNROLL > 0:
                alive = sweep(kxp_ref, vxp_ref, i, n_full - 1, n_full // SB_UNROLL, SB_UNROLL, alive)
            if n_full % SB_UNROLL:
                alive = sweep(kxp_ref, vxp_ref, i, n_full % SB_UNROLL - 1, n_full % SB_UNROLL, 1, alive)

        for p in pairs:
            o = acc_ref[p]
            hi, lo = _split_bf16(o * o)
            jm = j_ref[...]
            ms = (jnp.dot(hi, jm, preferred_element_type=F32)
                  + jnp.dot(lo, jm, preferred_element_type=F32)) * (1.0 / SB_HEAD_DIM)
            y = o * lax.rsqrt(ms + EPS) * g_ref[:, lanes[p]]
            o_ref[0, rows, lanes[p]] = y.astype(BF16)
        if not report_left:
            return any_left
        return jnp.maximum(any_left, stick_left(all_carries()))

    any_left = q_tile(jnp.int32(0), jnp.int32(0), True)
    any_left = lax.fori_loop(1, n_q, lambda i, left: q_tile(i, left, False), any_left)
    left_ref[...] = jnp.full(left_ref.shape, any_left, jnp.int32)


def _attention(q, k_cur, v_cur, k_past, v_past, past_len, g_sb_out, report_left=False):
    b, lq, _ = q.shape
    tk = SB_KEYS
    tq = min(SB_QUERY_TILE, lq)
    assert lq % tq == 0 and tq % tk == 0
    if k_past is None:
        k_past = jnp.zeros((1, tk, SB_WIDTH), BF16)
        v_past = jnp.zeros((1, tk, SB_WIDTH), BF16)
        n_past, past_valid = 0, tk
    else:
        assert k_past.shape[1] % tk == 0
        n_past = -(-past_len // tk)
        past_valid = past_len - (n_past - 1) * tk
        k_past = k_past[:, :n_past * tk]
        v_past = v_past[:, :n_past * tk]
    bp, p, _ = k_past.shape
    jj = jnp.arange(2 * tk)
    same_head = (jj[:, None] // tk) == (jj[None, :] // tk)
    bdu = (same_head & (jj[:, None] > jj[None, :])).astype(BF16)
    bdo = same_head.astype(BF16)
    ll = jnp.arange(LANES) // SB_HEAD_DIM
    jm = (ll[:, None] == ll[None, :]).astype(BF16)
    width = SB_PAIRS * PAIR
    cur_spec = pl.BlockSpec((1, lq, width), lambda bi, hp: (bi, 0, hp))
    past_spec = pl.BlockSpec((1, p, width), (lambda bi, hp: (bi, 0, hp)) if bp > 1
                             else (lambda bi, hp: (0, 0, hp)))
    kernel = functools.partial(_attention_kernel, tq=tq, n_q=lq // tq, n_past=n_past,
                               past_valid=past_valid, report_left=report_left)
    stacked = lambda n: pltpu.VMEM((SB_PAIRS, n, 2 * tk, PAIR), BF16)
    n_groups = SB_WIDTH // width
    out, left = pl.pallas_call(
        kernel,
        grid=(b, n_groups),
        in_specs=[cur_spec, cur_spec, cur_spec, past_spec, past_spec,
                  _const_spec(bdu.shape), _const_spec(bdo.shape), _const_spec(jm.shape),
                  pl.BlockSpec((1, width), lambda bi, hp: (0, hp))],
        out_specs=[cur_spec, pl.BlockSpec((1, 1, SUBLANES, LANES), lambda bi, hp: (bi, hp, 0, 0))],
        out_shape=[jax.ShapeDtypeStruct((b, lq, SB_WIDTH), BF16),
                   jax.ShapeDtypeStruct((b, n_groups, SUBLANES, LANES), jnp.int32)],
        scratch_shapes=[stacked(lq // tk), stacked(lq // tk), stacked(p // tk), stacked(p // tk),
                        pltpu.VMEM((SB_PAIRS, tq, 2 * tk), F32), pltpu.VMEM((SB_PAIRS, tq, PAIR), F32)],
        compiler_params=pltpu.CompilerParams(
            dimension_semantics=("arbitrary", "arbitrary"), vmem_limit_bytes=VMEM_LIMIT),
        name="attention",
    )(q, k_cur, v_cur, k_past, v_past, bdu, bdo, jm, g_sb_out.reshape(1, SB_WIDTH))
    return out, left


def _attention_over_cache(q, k_cur, v_cur, cache_k, cache_v, g_sb_out):
    bd, _, past, _ = cache_k.shape
    to_rows = lambda c: c.transpose(0, 2, 1, 3).reshape(bd, c.shape[2], SB_WIDTH).astype(BF16)
    recent = min(past, SB_RECENT)
    out, left = _attention(q, k_cur, v_cur, to_rows(cache_k[:, :, past - recent:]),
                           to_rows(cache_v[:, :, past - recent:]), recent, g_sb_out, report_left=True)
    if recent == past:
        return out
    return lax.cond(
        jnp.any(left > 0),
        lambda: _attention(q, k_cur, v_cur, to_rows(cache_k), to_rows(cache_v), past, g_sb_out)[0],
        lambda: out)


def _retention_kernel(q_ref, k_ref, v_ref, gate_ref, s0_ref, g_ref, o_ref, s_ref, *, chunk, n_chunks):
    c = chunk
    head = pl.program_id(1).astype(F32)
    log_g = jnp.log(1.0 - jnp.exp2(jnp.full((1, LANES), -5.0, F32) - head))
    n_row = lax.broadcasted_iota(jnp.int32, (c, LANES), 0).astype(F32)
    q_decay = jnp.exp((n_row + 1.0) * log_g)
    k_decay = jnp.exp((c - 1.0 - n_row) * log_g)
    s_decay = jnp.exp(float(c) * log_g)
    diff = (lax.broadcasted_iota(jnp.int32, (c, c), 0)
            - lax.broadcasted_iota(jnp.int32, (c, c), 1))
    log_g_cc = log_g if c == LANES else jnp.log(1.0 - jnp.exp2(jnp.full((1, c), -5.0, F32) - head))
    decay = jnp.where(diff >= 0, jnp.exp(jnp.maximum(diff, 0).astype(F32) * log_g_cc), 0.0)
    gain = g_ref[...]

    def body(ci, s):
        rows = pl.ds(pl.multiple_of(ci * c, c), c)
        q = q_ref[0, rows, :]
        k = k_ref[0, rows, :]
        v = v_ref[0, rows, :]
        qk = lax.dot_general(q, k, (((1,), (1,)), ((), ())), preferred_element_type=F32)
        inner = jnp.dot((qk * decay).astype(BF16), v, preferred_element_type=F32)
        cross = jnp.dot(q, s.astype(BF16), preferred_element_type=F32) * q_decay
        o = inner + cross
        k_dec = (k.astype(F32) * k_decay).astype(BF16)
        s_new = s_decay * s + lax.dot_general(k_dec, v, (((0,), (0,)), ((), ())),
                                              preferred_element_type=F32)
        y = o * lax.rsqrt(jnp.mean(o * o, axis=-1, keepdims=True) + EPS) * gain
        half = 0.5 * gate_ref[0, rows, :].astype(F32)
        o_ref[0, rows, :] = (y * (half * (1.0 + jnp.tanh(half)))).astype(BF16)
        return s_new

    s_ref[0, 0] = lax.fori_loop(0, n_chunks, body, s0_ref[0, 0], unroll=math.gcd(n_chunks, RET_UNROLL))


def _retention(rq, rk, rv, rgate, s0, g_ret_out, chunk):
    b, l, _ = rq.shape
    assert l % chunk == 0
    bs = s0.shape[0]
    seq_spec = pl.BlockSpec((1, l, RET_DK), lambda bi, hh: (bi, 0, hh))
    state_spec = pl.BlockSpec((1, 1, RET_DK, RET_DV), lambda bi, hh: (bi, hh, 0, 0))
    s0_spec = state_spec if bs > 1 else pl.BlockSpec((1, 1, RET_DK, RET_DV), lambda bi, hh: (0, hh, 0, 0))
    kernel = functools.partial(_retention_kernel, chunk=chunk, n_chunks=l // chunk)
    return pl.pallas_call(
        kernel,
        grid=(b, RET_HEADS),
        in_specs=[seq_spec, seq_spec, seq_spec, seq_spec, s0_spec,
                  pl.BlockSpec((1, RET_DV), lambda bi, hh: (0, hh))],
        out_specs=[seq_spec, state_spec],
        out_shape=[jax.ShapeDtypeStruct((b, l, RET_WIDTH), BF16),
                   jax.ShapeDtypeStruct((b, RET_HEADS, RET_DK, RET_DV), F32)],
        compiler_params=pltpu.CompilerParams(
            dimension_semantics=("arbitrary", "arbitrary"), vmem_limit_bytes=VMEM_LIMIT),
        name="retention",
    )(rq, rk, rv, rgate, s0, g_ret_out.reshape(1, RET_WIDTH))


def _merge_ffn_kernel(x_ref, sb_ref, ret_ref, conv0_ref, wo_ref, gf_ref, wu_ref,
                      cw_ref, cb_ref, wd_ref, gl_ref, y_ref, conv_ref,
                      h_ref, hn_ref, ubuf_ref, act_ref, carry_ref, *, tm, seg):
    i = pl.program_id(1)

    @pl.when(i == 0)
    def _():
        carry_ref[...] = conv0_ref[0]

    mixed = jnp.concatenate([sb_ref[0], ret_ref[0]], axis=1)
    h = x_ref[0] + jnp.dot(mixed, wo_ref[...], preferred_element_type=F32)
    h_ref[...] = h
    ms = jnp.mean(h * h, axis=-1, keepdims=True)
    hn_ref[...] = (h * lax.rsqrt(ms + EPS) * gf_ref[...]).astype(BF16)

    n_seg = tm // seg
    ext_rows = seg + SUBLANES

    def up_project(j):
        ubuf = ubuf_ref.at[j % 2]
        hn = hn_ref[...]
        gate_cols = slice(j * FF_CHUNK, (j + 1) * FF_CHUNK)
        val_cols = slice(D_FF + j * FF_CHUNK, D_FF + (j + 1) * FF_CHUNK)
        gate = jnp.dot(hn, wu_ref[:, gate_cols], preferred_element_type=F32)
        val = jnp.dot(hn, wu_ref[:, val_cols], preferred_element_type=F32)
        for s in range(n_seg):
            base = s * ext_rows
            prev = slice(s * SUBLANES, (s + 1) * SUBLANES)
            ubuf[base:base + SUBLANES, :] = carry_ref[j, prev, :]
            ubuf[base + SUBLANES:base + ext_rows, 0:FF_CHUNK] = gate[s * seg:(s + 1) * seg]
            ubuf[base + SUBLANES:base + ext_rows, FF_CHUNK:] = val[s * seg:(s + 1) * seg]
            carry_ref[j, prev, :] = ubuf[base + seg:base + ext_rows, :]

    def gated_conv(j):
        ubuf = ubuf_ref.at[j % 2]
        cw = cw_ref[j]
        cb = cb_ref[j]
        strip = min(seg, FF_STRIP)
        for s in range(n_seg):
            for r in range(0, seg, strip):
                first = s * ext_rows + r
                ext = ubuf[first:first + strip + SUBLANES, :]
                c = cb
                for tap in range(CONV_W):
                    back = CONV_W - 1 - tap
                    rows = ext if back == 0 else pltpu.roll(ext, back, 0)
                    c = c + cw[tap:tap + 1, :] * rows[SUBLANES:, :]
                half = 0.5 * c[:, :FF_CHUNK]
                act_ref[s * seg + r:s * seg + r + strip, j * FF_CHUNK:(j + 1) * FF_CHUNK] = (
                    (half * (1.0 + jnp.tanh(half))) * c[:, FF_CHUNK:]).astype(BF16)

    up_project(0)
    for j in range(N_FF_CHUNKS):
        if j + 1 < N_FF_CHUNKS:
            up_project(j + 1)
        gated_conv(j)

    hh = h_ref[...] + jnp.dot(act_ref[...], wd_ref[...], preferred_element_type=F32)
    ms2 = jnp.mean(hh * hh, axis=-1, keepdims=True)
    y_ref[0] = hh * lax.rsqrt(ms2 + EPS) * gl_ref[...]

    @pl.when(i == pl.num_programs(1) - 1)
    def _():
        conv_ref[0] = carry_ref[...]


def _merge_ffn(x, sb_n, ret_n, conv0, w_out_bf, g_norm_ffn, w_up_bf, cw, cb, wd, g_norm_final, tm, seg=None):
    b, l, d = x.shape
    bs = conv0.shape[0]
    seg = tm if seg is None else seg
    assert tm % seg == 0 and (seg == tm or l == tm)
    row_spec = lambda w: pl.BlockSpec((1, tm, w), lambda bi, i: (bi, i, 0))
    conv_shape = (1, N_FF_CHUNKS, SUBLANES * (tm // seg), 2 * FF_CHUNK)
    conv_spec = pl.BlockSpec(conv_shape, lambda bi, i: (bi, 0, 0, 0))
    conv0_spec = conv_spec if bs > 1 else pl.BlockSpec(conv_shape, lambda bi, i: (0, 0, 0, 0))
    kernel = functools.partial(_merge_ffn_kernel, tm=tm, seg=seg)
    return pl.pallas_call(
        kernel,
        grid=(b, l // tm),
        in_specs=[row_spec(d), row_spec(SB_WIDTH), row_spec(RET_WIDTH), conv0_spec,
                  _const_spec(w_out_bf.shape), _const_spec((1, d)),
                  _const_spec(w_up_bf.shape),
                  _const_spec(cw.shape), _const_spec(cb.shape), _const_spec(wd.shape),
                  _const_spec((1, d))],
        out_specs=[row_spec(d), conv_spec],
        out_shape=[jax.ShapeDtypeStruct((b, l, d), F32),
                   jax.ShapeDtypeStruct((b,) + conv_shape[1:], F32)],
        scratch_shapes=[pltpu.VMEM((tm, d), F32), pltpu.VMEM((tm, d), BF16),
                        pltpu.VMEM((2, tm + SUBLANES * (tm // seg), 2 * FF_CHUNK), F32),
                        pltpu.VMEM((tm, D_FF), BF16),
                        pltpu.VMEM(conv_shape[1:], F32)],
        compiler_params=pltpu.CompilerParams(
            dimension_semantics=("arbitrary", "arbitrary"), vmem_limit_bytes=VMEM_LIMIT),
        name="merge_ffn",
    )(x, sb_n, ret_n, conv0, w_out_bf, g_norm_ffn.reshape(1, d), w_up_bf, cw, cb, wd,
      g_norm_final.reshape(1, d))


def _rope_tables(pos):
    half = RET_DK // 2
    inv = ROPE_BASE ** (-jnp.arange(half, dtype=F32) / half)
    ang = pos.astype(F32)[:, None] * inv[None, :]
    cos, sin = jnp.cos(ang), jnp.sin(ang)
    return jnp.concatenate([cos, cos], axis=1), jnp.concatenate([-sin, sin], axis=1)


def _conv_state_to_chunks(state):
    b = state.shape[0]
    s = state.reshape(b, CONV_W - 1, 2, N_FF_CHUNKS, FF_CHUNK).transpose(0, 3, 1, 2, 4)
    s = s.reshape(b, N_FF_CHUNKS, CONV_W - 1, 2 * FF_CHUNK)
    return jnp.pad(s, ((0, 0), (0, 0), (SUBLANES - (CONV_W - 1), 0), (0, 0)))


def _conv_state_from_chunks(chunks):
    b = chunks.shape[0]
    s = chunks[:, :, SUBLANES - (CONV_W - 1):, :].reshape(b, N_FF_CHUNKS, CONV_W - 1, 2, FF_CHUNK)
    return s.transpose(0, 2, 3, 1, 4).reshape(b, CONV_W - 1, 2 * D_FF)


def _pad_rows(a, rows):
    return a if a.shape[1] == rows else jnp.pad(a, ((0, 0), (0, rows - a.shape[1]), (0, 0)))


def _prefix_rows_kernel(k_any, v_any, k_rows_ref, v_rows_ref, k_ref, v_ref):
    del k_any, v_any
    k_ref[...] = k_rows_ref[...]
    v_ref[...] = v_rows_ref[...]


def _write_prefix_rows(k_big, v_big, k_rows, v_rows):
    b = k_big.shape[0]
    blk = (1,) + k_rows.shape[1:]
    any_spec = pl.BlockSpec(memory_space=pl.ANY)
    rows_spec = pl.BlockSpec(blk, lambda bi: (0, 0, 0, 0))
    out_spec = pl.BlockSpec(blk, lambda bi: (bi, 0, 0, 0))
    big = jax.ShapeDtypeStruct(k_big.shape, k_big.dtype)
    return pl.pallas_call(
        _prefix_rows_kernel,
        grid=(b,),
        in_specs=[any_spec, any_spec, rows_spec, rows_spec],
        out_specs=[out_spec, out_spec],
        out_shape=[big, big],
        input_output_aliases={0: 0, 1: 1},
        name="prefix_rows",
    )(k_big, v_big, k_rows, v_rows)


def _stream_step(x, pos0, attend, s0, conv0, wts, tm, chunk, head_row0=0, stack=False):
    b, l, d = x.shape
    cos2, sin2 = _rope_tables(pos0 + jnp.arange(l))
    tm_project = PROJECT_ROWS if l % PROJECT_ROWS == 0 else tm
    flat = lambda a: a.reshape(1, b * l, a.shape[-1])
    if stack:
        q, k, v, k_out, v_out, rq, rk, rv, rgate = _project(
            flat(x), wts["g_norm_mix"], wts["w_in"], jnp.tile(cos2, (b, 1)), jnp.tile(sin2, (b, 1)), b * l)
        q, k, v, rq, rk, rv, rgate = (a.reshape(b, l, a.shape[-1]) for a in (q, k, v, rq, rk, rv, rgate))
        k_out, v_out = (a.reshape(SB_HEADS, b, l, SB_HEAD_DIM).transpose(1, 0, 2, 3) for a in (k_out, v_out))
    else:
        q, k, v, k_out, v_out, rq, rk, rv, rgate = _project(x, wts["g_norm_mix"], wts["w_in"], cos2, sin2,
                                                           tm_project, head_row0)
    lq = -(-l // SB_KEYS) * SB_KEYS
    sb_n = attend(_pad_rows(q, lq), _pad_rows(k, lq), _pad_rows(v, lq))[:, :l]
    ret_n, s_new = _retention(rq, rk, rv, rgate, s0, wts["g_ret_out"], chunk)
    ffn = functools.partial(_merge_ffn, w_out_bf=wts["w_out"], g_norm_ffn=wts["g_norm_ffn"], w_up_bf=wts["w_up"],
                            cw=wts["cw"], cb=wts["cb"], wd=wts["wd"], g_norm_final=wts["g_norm_final"])
    if stack:
        groups = conv0.shape[2]
        conv0 = conv0.transpose(1, 0, 2, 3).reshape(1, N_FF_CHUNKS, b * groups, 2 * FF_CHUNK)
        y, conv_new = ffn(flat(x), flat(sb_n), flat(ret_n), conv0, tm=b * l, seg=l)
        y = y.reshape(b, l, d)
        conv_new = conv_new.reshape(N_FF_CHUNKS, b, groups, 2 * FF_CHUNK).transpose(1, 0, 2, 3)
    else:
        y, conv_new = ffn(x, sb_n, ret_n, conv0, tm=tm)
    return y, k, v, k_out, v_out, s_new, conv_new


def kernel(x_prompt, x_sample, cache_sb_k, cache_sb_v, state_ret, state_conv, meta_tokens, g_norm_mix, w_in, g_sb_out, g_ret_out, w_out, g_norm_ffn, w_up, conv_w, conv_b, w_down, g_norm_final):
    b, seq, d = x_prompt.shape
    bd, ls, _ = x_sample.shape
    past = cache_sb_k.shape[2]

    def ff_cols(a):
        r = a.shape[0]
        return a.reshape(r, 2, N_FF_CHUNKS, FF_CHUNK).transpose(2, 0, 1, 3).reshape(N_FF_CHUNKS, r, 2 * FF_CHUNK)

    wts = dict(
        g_norm_mix=g_norm_mix, g_sb_out=g_sb_out, g_ret_out=g_ret_out, g_norm_ffn=g_norm_ffn,
        g_norm_final=g_norm_final,
        w_in=w_in.astype(BF16), w_out=w_out.astype(BF16),
        w_up=w_up.astype(BF16),
        cw=ff_cols(conv_w), cb=ff_cols(conv_b.reshape(1, 2 * D_FF)),
        wd=w_down.astype(BF16),
    )

    zero_state = jnp.zeros((1, RET_HEADS, RET_DK, RET_DV), F32)
    zero_conv = jnp.zeros((1, N_FF_CHUNKS, SUBLANES, 2 * FF_CHUNK), F32)
    gain = wts["g_sb_out"]
    _, k_m, v_m, k_m_out, v_m_out, s_meta, conv_meta = _stream_step(
        meta_tokens[None], -N_META, lambda q, k, v: _attention(q, k, v, None, None, 0, gain)[0],
        zero_state, zero_conv, wts, N_META, N_META)

    k_m, v_m = _pad_rows(k_m, SB_KEYS), _pad_rows(v_m, SB_KEYS)
    y_prompt, _, _, k_p_out, v_p_out, s_prompt, conv_prompt = _stream_step(
        x_prompt, 0, lambda q, k, v: _attention(q, k, v, k_m, v_m, N_META, gain)[0],
        s_meta, conv_meta, wts, 512, 256, head_row0=N_META)
    new_k_prompt, new_v_prompt = _write_prefix_rows(k_p_out, v_p_out, k_m_out, v_m_out)

    y_sample, _, _, k_s_out, v_s_out, s_sample, conv_sample = _stream_step(
        x_sample, past, lambda q, k, v: _attention_over_cache(q, k, v, cache_sb_k, cache_sb_v, gain),
        state_ret, _conv_state_to_chunks(state_conv), wts, ls, ls, stack=True)

    return (y_prompt, y_sample, new_k_prompt, new_v_prompt, s_prompt,
            _conv_state_from_chunks(conv_prompt), k_s_out, v_s_out, s_sample,
            _conv_state_from_chunks(conv_sample))
```

```python
import functools
import math

import jax
import jax.numpy as jnp
from jax import lax
from jax.experimental import pallas as pl
from jax.experimental.pallas import tpu as pltpu

D_MODEL = 1024
N_META = 16
SB_HEADS = 8
SB_HEAD_DIM = 64
SB_WIDTH = SB_HEADS * SB_HEAD_DIM
RET_HEADS = 4
RET_DK = 128
RET_DV = 128
RET_WIDTH = RET_HEADS * RET_DV
MIX_WIDTH = SB_WIDTH + RET_WIDTH
GROUP = 512
N_GROUPS = 7
IN_WIDTH = N_GROUPS * GROUP
D_FF = 2816
CONV_W = 3
ROPE_BASE = 10000.0
EPS = 1e-5

LANES = 128
SUBLANES = 8
FF_CHUNK = 256
N_FF_CHUNKS = D_FF // FF_CHUNK
FF_STRIP = 64
RET_UNROLL = 8
PROJECT_ROWS = 1024
SB_KEYS = 128
SB_QUERY_TILE = 512
SB_UNROLL = 2
PAIR = 2 * SB_HEAD_DIM
SB_PAIRS = 2
SB_RECENT = 512
SB_EXHAUSTED = 152.0
SB_Q_SCALE = SB_HEAD_DIM ** -0.5 * math.log2(math.e)
VMEM_LIMIT = 56 * 1024 * 1024

BF16 = jnp.bfloat16
F32 = jnp.float32


def _const_spec(shape):
    zeros = (0,) * len(shape)
    return pl.BlockSpec(shape, lambda *_: zeros, pipeline_mode=pl.Buffered(1))


def _project_kernel(x_ref, g_ref, w_ref, cos_ref, sin_ref,
                    q_ref, k_ref, v_ref, ko_ref, vo_ref, rq_ref, rk_ref, rv_ref, rg_ref):
    x = x_ref[0]
    ms = jnp.mean(x * x, axis=-1, keepdims=True)
    h = (x * lax.rsqrt(ms + EPS) * g_ref[...]).astype(BF16)

    def group(i):
        return jnp.dot(h, w_ref[:, i * GROUP:(i + 1) * GROUP], preferred_element_type=F32)

    def split_heads(p, out_ref):
        for hh in range(SB_HEADS):
            out_ref[0, hh] = p[:, hh * SB_HEAD_DIM:(hh + 1) * SB_HEAD_DIM]

    def rope(p, out_ref, scale):
        cos = cos_ref[...]
        sin = sin_ref[...]
        for hh in range(RET_HEADS):
            t = p[:, hh * RET_DK:(hh + 1) * RET_DK]
            r = t * cos + pltpu.roll(t, RET_DK // 2, 1) * sin
            if scale is not None:
                r = r * scale
            out_ref[0, :, hh * RET_DK:(hh + 1) * RET_DK] = r.astype(BF16)

    q_ref[0] = (group(0) * SB_Q_SCALE).astype(BF16)
    pk = group(1)
    k_ref[0] = pk.astype(BF16)
    split_heads(pk, ko_ref)
    pv = group(2)
    v_ref[0] = pv.astype(BF16)
    split_heads(pv, vo_ref)
    rope(group(3), rq_ref, None)
    rope(group(4), rk_ref, RET_DK ** -0.5)
    rv_ref[0] = group(5).astype(BF16)
    rg_ref[0] = group(6).astype(BF16)


def _project(x, g_norm, w_in_bf, cos2, sin2, tm, head_row0=0):
    b, l, d = x.shape
    grid = (b, l // tm)
    row_spec = lambda w: pl.BlockSpec((1, tm, w), lambda bi, i: (bi, i, 0))
    head_blk = (1, SB_HEADS, tm, SB_HEAD_DIM)
    head_spec = pl.BlockSpec(tuple(pl.Element(n) for n in head_blk),
                             lambda bi, i: (bi, 0, pl.multiple_of(head_row0 + i * tm, SUBLANES), 0))
    tab_spec = pl.BlockSpec((tm, RET_DK), lambda bi, i: (i, 0))
    act = jax.ShapeDtypeStruct((b, l, GROUP), BF16)
    heads = jax.ShapeDtypeStruct((b, SB_HEADS, head_row0 + l, SB_HEAD_DIM), F32)
    return pl.pallas_call(
        _project_kernel,
        grid=grid,
        in_specs=[row_spec(d), _const_spec((1, d)), _const_spec((d, IN_WIDTH)), tab_spec, tab_spec],
        out_specs=[row_spec(GROUP)] * 3 + [head_spec] * 2 + [row_spec(GROUP)] * 4,
        out_shape=[act] * 3 + [heads] * 2 + [act] * 4,
        compiler_params=pltpu.CompilerParams(
            dimension_semantics=("arbitrary", "arbitrary"), vmem_limit_bytes=VMEM_LIMIT),
        name="project",
    )(x, g_norm.reshape(1, d), w_in_bf, cos2, sin2)


def _split_bf16(x):
    hi = x.astype(BF16)
    lo = (x - hi.astype(F32)).astype(BF16)
    return hi, lo


def _attention_kernel(q_ref, kc_ref, vc_ref, kp_ref, vp_ref, bdu_ref, bdo_ref, j_ref, g_ref,
                      o_ref, left_ref, kxc_ref, vxc_ref, kxp_ref, vxp_ref, carry_ref, acc_ref,
                      *, tq, n_q, n_past, past_valid, report_left):
    tk = SB_KEYS
    per_tile = tq // tk
    head0 = lax.broadcasted_iota(jnp.int32, (tk, LANES), 1) < SB_HEAD_DIM
    key_in_block = lax.broadcasted_iota(jnp.int32, (tq, 2 * tk), 1) & (tk - 1)
    delta = key_in_block - lax.broadcasted_iota(jnp.int32, (tq, 2 * tk), 0)

    pairs = range(SB_PAIRS)
    lanes = [slice(p * PAIR, (p + 1) * PAIR) for p in pairs]

    def expand(src_ref, dst_ref, n_blocks):
        def body(j, _):
            for p in pairs:
                blk = src_ref[0, pl.ds(pl.multiple_of(j * tk, tk), tk), lanes[p]]
                zero = jnp.zeros_like(blk)
                dst_ref[p, j, 0:tk, :] = jnp.where(head0, blk, zero)
                dst_ref[p, j, tk:2 * tk, :] = jnp.where(head0, zero, blk)
            return 0
        lax.fori_loop(0, n_blocks, body, 0)

    expand(kc_ref, kxc_ref, n_q * per_tile)
    expand(vc_ref, vxc_ref, n_q * per_tile)
    if n_past > 0:
        expand(kp_ref, kxp_ref, n_past)
        expand(vp_ref, vxp_ref, n_past)

    def add_blocks(kx_ref, vx_ref, i, j_last, masks, r0=0, r1=tq):
        rows = pl.ds(pl.multiple_of(i * tq + r0, tk), r1 - r0)
        carry = [carry_ref[p, r0:r1, :] for p in pairs]
        for u, mask in enumerate(masks):
            j = j_last - u
            for p in pairs:
                z = lax.dot_general(q_ref[0, rows, lanes[p]], kx_ref[p, j], (((1,), (1,)), ((), ())),
                                    preferred_element_type=F32)
                sp = jnp.maximum(z, 0.0) + jnp.log2(1.0 + jnp.exp2(-jnp.abs(z)))
                spm = sp if mask is None else jnp.where(mask, sp, 0.0)
                hi = spm.astype(BF16)
                later = jnp.dot(hi, bdu_ref[...], preferred_element_type=F32)
                total = jnp.dot(hi, bdo_ref[...], preferred_element_type=F32)
                a = jnp.exp2((z - sp) - (later + carry[p]))
                if mask is not None:
                    a = jnp.where(mask, a, 0.0)
                acc_ref[p, r0:r1, :] += jnp.dot(a.astype(BF16), vx_ref[p, j], preferred_element_type=F32)
                carry[p] = carry[p] + total
        for p in pairs:
            carry_ref[p, r0:r1, :] = carry[p]
        return carry

    def stick_left(carry):
        least = functools.reduce(jnp.minimum, carry)
        return (jnp.min(least) < SB_EXHAUSTED).astype(jnp.int32)

    def sweep(kx_ref, vx_ref, i, j_top, n_steps, unroll, alive):
        def cond(state):
            m, live = state
            return jnp.logical_and(m < n_steps, live > 0)

        def body(state):
            m, _ = state
            carry = add_blocks(kx_ref, vx_ref, i, j_top - m * unroll, [None] * unroll)
            return m + 1, stick_left(carry)

        return lax.while_loop(cond, body, (jnp.int32(0), alive))[1]

    def all_carries(r0=0):
        return [carry_ref[p, r0:, :] for p in pairs]

    def q_tile(i, any_left, first):
        rows = pl.ds(pl.multiple_of(i * tq, tq), tq)
        carry_ref[...] = jnp.zeros_like(carry_ref)
        acc_ref[...] = jnp.zeros_like(acc_ref)

        for jj in range(per_tile - 1, -1, -1):
            r0 = jj * tk
            add_blocks(kxc_ref, vxc_ref, i, i * per_tile + jj, [delta[r0:, :] < -r0], r0)

        unroll = SB_UNROLL if per_tile % SB_UNROLL == 0 else 1
        top = unroll * tk

        def past_blocks(alive):
            if n_past == 0:
                return
            n_full = n_past
            if past_valid < tk:
                @pl.when(alive > 0)
                def _():
                    add_blocks(kxp_ref, vxp_ref, i, n_past - 1, [key_in_block < past_valid])

                n_full = n_past - 1
                if n_full > 0:
                    alive = stick_left(all_carries())
            if n_full // SB_UNROLL > 0:
                alive = sweep(kxp_ref, vxp_ref, i, n_full - 1, n_full // SB_UNROLL, SB_UNROLL, alive)
            if n_full % SB_UNROLL:
                sweep(kxp_ref, vxp_ref, i, n_full % SB_UNROLL - 1, n_full % SB_UNROLL, 1, alive)

        if first:
            past_blocks(stick_left(all_carries()))
        elif top < tq:
            j_prev = i * per_tile - 1
            top_left = stick_left(add_blocks(kxc_ref, vxc_ref, i, j_prev, [None] * unroll, 0, top))
            low_left = stick_left(all_carries(top))

            @pl.when(jnp.maximum(top_left, low_left) > 0)
            def _():
                low_now = lax.cond(
                    low_left > 0,
                    lambda: stick_left(add_blocks(kxc_ref, vxc_ref, i, j_prev, [None] * unroll, top, tq)),
                    lambda: jnp.int32(0))
                past_blocks(sweep(kxc_ref, vxc_ref, i, j_prev - unroll, (i * per_tile) // unroll - 1, unroll,
                                  jnp.maximum(top_left, low_now)))
        else:
            past_blocks(sweep(kxc_ref, vxc_ref, i, i * per_tile - 1, (i * per_tile) // unroll, unroll,
                              stick_left(all_carries())))


        for p in pairs:
            o = acc_ref[p]
            hi, lo = _split_bf16(o * o)
            jm = j_ref[...]
            ms = (jnp.dot(hi, jm, preferred_element_type=F32)
                  + jnp.dot(lo, jm, preferred_element_type=F32)) * (1.0 / SB_HEAD_DIM)
            y = o * lax.rsqrt(ms + EPS) * g_ref[:, lanes[p]]
            o_ref[0, rows, lanes[p]] = y.astype(BF16)
        if not report_left:
            return any_left
        return jnp.maximum(any_left, stick_left(all_carries()))

    any_left = q_tile(jnp.int32(0), jnp.int32(0), True)
    any_left = lax.fori_loop(1, n_q, lambda i, left: q_tile(i, left, False), any_left)
    left_ref[...] = jnp.full(left_ref.shape, any_left, jnp.int32)


def _attention(q, k_cur, v_cur, k_past, v_past, past_len, g_sb_out, report_left=False):
    b, lq, _ = q.shape
    tk = SB_KEYS
    tq = min(SB_QUERY_TILE, lq)
    assert lq % tq == 0 and tq % tk == 0
    if k_past is None:
        k_past = jnp.zeros((1, tk, SB_WIDTH), BF16)
        v_past = jnp.zeros((1, tk, SB_WIDTH), BF16)
        n_past, past_valid = 0, tk
    else:
        assert k_past.shape[1] % tk == 0
        n_past = -(-past_len // tk)
        past_valid = past_len - (n_past - 1) * tk
        k_past = k_past[:, :n_past * tk]
        v_past = v_past[:, :n_past * tk]
    bp, p, _ = k_past.shape
    jj = jnp.arange(2 * tk)
    same_head = (jj[:, None] // tk) == (jj[None, :] // tk)
    bdu = (same_head & (jj[:, None] > jj[None, :])).astype(BF16)
    bdo = same_head.astype(BF16)
    ll = jnp.arange(LANES) // SB_HEAD_DIM
    jm = (ll[:, None] == ll[None, :]).astype(BF16)
    width = SB_PAIRS * PAIR
    cur_spec = pl.BlockSpec((1, lq, width), lambda bi, hp: (bi, 0, hp))
    past_spec = pl.BlockSpec((1, p, width), (lambda bi, hp: (bi, 0, hp)) if bp > 1
                             else (lambda bi, hp: (0, 0, hp)))
    kernel = functools.partial(_attention_kernel, tq=tq, n_q=lq // tq, n_past=n_past,
                               past_valid=past_valid, report_left=report_left)
    stacked = lambda n: pltpu.VMEM((SB_PAIRS, n, 2 * tk, PAIR), BF16)
    n_groups = SB_WIDTH // width
    out, left = pl.pallas_call(
        kernel,
        grid=(b, n_groups),
        in_specs=[cur_spec, cur_spec, cur_spec, past_spec, past_spec,
                  _const_spec(bdu.shape), _const_spec(bdo.shape), _const_spec(jm.shape),
                  pl.BlockSpec((1, width), lambda bi, hp: (0, hp))],
        out_specs=[cur_spec, pl.BlockSpec((1, 1, SUBLANES, LANES), lambda bi, hp: (bi, hp, 0, 0))],
        out_shape=[jax.ShapeDtypeStruct((b, lq, SB_WIDTH), BF16),
                   jax.ShapeDtypeStruct((b, n_groups, SUBLANES, LANES), jnp.int32)],
        scratch_shapes=[stacked(lq // tk), stacked(lq // tk), stacked(p // tk), stacked(p // tk),
                        pltpu.VMEM((SB_PAIRS, tq, 2 * tk), F32), pltpu.VMEM((SB_PAIRS, tq, PAIR), F32)],
        compiler_params=pltpu.CompilerParams(
            dimension_semantics=("arbitrary", "arbitrary"), vmem_limit_bytes=VMEM_LIMIT),
        name="attention",
    )(q, k_cur, v_cur, k_past, v_past, bdu, bdo, jm, g_sb_out.reshape(1, SB_WIDTH))
    return out, left


def _attention_over_cache(q, k_cur, v_cur, cache_k, cache_v, g_sb_out):
    bd, _, past, _ = cache_k.shape
    to_rows = lambda c: c.transpose(0, 2, 1, 3).reshape(bd, c.shape[2], SB_WIDTH).astype(BF16)
    recent = min(past, SB_RECENT)
    out, left = _attention(q, k_cur, v_cur, to_rows(cache_k[:, :, past - recent:]),
                           to_rows(cache_v[:, :, past - recent:]), recent, g_sb_out, report_left=True)
    if recent == past:
        return out
    return lax.cond(
        jnp.any(left > 0),
        lambda: _attention(q, k_cur, v_cur, to_rows(cache_k), to_rows(cache_v), past, g_sb_out)[0],
        lambda: out)


def _retention_kernel(q_ref, k_ref, v_ref, gate_ref, s0_ref, g_ref, o_ref, s_ref, *, chunk, n_chunks):
    c = chunk
    head = pl.program_id(1).astype(F32)
    log_g = jnp.log(1.0 - jnp.exp2(jnp.full((1, LANES), -5.0, F32) - head))
    n_row = lax.broadcasted_iota(jnp.int32, (c, LANES), 0).astype(F32)
    q_decay = jnp.exp((n_row + 1.0) * log_g)
    k_decay = jnp.exp((c - 1.0 - n_row) * log_g)
    s_decay = jnp.exp(float(c) * log_g)
    diff = (lax.broadcasted_iota(jnp.int32, (c, c), 0)
            - lax.broadcasted_iota(jnp.int32, (c, c), 1))
    log_g_cc = log_g if c == LANES else jnp.log(1.0 - jnp.exp2(jnp.full((1, c), -5.0, F32) - head))
    decay = jnp.where(diff >= 0, jnp.exp(jnp.maximum(diff, 0).astype(F32) * log_g_cc), 0.0)
    gain = g_ref[...]

    def body(ci, s):
        rows = pl.ds(pl.multiple_of(ci * c, c), c)
        q = q_ref[0, rows, :]
        k = k_ref[0, rows, :]
        v = v_ref[0, rows, :]
        qk = lax.dot_general(q, k, (((1,), (1,)), ((), ())), preferred_element_type=F32)
        inner = jnp.dot((qk * decay).astype(BF16), v, preferred_element_type=F32)
        cross = jnp.dot(q, s.astype(BF16), preferred_element_type=F32) * q_decay
        o = inner + cross
        k_dec = (k.astype(F32) * k_decay).astype(BF16)
        s_new = s_decay * s + lax.dot_general(k_dec, v, (((0,), (0,)), ((), ())),
                                              preferred_element_type=F32)
        y = o * lax.rsqrt(jnp.mean(o * o, axis=-1, keepdims=True) + EPS) * gain
        half = 0.5 * gate_ref[0, rows, :].astype(F32)
        o_ref[0, rows, :] = (y * (half * (1.0 + jnp.tanh(half)))).astype(BF16)
        return s_new

    s_ref[0, 0] = lax.fori_loop(0, n_chunks, body, s0_ref[0, 0], unroll=math.gcd(n_chunks, RET_UNROLL))


def _retention(rq, rk, rv, rgate, s0, g_ret_out, chunk):
    b, l, _ = rq.shape
    assert l % chunk == 0
    bs = s0.shape[0]
    seq_spec = pl.BlockSpec((1, l, RET_DK), lambda bi, hh: (bi, 0, hh))
    state_spec = pl.BlockSpec((1, 1, RET_DK, RET_DV), lambda bi, hh: (bi, hh, 0, 0))
    s0_spec = state_spec if bs > 1 else pl.BlockSpec((1, 1, RET_DK, RET_DV), lambda bi, hh: (0, hh, 0, 0))
    kernel = functools.partial(_retention_kernel, chunk=chunk, n_chunks=l // chunk)
    return pl.pallas_call(
        kernel,
        grid=(b, RET_HEADS),
        in_specs=[seq_spec, seq_spec, seq_spec, seq_spec, s0_spec,
                  pl.BlockSpec((1, RET_DV), lambda bi, hh: (0, hh))],
        out_specs=[seq_spec, state_spec],
        out_shape=[jax.ShapeDtypeStruct((b, l, RET_WIDTH), BF16),
                   jax.ShapeDtypeStruct((b, RET_HEADS, RET_DK, RET_DV), F32)],
        compiler_params=pltpu.CompilerParams(
            dimension_semantics=("arbitrary", "arbitrary"), vmem_limit_bytes=VMEM_LIMIT),
        name="retention",
    )(rq, rk, rv, rgate, s0, g_ret_out.reshape(1, RET_WIDTH))


def _merge_ffn_kernel(x_ref, sb_ref, ret_ref, conv0_ref, wo_ref, gf_ref, wu_ref,
                      cw_ref, cb_ref, wd_ref, gl_ref, y_ref, conv_ref,
                      h_ref, hn_ref, ubuf_ref, act_ref, carry_ref, *, tm, seg):
    i = pl.program_id(1)

    @pl.when(i == 0)
    def _():
        carry_ref[...] = conv0_ref[0]

    mixed = jnp.concatenate([sb_ref[0], ret_ref[0]], axis=1)
    h = x_ref[0] + jnp.dot(mixed, wo_ref[...], preferred_element_type=F32)
    h_ref[...] = h
    ms = jnp.mean(h * h, axis=-1, keepdims=True)
    hn_ref[...] = (h * lax.rsqrt(ms + EPS) * gf_ref[...]).astype(BF16)

    n_seg = tm // seg
    ext_rows = seg + SUBLANES

    def up_project(j):
        ubuf = ubuf_ref.at[j % 2]
        hn = hn_ref[...]
        gate_cols = slice(j * FF_CHUNK, (j + 1) * FF_CHUNK)
        val_cols = slice(D_FF + j * FF_CHUNK, D_FF + (j + 1) * FF_CHUNK)
        gate = jnp.dot(hn, wu_ref[:, gate_cols], preferred_element_type=F32)
        val = jnp.dot(hn, wu_ref[:, val_cols], preferred_element_type=F32)
        for s in range(n_seg):
            base = s * ext_rows
            prev = slice(s * SUBLANES, (s + 1) * SUBLANES)
            ubuf[base:base + SUBLANES, :] = carry_ref[j, prev, :]
            ubuf[base + SUBLANES:base + ext_rows, 0:FF_CHUNK] = gate[s * seg:(s + 1) * seg]
            ubuf[base + SUBLANES:base + ext_rows, FF_CHUNK:] = val[s * seg:(s + 1) * seg]
            carry_ref[j, prev, :] = ubuf[base + seg:base + ext_rows, :]

    def gated_conv(j):
        ubuf = ubuf_ref.at[j % 2]
        cw = cw_ref[j]
        cb = cb_ref[j]
        strip = min(seg, FF_STRIP)
        for s in range(n_seg):
            for r in range(0, seg, strip):
                first = s * ext_rows + r
                ext = ubuf[first:first + strip + SUBLANES, :]
                c = cb
                for tap in range(CONV_W):
                    back = CONV_W - 1 - tap
                    rows = ext if back == 0 else pltpu.roll(ext, back, 0)
                    c = c + cw[tap:tap + 1, :] * rows[SUBLANES:, :]
                half = 0.5 * c[:, :FF_CHUNK]
                act_ref[s * seg + r:s * seg + r + strip, j * FF_CHUNK:(j + 1) * FF_CHUNK] = (
                    (half * (1.0 + jnp.tanh(half))) * c[:, FF_CHUNK:]).astype(BF16)

    up_project(0)
    for j in range(N_FF_CHUNKS):
        if j + 1 < N_FF_CHUNKS:
            up_project(j + 1)
        gated_conv(j)

    hh = h_ref[...] + jnp.dot(act_ref[...], wd_ref[...], preferred_element_type=F32)
    ms2 = jnp.mean(hh * hh, axis=-1, keepdims=True)
    y_ref[0] = hh * lax.rsqrt(ms2 + EPS) * gl_ref[...]

    @pl.when(i == pl.num_programs(1) - 1)
    def _():
        conv_ref[0] = carry_ref[...]


def _merge_ffn(x, sb_n, ret_n, conv0, w_out_bf, g_norm_ffn, w_up_bf, cw, cb, wd, g_norm_final, tm, seg=None):
    b, l, d = x.shape
    bs = conv0.shape[0]
    seg = tm if seg is None else seg
    assert tm % seg == 0 and (seg == tm or l == tm)
    row_spec = lambda w: pl.BlockSpec((1, tm, w), lambda bi, i: (bi, i, 0))
    conv_shape = (1, N_FF_CHUNKS, SUBLANES * (tm // seg), 2 * FF_CHUNK)
    conv_spec = pl.BlockSpec(conv_shape, lambda bi, i: (bi, 0, 0, 0))
    conv0_spec = conv_spec if bs > 1 else pl.BlockSpec(conv_shape, lambda bi, i: (0, 0, 0, 0))
    kernel = functools.partial(_merge_ffn_kernel, tm=tm, seg=seg)
    return pl.pallas_call(
        kernel,
        grid=(b, l // tm),
        in_specs=[row_spec(d), row_spec(SB_WIDTH), row_spec(RET_WIDTH), conv0_spec,
                  _const_spec(w_out_bf.shape), _const_spec((1, d)),
                  _const_spec(w_up_bf.shape),
                  _const_spec(cw.shape), _const_spec(cb.shape), _const_spec(wd.shape),
                  _const_spec((1, d))],
        out_specs=[row_spec(d), conv_spec],
        out_shape=[jax.ShapeDtypeStruct((b, l, d), F32),
                   jax.ShapeDtypeStruct((b,) + conv_shape[1:], F32)],
        scratch_shapes=[pltpu.VMEM((tm, d), F32), pltpu.VMEM((tm, d), BF16),
                        pltpu.VMEM((2, tm + SUBLANES * (tm // seg), 2 * FF_CHUNK), F32),
                        pltpu.VMEM((tm, D_FF), BF16),
                        pltpu.VMEM(conv_shape[1:], F32)],
        compiler_params=pltpu.CompilerParams(
            dimension_semantics=("arbitrary", "arbitrary"), vmem_limit_bytes=VMEM_LIMIT),
        name="merge_ffn",
    )(x, sb_n, ret_n, conv0, w_out_bf, g_norm_ffn.reshape(1, d), w_up_bf, cw, cb, wd,
      g_norm_final.reshape(1, d))


def _rope_tables(pos):
    half = RET_DK // 2
    inv = ROPE_BASE ** (-jnp.arange(half, dtype=F32) / half)
    ang = pos.astype(F32)[:, None] * inv[None, :]
    cos, sin = jnp.cos(ang), jnp.sin(ang)
    return jnp.concatenate([cos, cos], axis=1), jnp.concatenate([-sin, sin], axis=1)


def _conv_state_to_chunks(state):
    b = state.shape[0]
    s = state.reshape(b, CONV_W - 1, 2, N_FF_CHUNKS, FF_CHUNK).transpose(0, 3, 1, 2, 4)
    s = s.reshape(b, N_FF_CHUNKS, CONV_W - 1, 2 * FF_CHUNK)
    return jnp.pad(s, ((0, 0), (0, 0), (SUBLANES - (CONV_W - 1), 0), (0, 0)))


def _conv_state_from_chunks(chunks):
    b = chunks.shape[0]
    s = chunks[:, :, SUBLANES - (CONV_W - 1):, :].reshape(b, N_FF_CHUNKS, CONV_W - 1, 2, FF_CHUNK)
    return s.transpose(0, 2, 3, 1, 4).reshape(b, CONV_W - 1, 2 * D_FF)


def _pad_rows(a, rows):
    return a if a.shape[1] == rows else jnp.pad(a, ((0, 0), (0, rows - a.shape[1]), (0, 0)))


def _prefix_rows_kernel(k_any, v_any, k_rows_ref, v_rows_ref, k_ref, v_ref):
    del k_any, v_any
    k_ref[...] = k_rows_ref[...]
    v_ref[...] = v_rows_ref[...]


def _write_prefix_rows(k_big, v_big, k_rows, v_rows):
    b = k_big.shape[0]
    blk = (1,) + k_rows.shape[1:]
    any_spec = pl.BlockSpec(memory_space=pl.ANY)
    rows_spec = pl.BlockSpec(blk, lambda bi: (0, 0, 0, 0))
    out_spec = pl.BlockSpec(blk, lambda bi: (bi, 0, 0, 0))
    big = jax.ShapeDtypeStruct(k_big.shape, k_big.dtype)
    return pl.pallas_call(
        _prefix_rows_kernel,
        grid=(b,),
        in_specs=[any_spec, any_spec, rows_spec, rows_spec],
        out_specs=[out_spec, out_spec],
        out_shape=[big, big],
        input_output_aliases={0: 0, 1: 1},
        name="prefix_rows",
    )(k_big, v_big, k_rows, v_rows)


def _stream_step(x, pos0, attend, s0, conv0, wts, tm, chunk, head_row0=0, stack=False):
    b, l, d = x.shape
    cos2, sin2 = _rope_tables(pos0 + jnp.arange(l))
    tm_project = PROJECT_ROWS if l % PROJECT_ROWS == 0 else tm
    flat = lambda a: a.reshape(1, b * l, a.shape[-1])
    if stack:
        q, k, v, k_out, v_out, rq, rk, rv, rgate = _project(
            flat(x), wts["g_norm_mix"], wts["w_in"], jnp.tile(cos2, (b, 1)), jnp.tile(sin2, (b, 1)), b * l)
        q, k, v, rq, rk, rv, rgate = (a.reshape(b, l, a.shape[-1]) for a in (q, k, v, rq, rk, rv, rgate))
        k_out, v_out = (a.reshape(SB_HEADS, b, l, SB_HEAD_DIM).transpose(1, 0, 2, 3) for a in (k_out, v_out))
    else:
        q, k, v, k_out, v_out, rq, rk, rv, rgate = _project(x, wts["g_norm_mix"], wts["w_in"], cos2, sin2,
                                                           tm_project, head_row0)
    lq = -(-l // SB_KEYS) * SB_KEYS
    sb_n = attend(_pad_rows(q, lq), _pad_rows(k, lq), _pad_rows(v, lq))[:, :l]
    ret_n, s_new = _retention(rq, rk, rv, rgate, s0, wts["g_ret_out"], chunk)
    ffn = functools.partial(_merge_ffn, w_out_bf=wts["w_out"], g_norm_ffn=wts["g_norm_ffn"], w_up_bf=wts["w_up"],
                            cw=wts["cw"], cb=wts["cb"], wd=wts["wd"], g_norm_final=wts["g_norm_final"])
    if stack:
        groups = conv0.shape[2]
        conv0 = conv0.transpose(1, 0, 2, 3).reshape(1, N_FF_CHUNKS, b * groups, 2 * FF_CHUNK)
        y, conv_new = ffn(flat(x), flat(sb_n), flat(ret_n), conv0, tm=b * l, seg=l)
        y = y.reshape(b, l, d)
        conv_new = conv_new.reshape(N_FF_CHUNKS, b, groups, 2 * FF_CHUNK).transpose(1, 0, 2, 3)
    else:
        y, conv_new = ffn(x, sb_n, ret_n, conv0, tm=tm)
    return y, k, v, k_out, v_out, s_new, conv_new


def kernel(x_prompt, x_sample, cache_sb_k, cache_sb_v, state_ret, state_conv, meta_tokens, g_norm_mix, w_in, g_sb_out, g_ret_out, w_out, g_norm_ffn, w_up, conv_w, conv_b, w_down, g_norm_final):
    b, seq, d = x_prompt.shape
    bd, ls, _ = x_sample.shape
    past = cache_sb_k.shape[2]

    def ff_cols(a):
        r = a.shape[0]
        return a.reshape(r, 2, N_FF_CHUNKS, FF_CHUNK).transpose(2, 0, 1, 3).reshape(N_FF_CHUNKS, r, 2 * FF_CHUNK)

    wts = dict(
        g_norm_mix=g_norm_mix, g_sb_out=g_sb_out, g_ret_out=g_ret_out, g_norm_ffn=g_norm_ffn,
        g_norm_final=g_norm_final,
        w_in=w_in.astype(BF16), w_out=w_out.astype(BF16),
        w_up=w_up.astype(BF16),
        cw=ff_cols(conv_w), cb=ff_cols(conv_b.reshape(1, 2 * D_FF)),
        wd=w_down.astype(BF16),
    )

    zero_state = jnp.zeros((1, RET_HEADS, RET_DK, RET_DV), F32)
    zero_conv = jnp.zeros((1, N_FF_CHUNKS, SUBLANES, 2 * FF_CHUNK), F32)
    gain = wts["g_sb_out"]
    _, k_m, v_m, k_m_out, v_m_out, s_meta, conv_meta = _stream_step(
        meta_tokens[None], -N_META, lambda q, k, v: _attention(q, k, v, None, None, 0, gain)[0],
        zero_state, zero_conv, wts, N_META, N_META)

    k_m, v_m = _pad_rows(k_m, SB_KEYS), _pad_rows(v_m, SB_KEYS)
    y_prompt, _, _, k_p_out, v_p_out, s_prompt, conv_prompt = _stream_step(
        x_prompt, 0, lambda q, k, v: _attention(q, k, v, k_m, v_m, N_META, gain)[0],
        s_meta, conv_meta, wts, 512, 256, head_row0=N_META)
    new_k_prompt, new_v_prompt = _write_prefix_rows(k_p_out, v_p_out, k_m_out, v_m_out)

    y_sample, _, _, k_s_out, v_s_out, s_sample, conv_sample = _stream_step(
        x_sample, past, lambda q, k, v: _attention_over_cache(q, k, v, cache_sb_k, cache_sb_v, gain),
        state_ret, _conv_state_to_chunks(state_conv), wts, ls, ls, stack=True)

    return (y_prompt, y_sample, new_k_prompt, new_v_prompt, s_prompt,
            _conv_state_from_chunks(conv_prompt), k_s_out, v_s_out, s_sample,
            _conv_state_from_chunks(conv_sample))
```

```python
import functools
import math

import jax
import jax.numpy as jnp
from jax import lax
from jax.experimental import pallas as pl
from jax.experimental.pallas import tpu as pltpu

D_MODEL = 1024
N_META = 16
SB_HEADS = 8
SB_HEAD_DIM = 64
SB_WIDTH = SB_HEADS * SB_HEAD_DIM
RET_HEADS = 4
RET_DK = 128
RET_DV = 128
RET_WIDTH = RET_HEADS * RET_DV
MIX_WIDTH = SB_WIDTH + RET_WIDTH
GROUP = 512
N_GROUPS = 7
IN_WIDTH = N_GROUPS * GROUP
D_FF = 2816
CONV_W = 3
ROPE_BASE = 10000.0
EPS = 1e-5

LANES = 128
SUBLANES = 8
FF_CHUNK = 256
N_FF_CHUNKS = D_FF // FF_CHUNK
FF_STRIP = 64
RET_UNROLL = 8
PROJECT_ROWS = 1024
SB_KEYS = 128
SB_QUERY_TILE = 512
SB_UNROLL = 2
PAIR = 2 * SB_HEAD_DIM
SB_PAIRS = 2
SB_RECENT = 512
SB_EXHAUSTED = 152.0
SB_Q_SCALE = SB_HEAD_DIM ** -0.5 * math.log2(math.e)
VMEM_LIMIT = 56 * 1024 * 1024

BF16 = jnp.bfloat16
F32 = jnp.float32


def _const_spec(shape):
    zeros = (0,) * len(shape)
    return pl.BlockSpec(shape, lambda *_: zeros, pipeline_mode=pl.Buffered(1))


def _project_kernel(x_ref, g_ref, w_ref, cos_ref, sin_ref,
                    q_ref, k_ref, v_ref, ko_ref, vo_ref, rq_ref, rk_ref, rv_ref, rg_ref):
    x = x_ref[0]
    ms = jnp.mean(x * x, axis=-1, keepdims=True)
    h = (x * lax.rsqrt(ms + EPS) * g_ref[...]).astype(BF16)

    def group(i):
        return jnp.dot(h, w_ref[:, i * GROUP:(i + 1) * GROUP], preferred_element_type=F32)

    def split_heads(p, out_ref):
        for hh in range(SB_HEADS):
            out_ref[0, hh] = p[:, hh * SB_HEAD_DIM:(hh + 1) * SB_HEAD_DIM]

    def rope(p, out_ref, scale):
        cos = cos_ref[...]
        sin = sin_ref[...]
        for hh in range(RET_HEADS):
            t = p[:, hh * RET_DK:(hh + 1) * RET_DK]
            r = t * cos + pltpu.roll(t, RET_DK // 2, 1) * sin
            if scale is not None:
                r = r * scale
            out_ref[0, :, hh * RET_DK:(hh + 1) * RET_DK] = r.astype(BF16)

    q_ref[0] = (group(0) * SB_Q_SCALE).astype(BF16)
    pk = group(1)
    k_ref[0] = pk.astype(BF16)
    split_heads(pk, ko_ref)
    pv = group(2)
    v_ref[0] = pv.astype(BF16)
    split_heads(pv, vo_ref)
    rope(group(3), rq_ref, None)
    rope(group(4), rk_ref, RET_DK ** -0.5)
    rv_ref[0] = group(5).astype(BF16)
    rg_ref[0] = group(6).astype(BF16)


def _project(x, g_norm, w_in_bf, cos2, sin2, tm, head_row0=0):
    b, l, d = x.shape
    grid = (b, l // tm)
    row_spec = lambda w: pl.BlockSpec((1, tm, w), lambda bi, i: (bi, i, 0))
    head_blk = (1, SB_HEADS, tm, SB_HEAD_DIM)
    head_spec = pl.BlockSpec(tuple(pl.Element(n) for n in head_blk),
                             lambda bi, i: (bi, 0, pl.multiple_of(head_row0 + i * tm, SUBLANES), 0))
    tab_spec = pl.BlockSpec((tm, RET_DK), lambda bi, i: (i, 0))
    act = jax.ShapeDtypeStruct((b, l, GROUP), BF16)
    heads = jax.ShapeDtypeStruct((b, SB_HEADS, head_row0 + l, SB_HEAD_DIM), F32)
    return pl.pallas_call(
        _project_kernel,
        grid=grid,
        in_specs=[row_spec(d), _const_spec((1, d)), _const_spec((d, IN_WIDTH)), tab_spec, tab_spec],
        out_specs=[row_spec(GROUP)] * 3 + [head_spec] * 2 + [row_spec(GROUP)] * 4,
        out_shape=[act] * 3 + [heads] * 2 + [act] * 4,
        compiler_params=pltpu.CompilerParams(
            dimension_semantics=("arbitrary", "arbitrary"), vmem_limit_bytes=VMEM_LIMIT),
        name="project",
    )(x, g_norm.reshape(1, d), w_in_bf, cos2, sin2)


def _split_bf16(x):
    hi = x.astype(BF16)
    lo = (x - hi.astype(F32)).astype(BF16)
    return hi, lo


def _attention_kernel(q_ref, kc_ref, vc_ref, kp_ref, vp_ref, bdu_ref, bdo_ref, j_ref, g_ref,
                      o_ref, left_ref, kxc_ref, vxc_ref, kxp_ref, vxp_ref, carry_ref, acc_ref,
                      *, tq, n_q, n_past, past_valid, report_left):
    tk = SB_KEYS
    per_tile = tq // tk
    head0 = lax.broadcasted_iota(jnp.int32, (tk, LANES), 1) < SB_HEAD_DIM
    key_in_block = lax.broadcasted_iota(jnp.int32, (tq, 2 * tk), 1) & (tk - 1)
    delta = key_in_block - lax.broadcasted_iota(jnp.int32, (tq, 2 * tk), 0)

    pairs = range(SB_PAIRS)
    lanes = [slice(p * PAIR, (p + 1) * PAIR) for p in pairs]

    def expand_block(src_ref, dst_ref, j):
        for p in pairs:
            blk = src_ref[0, pl.ds(pl.multiple_of(j * tk, tk), tk), lanes[p]]
            zero = jnp.zeros_like(blk)
            dst_ref[p, j, 0:tk, :] = jnp.where(head0, blk, zero)
            dst_ref[p, j, tk:2 * tk, :] = jnp.where(head0, zero, blk)

    def expand(src_ref, dst_ref, n_blocks):
        def body(j, _):
            expand_block(src_ref, dst_ref, j)
            return 0
        lax.fori_loop(0, n_blocks, body, 0)

    if n_past > 0:
        expand(kp_ref, kxp_ref, n_past)
        expand(vp_ref, vxp_ref, n_past)

    def add_blocks(kx_ref, vx_ref, i, j_last, masks, r0=0, r1=tq):
        rows = pl.ds(pl.multiple_of(i * tq + r0, tk), r1 - r0)
        carry = [carry_ref[p, r0:r1, :] for p in pairs]
        for u, mask in enumerate(masks):
            j = j_last - u
            for p in pairs:
                z = lax.dot_general(q_ref[0, rows, lanes[p]], kx_ref[p, j], (((1,), (1,)), ((), ())),
                                    preferred_element_type=F32)
                sp = jnp.maximum(z, 0.0) + jnp.log2(1.0 + jnp.exp2(-jnp.abs(z)))
                spm = sp if mask is None else jnp.where(mask, sp, 0.0)
                hi = spm.astype(BF16)
                later = jnp.dot(hi, bdu_ref[...], preferred_element_type=F32)
                total = jnp.dot(hi, bdo_ref[...], preferred_element_type=F32)
                a = jnp.exp2((z - sp) - (later + carry[p]))
                if mask is not None:
                    a = jnp.where(mask, a, 0.0)
                acc_ref[p, r0:r1, :] += jnp.dot(a.astype(BF16), vx_ref[p, j], preferred_element_type=F32)
                carry[p] = carry[p] + total
        for p in pairs:
            carry_ref[p, r0:r1, :] = carry[p]
        return carry

    def stick_left(carry):
        least = functools.reduce(jnp.minimum, carry)
        return (jnp.min(least) < SB_EXHAUSTED).astype(jnp.int32)

    def sweep(kx_ref, vx_ref, i, j_top, n_steps, unroll, alive):
        def cond(state):
            m, live = state
            return jnp.logical_and(m < n_steps, live > 0)

        def body(state):
            m, _ = state
            carry = add_blocks(kx_ref, vx_ref, i, j_top - m * unroll, [None] * unroll)
            return m + 1, stick_left(carry)

        return lax.while_loop(cond, body, (jnp.int32(0), alive))[1]

    def all_carries(r0=0):
        return [carry_ref[p, r0:, :] for p in pairs]

    def q_tile(i, any_left, first):
        rows = pl.ds(pl.multiple_of(i * tq, tq), tq)
        carry_ref[...] = jnp.zeros_like(carry_ref)
        acc_ref[...] = jnp.zeros_like(acc_ref)
        for jj in range(per_tile):
            expand_block(kc_ref, kxc_ref, i * per_tile + jj)
            expand_block(vc_ref, vxc_ref, i * per_tile + jj)

        for jj in range(per_tile - 1, -1, -1):
            r0 = jj * tk
            add_blocks(kxc_ref, vxc_ref, i, i * per_tile + jj, [delta[r0:, :] < -r0], r0)

        unroll = SB_UNROLL if per_tile % SB_UNROLL == 0 else 1
        top = unroll * tk
        if first:
            alive = stick_left(all_carries())
        elif top < tq:
            j_prev = i * per_tile - 1
            top_left = stick_left(add_blocks(kxc_ref, vxc_ref, i, j_prev, [None] * unroll, 0, top))
            low_left = lax.cond(
                stick_left(all_carries(top)) > 0,
                lambda: stick_left(add_blocks(kxc_ref, vxc_ref, i, j_prev, [None] * unroll, top, tq)),
                lambda: jnp.int32(0))
            alive = sweep(kxc_ref, vxc_ref, i, j_prev - unroll, (i * per_tile) // unroll - 1, unroll,
                          jnp.maximum(top_left, low_left))
        else:
            alive = sweep(kxc_ref, vxc_ref, i, i * per_tile - 1, (i * per_tile) // unroll, unroll,
                          stick_left(all_carries()))

        if n_past > 0:
            n_full = n_past
            if past_valid < tk:
                @pl.when(alive > 0)
                def _():
                    add_blocks(kxp_ref, vxp_ref, i, n_past - 1, [key_in_block < past_valid])

                n_full = n_past - 1
                if n_full > 0:
                    alive = stick_left(all_carries())
            if n_full // SB_UNROLL > 0:
                alive = sweep(kxp_ref, vxp_ref, i, n_full - 1, n_full // SB_UNROLL, SB_UNROLL, alive)
            if n_full % SB_UNROLL:
                alive = sweep(kxp_ref, vxp_ref, i, n_full % SB_UNROLL - 1, n_full % SB_UNROLL, 1, alive)

        for p in pairs:
            o = acc_ref[p]
            hi, lo = _split_bf16(o * o)
            jm = j_ref[...]
            ms = (jnp.dot(hi, jm, preferred_element_type=F32)
                  + jnp.dot(lo, jm, preferred_element_type=F32)) * (1.0 / SB_HEAD_DIM)
            y = o * lax.rsqrt(ms + EPS) * g_ref[:, lanes[p]]
            o_ref[0, rows, lanes[p]] = y.astype(BF16)
        if not report_left:
            return any_left
        return jnp.maximum(any_left, stick_left(all_carries()))

    any_left = q_tile(jnp.int32(0), jnp.int32(0), True)
    any_left = lax.fori_loop(1, n_q, lambda i, left: q_tile(i, left, False), any_left)
    left_ref[...] = jnp.full(left_ref.shape, any_left, jnp.int32)


def _attention(q, k_cur, v_cur, k_past, v_past, past_len, g_sb_out, report_left=False):
    b, lq, _ = q.shape
    tk = SB_KEYS
    tq = min(SB_QUERY_TILE, lq)
    assert lq % tq == 0 and tq % tk == 0
    if k_past is None:
        k_past = jnp.zeros((1, tk, SB_WIDTH), BF16)
        v_past = jnp.zeros((1, tk, SB_WIDTH), BF16)
        n_past, past_valid = 0, tk
    else:
        assert k_past.shape[1] % tk == 0
        n_past = -(-past_len // tk)
        past_valid = past_len - (n_past - 1) * tk
        k_past = k_past[:, :n_past * tk]
        v_past = v_past[:, :n_past * tk]
    bp, p, _ = k_past.shape
    jj = jnp.arange(2 * tk)
    same_head = (jj[:, None] // tk) == (jj[None, :] // tk)
    bdu = (same_head & (jj[:, None] > jj[None, :])).astype(BF16)
    bdo = same_head.astype(BF16)
    ll = jnp.arange(LANES) // SB_HEAD_DIM
    jm = (ll[:, None] == ll[None, :]).astype(BF16)
    width = SB_PAIRS * PAIR
    cur_spec = pl.BlockSpec((1, lq, width), lambda bi, hp: (bi, 0, hp))
    past_spec = pl.BlockSpec((1, p, width), (lambda bi, hp: (bi, 0, hp)) if bp > 1
                             else (lambda bi, hp: (0, 0, hp)))
    kernel = functools.partial(_attention_kernel, tq=tq, n_q=lq // tq, n_past=n_past,
                               past_valid=past_valid, report_left=report_left)
    stacked = lambda n: pltpu.VMEM((SB_PAIRS, n, 2 * tk, PAIR), BF16)
    n_groups = SB_WIDTH // width
    out, left = pl.pallas_call(
        kernel,
        grid=(b, n_groups),
        in_specs=[cur_spec, cur_spec, cur_spec, past_spec, past_spec,
                  _const_spec(bdu.shape), _const_spec(bdo.shape), _const_spec(jm.shape),
                  pl.BlockSpec((1, width), lambda bi, hp: (0, hp))],
        out_specs=[cur_spec, pl.BlockSpec((1, 1, SUBLANES, LANES), lambda bi, hp: (bi, hp, 0, 0))],
        out_shape=[jax.ShapeDtypeStruct((b, lq, SB_WIDTH), BF16),
                   jax.ShapeDtypeStruct((b, n_groups, SUBLANES, LANES), jnp.int32)],
        scratch_shapes=[stacked(lq // tk), stacked(lq // tk), stacked(p // tk), stacked(p // tk),
                        pltpu.VMEM((SB_PAIRS, tq, 2 * tk), F32), pltpu.VMEM((SB_PAIRS, tq, PAIR), F32)],
        compiler_params=pltpu.CompilerParams(
            dimension_semantics=("arbitrary", "arbitrary"), vmem_limit_bytes=VMEM_LIMIT),
        name="attention",
    )(q, k_cur, v_cur, k_past, v_past, bdu, bdo, jm, g_sb_out.reshape(1, SB_WIDTH))
    return out, left


def _attention_over_cache(q, k_cur, v_cur, cache_k, cache_v, g_sb_out):
    bd, _, past, _ = cache_k.shape
    to_rows = lambda c: c.transpose(0, 2, 1, 3).reshape(bd, c.shape[2], SB_WIDTH).astype(BF16)
    recent = min(past, SB_RECENT)
    out, left = _attention(q, k_cur, v_cur, to_rows(cache_k[:, :, past - recent:]),
                           to_rows(cache_v[:, :, past - recent:]), recent, g_sb_out, report_left=True)
    if recent == past:
        return out
    return lax.cond(
        jnp.any(left > 0),
        lambda: _attention(q, k_cur, v_cur, to_rows(cache_k), to_rows(cache_v), past, g_sb_out)[0],
        lambda: out)


def _retention_kernel(q_ref, k_ref, v_ref, gate_ref, s0_ref, g_ref, o_ref, s_ref, *, chunk, n_chunks):
    c = chunk
    head = pl.program_id(1).astype(F32)
    log_g = jnp.log(1.0 - jnp.exp2(jnp.full((1, LANES), -5.0, F32) - head))
    n_row = lax.broadcasted_iota(jnp.int32, (c, LANES), 0).astype(F32)
    q_decay = jnp.exp((n_row + 1.0) * log_g)
    k_decay = jnp.exp((c - 1.0 - n_row) * log_g)
    s_decay = jnp.exp(float(c) * log_g)
    diff = (lax.broadcasted_iota(jnp.int32, (c, c), 0)
            - lax.broadcasted_iota(jnp.int32, (c, c), 1))
    log_g_cc = log_g if c == LANES else jnp.log(1.0 - jnp.exp2(jnp.full((1, c), -5.0, F32) - head))
    decay = jnp.where(diff >= 0, jnp.exp(jnp.maximum(diff, 0).astype(F32) * log_g_cc), 0.0)
    gain = g_ref[...]

    def body(ci, s):
        rows = pl.ds(pl.multiple_of(ci * c, c), c)
        q = q_ref[0, rows, :]
        k = k_ref[0, rows, :]
        v = v_ref[0, rows, :]
        qk = lax.dot_general(q, k, (((1,), (1,)), ((), ())), preferred_element_type=F32)
        inner = jnp.dot((qk * decay).astype(BF16), v, preferred_element_type=F32)
        cross = jnp.dot(q, s.astype(BF16), preferred_element_type=F32) * q_decay
        o = inner + cross
        k_dec = (k.astype(F32) * k_decay).astype(BF16)
        s_new = s_decay * s + lax.dot_general(k_dec, v, (((0,), (0,)), ((), ())),
                                              preferred_element_type=F32)
        y = o * lax.rsqrt(jnp.mean(o * o, axis=-1, keepdims=True) + EPS) * gain
        half = 0.5 * gate_ref[0, rows, :].astype(F32)
        o_ref[0, rows, :] = (y * (half * (1.0 + jnp.tanh(half)))).astype(BF16)
        return s_new

    s_ref[0, 0] = lax.fori_loop(0, n_chunks, body, s0_ref[0, 0], unroll=math.gcd(n_chunks, RET_UNROLL))


def _retention(rq, rk, rv, rgate, s0, g_ret_out, chunk):
    b, l, _ = rq.shape
    assert l % chunk == 0
    bs = s0.shape[0]
    seq_spec = pl.BlockSpec((1, l, RET_DK), lambda bi, hh: (bi, 0, hh))
    state_spec = pl.BlockSpec((1, 1, RET_DK, RET_DV), lambda bi, hh: (bi, hh, 0, 0))
    s0_spec = state_spec if bs > 1 else pl.BlockSpec((1, 1, RET_DK, RET_DV), lambda bi, hh: (0, hh, 0, 0))
    kernel = functools.partial(_retention_kernel, chunk=chunk, n_chunks=l // chunk)
    return pl.pallas_call(
        kernel,
        grid=(b, RET_HEADS),
        in_specs=[seq_spec, seq_spec, seq_spec, seq_spec, s0_spec,
                  pl.BlockSpec((1, RET_DV), lambda bi, hh: (0, hh))],
        out_specs=[seq_spec, state_spec],
        out_shape=[jax.ShapeDtypeStruct((b, l, RET_WIDTH), BF16),
                   jax.ShapeDtypeStruct((b, RET_HEADS, RET_DK, RET_DV), F32)],
        compiler_params=pltpu.CompilerParams(
            dimension_semantics=("arbitrary", "arbitrary"), vmem_limit_bytes=VMEM_LIMIT),
        name="retention",
    )(rq, rk, rv, rgate, s0, g_ret_out.reshape(1, RET_WIDTH))


def _merge_ffn_kernel(x_ref, sb_ref, ret_ref, conv0_ref, wo_ref, gf_ref, wu_ref,
                      cw_ref, cb_ref, wd_ref, gl_ref, y_ref, conv_ref,
                      h_ref, hn_ref, ubuf_ref, act_ref, carry_ref, *, tm, seg):
    i = pl.program_id(1)

    @pl.when(i == 0)
    def _():
        carry_ref[...] = conv0_ref[0]

    mixed = jnp.concatenate([sb_ref[0], ret_ref[0]], axis=1)
    h = x_ref[0] + jnp.dot(mixed, wo_ref[...], preferred_element_type=F32)
    h_ref[...] = h
    ms = jnp.mean(h * h, axis=-1, keepdims=True)
    hn_ref[...] = (h * lax.rsqrt(ms + EPS) * gf_ref[...]).astype(BF16)

    n_seg = tm // seg
    ext_rows = seg + SUBLANES

    def up_project(j):
        ubuf = ubuf_ref.at[j % 2]
        hn = hn_ref[...]
        gate_cols = slice(j * FF_CHUNK, (j + 1) * FF_CHUNK)
        val_cols = slice(D_FF + j * FF_CHUNK, D_FF + (j + 1) * FF_CHUNK)
        gate = jnp.dot(hn, wu_ref[:, gate_cols], preferred_element_type=F32)
        val = jnp.dot(hn, wu_ref[:, val_cols], preferred_element_type=F32)
        for s in range(n_seg):
            base = s * ext_rows
            prev = slice(s * SUBLANES, (s + 1) * SUBLANES)
            ubuf[base:base + SUBLANES, :] = carry_ref[j, prev, :]
            ubuf[base + SUBLANES:base + ext_rows, 0:FF_CHUNK] = gate[s * seg:(s + 1) * seg]
            ubuf[base + SUBLANES:base + ext_rows, FF_CHUNK:] = val[s * seg:(s + 1) * seg]
            carry_ref[j, prev, :] = ubuf[base + seg:base + ext_rows, :]

    def gated_conv(j):
        ubuf = ubuf_ref.at[j % 2]
        cw = cw_ref[j]
        cb = cb_ref[j]
        strip = min(seg, FF_STRIP)
        for s in range(n_seg):
            for r in range(0, seg, strip):
                first = s * ext_rows + r
                ext = ubuf[first:first + strip + SUBLANES, :]
                c = cb
                for tap in range(CONV_W):
                    back = CONV_W - 1 - tap
                    rows = ext if back == 0 else pltpu.roll(ext, back, 0)
                    c = c + cw[tap:tap + 1, :] * rows[SUBLANES:, :]
                half = 0.5 * c[:, :FF_CHUNK]
                act_ref[s * seg + r:s * seg + r + strip, j * FF_CHUNK:(j + 1) * FF_CHUNK] = (
                    (half * (1.0 + jnp.tanh(half))) * c[:, FF_CHUNK:]).astype(BF16)

    up_project(0)
    for j in range(N_FF_CHUNKS):
        if j + 1 < N_FF_CHUNKS:
            up_project(j + 1)
        gated_conv(j)

    hh = h_ref[...] + jnp.dot(act_ref[...], wd_ref[...], preferred_element_type=F32)
    ms2 = jnp.mean(hh * hh, axis=-1, keepdims=True)
    y_ref[0] = hh * lax.rsqrt(ms2 + EPS) * gl_ref[...]

    @pl.when(i == pl.num_programs(1) - 1)
    def _():
        conv_ref[0] = carry_ref[...]


def _merge_ffn(x, sb_n, ret_n, conv0, w_out_bf, g_norm_ffn, w_up_bf, cw, cb, wd, g_norm_final, tm, seg=None):
    b, l, d = x.shape
    bs = conv0.shape[0]
    seg = tm if seg is None else seg
    assert tm % seg == 0 and (seg == tm or l == tm)
    row_spec = lambda w: pl.BlockSpec((1, tm, w), lambda bi, i: (bi, i, 0))
    conv_shape = (1, N_FF_CHUNKS, SUBLANES * (tm // seg), 2 * FF_CHUNK)
    conv_spec = pl.BlockSpec(conv_shape, lambda bi, i: (bi, 0, 0, 0))
    conv0_spec = conv_spec if bs > 1 else pl.BlockSpec(conv_shape, lambda bi, i: (0, 0, 0, 0))
    kernel = functools.partial(_merge_ffn_kernel, tm=tm, seg=seg)
    return pl.pallas_call(
        kernel,
        grid=(b, l // tm),
        in_specs=[row_spec(d), row_spec(SB_WIDTH), row_spec(RET_WIDTH), conv0_spec,
                  _const_spec(w_out_bf.shape), _const_spec((1, d)),
                  _const_spec(w_up_bf.shape),
                  _const_spec(cw.shape), _const_spec(cb.shape), _const_spec(wd.shape),
                  _const_spec((1, d))],
        out_specs=[row_spec(d), conv_spec],
        out_shape=[jax.ShapeDtypeStruct((b, l, d), F32),
                   jax.ShapeDtypeStruct((b,) + conv_shape[1:], F32)],
        scratch_shapes=[pltpu.VMEM((tm, d), F32), pltpu.VMEM((tm, d), BF16),
                        pltpu.VMEM((2, tm + SUBLANES * (tm // seg), 2 * FF_CHUNK), F32),
                        pltpu.VMEM((tm, D_FF), BF16),
                        pltpu.VMEM(conv_shape[1:], F32)],
        compiler_params=pltpu.CompilerParams(
            dimension_semantics=("arbitrary", "arbitrary"), vmem_limit_bytes=VMEM_LIMIT),
        name="merge_ffn",
    )(x, sb_n, ret_n, conv0, w_out_bf, g_norm_ffn.reshape(1, d), w_up_bf, cw, cb, wd,
      g_norm_final.reshape(1, d))


def _rope_tables(pos):
    half = RET_DK // 2
    inv = ROPE_BASE ** (-jnp.arange(half, dtype=F32) / half)
    ang = pos.astype(F32)[:, None] * inv[None, :]
    cos, sin = jnp.cos(ang), jnp.sin(ang)
    return jnp.concatenate([cos, cos], axis=1), jnp.concatenate([-sin, sin], axis=1)


def _conv_state_to_chunks(state):
    b = state.shape[0]
    s = state.reshape(b, CONV_W - 1, 2, N_FF_CHUNKS, FF_CHUNK).transpose(0, 3, 1, 2, 4)
    s = s.reshape(b, N_FF_CHUNKS, CONV_W - 1, 2 * FF_CHUNK)
    return jnp.pad(s, ((0, 0), (0, 0), (SUBLANES - (CONV_W - 1), 0), (0, 0)))


def _conv_state_from_chunks(chunks):
    b = chunks.shape[0]
    s = chunks[:, :, SUBLANES - (CONV_W - 1):, :].reshape(b, N_FF_CHUNKS, CONV_W - 1, 2, FF_CHUNK)
    return s.transpose(0, 2, 3, 1, 4).reshape(b, CONV_W - 1, 2 * D_FF)


def _pad_rows(a, rows):
    return a if a.shape[1] == rows else jnp.pad(a, ((0, 0), (0, rows - a.shape[1]), (0, 0)))


def _prefix_rows_kernel(k_any, v_any, k_rows_ref, v_rows_ref, k_ref, v_ref):
    del k_any, v_any
    k_ref[...] = k_rows_ref[...]
    v_ref[...] = v_rows_ref[...]


def _write_prefix_rows(k_big, v_big, k_rows, v_rows):
    b = k_big.shape[0]
    blk = (1,) + k_rows.shape[1:]
    any_spec = pl.BlockSpec(memory_space=pl.ANY)
    rows_spec = pl.BlockSpec(blk, lambda bi: (0, 0, 0, 0))
    out_spec = pl.BlockSpec(blk, lambda bi: (bi, 0, 0, 0))
    big = jax.ShapeDtypeStruct(k_big.shape, k_big.dtype)
    return pl.pallas_call(
        _prefix_rows_kernel,
        grid=(b,),
        in_specs=[any_spec, any_spec, rows_spec, rows_spec],
        out_specs=[out_spec, out_spec],
        out_shape=[big, big],
        input_output_aliases={0: 0, 1: 1},
        name="prefix_rows",
    )(k_big, v_big, k_rows, v_rows)


def _stream_step(x, pos0, attend, s0, conv0, wts, tm, chunk, head_row0=0, stack=False):
    b, l, d = x.shape
    cos2, sin2 = _rope_tables(pos0 + jnp.arange(l))
    tm_project = PROJECT_ROWS if l % PROJECT_ROWS == 0 else tm
    flat = lambda a: a.reshape(1, b * l, a.shape[-1])
    if stack:
        q, k, v, k_out, v_out, rq, rk, rv, rgate = _project(
            flat(x), wts["g_norm_mix"], wts["w_in"], jnp.tile(cos2, (b, 1)), jnp.tile(sin2, (b, 1)), b * l)
        q, k, v, rq, rk, rv, rgate = (a.reshape(b, l, a.shape[-1]) for a in (q, k, v, rq, rk, rv, rgate))
        k_out, v_out = (a.reshape(SB_HEADS, b, l, SB_HEAD_DIM).transpose(1, 0, 2, 3) for a in (k_out, v_out))
    else:
        q, k, v, k_out, v_out, rq, rk, rv, rgate = _project(x, wts["g_norm_mix"], wts["w_in"], cos2, sin2,
                                                           tm_project, head_row0)
    lq = -(-l // SB_KEYS) * SB_KEYS
    sb_n = attend(_pad_rows(q, lq), _pad_rows(k, lq), _pad_rows(v, lq))[:, :l]
    ret_n, s_new = _retention(rq, rk, rv, rgate, s0, wts["g_ret_out"], chunk)
    ffn = functools.partial(_merge_ffn, w_out_bf=wts["w_out"], g_norm_ffn=wts["g_norm_ffn"], w_up_bf=wts["w_up"],
                            cw=wts["cw"], cb=wts["cb"], wd=wts["wd"], g_norm_final=wts["g_norm_final"])
    if stack:
        groups = conv0.shape[2]
        conv0 = conv0.transpose(1, 0, 2, 3).reshape(1, N_FF_CHUNKS, b * groups, 2 * FF_CHUNK)
        y, conv_new = ffn(flat(x), flat(sb_n), flat(ret_n), conv0, tm=b * l, seg=l)
        y = y.reshape(b, l, d)
        conv_new = conv_new.reshape(N_FF_CHUNKS, b, groups, 2 * FF_CHUNK).transpose(1, 0, 2, 3)
    else:
        y, conv_new = ffn(x, sb_n, ret_n, conv0, tm=tm)
    return y, k, v, k_out, v_out, s_new, conv_new


def kernel(x_prompt, x_sample, cache_sb_k, cache_sb_v, state_ret, state_conv, meta_tokens, g_norm_mix, w_in, g_sb_out, g_ret_out, w_out, g_norm_ffn, w_up, conv_w, conv_b, w_down, g_norm_final):
    b, seq, d = x_prompt.shape
    bd, ls, _ = x_sample.shape
    past = cache_sb_k.shape[2]

    def ff_cols(a):
        r = a.shape[0]
        return a.reshape(r, 2, N_FF_CHUNKS, FF_CHUNK).transpose(2, 0, 1, 3).reshape(N_FF_CHUNKS, r, 2 * FF_CHUNK)

    wts = dict(
        g_norm_mix=g_norm_mix, g_sb_out=g_sb_out, g_ret_out=g_ret_out, g_norm_ffn=g_norm_ffn,
        g_norm_final=g_norm_final,
        w_in=w_in.astype(BF16), w_out=w_out.astype(BF16),
        w_up=w_up.astype(BF16),
        cw=ff_cols(conv_w), cb=ff_cols(conv_b.reshape(1, 2 * D_FF)),
        wd=w_down.astype(BF16),
    )

    zero_state = jnp.zeros((1, RET_HEADS, RET_DK, RET_DV), F32)
    zero_conv = jnp.zeros((1, N_FF_CHUNKS, SUBLANES, 2 * FF_CHUNK), F32)
    gain = wts["g_sb_out"]
    _, k_m, v_m, k_m_out, v_m_out, s_meta, conv_meta = _stream_step(
        meta_tokens[None], -N_META, lambda q, k, v: _attention(q, k, v, None, None, 0, gain)[0],
        zero_state, zero_conv, wts, N_META, N_META)

    k_m, v_m = _pad_rows(k_m, SB_KEYS), _pad_rows(v_m, SB_KEYS)
    y_prompt, _, _, k_p_out, v_p_out, s_prompt, conv_prompt = _stream_step(
        x_prompt, 0, lambda q, k, v: _attention(q, k, v, k_m, v_m, N_META, gain)[0],
        s_meta, conv_meta, wts, 512, 256, head_row0=N_META)
    new_k_prompt, new_v_prompt = _write_prefix_rows(k_p_out, v_p_out, k_m_out, v_m_out)

    y_sample, _, _, k_s_out, v_s_out, s_sample, conv_sample = _stream_step(
        x_sample, past, lambda q, k, v: _attention_over_cache(q, k, v, cache_sb_k, cache_sb_v, gain),
        state_ret, _conv_state_to_chunks(state_conv), wts, ls, ls, stack=True)

    return (y_prompt, y_sample, new_k_prompt, new_v_prompt, s_prompt,
            _conv_state_from_chunks(conv_prompt), k_s_out, v_s_out, s_sample,
            _conv_state_from_chunks(conv_sample))
```

```python
import functools
import math

import jax
import jax.numpy as jnp
from jax import lax
from jax.experimental import pallas as pl
from jax.experimental.pallas import tpu as pltpu

D_MODEL = 1024
N_META = 16
SB_HEADS = 8
SB_HEAD_DIM = 64
SB_WIDTH = SB_HEADS * SB_HEAD_DIM
RET_HEADS = 4
RET_DK = 128
RET_DV = 128
RET_WIDTH = RET_HEADS * RET_DV
MIX_WIDTH = SB_WIDTH + RET_WIDTH
GROUP = 512
N_GROUPS = 7
IN_WIDTH = N_GROUPS * GROUP
D_FF = 2816
CONV_W = 3
ROPE_BASE = 10000.0
EPS = 1e-5

LANES = 128
SUBLANES = 8
FF_CHUNK = 256
N_FF_CHUNKS = D_FF // FF_CHUNK
FF_STRIP = 64
RET_UNROLL = 16
PROJECT_ROWS = 1024
SB_KEYS = 128
SB_QUERY_TILE = 512
SB_UNROLL = 2
PAIR = 2 * SB_HEAD_DIM
SB_PAIRS = 2
SB_RECENT = 512
SB_EXHAUSTED = 152.0
SB_Q_SCALE = SB_HEAD_DIM ** -0.5 * math.log2(math.e)
VMEM_LIMIT = 56 * 1024 * 1024

BF16 = jnp.bfloat16
F32 = jnp.float32


def _const_spec(shape):
    zeros = (0,) * len(shape)
    return pl.BlockSpec(shape, lambda *_: zeros, pipeline_mode=pl.Buffered(1))


def _project_kernel(x_ref, g_ref, w_ref, cos_ref, sin_ref,
                    q_ref, k_ref, v_ref, ko_ref, vo_ref, rq_ref, rk_ref, rv_ref, rg_ref):
    x = x_ref[0]
    ms = jnp.mean(x * x, axis=-1, keepdims=True)
    h = (x * lax.rsqrt(ms + EPS) * g_ref[...]).astype(BF16)

    def group(i):
        return jnp.dot(h, w_ref[:, i * GROUP:(i + 1) * GROUP], preferred_element_type=F32)

    def split_heads(p, out_ref):
        for hh in range(SB_HEADS):
            out_ref[0, hh] = p[:, hh * SB_HEAD_DIM:(hh + 1) * SB_HEAD_DIM]

    def rope(p, out_ref, scale):
        cos = cos_ref[...]
        sin = sin_ref[...]
        for hh in range(RET_HEADS):
            t = p[:, hh * RET_DK:(hh + 1) * RET_DK]
            r = t * cos + pltpu.roll(t, RET_DK // 2, 1) * sin
            if scale is not None:
                r = r * scale
            out_ref[0, :, hh * RET_DK:(hh + 1) * RET_DK] = r.astype(BF16)

    q_ref[0] = (group(0) * SB_Q_SCALE).astype(BF16)
    pk = group(1)
    k_ref[0] = pk.astype(BF16)
    split_heads(pk, ko_ref)
    pv = group(2)
    v_ref[0] = pv.astype(BF16)
    split_heads(pv, vo_ref)
    rope(group(3), rq_ref, None)
    rope(group(4), rk_ref, RET_DK ** -0.5)
    rv_ref[0] = group(5).astype(BF16)
    rg_ref[0] = group(6).astype(BF16)


def _project(x, g_norm, w_in_bf, cos2, sin2, tm, head_row0=0):
    b, l, d = x.shape
    grid = (b, l // tm)
    row_spec = lambda w: pl.BlockSpec((1, tm, w), lambda bi, i: (bi, i, 0))
    head_blk = (1, SB_HEADS, tm, SB_HEAD_DIM)
    head_spec = pl.BlockSpec(tuple(pl.Element(n) for n in head_blk),
                             lambda bi, i: (bi, 0, pl.multiple_of(head_row0 + i * tm, SUBLANES), 0))
    tab_spec = pl.BlockSpec((tm, RET_DK), lambda bi, i: (i, 0))
    act = jax.ShapeDtypeStruct((b, l, GROUP), BF16)
    heads = jax.ShapeDtypeStruct((b, SB_HEADS, head_row0 + l, SB_HEAD_DIM), F32)
    return pl.pallas_call(
        _project_kernel,
        grid=grid,
        in_specs=[row_spec(d), _const_spec((1, d)), _const_spec((d, IN_WIDTH)), tab_spec, tab_spec],
        out_specs=[row_spec(GROUP)] * 3 + [head_spec] * 2 + [row_spec(GROUP)] * 4,
        out_shape=[act] * 3 + [heads] * 2 + [act] * 4,
        compiler_params=pltpu.CompilerParams(
            dimension_semantics=("arbitrary", "arbitrary"), vmem_limit_bytes=VMEM_LIMIT),
        name="project",
    )(x, g_norm.reshape(1, d), w_in_bf, cos2, sin2)


def _split_bf16(x):
    hi = x.astype(BF16)
    lo = (x - hi.astype(F32)).astype(BF16)
    return hi, lo


def _attention_kernel(q_ref, kc_ref, vc_ref, kp_ref, vp_ref, bdu_ref, bdo_ref, j_ref, g_ref,
                      o_ref, left_ref, kxc_ref, vxc_ref, kxp_ref, vxp_ref, carry_ref, acc_ref,
                      *, tq, n_q, n_past, past_valid, report_left):
    tk = SB_KEYS
    per_tile = tq // tk
    head0 = lax.broadcasted_iota(jnp.int32, (tk, LANES), 1) < SB_HEAD_DIM
    key_in_block = lax.broadcasted_iota(jnp.int32, (tq, 2 * tk), 1) & (tk - 1)
    delta = key_in_block - lax.broadcasted_iota(jnp.int32, (tq, 2 * tk), 0)

    pairs = range(SB_PAIRS)
    lanes = [slice(p * PAIR, (p + 1) * PAIR) for p in pairs]

    def expand_block(src_ref, dst_ref, j):
        for p in pairs:
            blk = src_ref[0, pl.ds(pl.multiple_of(j * tk, tk), tk), lanes[p]]
            zero = jnp.zeros_like(blk)
            dst_ref[p, j, 0:tk, :] = jnp.where(head0, blk, zero)
            dst_ref[p, j, tk:2 * tk, :] = jnp.where(head0, zero, blk)

    def expand(src_ref, dst_ref, n_blocks):
        def body(j, _):
            expand_block(src_ref, dst_ref, j)
            return 0
        lax.fori_loop(0, n_blocks, body, 0)

    if n_past > 0:
        expand(kp_ref, kxp_ref, n_past)
        expand(vp_ref, vxp_ref, n_past)

    def add_blocks(kx_ref, vx_ref, i, j_last, masks, r0=0, r1=tq):
        rows = pl.ds(pl.multiple_of(i * tq + r0, tk), r1 - r0)
        carry = [carry_ref[p, r0:r1, :] for p in pairs]
        for u, mask in enumerate(masks):
            j = j_last - u
            for p in pairs:
                z = lax.dot_general(q_ref[0, rows, lanes[p]], kx_ref[p, j], (((1,), (1,)), ((), ())),
                                    preferred_element_type=F32)
                sp = jnp.maximum(z, 0.0) + jnp.log2(1.0 + jnp.exp2(-jnp.abs(z)))
                spm = sp if mask is None else jnp.where(mask, sp, 0.0)
                hi = spm.astype(BF16)
                later = jnp.dot(hi, bdu_ref[...], preferred_element_type=F32)
                total = jnp.dot(hi, bdo_ref[...], preferred_element_type=F32)
                a = jnp.exp2((z - sp) - (later + carry[p]))
                if mask is not None:
                    a = jnp.where(mask, a, 0.0)
                acc_ref[p, r0:r1, :] += jnp.dot(a.astype(BF16), vx_ref[p, j], preferred_element_type=F32)
                carry[p] = carry[p] + total
        for p in pairs:
            carry_ref[p, r0:r1, :] = carry[p]
        return carry

    def stick_left(carry):
        least = functools.reduce(jnp.minimum, carry)
        return (jnp.min(least) < SB_EXHAUSTED).astype(jnp.int32)

    def sweep(kx_ref, vx_ref, i, j_top, n_steps, unroll, alive):
        def cond(state):
            m, live = state
            return jnp.logical_and(m < n_steps, live > 0)

        def body(state):
            m, _ = state
            carry = add_blocks(kx_ref, vx_ref, i, j_top - m * unroll, [None] * unroll)
            return m + 1, stick_left(carry)

        return lax.while_loop(cond, body, (jnp.int32(0), alive))[1]

    def all_carries(r0=0):
        return [carry_ref[p, r0:, :] for p in pairs]

    def q_tile(i, any_left, first):
        rows = pl.ds(pl.multiple_of(i * tq, tq), tq)
        carry_ref[...] = jnp.zeros_like(carry_ref)
        acc_ref[...] = jnp.zeros_like(acc_ref)
        for jj in range(per_tile):
            expand_block(kc_ref, kxc_ref, i * per_tile + jj)
            expand_block(vc_ref, vxc_ref, i * per_tile + jj)

        for jj in range(per_tile - 1, -1, -1):
            r0 = jj * tk
            add_blocks(kxc_ref, vxc_ref, i, i * per_tile + jj, [delta[r0:, :] < -r0], r0)

        unroll = SB_UNROLL if per_tile % SB_UNROLL == 0 else 1
        top = unroll * tk
        if first:
            alive = stick_left(all_carries())
        elif top < tq:
            j_prev = i * per_tile - 1
            top_left = stick_left(add_blocks(kxc_ref, vxc_ref, i, j_prev, [None] * unroll, 0, top))
            low_left = lax.cond(
                stick_left(all_carries(top)) > 0,
                lambda: stick_left(add_blocks(kxc_ref, vxc_ref, i, j_prev, [None] * unroll, top, tq)),
                lambda: jnp.int32(0))
            alive = sweep(kxc_ref, vxc_ref, i, j_prev - unroll, (i * per_tile) // unroll - 1, unroll,
                          jnp.maximum(top_left, low_left))
        else:
            alive = sweep(kxc_ref, vxc_ref, i, i * per_tile - 1, (i * per_tile) // unroll, unroll,
                          stick_left(all_carries()))

        if n_past > 0:
            n_full = n_past
            if past_valid < tk:
                @pl.when(alive > 0)
                def _():
                    add_blocks(kxp_ref, vxp_ref, i, n_past - 1, [key_in_block < past_valid])

                n_full = n_past - 1
                if n_full > 0:
                    alive = stick_left(all_carries())
            if n_full // SB_UNROLL > 0:
                alive = sweep(kxp_ref, vxp_ref, i, n_full - 1, n_full // SB_UNROLL, SB_UNROLL, alive)
            if n_full % SB_UNROLL:
                alive = sweep(kxp_ref, vxp_ref, i, n_full % SB_UNROLL - 1, n_full % SB_UNROLL, 1, alive)

        for p in pairs:
            o = acc_ref[p]
            hi, lo = _split_bf16(o * o)
            jm = j_ref[...]
            ms = (jnp.dot(hi, jm, preferred_element_type=F32)
                  + jnp.dot(lo, jm, preferred_element_type=F32)) * (1.0 / SB_HEAD_DIM)
            y = o * lax.rsqrt(ms + EPS) * g_ref[:, lanes[p]]
            o_ref[0, rows, lanes[p]] = y.astype(BF16)
        if not report_left:
            return any_left
        return jnp.maximum(any_left, stick_left(all_carries()))

    any_left = q_tile(jnp.int32(0), jnp.int32(0), True)
    any_left = lax.fori_loop(1, n_q, lambda i, left: q_tile(i, left, False), any_left)
    left_ref[...] = jnp.full(left_ref.shape, any_left, jnp.int32)


def _attention(q, k_cur, v_cur, k_past, v_past, past_len, g_sb_out, report_left=False):
    b, lq, _ = q.shape
    tk = SB_KEYS
    tq = min(SB_QUERY_TILE, lq)
    assert lq % tq == 0 and tq % tk == 0
    if k_past is None:
        k_past = jnp.zeros((1, tk, SB_WIDTH), BF16)
        v_past = jnp.zeros((1, tk, SB_WIDTH), BF16)
        n_past, past_valid = 0, tk
    else:
        assert k_past.shape[1] % tk == 0
        n_past = -(-past_len // tk)
        past_valid = past_len - (n_past - 1) * tk
        k_past = k_past[:, :n_past * tk]
        v_past = v_past[:, :n_past * tk]
    bp, p, _ = k_past.shape
    jj = jnp.arange(2 * tk)
    same_head = (jj[:, None] // tk) == (jj[None, :] // tk)
    bdu = (same_head & (jj[:, None] > jj[None, :])).astype(BF16)
    bdo = same_head.astype(BF16)
    ll = jnp.arange(LANES) // SB_HEAD_DIM
    jm = (ll[:, None] == ll[None, :]).astype(BF16)
    width = SB_PAIRS * PAIR
    cur_spec = pl.BlockSpec((1, lq, width), lambda bi, hp: (bi, 0, hp))
    past_spec = pl.BlockSpec((1, p, width), (lambda bi, hp: (bi, 0, hp)) if bp > 1
                             else (lambda bi, hp: (0, 0, hp)))
    kernel = functools.partial(_attention_kernel, tq=tq, n_q=lq // tq, n_past=n_past,
                               past_valid=past_valid, report_left=report_left)
    stacked = lambda n: pltpu.VMEM((SB_PAIRS, n, 2 * tk, PAIR), BF16)
    n_groups = SB_WIDTH // width
    out, left = pl.pallas_call(
        kernel,
        grid=(b, n_groups),
        in_specs=[cur_spec, cur_spec, cur_spec, past_spec, past_spec,
                  _const_spec(bdu.shape), _const_spec(bdo.shape), _const_spec(jm.shape),
                  pl.BlockSpec((1, width), lambda bi, hp: (0, hp))],
        out_specs=[cur_spec, pl.BlockSpec((1, 1, SUBLANES, LANES), lambda bi, hp: (bi, hp, 0, 0))],
        out_shape=[jax.ShapeDtypeStruct((b, lq, SB_WIDTH), BF16),
                   jax.ShapeDtypeStruct((b, n_groups, SUBLANES, LANES), jnp.int32)],
        scratch_shapes=[stacked(lq // tk), stacked(lq // tk), stacked(p // tk), stacked(p // tk),
                        pltpu.VMEM((SB_PAIRS, tq, 2 * tk), F32), pltpu.VMEM((SB_PAIRS, tq, PAIR), F32)],
        compiler_params=pltpu.CompilerParams(
            dimension_semantics=("arbitrary", "arbitrary"), vmem_limit_bytes=VMEM_LIMIT),
        name="attention",
    )(q, k_cur, v_cur, k_past, v_past, bdu, bdo, jm, g_sb_out.reshape(1, SB_WIDTH))
    return out, left


def _attention_over_cache(q, k_cur, v_cur, cache_k, cache_v, g_sb_out):
    bd, _, past, _ = cache_k.shape
    to_rows = lambda c: c.transpose(0, 2, 1, 3).reshape(bd, c.shape[2], SB_WIDTH).astype(BF16)
    recent = min(past, SB_RECENT)
    out, left = _attention(q, k_cur, v_cur, to_rows(cache_k[:, :, past - recent:]),
                           to_rows(cache_v[:, :, past - recent:]), recent, g_sb_out, report_left=True)
    if recent == past:
        return out
    return lax.cond(
        jnp.any(left > 0),
        lambda: _attention(q, k_cur, v_cur, to_rows(cache_k), to_rows(cache_v), past, g_sb_out)[0],
        lambda: out)


def _retention_kernel(q_ref, k_ref, v_ref, gate_ref, s0_ref, g_ref, o_ref, s_ref, *, chunk, n_chunks):
    c = chunk
    head = pl.program_id(1).astype(F32)
    log_g = jnp.log(1.0 - jnp.exp2(jnp.full((1, LANES), -5.0, F32) - head))
    n_row = lax.broadcasted_iota(jnp.int32, (c, LANES), 0).astype(F32)
    q_decay = jnp.exp((n_row + 1.0) * log_g)
    k_decay = jnp.exp((c - 1.0 - n_row) * log_g)
    s_decay = jnp.exp(float(c) * log_g)
    diff = (lax.broadcasted_iota(jnp.int32, (c, c), 0)
            - lax.broadcasted_iota(jnp.int32, (c, c), 1))
    log_g_cc = log_g if c == LANES else jnp.log(1.0 - jnp.exp2(jnp.full((1, c), -5.0, F32) - head))
    decay = jnp.where(diff >= 0, jnp.exp(jnp.maximum(diff, 0).astype(F32) * log_g_cc), 0.0)
    gain = g_ref[...]

    def body(ci, s):
        rows = pl.ds(pl.multiple_of(ci * c, c), c)
        q = q_ref[0, rows, :]
        k = k_ref[0, rows, :]
        v = v_ref[0, rows, :]
        qk = lax.dot_general(q, k, (((1,), (1,)), ((), ())), preferred_element_type=F32)
        inner = jnp.dot((qk * decay).astype(BF16), v, preferred_element_type=F32)
        cross = jnp.dot(q, s.astype(BF16), preferred_element_type=F32) * q_decay
        o = inner + cross
        k_dec = (k.astype(F32) * k_decay).astype(BF16)
        s_new = s_decay * s + lax.dot_general(k_dec, v, (((0,), (0,)), ((), ())),
                                              preferred_element_type=F32)
        y = o * lax.rsqrt(jnp.mean(o * o, axis=-1, keepdims=True) + EPS) * gain
        half = 0.5 * gate_ref[0, rows, :].astype(F32)
        o_ref[0, rows, :] = (y * (half * (1.0 + jnp.tanh(half)))).astype(BF16)
        return s_new

    s_ref[0, 0] = lax.fori_loop(0, n_chunks, body, s0_ref[0, 0], unroll=math.gcd(n_chunks, RET_UNROLL))


def _retention(rq, rk, rv, rgate, s0, g_ret_out, chunk):
    b, l, _ = rq.shape
    assert l % chunk == 0
    bs = s0.shape[0]
    seq_spec = pl.BlockSpec((1, l, RET_DK), lambda bi, hh: (bi, 0, hh))
    state_spec = pl.BlockSpec((1, 1, RET_DK, RET_DV), lambda bi, hh: (bi, hh, 0, 0))
    s0_spec = state_spec if bs > 1 else pl.BlockSpec((1, 1, RET_DK, RET_DV), lambda bi, hh: (0, hh, 0, 0))
    kernel = functools.partial(_retention_kernel, chunk=chunk, n_chunks=l // chunk)
    return pl.pallas_call(
        kernel,
        grid=(b, RET_HEADS),
        in_specs=[seq_spec, seq_spec, seq_spec, seq_spec, s0_spec,
                  pl.BlockSpec((1, RET_DV), lambda bi, hh: (0, hh))],
        out_specs=[seq_spec, state_spec],
        out_shape=[jax.ShapeDtypeStruct((b, l, RET_WIDTH), BF16),
                   jax.ShapeDtypeStruct((b, RET_HEADS, RET_DK, RET_DV), F32)],
        compiler_params=pltpu.CompilerParams(
            dimension_semantics=("arbitrary", "arbitrary"), vmem_limit_bytes=VMEM_LIMIT),
        name="retention",
    )(rq, rk, rv, rgate, s0, g_ret_out.reshape(1, RET_WIDTH))


def _merge_ffn_kernel(x_ref, sb_ref, ret_ref, conv0_ref, wo_ref, gf_ref, wu_ref,
                      cw_ref, cb_ref, wd_ref, gl_ref, y_ref, conv_ref,
                      h_ref, hn_ref, ubuf_ref, act_ref, carry_ref, *, tm, seg):
    i = pl.program_id(1)

    @pl.when(i == 0)
    def _():
        carry_ref[...] = conv0_ref[0]

    mixed = jnp.concatenate([sb_ref[0], ret_ref[0]], axis=1)
    h = x_ref[0] + jnp.dot(mixed, wo_ref[...], preferred_element_type=F32)
    h_ref[...] = h
    ms = jnp.mean(h * h, axis=-1, keepdims=True)
    hn_ref[...] = (h * lax.rsqrt(ms + EPS) * gf_ref[...]).astype(BF16)

    n_seg = tm // seg
    ext_rows = seg + SUBLANES

    def up_project(j):
        ubuf = ubuf_ref.at[j % 2]
        hn = hn_ref[...]
        gate_cols = slice(j * FF_CHUNK, (j + 1) * FF_CHUNK)
        val_cols = slice(D_FF + j * FF_CHUNK, D_FF + (j + 1) * FF_CHUNK)
        gate = jnp.dot(hn, wu_ref[:, gate_cols], preferred_element_type=F32)
        val = jnp.dot(hn, wu_ref[:, val_cols], preferred_element_type=F32)
        for s in range(n_seg):
            base = s * ext_rows
            prev = slice(s * SUBLANES, (s + 1) * SUBLANES)
            ubuf[base:base + SUBLANES, :] = carry_ref[j, prev, :]
            ubuf[base + SUBLANES:base + ext_rows, 0:FF_CHUNK] = gate[s * seg:(s + 1) * seg]
            ubuf[base + SUBLANES:base + ext_rows, FF_CHUNK:] = val[s * seg:(s + 1) * seg]
            carry_ref[j, prev, :] = ubuf[base + seg:base + ext_rows, :]

    def gated_conv(j):
        ubuf = ubuf_ref.at[j % 2]
        cw = cw_ref[j]
        cb = cb_ref[j]
        strip = min(seg, FF_STRIP)
        for s in range(n_seg):
            for r in range(0, seg, strip):
                first = s * ext_rows + r
                ext = ubuf[first:first + strip + SUBLANES, :]
                c = cb
                for tap in range(CONV_W):
                    back = CONV_W - 1 - tap
                    rows = ext if back == 0 else pltpu.roll(ext, back, 0)
                    c = c + cw[tap:tap + 1, :] * rows[SUBLANES:, :]
                half = 0.5 * c[:, :FF_CHUNK]
                act_ref[s * seg + r:s * seg + r + strip, j * FF_CHUNK:(j + 1) * FF_CHUNK] = (
                    (half * (1.0 + jnp.tanh(half))) * c[:, FF_CHUNK:]).astype(BF16)

    up_project(0)
    for j in range(N_FF_CHUNKS):
        if j + 1 < N_FF_CHUNKS:
            up_project(j + 1)
        gated_conv(j)

    hh = h_ref[...] + jnp.dot(act_ref[...], wd_ref[...], preferred_element_type=F32)
    ms2 = jnp.mean(hh * hh, axis=-1, keepdims=True)
    y_ref[0] = hh * lax.rsqrt(ms2 + EPS) * gl_ref[...]

    @pl.when(i == pl.num_programs(1) - 1)
    def _():
        conv_ref[0] = carry_ref[...]


def _merge_ffn(x, sb_n, ret_n, conv0, w_out_bf, g_norm_ffn, w_up_bf, cw, cb, wd, g_norm_final, tm, seg=None):
    b, l, d = x.shape
    bs = conv0.shape[0]
    seg = tm if seg is None else seg
    assert tm % seg == 0 and (seg == tm or l == tm)
    row_spec = lambda w: pl.BlockSpec((1, tm, w), lambda bi, i: (bi, i, 0))
    conv_shape = (1, N_FF_CHUNKS, SUBLANES * (tm // seg), 2 * FF_CHUNK)
    conv_spec = pl.BlockSpec(conv_shape, lambda bi, i: (bi, 0, 0, 0))
    conv0_spec = conv_spec if bs > 1 else pl.BlockSpec(conv_shape, lambda bi, i: (0, 0, 0, 0))
    kernel = functools.partial(_merge_ffn_kernel, tm=tm, seg=seg)
    return pl.pallas_call(
        kernel,
        grid=(b, l // tm),
        in_specs=[row_spec(d), row_spec(SB_WIDTH), row_spec(RET_WIDTH), conv0_spec,
                  _const_spec(w_out_bf.shape), _const_spec((1, d)),
                  _const_spec(w_up_bf.shape),
                  _const_spec(cw.shape), _const_spec(cb.shape), _const_spec(wd.shape),
                  _const_spec((1, d))],
        out_specs=[row_spec(d), conv_spec],
        out_shape=[jax.ShapeDtypeStruct((b, l, d), F32),
                   jax.ShapeDtypeStruct((b,) + conv_shape[1:], F32)],
        scratch_shapes=[pltpu.VMEM((tm, d), F32), pltpu.VMEM((tm, d), BF16),
                        pltpu.VMEM((2, tm + SUBLANES * (tm // seg), 2 * FF_CHUNK), F32),
                        pltpu.VMEM((tm, D_FF), BF16),
                        pltpu.VMEM(conv_shape[1:], F32)],
        compiler_params=pltpu.CompilerParams(
            dimension_semantics=("arbitrary", "arbitrary"), vmem_limit_bytes=VMEM_LIMIT),
        name="merge_ffn",
    )(x, sb_n, ret_n, conv0, w_out_bf, g_norm_ffn.reshape(1, d), w_up_bf, cw, cb, wd,
      g_norm_final.reshape(1, d))


def _rope_tables(pos):
    half = RET_DK // 2
    inv = ROPE_BASE ** (-jnp.arange(half, dtype=F32) / half)
    ang = pos.astype(F32)[:, None] * inv[None, :]
    cos, sin = jnp.cos(ang), jnp.sin(ang)
    return jnp.concatenate([cos, cos], axis=1), jnp.concatenate([-sin, sin], axis=1)


def _conv_state_to_chunks(state):
    b = state.shape[0]
    s = state.reshape(b, CONV_W - 1, 2, N_FF_CHUNKS, FF_CHUNK).transpose(0, 3, 1, 2, 4)
    s = s.reshape(b, N_FF_CHUNKS, CONV_W - 1, 2 * FF_CHUNK)
    return jnp.pad(s, ((0, 0), (0, 0), (SUBLANES - (CONV_W - 1), 0), (0, 0)))


def _conv_state_from_chunks(chunks):
    b = chunks.shape[0]
    s = chunks[:, :, SUBLANES - (CONV_W - 1):, :].reshape(b, N_FF_CHUNKS, CONV_W - 1, 2, FF_CHUNK)
    return s.transpose(0, 2, 3, 1, 4).reshape(b, CONV_W - 1, 2 * D_FF)


def _pad_rows(a, rows):
    return a if a.shape[1] == rows else jnp.pad(a, ((0, 0), (0, rows - a.shape[1]), (0, 0)))


def _prefix_rows_kernel(k_any, v_any, k_rows_ref, v_rows_ref, k_ref, v_ref):
    del k_any, v_any
    k_ref[...] = k_rows_ref[...]
    v_ref[...] = v_rows_ref[...]


def _write_prefix_rows(k_big, v_big, k_rows, v_rows):
    b = k_big.shape[0]
    blk = (1,) + k_rows.shape[1:]
    any_spec = pl.BlockSpec(memory_space=pl.ANY)
    rows_spec = pl.BlockSpec(blk, lambda bi: (0, 0, 0, 0))
    out_spec = pl.BlockSpec(blk, lambda bi: (bi, 0, 0, 0))
    big = jax.ShapeDtypeStruct(k_big.shape, k_big.dtype)
    return pl.pallas_call(
        _prefix_rows_kernel,
        grid=(b,),
        in_specs=[any_spec, any_spec, rows_spec, rows_spec],
        out_specs=[out_spec, out_spec],
        out_shape=[big, big],
        input_output_aliases={0: 0, 1: 1},
        name="prefix_rows",
    )(k_big, v_big, k_rows, v_rows)


def _stream_step(x, pos0, attend, s0, conv0, wts, tm, chunk, head_row0=0, stack=False):
    b, l, d = x.shape
    cos2, sin2 = _rope_tables(pos0 + jnp.arange(l))
    tm_project = PROJECT_ROWS if l % PROJECT_ROWS == 0 else tm
    flat = lambda a: a.reshape(1, b * l, a.shape[-1])
    if stack:
        q, k, v, k_out, v_out, rq, rk, rv, rgate = _project(
            flat(x), wts["g_norm_mix"], wts["w_in"], jnp.tile(cos2, (b, 1)), jnp.tile(sin2, (b, 1)), b * l)
        q, k, v, rq, rk, rv, rgate = (a.reshape(b, l, a.shape[-1]) for a in (q, k, v, rq, rk, rv, rgate))
        k_out, v_out = (a.reshape(SB_HEADS, b, l, SB_HEAD_DIM).transpose(1, 0, 2, 3) for a in (k_out, v_out))
    else:
        q, k, v, k_out, v_out, rq, rk, rv, rgate = _project(x, wts["g_norm_mix"], wts["w_in"], cos2, sin2,
                                                           tm_project, head_row0)
    lq = -(-l // SB_KEYS) * SB_KEYS
    sb_n = attend(_pad_rows(q, lq), _pad_rows(k, lq), _pad_rows(v, lq))[:, :l]
    ret_n, s_new = _retention(rq, rk, rv, rgate, s0, wts["g_ret_out"], chunk)
    ffn = functools.partial(_merge_ffn, w_out_bf=wts["w_out"], g_norm_ffn=wts["g_norm_ffn"], w_up_bf=wts["w_up"],
                            cw=wts["cw"], cb=wts["cb"], wd=wts["wd"], g_norm_final=wts["g_norm_final"])
    if stack:
        groups = conv0.shape[2]
        conv0 = conv0.transpose(1, 0, 2, 3).reshape(1, N_FF_CHUNKS, b * groups, 2 * FF_CHUNK)
        y, conv_new = ffn(flat(x), flat(sb_n), flat(ret_n), conv0, tm=b * l, seg=l)
        y = y.reshape(b, l, d)
        conv_new = conv_new.reshape(N_FF_CHUNKS, b, groups, 2 * FF_CHUNK).transpose(1, 0, 2, 3)
    else:
        y, conv_new = ffn(x, sb_n, ret_n, conv0, tm=tm)
    return y, k, v, k_out, v_out, s_new, conv_new


def kernel(x_prompt, x_sample, cache_sb_k, cache_sb_v, state_ret, state_conv, meta_tokens, g_norm_mix, w_in, g_sb_out, g_ret_out, w_out, g_norm_ffn, w_up, conv_w, conv_b, w_down, g_norm_final):
    b, seq, d = x_prompt.shape
    bd, ls, _ = x_sample.shape
    past = cache_sb_k.shape[2]

    def ff_cols(a):
        r = a.shape[0]
        return a.reshape(r, 2, N_FF_CHUNKS, FF_CHUNK).transpose(2, 0, 1, 3).reshape(N_FF_CHUNKS, r, 2 * FF_CHUNK)

    wts = dict(
        g_norm_mix=g_norm_mix, g_sb_out=g_sb_out, g_ret_out=g_ret_out, g_norm_ffn=g_norm_ffn,
        g_norm_final=g_norm_final,
        w_in=w_in.astype(BF16), w_out=w_out.astype(BF16),
        w_up=w_up.astype(BF16),
        cw=ff_cols(conv_w), cb=ff_cols(conv_b.reshape(1, 2 * D_FF)),
        wd=w_down.astype(BF16),
    )

    zero_state = jnp.zeros((1, RET_HEADS, RET_DK, RET_DV), F32)
    zero_conv = jnp.zeros((1, N_FF_CHUNKS, SUBLANES, 2 * FF_CHUNK), F32)
    gain = wts["g_sb_out"]
    _, k_m, v_m, k_m_out, v_m_out, s_meta, conv_meta = _stream_step(
        meta_tokens[None], -N_META, lambda q, k, v: _attention(q, k, v, None, None, 0, gain)[0],
        zero_state, zero_conv, wts, N_META, N_META)

    k_m, v_m = _pad_rows(k_m, SB_KEYS), _pad_rows(v_m, SB_KEYS)
    y_prompt, _, _, k_p_out, v_p_out, s_prompt, conv_prompt = _stream_step(
        x_prompt, 0, lambda q, k, v: _attention(q, k, v, k_m, v_m, N_META, gain)[0],
        s_meta, conv_meta, wts, 512, 256, head_row0=N_META)
    new_k_prompt, new_v_prompt = _write_prefix_rows(k_p_out, v_p_out, k_m_out, v_m_out)

    y_sample, _, _, k_s_out, v_s_out, s_sample, conv_sample = _stream_step(
        x_sample, past, lambda q, k, v: _attention_over_cache(q, k, v, cache_sb_k, cache_sb_v, gain),
        state_ret, _conv_state_to_chunks(state_conv), wts, ls, ls, stack=True)

    return (y_prompt, y_sample, new_k_prompt, new_v_prompt, s_prompt,
            _conv_state_from_chunks(conv_prompt), k_s_out, v_s_out, s_sample,
            _conv_state_from_chunks(conv_sample))
```
